```python
import math
import jax, jax.numpy as jnp
from jax import lax
import numpy as np

D_MODEL = 1024
BATCH = 8
SEQ = 2048
DEPTH = 1

RET_HEADS = 4
RET_DK = 64
RET_DV = 128
RET_CHUNK = 128
RET_ROPE_BASE = 10000.0
SWA_HEADS = 8
SWA_KV_HEADS = 2
SWA_HEAD_DIM = 64
SWA_GROUP = SWA_HEADS // SWA_KV_HEADS
WINDOW = 128
SWA_BLOCK = WINDOW
NUM_BUCKETS = 32
MAX_DISTANCE = 128
RET_QK = RET_HEADS * RET_DK
RET_V = RET_HEADS * RET_DV
SWA_Q = SWA_HEADS * SWA_HEAD_DIM
SWA_KV = SWA_KV_HEADS * SWA_HEAD_DIM
D_MIX = RET_V + SWA_Q
SPLIT_SIZES = (RET_QK, RET_QK, RET_V, RET_V, SWA_Q, SWA_KV, SWA_KV, SWA_Q)
D_IN = sum(SPLIT_SIZES)
NORM_EPS = 1e-6
GN_EPS = 1e-5
NEG_INF = -1e30

kernel_name = "hybrid_retention_swa_sink_layer"


def rms_norm(x, w, eps=NORM_EPS):
    xf = x.astype(jnp.float32)
    y = xf * lax.rsqrt(jnp.mean(xf * xf, axis=-1, keepdims=True) + eps)
    return y * w.astype(jnp.float32)


def rotary(t, base):
    S, d = t.shape[1], t.shape[-1]
    half = d // 2
    inv_freq = base ** (-jnp.arange(half, dtype=jnp.float32) / half)
    ang = jnp.arange(S, dtype=jnp.float32)[:, None] * inv_freq[None, :]
    cos = jnp.cos(ang)[None, :, None, :]
    sin = jnp.sin(ang)[None, :, None, :]
    t1, t2 = t[..., :half], t[..., half:]
    return jnp.concatenate([t1 * cos - t2 * sin, t1 * sin + t2 * cos], axis=-1)


def retention(q, k, v):
    B, S = q.shape[0], q.shape[1]
    N, C, H = S // RET_CHUNK, RET_CHUNK, RET_HEADS
    k = k * (RET_DK ** -0.5)

    def chunks(t):
        return t.reshape(B, N, C, H, t.shape[-1]).transpose(0, 3, 1, 2, 4)

    qc, kc, vc = chunks(q), chunks(k), chunks(v)
    gamma = 1.0 - jnp.exp2(-5.0 - jnp.arange(H, dtype=jnp.float32))
    log_g = jnp.log(gamma)
    i = jnp.arange(C, dtype=jnp.float32)
    diff = i[:, None] - i[None, :]
    decay = jnp.where(diff >= 0, jnp.exp(log_g[:, None, None] * jnp.maximum(diff, 0.0)), 0.0)
    scores = jnp.einsum('bhnid,bhnjd->bhnij', qc, kc) * decay[None, :, None]
    intra = jnp.einsum('bhnij,bhnje->bhnie', scores, vc)
    zeta = jnp.exp(log_g[:, None] * (C - 1.0 - i))
    kv = jnp.einsum('bhnjd,bhnje->nbhde', kc * zeta[None, :, None, :, None], vc)
    chunk_decay = jnp.exp(log_g * C)[None, :, None, None]

    def step(state, kv_n):
        return chunk_decay * state + kv_n, state

    _, prev = lax.scan(step, jnp.zeros((B, H, RET_DK, RET_DV), jnp.float32), kv)
    xi = jnp.exp(log_g[:, None] * (i + 1.0))
    cross = jnp.einsum('bhnid,nbhde->bhnie', qc * xi[None, :, None, :, None], prev)
    return (intra + cross).transpose(0, 2, 3, 1, 4).reshape(B, S, H, RET_DV)


def t5_bucket(n):
    max_exact = NUM_BUCKETS // 2
    nf = jnp.maximum(n, 1).astype(jnp.float32)
    large = max_exact + (jnp.log(nf / max_exact) / math.log(MAX_DISTANCE / max_exact)
                         * (NUM_BUCKETS - max_exact)).astype(jnp.int32)
    large = jnp.minimum(large, NUM_BUCKETS - 1)
    return jnp.where(n < max_exact, n, large)


def sliding_window_attention(q, k, v, q_norm_w, k_norm_w, sinks, rel_bias):
    B, S = q.shape[0], q.shape[1]
    N, C = S // SWA_BLOCK, SWA_BLOCK
    q = rms_norm(q, q_norm_w)
    k = rms_norm(k, k_norm_w)
    qb = q.reshape(B, N, C, SWA_KV_HEADS, SWA_GROUP, SWA_HEAD_DIM)

    def band(t):
        tb = t.reshape(B, N, C, SWA_KV_HEADS, SWA_HEAD_DIM)
        prev = jnp.pad(tb, ((0, 0), (1, 0), (0, 0), (0, 0), (0, 0)))[:, :-1]
        return jnp.concatenate([prev, tb], axis=2)

    kband, vband = band(k), band(v)
    logits = jnp.einsum('bnqhgd,bnkhd->bhgnqk', qb, kband) * (SWA_HEAD_DIM ** -0.5)
    qi = jnp.arange(C)[:, None]
    kj = jnp.arange(2 * C)[None, :]
    dist = qi + C - kj
    bucket = t5_bucket(jnp.maximum(dist, 0))
    bias = rel_bias[bucket].astype(jnp.float32).transpose(2, 0, 1)
    bias = bias.reshape(SWA_KV_HEADS, SWA_GROUP, 1, C, 2 * C)
    key_pos = jnp.arange(N)[:, None, None] * C - C + kj[None]
    mask = (dist[None] >= 0) & (dist[None] < WINDOW) & (key_pos >= 0)
    logits = jnp.where(mask, logits + bias, NEG_INF)
    sink = jnp.broadcast_to(sinks.astype(jnp.float32).reshape(SWA_KV_HEADS, SWA_GROUP, 1, 1, 1),
                            logits.shape[:-1] + (1,))
    probs = jax.nn.softmax(jnp.concatenate([logits, sink], axis=-1), axis=-1)[..., :-1]
    out = jnp.einsum('bhgnqk,bnkhd->bnqhgd', probs, vband)
    return out.reshape(B, S, SWA_Q)


def setup_inputs(seed: int = 0) -> dict:
    key = jax.random.key(seed)
    ks = jax.random.split(key, 9)
    f32 = jnp.float32
    x = jax.random.normal(ks[0], (BATCH, SEQ, D_MODEL), f32)
    norm_w = 1.0 + 0.02 * jax.random.normal(ks[1], (D_MODEL,), f32)
    w_in = jax.random.normal(ks[2], (D_MODEL, D_IN), f32) * D_MODEL ** -0.5
    ret_norm_w = 1.0 + 0.02 * jax.random.normal(ks[3], (RET_V,), f32)
    q_norm_w = 1.0 + 0.02 * jax.random.normal(ks[4], (SWA_HEAD_DIM,), f32)
    k_norm_w = 1.0 + 0.02 * jax.random.normal(ks[5], (SWA_HEAD_DIM,), f32)
    sinks = 0.5 * jax.random.normal(ks[6], (SWA_HEADS,), f32)
    rel_bias = 0.1 * jax.random.normal(ks[7], (NUM_BUCKETS, SWA_HEADS), f32)
    w_out = jax.random.normal(ks[8], (D_MIX, D_MODEL), f32) * D_MIX ** -0.5
    return {"x": x, "norm_w": norm_w, "w_in": w_in, "ret_norm_w": ret_norm_w,
            "q_norm_w": q_norm_w, "k_norm_w": k_norm_w, "sinks": sinks,
            "rel_bias": rel_bias, "w_out": w_out}


def reference(x, norm_w, w_in, ret_norm_w, q_norm_w, k_norm_w, sinks, rel_bias, w_out):
    B, S = x.shape[0], x.shape[1]
    offsets = []
    acc = 0
    for sz in SPLIT_SIZES[:-1]:
        acc += sz
        offsets.append(acc)
    for _ in range(DEPTH):
        h = rms_norm(x, norm_w).astype(x.dtype)
        proj = jnp.einsum('bsd,de->bse', h, w_in).astype(jnp.float32)
        rq, rk, rv, rg, sq, sk, sv, sg = jnp.split(proj, offsets, axis=-1)
        rq = rotary(rq.reshape(B, S, RET_HEADS, RET_DK), RET_ROPE_BASE)
        rk = rotary(rk.reshape(B, S, RET_HEADS, RET_DK), RET_ROPE_BASE)
        ro = retention(rq, rk, rv.reshape(B, S, RET_HEADS, RET_DV))
        mu = jnp.mean(ro, axis=-1, keepdims=True)
        var = jnp.mean(jnp.square(ro - mu), axis=-1, keepdims=True)
        ro = ((ro - mu) * lax.rsqrt(var + GN_EPS)).reshape(B, S, RET_V) * ret_norm_w.astype(jnp.float32)
        ro = ro * jax.nn.silu(rg)
        so = sliding_window_attention(
            sq.reshape(B, S, SWA_HEADS, SWA_HEAD_DIM),
            sk.reshape(B, S, SWA_KV_HEADS, SWA_HEAD_DIM),
            sv.reshape(B, S, SWA_KV_HEADS, SWA_HEAD_DIM),
            q_norm_w, k_norm_w, sinks, rel_bias)
        so = so * jax.nn.silu(sg)
        mixed = jnp.concatenate([ro, so], axis=-1).astype(x.dtype)
        x = x + jnp.einsum('bse,ed->bsd', mixed, w_out)
    return x
```

```python
import math

import numpy as np
import jax
import jax.numpy as jnp
from jax import lax
from jax.experimental import pallas as pl
from jax.experimental.pallas import tpu as pltpu

D_MODEL = 1024
RET_HEADS = 4
RET_DK = 64
RET_DV = 128
CHUNK = 128
RET_ROPE_BASE = 10000.0
SWA_HEADS = 8
SWA_KV_HEADS = 2
SWA_HEAD_DIM = 64
SWA_GROUP = SWA_HEADS // SWA_KV_HEADS
NUM_BUCKETS = 32
MAX_DISTANCE = 128
NORM_EPS = 1e-6
GN_EPS = 1e-5
NEG_INF = -1e30

OFF_RQ, OFF_RK, OFF_RV, OFF_RG = 0, 256, 512, 1024
OFF_SQ, OFF_SK, OFF_SV, OFF_SG = 1536, 2048, 2176, 2304
D_IN = 2816
D_MIX = 1024

LANES = 128
SEQ_TILE = 512
VMEM_LIMIT_BYTES = 56 * 1024 * 1024

BF16 = jnp.bfloat16
F32 = jnp.float32


def _t5_bucket_np(n):
    max_exact = NUM_BUCKETS // 2
    nf = np.maximum(n, 1).astype(np.float64)
    large = max_exact + (np.log(nf / max_exact) / math.log(MAX_DISTANCE / max_exact)
                         * (NUM_BUCKETS - max_exact)).astype(np.int32)
    large = np.minimum(large, NUM_BUCKETS - 1)
    return np.where(n < max_exact, n, large).astype(np.int32)


def _constant_tables(seq):
    half = RET_DK // 2
    inv_freq = RET_ROPE_BASE ** (-np.arange(half, dtype=np.float64) / half)
    ang = np.arange(seq, dtype=np.float64)[:, None] * inv_freq[None, :]
    cos, sin = np.cos(ang), np.sin(ang)
    zeros = np.zeros_like(sin)
    cos_t = np.tile(np.concatenate([cos, cos], axis=1), (1, 2))
    sin_a = np.tile(np.concatenate([-sin, zeros], axis=1), (1, 2))
    sin_b = np.tile(np.concatenate([zeros, sin], axis=1), (1, 2))

    gamma = 1.0 - np.exp2(-5.0 - np.arange(RET_HEADS, dtype=np.float64))
    log_g = np.log(gamma)
    i = np.arange(CHUNK, dtype=np.float64)
    diff = i[:, None] - i[None, :]
    k_scale = RET_DK ** -0.5
    decay = np.where(diff >= 0, np.exp(log_g[:, None, None] * np.maximum(diff, 0.0)), 0.0) * k_scale
    dec = np.stack([np.concatenate([decay[2 * p], decay[2 * p + 1]], axis=1) for p in range(2)])
    xi = np.exp(log_g[:, None] * (i + 1.0))
    zeta = np.exp(log_g[:, None] * (CHUNK - 1.0 - i)) * k_scale

    def per_lane(t):
        return np.stack([np.concatenate([np.repeat(t[2 * p][:, None], RET_DK, 1),
                                         np.repeat(t[2 * p + 1][:, None], RET_DK, 1)], axis=1)
                         for p in range(2)])

    chunk_decay = np.exp(log_g * CHUNK)
    cd = np.stack([np.concatenate([np.full((1, RET_DV), chunk_decay[2 * p]),
                                   np.full((1, RET_DV), chunk_decay[2 * p + 1])], axis=1)
                   for p in range(2)])

    r = np.arange(CHUNK)[:, None]
    c = np.arange(CHUNK)[None, :]
    dist = np.where(c <= r, r - c, r + CHUNK - c)
    bucket = _t5_bucket_np(dist)

    seg = np.kron(np.eye(4), np.full((SWA_HEAD_DIM, SWA_HEAD_DIM), 1.0 / SWA_HEAD_DIM))

    f = lambda a: jnp.asarray(a, dtype=F32)
    return dict(cos=f(cos_t), sin_a=f(sin_a), sin_b=f(sin_b), dec=f(dec), xi=f(per_lane(xi)),
                zeta=f(per_lane(zeta)), cd=f(cd), bucket=jnp.asarray(bucket, dtype=jnp.int32),
                seg=jnp.asarray(seg, dtype=BF16))


def _dot(a, b):
    return jnp.dot(a, b, preferred_element_type=F32)


def _dot_nt(a, b):
    return lax.dot_general(a, b, (((1,), (1,)), ((), ())), preferred_element_type=F32)


def _dot_tn(a, b):
    return lax.dot_general(a, b, (((0,), (0,)), ((), ())), preferred_element_type=F32)


def _layer_kernel(x_ref, nw_ref, win_ref, wout_ref, cos_ref, sina_ref, sinb_ref, dec_ref, xi_ref,
                  zeta_ref, cd_ref, retw_ref, qnw_ref, knw_ref, seg_ref, bkt_ref, relb_ref, sinks_ref,
                  o_ref, proj_s, mixed_s, state_s, kband_s, vband_s, bias_s):
    b = pl.program_id(0)
    t = pl.program_id(1)
    n_chunks = SEQ_TILE // CHUNK

    @pl.when((b == 0) & (t == 0))
    def _build_bias():
        bk = bkt_ref[...]
        for h in range(SWA_HEADS):
            acc = jnp.zeros((CHUNK, CHUNK), F32)
            for u in range(NUM_BUCKETS):
                acc = jnp.where(bk == u, relb_ref[u, h], acc)
            bias_s[h] = acc

    @pl.when(t == 0)
    def _reset_carries():
        state_s[...] = jnp.zeros_like(state_s)
        kband_s[:, 0:CHUNK, :] = jnp.zeros((SWA_KV_HEADS, CHUNK, LANES), BF16)
        vband_s[:, :, 0:CHUNK, :] = jnp.zeros((SWA_KV_HEADS, 2, CHUNK, LANES), BF16)

    x = x_ref[0]
    ms = jnp.mean(x * x, axis=-1, keepdims=True)
    h = (x * lax.rsqrt(ms + NORM_EPS) * nw_ref[...]).astype(BF16)
    proj_s[...] = _dot(h, win_ref[...])

    lane = lax.broadcasted_iota(jnp.int32, (CHUNK, LANES), 1)
    lo = lane < RET_DK
    row = lax.broadcasted_iota(jnp.int32, (CHUNK, CHUNK), 0)
    col = lax.broadcasted_iota(jnp.int32, (CHUNK, CHUNK), 1)
    tri = col <= row
    qnw = qnw_ref[...] * (SWA_HEAD_DIM ** -0.5)
    knw = knw_ref[...]

    def silu_gate(off, rows):
        g = proj_s[rows, off:off + LANES]
        return g * jax.nn.sigmoid(g)

    def chunk_body(c, carry):
        r0 = pl.multiple_of(c * CHUNK, CHUNK)
        rows = pl.ds(r0, CHUNK)
        nxt = pl.ds(r0 + CHUNK, CHUNK)
        band = pl.ds(r0, 2 * CHUNK)

        cosv, sa, sb = cos_ref[rows, :], sina_ref[rows, :], sinb_ref[rows, :]

        def rot(v):
            return v * cosv + pltpu.roll(v, LANES - 32, 1) * sa + pltpu.roll(v, 32, 1) * sb

        for p in range(2):
            q = rot(proj_s[rows, OFF_RQ + p * LANES:OFF_RQ + (p + 1) * LANES])
            k = rot(proj_s[rows, OFF_RK + p * LANES:OFF_RK + (p + 1) * LANES])
            v2 = proj_s[rows, OFF_RV + p * 2 * RET_DV:OFF_RV + (p + 1) * 2 * RET_DV].astype(BF16)
            kcat = jnp.concatenate([jnp.where(lo, k, 0.0), jnp.where(lo, 0.0, k)], axis=0).astype(BF16)
            s2 = (_dot_nt(q.astype(BF16), kcat) * dec_ref[p]).astype(BF16)
            full = state_s[p]
            full_b = full.astype(BF16)
            qxi = q * xi_ref[p]
            q_lo = jnp.where(lo, qxi, 0.0).astype(BF16)
            q_hi = jnp.where(lo, 0.0, qxi).astype(BF16)
            outs = (
                _dot(jnp.concatenate([s2[:, :CHUNK], q_lo], axis=1),
                     jnp.concatenate([v2[:, :RET_DV], full_b[:, :RET_DV]], axis=0)),
                _dot(jnp.concatenate([s2[:, CHUNK:], q_hi], axis=1),
                     jnp.concatenate([v2[:, RET_DV:], full_b[:, RET_DV:]], axis=0)),
            )
            kz = (k * zeta_ref[p]).astype(BF16)
            state_s[p] = full * cd_ref[p] + _dot_tn(kz, v2)
            for j, o in enumerate(outs):
                hd = 2 * p + j
                mu = jnp.mean(o, axis=-1, keepdims=True)
                d = o - mu
                var = jnp.mean(d * d, axis=-1, keepdims=True)
                y = d * lax.rsqrt(var + GN_EPS) * retw_ref[:, hd * RET_DV:(hd + 1) * RET_DV]
                mixed_s[rows, hd * RET_DV:(hd + 1) * RET_DV] = (
                    y * silu_gate(OFF_RG + hd * RET_DV, rows)).astype(BF16)

        sk = proj_s[rows, OFF_SK:OFF_SK + LANES]
        ssk = _dot((sk * sk).astype(BF16), seg_ref[0:LANES, 0:LANES])
        kn = sk * lax.rsqrt(ssk + NORM_EPS) * knw
        kn_sw = pltpu.roll(kn, SWA_HEAD_DIM, 1)
        sv = proj_s[rows, OFF_SV:OFF_SV + LANES]
        sv_sw = pltpu.roll(sv, SWA_HEAD_DIM, 1)
        kband_s[0, nxt, :] = jnp.where(lo, kn, kn_sw).astype(BF16)
        kband_s[1, nxt, :] = jnp.where(lo, kn_sw, kn).astype(BF16)
        vband_s[0, 0, nxt, :] = jnp.where(lo, sv, 0.0).astype(BF16)
        vband_s[0, 1, nxt, :] = jnp.where(lo, 0.0, sv_sw).astype(BF16)
        vband_s[1, 0, nxt, :] = jnp.where(lo, sv_sw, 0.0).astype(BF16)
        vband_s[1, 1, nxt, :] = jnp.where(lo, 0.0, sv).astype(BF16)

        neg_first = jnp.where((t == 0) & (c == 0), NEG_INF, 0.0).astype(F32)

        for g in range(SWA_KV_HEADS):
            q_rows = []
            for s in range(2):
                slab = 2 * g + s
                sq = proj_s[rows, OFF_SQ + slab * LANES:OFF_SQ + (slab + 1) * LANES]
                ssq = _dot((sq * sq).astype(BF16), seg_ref[0:LANES, 0:LANES])
                qn = sq * lax.rsqrt(ssq + NORM_EPS) * qnw[:, slab * LANES:(slab + 1) * LANES]
                q_rows += [jnp.where(lo, qn, 0.0), jnp.where(lo, 0.0, qn)]
            qst = jnp.concatenate(q_rows, axis=0).astype(BF16)
            l2 = _dot_nt(qst, kband_s[g, band, :])
            for s in range(2):
                pcat = []
                for j in range(2):
                    hh = 2 * s + j
                    hd = SWA_GROUP * g + hh
                    lp = l2[hh * CHUNK:(hh + 1) * CHUNK, 0:CHUNK] + neg_first
                    lc = l2[hh * CHUNK:(hh + 1) * CHUNK, CHUNK:2 * CHUNK]
                    lg = jnp.where(tri, lc, lp) + bias_s[hd]
                    sink = sinks_ref[hd]
                    m = jnp.maximum(jnp.max(lg, axis=-1, keepdims=True), sink)
                    e = jnp.exp(lg - m)
                    den = jnp.sum(e, axis=-1, keepdims=True) + jnp.exp(sink - m)
                    pr = e * (1.0 / den)
                    pcat.append(jnp.concatenate([jnp.where(tri, 0.0, pr), jnp.where(tri, pr, 0.0)],
                                                axis=1).astype(BF16))
                so = _dot(pcat[0], vband_s[g, 0, band, :]) + _dot(pcat[1], vband_s[g, 1, band, :])
                slab = 2 * g + s
                mixed_s[rows, RET_HEADS * RET_DV + slab * LANES:RET_HEADS * RET_DV + (slab + 1) * LANES] = (
                    so * silu_gate(OFF_SG + slab * LANES, rows)).astype(BF16)
        return carry

    lax.fori_loop(0, n_chunks, chunk_body, 0)

    last = pl.ds(SEQ_TILE, CHUNK)
    kband_s[:, 0:CHUNK, :] = kband_s[:, last, :]
    vband_s[:, :, 0:CHUNK, :] = vband_s[:, :, last, :]

    o_ref[0] = x_ref[0] + _dot(mixed_s[...], wout_ref[...])


def kernel(x, norm_w, w_in, ret_norm_w, q_norm_w, k_norm_w, sinks, rel_bias, w_out):
    batch, seq, d_model = x.shape
    assert d_model == D_MODEL and seq % SEQ_TILE == 0
    assert w_in.shape == (D_MODEL, D_IN) and w_out.shape == (D_MIX, D_MODEL)
    tb = _constant_tables(seq)
    n_tiles = seq // SEQ_TILE

    const2 = lambda b, t: (0, 0)
    const3 = lambda b, t: (0, 0, 0)
    pos_spec = pl.BlockSpec((SEQ_TILE, LANES), lambda b, t: (t, 0))
    smem = pl.BlockSpec(memory_space=pltpu.SMEM)

    in_specs = [
        pl.BlockSpec((1, SEQ_TILE, D_MODEL), lambda b, t: (b, t, 0)),
        pl.BlockSpec((1, D_MODEL), const2),
        pl.BlockSpec((D_MODEL, D_IN), const2),
        pl.BlockSpec((D_MIX, D_MODEL), const2),
        pos_spec, pos_spec, pos_spec,
        pl.BlockSpec((2, CHUNK, 2 * CHUNK), const3),
        pl.BlockSpec((2, CHUNK, LANES), const3),
        pl.BlockSpec((2, CHUNK, LANES), const3),
        pl.BlockSpec((2, 1, 2 * RET_DV), const3),
        pl.BlockSpec((1, RET_HEADS * RET_DV), const2),
        pl.BlockSpec((1, SWA_HEADS * SWA_HEAD_DIM), const2),
        pl.BlockSpec((1, LANES), const2),
        pl.BlockSpec((2 * LANES, 2 * LANES), const2),
        pl.BlockSpec((CHUNK, CHUNK), const2),
        smem, smem,
    ]
    scratch = [
        pltpu.VMEM((SEQ_TILE, D_IN), F32),
        pltpu.VMEM((SEQ_TILE, D_MIX), BF16),
        pltpu.VMEM((2, 2 * RET_DK, 2 * RET_DV), F32),
        pltpu.VMEM((SWA_KV_HEADS, SEQ_TILE + CHUNK, LANES), BF16),
        pltpu.VMEM((SWA_KV_HEADS, 2, SEQ_TILE + CHUNK, LANES), BF16),
        pltpu.VMEM((SWA_HEADS, CHUNK, CHUNK), F32),
    ]
    return pl.pallas_call(
        _layer_kernel,
        grid=(batch, n_tiles),
        in_specs=in_specs,
        out_specs=pl.BlockSpec((1, SEQ_TILE, D_MODEL), lambda b, t: (b, t, 0)),
        out_shape=jax.ShapeDtypeStruct(x.shape, x.dtype),
        scratch_shapes=scratch,
        compiler_params=pltpu.CompilerParams(
            dimension_semantics=("arbitrary", "arbitrary"),
            vmem_limit_bytes=VMEM_LIMIT_BYTES),
    )(x, norm_w.reshape(1, D_MODEL), w_in.astype(BF16), w_out.astype(BF16),
      tb["cos"], tb["sin_a"], tb["sin_b"], tb["dec"], tb["xi"], tb["zeta"], tb["cd"],
      ret_norm_w.reshape(1, -1), jnp.tile(q_norm_w, SWA_HEADS).reshape(1, -1),
      jnp.tile(k_norm_w, SWA_KV_HEADS).reshape(1, -1), tb["seg"], tb["bucket"],
      rel_bias.astype(F32), sinks.astype(F32))
```

```python
import math

import numpy as np
import jax
import jax.numpy as jnp
from jax import lax
from jax.experimental import pallas as pl
from jax.experimental.pallas import tpu as pltpu

D_MODEL = 1024
RET_HEADS = 4
RET_DK = 64
RET_DV = 128
CHUNK = 128
RET_ROPE_BASE = 10000.0
SWA_HEADS = 8
SWA_KV_HEADS = 2
SWA_HEAD_DIM = 64
SWA_GROUP = SWA_HEADS // SWA_KV_HEADS
NUM_BUCKETS = 32
MAX_DISTANCE = 128
NORM_EPS = 1e-6
GN_EPS = 1e-5
NEG_INF = -1e30

OFF_RQ, OFF_RK, OFF_RV, OFF_RG = 0, 256, 512, 1024
OFF_SQ, OFF_SK, OFF_SG = 1536, 2048, 2304
D_IN = 2816
D_MIX = 1024

LANES = 128
SEQ_TILE = 512
N_CHUNKS = SEQ_TILE // CHUNK
ROW_BLOCK = 32
VMEM_LIMIT_BYTES = 56 * 1024 * 1024

BF16 = jnp.bfloat16
F32 = jnp.float32


def _t5_bucket_np(n):
    max_exact = NUM_BUCKETS // 2
    nf = np.maximum(n, 1).astype(np.float64)
    large = max_exact + (np.log(nf / max_exact) / math.log(MAX_DISTANCE / max_exact)
                         * (NUM_BUCKETS - max_exact)).astype(np.int32)
    large = np.minimum(large, NUM_BUCKETS - 1)
    return np.where(n < max_exact, n, large).astype(np.int32)


def _constant_tables(seq):
    half = RET_DK // 2
    inv_freq = RET_ROPE_BASE ** (-np.arange(half, dtype=np.float64) / half)
    ang = np.arange(seq, dtype=np.float64)[:, None] * inv_freq[None, :]
    cos, sin = np.cos(ang), np.sin(ang)
    zeros = np.zeros_like(sin)
    cos_t = np.tile(np.concatenate([cos, cos], axis=1), (1, 2))
    sin_a = np.tile(np.concatenate([-sin, zeros], axis=1), (1, 2))
    sin_b = np.tile(np.concatenate([zeros, sin], axis=1), (1, 2))

    gamma = 1.0 - np.exp2(-5.0 - np.arange(RET_HEADS, dtype=np.float64))
    log_g = np.log(gamma)
    i = np.arange(CHUNK, dtype=np.float64)
    diff = i[:, None] - i[None, :]
    k_scale = RET_DK ** -0.5
    decay = np.where(diff >= 0, np.exp(log_g[:, None, None] * np.maximum(diff, 0.0)), 0.0) * k_scale
    dec = np.stack([np.concatenate([decay[2 * p], decay[2 * p + 1]], axis=1) for p in range(2)])
    xi = np.exp(log_g[:, None] * (i + 1.0))
    zeta = np.exp(log_g[:, None] * (CHUNK - 1.0 - i)) * k_scale

    def per_lane(t):
        return np.stack([np.concatenate([np.repeat(t[2 * p][:, None], RET_DK, 1),
                                         np.repeat(t[2 * p + 1][:, None], RET_DK, 1)], axis=1)
                         for p in range(2)])

    lo = (np.arange(LANES) < RET_DK)[None, None, :]
    xi_l = per_lane(xi)
    xi_m = np.stack([np.where(lo, xi_l, 0.0), np.where(lo, 0.0, xi_l)], axis=1)

    chunk_decay = np.exp(log_g * CHUNK)
    cd = np.stack([np.concatenate([np.full((1, RET_DV), chunk_decay[2 * p]),
                                   np.full((1, RET_DV), chunk_decay[2 * p + 1])], axis=1)
                   for p in range(2)])

    r = np.arange(CHUNK)[:, None]
    c = np.arange(CHUNK)[None, :]
    dist = np.where(c <= r, r - c, r + CHUNK - c)
    bucket = _t5_bucket_np(dist)

    seg = np.kron(np.eye(4), np.full((SWA_HEAD_DIM, SWA_HEAD_DIM), 1.0 / SWA_HEAD_DIM))

    f = lambda a: jnp.asarray(a, dtype=F32)
    return dict(cos=f(cos_t), sin_a=f(sin_a), sin_b=f(sin_b), dec=f(dec), xi=f(xi_m),
                zeta=f(per_lane(zeta)), cd=f(cd), bucket=jnp.asarray(bucket, dtype=jnp.int32),
                seg=jnp.asarray(seg, dtype=BF16))


def _dot(a, b):
    return jnp.dot(a, b, preferred_element_type=F32)


def _dot_nt(a, b):
    return lax.dot_general(a, b, (((1,), (1,)), ((), ())), preferred_element_type=F32)


def _dot_tn(a, b):
    return lax.dot_general(a, b, (((0,), (0,)), ((), ())), preferred_element_type=F32)


def _silu(g):
    hg = 0.5 * g
    return hg + hg * jnp.tanh(hg)


def _row_blocks(n_rows):
    return [(r0, slice(r0, r0 + ROW_BLOCK)) for r0 in range(0, n_rows, ROW_BLOCK)]


def _layer_kernel(x_ref, nw_ref, win_ref, wout_ref, cos_ref, sina_ref, sinb_ref, dec_ref, xi_ref,
                  zeta_ref, cd_ref, retw_ref, qnw_ref, knw_ref, seg_ref, bkt_ref, relb_ref, sinks_ref,
                  o_ref,
                  h_s, rq_s, rqx_s, rk_s, rkz_s, rv_s, gate_s, sq_s, kband_s, vband_s, pcat_s,
                  mixed_s, state_s, bias_s):
    b = pl.program_id(0)
    t = pl.program_id(1)

    @pl.when((b == 0) & (t == 0))
    def _build_bias():
        bk = bkt_ref[...]
        row = lax.broadcasted_iota(jnp.int32, (CHUNK, CHUNK), 0)
        col = lax.broadcasted_iota(jnp.int32, (CHUNK, CHUNK), 1)
        for h in range(SWA_HEADS):
            acc = jnp.zeros((CHUNK, CHUNK), F32)
            for u in range(NUM_BUCKETS):
                acc = jnp.where(bk == u, relb_ref[u, h], acc)
            bias_s[h] = acc
            bias_s[SWA_HEADS + h] = jnp.where(col <= row, acc, NEG_INF)

    @pl.when(t == 0)
    def _reset_carries():
        state_s[...] = jnp.zeros_like(state_s)
        kband_s[:, 0:CHUNK, :] = jnp.zeros((SWA_KV_HEADS, CHUNK, LANES), BF16)
        vband_s[:, :, 0:CHUNK, :] = jnp.zeros((SWA_KV_HEADS, 2, CHUNK, LANES), BF16)

    lo = lax.broadcasted_iota(jnp.int32, (ROW_BLOCK, LANES), 1) < RET_DK

    for r0, rs in _row_blocks(SEQ_TILE):
        xb = x_ref[0, rs, :]
        ms = jnp.mean(xb * xb, axis=-1, keepdims=True)
        h_s[rs, :] = (xb * lax.rsqrt(ms + NORM_EPS) * nw_ref[...]).astype(BF16)

    def proj(off, width=4 * LANES):
        return _dot(h_s[...], win_ref[:, off:off + width])

    def rot(v, rs):
        return (v * cos_ref[rs, :] + pltpu.roll(v, LANES - 32, 1) * sina_ref[rs, :]
                + pltpu.roll(v, 32, 1) * sinb_ref[rs, :])

    r = proj(OFF_RQ)
    for r0, rs in _row_blocks(SEQ_TILE):
        c, cr = r0 // CHUNK, r0 % CHUNK
        cs = slice(cr, cr + ROW_BLOCK)
        for p in range(2):
            q = rot(r[rs, p * LANES:(p + 1) * LANES], rs)
            rq_s[p, rs, :] = q.astype(BF16)
            rqx_s[p, 0, rs, :] = (q * xi_ref[p, 0, cs, :]).astype(BF16)
            rqx_s[p, 1, rs, :] = (q * xi_ref[p, 1, cs, :]).astype(BF16)
            k = rot(r[rs, (2 + p) * LANES:(3 + p) * LANES], rs)
            rk_s[p, c, cr:cr + ROW_BLOCK, :] = jnp.where(lo, k, 0.0).astype(BF16)
            rk_s[p, c, CHUNK + cr:CHUNK + cr + ROW_BLOCK, :] = jnp.where(lo, 0.0, k).astype(BF16)
            rkz_s[p, rs, :] = (k * zeta_ref[p, cs, :]).astype(BF16)

    r = proj(OFF_RV)
    for r0, rs in _row_blocks(SEQ_TILE):
        rv_s[rs, :] = r[rs, :].astype(BF16)

    for grp, off in enumerate((OFF_RG, OFF_SG)):
        r = proj(off)
        for r0, rs in _row_blocks(SEQ_TILE):
            gate_s[rs, grp * 4 * LANES:(grp + 1) * 4 * LANES] = _silu(r[rs, :])

    qnw = qnw_ref[...] * (SWA_HEAD_DIM ** -0.5)
    r = proj(OFF_SQ)
    ssq = jnp.concatenate(
        [_dot((r[:, hf * 2 * LANES:(hf + 1) * 2 * LANES] * r[:, hf * 2 * LANES:(hf + 1) * 2 * LANES]).astype(BF16),
              seg_ref[...]) for hf in range(2)], axis=1)
    for r0, rs in _row_blocks(SEQ_TILE):
        c, cr = r0 // CHUNK, r0 % CHUNK
        for slab in range(4):
            g, s = slab // 2, slab % 2
            cols = slice(slab * LANES, (slab + 1) * LANES)
            qn = r[rs, cols] * lax.rsqrt(ssq[rs, cols] + NORM_EPS) * qnw[:, cols]
            base = 2 * s * CHUNK + cr
            sq_s[g, c, base:base + ROW_BLOCK, :] = jnp.where(lo, qn, 0.0).astype(BF16)
            sq_s[g, c, base + CHUNK:base + CHUNK + ROW_BLOCK, :] = jnp.where(lo, 0.0, qn).astype(BF16)

    r = proj(OFF_SK, 2 * LANES)
    ssk = _dot((r[:, 0:LANES] * r[:, 0:LANES]).astype(BF16), seg_ref[0:LANES, 0:LANES])
    for r0, rs in _row_blocks(SEQ_TILE):
        nxt = slice(CHUNK + r0, CHUNK + r0 + ROW_BLOCK)
        kn = r[rs, 0:LANES] * lax.rsqrt(ssk[rs, :] + NORM_EPS) * knw_ref[...]
        kn_sw = pltpu.roll(kn, SWA_HEAD_DIM, 1)
        sv = r[rs, LANES:2 * LANES]
        sv_sw = pltpu.roll(sv, SWA_HEAD_DIM, 1)
        kband_s[0, nxt, :] = jnp.where(lo, kn, kn_sw).astype(BF16)
        kband_s[1, nxt, :] = jnp.where(lo, kn_sw, kn).astype(BF16)
        vband_s[0, 0, nxt, :] = jnp.where(lo, sv, 0.0).astype(BF16)
        vband_s[0, 1, nxt, :] = jnp.where(lo, 0.0, sv_sw).astype(BF16)
        vband_s[1, 0, nxt, :] = jnp.where(lo, sv_sw, 0.0).astype(BF16)
        vband_s[1, 1, nxt, :] = jnp.where(lo, 0.0, sv).astype(BF16)

    first_tile = (t == 0).astype(jnp.int32)
    col = lax.broadcasted_iota(jnp.int32, (ROW_BLOCK, CHUNK), 1)
    row = lax.broadcasted_iota(jnp.int32, (ROW_BLOCK, CHUNK), 0)

    for c in range(N_CHUNKS):
        rows = slice(c * CHUNK, (c + 1) * CHUNK)
        band = slice(c * CHUNK, (c + 2) * CHUNK)

        for p in range(2):
            s2 = _dot_nt(rq_s[p, rows, :], rk_s[p, c])
            s2 = jnp.concatenate([(s2[bs, :] * dec_ref[p, bs, :]).astype(BF16)
                                  for _, bs in _row_blocks(CHUNK)], axis=0)
            full = state_s[p]
            full_b = full.astype(BF16)
            v2 = rv_s[rows, p * 2 * RET_DV:(p + 1) * 2 * RET_DV]
            outs = [
                _dot(jnp.concatenate([s2[:, j * CHUNK:(j + 1) * CHUNK], rqx_s[p, j, rows, :]], axis=1),
                     jnp.concatenate([v2[:, j * RET_DV:(j + 1) * RET_DV],
                                      full_b[:, j * RET_DV:(j + 1) * RET_DV]], axis=0))
                for j in range(2)]
            state_s[p] = full * cd_ref[p] + _dot_tn(rkz_s[p, rows, :], v2)
            for j, o in enumerate(outs):
                hd = 2 * p + j
                cols = slice(hd * RET_DV, (hd + 1) * RET_DV)
                for r0, bs in _row_blocks(CHUNK):
                    ob = o[bs, :]
                    rs = slice(c * CHUNK + r0, c * CHUNK + r0 + ROW_BLOCK)
                    mu = jnp.mean(ob, axis=-1, keepdims=True)
                    d = ob - mu
                    var = jnp.mean(d * d, axis=-1, keepdims=True)
                    y = d * lax.rsqrt(var + GN_EPS) * retw_ref[:, cols]
                    mixed_s[rs, cols] = (y * gate_s[rs, cols]).astype(BF16)

        for g in range(SWA_KV_HEADS):
            l2 = _dot_nt(sq_s[g, c], kband_s[g, band, :])
            for hh in range(SWA_GROUP):
                hd = SWA_GROUP * g + hh
                sink = sinks_ref[hd]
                for r0, bs in _row_blocks(CHUNK):
                    ls = slice(hh * CHUNK + r0, hh * CHUNK + r0 + ROW_BLOCK)
                    tri = col <= row + r0
                    if c == 0:
                        bias = bias_s[hd + SWA_HEADS * first_tile, bs, :]
                    else:
                        bias = bias_s[hd, bs, :]
                    lg = jnp.where(tri, l2[ls, CHUNK:2 * CHUNK], l2[ls, 0:CHUNK]) + bias
                    m = jnp.maximum(jnp.max(lg, axis=-1, keepdims=True), sink)
                    e = jnp.exp(lg - m)
                    den = jnp.sum(e, axis=-1, keepdims=True) + jnp.exp(sink - m)
                    pr = e * (1.0 / den)
                    pcat_s[c % 2, g, ls, 0:CHUNK] = jnp.where(tri, 0.0, pr).astype(BF16)
                    pcat_s[c % 2, g, ls, CHUNK:2 * CHUNK] = jnp.where(tri, pr, 0.0).astype(BF16)
            for s in range(2):
                slab = 2 * g + s
                so = (_dot(pcat_s[c % 2, g, 2 * s * CHUNK:(2 * s + 1) * CHUNK, :], vband_s[g, 0, band, :])
                      + _dot(pcat_s[c % 2, g, (2 * s + 1) * CHUNK:(2 * s + 2) * CHUNK, :], vband_s[g, 1, band, :]))
                cols = slice(RET_HEADS * RET_DV + slab * LANES, RET_HEADS * RET_DV + (slab + 1) * LANES)
                for r0, bs in _row_blocks(CHUNK):
                    rs = slice(c * CHUNK + r0, c * CHUNK + r0 + ROW_BLOCK)
                    mixed_s[rs, cols] = (so[bs, :] * gate_s[rs, cols]).astype(BF16)

    last = slice(SEQ_TILE, SEQ_TILE + CHUNK)
    kband_s[:, 0:CHUNK, :] = kband_s[:, last, :]
    vband_s[:, :, 0:CHUNK, :] = vband_s[:, :, last, :]

    o_ref[0] = x_ref[0] + _dot(mixed_s[...], wout_ref[...])


def kernel(x, norm_w, w_in, ret_norm_w, q_norm_w, k_norm_w, sinks, rel_bias, w_out):
    batch, seq, d_model = x.shape
    assert d_model == D_MODEL and seq % SEQ_TILE == 0
    assert w_in.shape == (D_MODEL, D_IN) and w_out.shape == (D_MIX, D_MODEL)
    tb = _constant_tables(seq)
    n_tiles = seq // SEQ_TILE

    const2 = lambda b, t: (0, 0)
    const3 = lambda b, t: (0, 0, 0)
    const4 = lambda b, t: (0, 0, 0, 0)
    pos_spec = pl.BlockSpec((SEQ_TILE, LANES), lambda b, t: (t, 0))
    smem = pl.BlockSpec(memory_space=pltpu.SMEM)

    in_specs = [
        pl.BlockSpec((1, SEQ_TILE, D_MODEL), lambda b, t: (b, t, 0)),
        pl.BlockSpec((1, D_MODEL), const2),
        pl.BlockSpec((D_MODEL, D_IN), const2),
        pl.BlockSpec((D_MIX, D_MODEL), const2),
        pos_spec, pos_spec, pos_spec,
        pl.BlockSpec((2, CHUNK, 2 * CHUNK), const3),
        pl.BlockSpec((2, 2, CHUNK, LANES), const4),
        pl.BlockSpec((2, CHUNK, LANES), const3),
        pl.BlockSpec((2, 1, 2 * RET_DV), const3),
        pl.BlockSpec((1, RET_HEADS * RET_DV), const2),
        pl.BlockSpec((1, SWA_HEADS * SWA_HEAD_DIM), const2),
        pl.BlockSpec((1, LANES), const2),
        pl.BlockSpec((2 * LANES, 2 * LANES), const2),
        pl.BlockSpec((CHUNK, CHUNK), const2),
        smem, smem,
    ]
    scratch = [
        pltpu.VMEM((SEQ_TILE, D_MODEL), BF16),
        pltpu.VMEM((2, SEQ_TILE, LANES), BF16),
        pltpu.VMEM((2, 2, SEQ_TILE, LANES), BF16),
        pltpu.VMEM((2, N_CHUNKS, 2 * CHUNK, LANES), BF16),
        pltpu.VMEM((2, SEQ_TILE, LANES), BF16),
        pltpu.VMEM((SEQ_TILE, RET_HEADS * RET_DV), BF16),
        pltpu.VMEM((SEQ_TILE, D_MIX), F32),
        pltpu.VMEM((SWA_KV_HEADS, N_CHUNKS, SWA_GROUP * CHUNK, LANES), BF16),
        pltpu.VMEM((SWA_KV_HEADS, SEQ_TILE + CHUNK, LANES), BF16),
        pltpu.VMEM((SWA_KV_HEADS, 2, SEQ_TILE + CHUNK, LANES), BF16),
        pltpu.VMEM((2, SWA_KV_HEADS, SWA_GROUP * CHUNK, 2 * CHUNK), BF16),
        pltpu.VMEM((SEQ_TILE, D_MIX), BF16),
        pltpu.VMEM((2, 2 * RET_DK, 2 * RET_DV), F32),
        pltpu.VMEM((2 * SWA_HEADS, CHUNK, CHUNK), F32),
    ]
    return pl.pallas_call(
        _layer_kernel,
        grid=(batch, n_tiles),
        in_specs=in_specs,
        out_specs=pl.BlockSpec((1, SEQ_TILE, D_MODEL), lambda b, t: (b, t, 0)),
        out_shape=jax.ShapeDtypeStruct(x.shape, x.dtype),
        scratch_shapes=scratch,
        compiler_params=pltpu.CompilerParams(
            dimension_semantics=("arbitrary", "arbitrary"),
            vmem_limit_bytes=VMEM_LIMIT_BYTES),
    )(x, norm_w.reshape(1, D_MODEL), w_in.astype(BF16), w_out.astype(BF16),
      tb["cos"], tb["sin_a"], tb["sin_b"], tb["dec"], tb["xi"], tb["zeta"], tb["cd"],
      ret_norm_w.reshape(1, -1), jnp.tile(q_norm_w, SWA_HEADS).reshape(1, -1),
      jnp.tile(k_norm_w, SWA_KV_HEADS).reshape(1, -1), tb["seg"], tb["bucket"],
      rel_bias.astype(F32), sinks.astype(F32))
```

```python
import functools
import math

import numpy as np
import jax
import jax.numpy as jnp
from jax import lax
from jax.experimental import pallas as pl
from jax.experimental.pallas import tpu as pltpu

D_MODEL = 1024
RET_HEADS = 4
RET_DK = 64
RET_DV = 128
CHUNK = 128
RET_ROPE_BASE = 10000.0
SWA_HEADS = 8
SWA_KV_HEADS = 2
SWA_HEAD_DIM = 64
SWA_GROUP = SWA_HEADS // SWA_KV_HEADS
NUM_BUCKETS = 32
MAX_DISTANCE = 128
NORM_EPS = 1e-6
GN_EPS = 1e-5
NEG_INF = -1e30

OFF_RQ, OFF_RK, OFF_RV, OFF_RG = 0, 256, 512, 1024
OFF_SQ, OFF_SK, OFF_SG = 1536, 2048, 2304
D_IN = 2816
D_MIX = 1024

LANES = 128
SEQ_TILE = 512
N_CHUNKS = SEQ_TILE // CHUNK
ROW_BLOCK = 32
VMEM_LIMIT_BYTES = 56 * 1024 * 1024
MIX_ORDER = ((0, 1, 1, 1, 1) + (0, 1, 1, 0, 1, 1) + (0, 1, 1, 1, 0, 1, 1) + (0, 1, 1, 0, 1, 1, 1))
N_SLOT_BUFS = 9

BF16 = jnp.bfloat16
F32 = jnp.float32


def _t5_bucket_np(n):
    max_exact = NUM_BUCKETS // 2
    nf = np.maximum(n, 1).astype(np.float64)
    large = max_exact + (np.log(nf / max_exact) / math.log(MAX_DISTANCE / max_exact)
                         * (NUM_BUCKETS - max_exact)).astype(np.int32)
    large = np.minimum(large, NUM_BUCKETS - 1)
    return np.where(n < max_exact, n, large).astype(np.int32)


def _constant_tables(seq):
    half = RET_DK // 2
    inv_freq = RET_ROPE_BASE ** (-np.arange(half, dtype=np.float64) / half)
    ang = np.arange(seq, dtype=np.float64)[:, None] * inv_freq[None, :]
    cos, sin = np.cos(ang), np.sin(ang)
    zeros = np.zeros_like(sin)
    cos_t = np.tile(np.concatenate([cos, cos], axis=1), (1, 2))
    sin_a = np.tile(np.concatenate([-sin, zeros], axis=1), (1, 2))
    sin_b = np.tile(np.concatenate([zeros, sin], axis=1), (1, 2))

    gamma = 1.0 - np.exp2(-5.0 - np.arange(RET_HEADS, dtype=np.float64))
    log_g = np.log(gamma)
    i = np.arange(CHUNK, dtype=np.float64)
    diff = i[:, None] - i[None, :]
    k_scale = RET_DK ** -0.5
    decay = np.where(diff >= 0, np.exp(log_g[:, None, None] * np.maximum(diff, 0.0)), 0.0) * k_scale
    dec = np.stack([np.concatenate([decay[2 * p], decay[2 * p + 1]], axis=1) for p in range(2)])
    xi = np.exp(log_g[:, None] * (i + 1.0))
    zeta = np.exp(log_g[:, None] * (CHUNK - 1.0 - i)) * k_scale

    def per_lane(t):
        return np.stack([np.concatenate([np.repeat(t[2 * p][:, None], RET_DK, 1),
                                         np.repeat(t[2 * p + 1][:, None], RET_DK, 1)], axis=1)
                         for p in range(2)])

    lo = (np.arange(LANES) < RET_DK)[None, None, :]
    xi_l = per_lane(xi)
    xi_m = np.stack([np.where(lo, xi_l, 0.0), np.where(lo, 0.0, xi_l)], axis=1)

    chunk_decay = np.exp(log_g * CHUNK)
    cd = np.stack([np.concatenate([np.full((1, RET_DV), chunk_decay[2 * p]),
                                   np.full((1, RET_DV), chunk_decay[2 * p + 1])], axis=1)
                   for p in range(2)])

    r = np.arange(CHUNK)[:, None]
    c = np.arange(CHUNK)[None, :]
    dist = np.where(c <= r, r - c, r + CHUNK - c)
    bucket = _t5_bucket_np(dist)

    seg = np.kron(np.eye(4), np.full((SWA_HEAD_DIM, SWA_HEAD_DIM), 1.0 / SWA_HEAD_DIM))

    f = lambda a: jnp.asarray(a, dtype=F32)
    return dict(cos=f(cos_t), sin_a=f(sin_a), sin_b=f(sin_b), dec=f(dec), xi=f(xi_m),
                zeta=f(per_lane(zeta)), cd=f(cd), bucket=jnp.asarray(bucket, dtype=jnp.int32),
                seg=jnp.asarray(seg, dtype=BF16))


def _dot(a, b):
    return jnp.dot(a, b, preferred_element_type=F32)


def _dot_nt(a, b):
    return lax.dot_general(a, b, (((1,), (1,)), ((), ())), preferred_element_type=F32)


def _dot_tn(a, b):
    return lax.dot_general(a, b, (((0,), (0,)), ((), ())), preferred_element_type=F32)


def _silu(g):
    hg = 0.5 * g
    return hg + hg * jnp.tanh(hg)


def _row_blocks(n_rows):
    return [(r0, slice(r0, r0 + ROW_BLOCK)) for r0 in range(0, n_rows, ROW_BLOCK)]


def _layer_kernel(tiles_per_seq,
                  xin_ref, xres_ref, nw_ref, win_ref, wout_ref, cos_ref, sina_ref, sinb_ref, dec_ref,
                  xi_ref, zeta_ref, cd_ref, retw_ref, qnw_ref, knw_ref, seg_ref, bkt_ref, relb_ref,
                  sinks_ref, o_ref, h_s, *scratch):
    slots = (scratch[0:N_SLOT_BUFS], scratch[N_SLOT_BUFS:2 * N_SLOT_BUFS])
    pcat_s, mixed_s, state_s, bias_s = scratch[2 * N_SLOT_BUFS:]
    step = pl.program_id(0)
    mix_tile = jnp.maximum(step - 1, 0)
    first = (mix_tile % tiles_per_seq == 0) | (step == 0)
    first_i = first.astype(jnp.int32)
    keep_state = jnp.where(first, 0.0, 1.0).astype(F32)

    @pl.when(step == 0)
    def _init():
        bk = bkt_ref[...]
        row = lax.broadcasted_iota(jnp.int32, (CHUNK, CHUNK), 0)
        col = lax.broadcasted_iota(jnp.int32, (CHUNK, CHUNK), 1)
        for h in range(SWA_HEADS):
            acc = jnp.zeros((CHUNK, CHUNK), F32)
            for u in range(NUM_BUCKETS):
                acc = jnp.where(bk == u, relb_ref[u, h], acc)
            bias_s[h] = acc
            bias_s[SWA_HEADS + h] = jnp.where(col <= row, acc, NEG_INF)
        state_s[...] = jnp.zeros_like(state_s)
        for ref in slots[1]:
            ref[...] = jnp.zeros(ref.shape, ref.dtype)

    lo = lax.broadcasted_iota(jnp.int32, (ROW_BLOCK, LANES), 1) < RET_DK

    def project(slot):
        rq_s, rqx_s, rk_s, rkz_s, rv_s, gate_s, sq_s, kband_s, vband_s = slot
        for r0, rs in _row_blocks(SEQ_TILE):
            xb = xin_ref[0, rs, :]
            ms = jnp.mean(xb * xb, axis=-1, keepdims=True)
            h_s[rs, :] = (xb * lax.rsqrt(ms + NORM_EPS) * nw_ref[...]).astype(BF16)
        yield

        def proj(off, width=4 * LANES):
            return _dot(h_s[...], win_ref[:, off:off + width])

        def rot(v, rs):
            return (v * cos_ref[rs, :] + pltpu.roll(v, LANES - 32, 1) * sina_ref[rs, :]
                    + pltpu.roll(v, 32, 1) * sinb_ref[rs, :])

        r = proj(OFF_RQ)
        for r0, rs in _row_blocks(SEQ_TILE):
            c, cr = r0 // CHUNK, r0 % CHUNK
            cs = slice(cr, cr + ROW_BLOCK)
            for p in range(2):
                q = rot(r[rs, p * LANES:(p + 1) * LANES], rs)
                rq_s[p, rs, :] = q.astype(BF16)
                rqx_s[p, 0, rs, :] = (q * xi_ref[p, 0, cs, :]).astype(BF16)
                rqx_s[p, 1, rs, :] = (q * xi_ref[p, 1, cs, :]).astype(BF16)
                k = rot(r[rs, (2 + p) * LANES:(3 + p) * LANES], rs)
                rk_s[p, c, cr:cr + ROW_BLOCK, :] = jnp.where(lo, k, 0.0).astype(BF16)
                rk_s[p, c, CHUNK + cr:CHUNK + cr + ROW_BLOCK, :] = jnp.where(lo, 0.0, k).astype(BF16)
                rkz_s[p, rs, :] = (k * zeta_ref[p, cs, :]).astype(BF16)
        yield

        r = proj(OFF_RV)
        for r0, rs in _row_blocks(SEQ_TILE):
            rv_s[rs, :] = r[rs, :].astype(BF16)
        yield

        for grp, off in enumerate((OFF_RG, OFF_SG)):
            r = proj(off)
            for r0, rs in _row_blocks(SEQ_TILE):
                gate_s[rs, grp * 4 * LANES:(grp + 1) * 4 * LANES] = _silu(r[rs, :])
            yield

        qnw = qnw_ref[...] * (SWA_HEAD_DIM ** -0.5)
        r = proj(OFF_SQ)
        ssq = jnp.concatenate(
            [_dot((r[:, hf * 2 * LANES:(hf + 1) * 2 * LANES]
                   * r[:, hf * 2 * LANES:(hf + 1) * 2 * LANES]).astype(BF16), seg_ref[...])
             for hf in range(2)], axis=1)
        for r0, rs in _row_blocks(SEQ_TILE):
            c, cr = r0 // CHUNK, r0 % CHUNK
            for slab in range(4):
                g, s = slab // 2, slab % 2
                cols = slice(slab * LANES, (slab + 1) * LANES)
                qn = r[rs, cols] * lax.rsqrt(ssq[rs, cols] + NORM_EPS) * qnw[:, cols]
                base = 2 * s * CHUNK + cr
                sq_s[g, c, base:base + ROW_BLOCK, :] = jnp.where(lo, qn, 0.0).astype(BF16)
                sq_s[g, c, base + CHUNK:base + CHUNK + ROW_BLOCK, :] = (
                    jnp.where(lo, 0.0, qn).astype(BF16))
        yield

        r = proj(OFF_SK, 2 * LANES)
        ssk = _dot((r[:, 0:LANES] * r[:, 0:LANES]).astype(BF16), seg_ref[0:LANES, 0:LANES])
        for r0, rs in _row_blocks(SEQ_TILE):
            nxt = slice(CHUNK + r0, CHUNK + r0 + ROW_BLOCK)
            kn = r[rs, 0:LANES] * lax.rsqrt(ssk[rs, :] + NORM_EPS) * knw_ref[...]
            kn_sw = pltpu.roll(kn, SWA_HEAD_DIM, 1)
            sv = r[rs, LANES:2 * LANES]
            sv_sw = pltpu.roll(sv, SWA_HEAD_DIM, 1)
            kband_s[0, nxt, :] = jnp.where(lo, kn, kn_sw).astype(BF16)
            kband_s[1, nxt, :] = jnp.where(lo, kn_sw, kn).astype(BF16)
            vband_s[0, 0, nxt, :] = jnp.where(lo, sv, 0.0).astype(BF16)
            vband_s[0, 1, nxt, :] = jnp.where(lo, 0.0, sv_sw).astype(BF16)
            vband_s[1, 0, nxt, :] = jnp.where(lo, sv_sw, 0.0).astype(BF16)
            vband_s[1, 1, nxt, :] = jnp.where(lo, 0.0, sv).astype(BF16)
        yield

    def mix(slot, next_slot):
        rq_s, rqx_s, rk_s, rkz_s, rv_s, gate_s, sq_s, kband_s, vband_s = slot
        col = lax.broadcasted_iota(jnp.int32, (ROW_BLOCK, CHUNK), 1)
        row = lax.broadcasted_iota(jnp.int32, (ROW_BLOCK, CHUNK), 0)
        for c in range(N_CHUNKS):
            rows = slice(c * CHUNK, (c + 1) * CHUNK)
            band = slice(c * CHUNK, (c + 2) * CHUNK)

            for p in range(2):
                s2 = _dot_nt(rq_s[p, rows, :], rk_s[p, c])
                s2 = jnp.concatenate([(s2[bs, :] * dec_ref[p, bs, :]).astype(BF16)
                                      for _, bs in _row_blocks(CHUNK)], axis=0)
                full = state_s[p]
                if c == 0:
                    full = full * keep_state
                full_b = full.astype(BF16)
                v2 = rv_s[rows, p * 2 * RET_DV:(p + 1) * 2 * RET_DV]
                outs = [
                    _dot(jnp.concatenate([s2[:, j * CHUNK:(j + 1) * CHUNK], rqx_s[p, j, rows, :]], axis=1),
                         jnp.concatenate([v2[:, j * RET_DV:(j + 1) * RET_DV],
                                          full_b[:, j * RET_DV:(j + 1) * RET_DV]], axis=0))
                    for j in range(2)]
                state_s[p] = full * cd_ref[p] + _dot_tn(rkz_s[p, rows, :], v2)
                for j, o in enumerate(outs):
                    hd = 2 * p + j
                    cols = slice(hd * RET_DV, (hd + 1) * RET_DV)
                    for r0, bs in _row_blocks(CHUNK):
                        ob = o[bs, :]
                        rs = slice(c * CHUNK + r0, c * CHUNK + r0 + ROW_BLOCK)
                        mu = jnp.mean(ob, axis=-1, keepdims=True)
                        d = ob - mu
                        var = jnp.mean(d * d, axis=-1, keepdims=True)
                        y = d * lax.rsqrt(var + GN_EPS) * retw_ref[:, cols]
                        mixed_s[rs, cols] = (y * gate_s[rs, cols]).astype(BF16)
                yield

            for g in range(SWA_KV_HEADS):
                l2 = _dot_nt(sq_s[g, c], kband_s[g, band, :])
                for hh in range(SWA_GROUP):
                    hd = SWA_GROUP * g + hh
                    sink = sinks_ref[hd]
                    for r0, bs in _row_blocks(CHUNK):
                        ls = slice(hh * CHUNK + r0, hh * CHUNK + r0 + ROW_BLOCK)
                        tri = col <= row + r0
                        if c == 0:
                            bias = bias_s[hd + SWA_HEADS * first_i, bs, :]
                        else:
                            bias = bias_s[hd, bs, :]
                        lg = jnp.where(tri, l2[ls, CHUNK:2 * CHUNK], l2[ls, 0:CHUNK]) + bias
                        m = jnp.maximum(jnp.max(lg, axis=-1, keepdims=True), sink)
                        e = jnp.exp(lg - m)
                        den = jnp.sum(e, axis=-1, keepdims=True) + jnp.exp(sink - m)
                        pr = e * (1.0 / den)
                        pcat_s[c % 2, g, ls, 0:CHUNK] = jnp.where(tri, 0.0, pr).astype(BF16)
                        pcat_s[c % 2, g, ls, CHUNK:2 * CHUNK] = jnp.where(tri, pr, 0.0).astype(BF16)
                for s in range(2):
                    slab = 2 * g + s
                    so = (_dot(pcat_s[c % 2, g, 2 * s * CHUNK:(2 * s + 1) * CHUNK, :],
                               vband_s[g, 0, band, :])
                          + _dot(pcat_s[c % 2, g, (2 * s + 1) * CHUNK:(2 * s + 2) * CHUNK, :],
                                 vband_s[g, 1, band, :]))
                    cols = slice(RET_HEADS * RET_DV + slab * LANES, RET_HEADS * RET_DV + (slab + 1) * LANES)
                    for r0, bs in _row_blocks(CHUNK):
                        rs = slice(c * CHUNK + r0, c * CHUNK + r0 + ROW_BLOCK)
                        mixed_s[rs, cols] = (so[bs, :] * gate_s[rs, cols]).astype(BF16)
                yield

            if c % 2 == 1:
                pair = slice((c - 1) * CHUNK, (c + 1) * CHUNK)
                o_ref[0, pair, :] = xres_ref[0, pair, :] + _dot(mixed_s[pair, :], wout_ref[...])
                yield

        last = slice(SEQ_TILE, SEQ_TILE + CHUNK)
        next_slot[7][:, 0:CHUNK, :] = kband_s[:, last, :]
        next_slot[8][:, :, 0:CHUNK, :] = vband_s[:, :, last, :]

    def run(parity):
        streams = [project(slots[parity]), mix(slots[1 - parity], slots[parity])]
        for which in MIX_ORDER + (0,) * len(MIX_ORDER) + (1,) * len(MIX_ORDER):
            next(streams[which], None)

    for parity in range(2):
        pl.when(step % 2 == parity)(functools.partial(run, parity))


def kernel(x, norm_w, w_in, ret_norm_w, q_norm_w, k_norm_w, sinks, rel_bias, w_out):
    batch, seq, d_model = x.shape
    assert d_model == D_MODEL and seq % SEQ_TILE == 0
    assert w_in.shape == (D_MODEL, D_IN) and w_out.shape == (D_MIX, D_MODEL)
    tb = _constant_tables(seq)
    tiles_per_seq = seq // SEQ_TILE
    n_tiles = batch * tiles_per_seq

    def proj_tile(s):
        return jnp.minimum(s, n_tiles - 1)

    def mix_tile(s):
        return jnp.maximum(s - 1, 0)

    const2 = lambda s: (0, 0)
    const3 = lambda s: (0, 0, 0)
    const4 = lambda s: (0, 0, 0, 0)
    once = pl.Buffered(1)
    pos_spec = pl.BlockSpec((SEQ_TILE, LANES), lambda s: (proj_tile(s) % tiles_per_seq, 0))
    smem = pl.BlockSpec(memory_space=pltpu.SMEM)
    x_block = (1, SEQ_TILE, D_MODEL)
    mix_map = lambda s: (mix_tile(s) // tiles_per_seq, mix_tile(s) % tiles_per_seq, 0)

    in_specs = [
        pl.BlockSpec(x_block, lambda s: (proj_tile(s) // tiles_per_seq, proj_tile(s) % tiles_per_seq, 0)),
        pl.BlockSpec(x_block, mix_map),
        pl.BlockSpec((1, D_MODEL), const2),
        pl.BlockSpec((D_MODEL, D_IN), const2, pipeline_mode=once),
        pl.BlockSpec((D_MIX, D_MODEL), const2, pipeline_mode=once),
        pos_spec, pos_spec, pos_spec,
        pl.BlockSpec((2, CHUNK, 2 * CHUNK), const3),
        pl.BlockSpec((2, 2, CHUNK, LANES), const4),
        pl.BlockSpec((2, CHUNK, LANES), const3),
        pl.BlockSpec((2, 1, 2 * RET_DV), const3),
        pl.BlockSpec((1, RET_HEADS * RET_DV), const2),
        pl.BlockSpec((1, SWA_HEADS * SWA_HEAD_DIM), const2),
        pl.BlockSpec((1, LANES), const2),
        pl.BlockSpec((2 * LANES, 2 * LANES), const2),
        pl.BlockSpec((CHUNK, CHUNK), const2),
        smem, smem,
    ]
    slot_bufs = [
        pltpu.VMEM((2, SEQ_TILE, LANES), BF16),
        pltpu.VMEM((2, 2, SEQ_TILE, LANES), BF16),
        pltpu.VMEM((2, N_CHUNKS, 2 * CHUNK, LANES), BF16),
        pltpu.VMEM((2, SEQ_TILE, LANES), BF16),
        pltpu.VMEM((SEQ_TILE, RET_HEADS * RET_DV), BF16),
        pltpu.VMEM((SEQ_TILE, D_MIX), F32),
        pltpu.VMEM((SWA_KV_HEADS, N_CHUNKS, SWA_GROUP * CHUNK, LANES), BF16),
        pltpu.VMEM((SWA_KV_HEADS, SEQ_TILE + CHUNK, LANES), BF16),
        pltpu.VMEM((SWA_KV_HEADS, 2, SEQ_TILE + CHUNK, LANES), BF16),
    ]
    assert len(slot_bufs) == N_SLOT_BUFS
    scratch = [pltpu.VMEM((SEQ_TILE, D_MODEL), BF16)] + slot_bufs + slot_bufs + [
        pltpu.VMEM((2, SWA_KV_HEADS, SWA_GROUP * CHUNK, 2 * CHUNK), BF16),
        pltpu.VMEM((SEQ_TILE, D_MIX), BF16),
        pltpu.VMEM((2, 2 * RET_DK, 2 * RET_DV), F32),
        pltpu.VMEM((2 * SWA_HEADS, CHUNK, CHUNK), F32),
    ]
    return pl.pallas_call(
        functools.partial(_layer_kernel, tiles_per_seq),
        grid=(n_tiles + 1,),
        in_specs=in_specs,
        out_specs=pl.BlockSpec(x_block, mix_map),
        out_shape=jax.ShapeDtypeStruct(x.shape, x.dtype),
        scratch_shapes=scratch,
        compiler_params=pltpu.CompilerParams(
            dimension_semantics=("arbitrary",),
            vmem_limit_bytes=VMEM_LIMIT_BYTES),
    )(x, x, norm_w.reshape(1, D_MODEL), w_in.astype(BF16), w_out.astype(BF16),
      tb["cos"], tb["sin_a"], tb["sin_b"], tb["dec"], tb["xi"], tb["zeta"], tb["cd"],
      ret_norm_w.reshape(1, -1), jnp.tile(q_norm_w, SWA_HEADS).reshape(1, -1),
      jnp.tile(k_norm_w, SWA_KV_HEADS).reshape(1, -1), tb["seg"], tb["bucket"],
      rel_bias.astype(F32), sinks.astype(F32))
```

```python
import functools
import math

import numpy as np
import jax
import jax.numpy as jnp
from jax import lax
from jax.experimental import pallas as pl
from jax.experimental.pallas import tpu as pltpu

D_MODEL = 1024
RET_HEADS = 4
RET_DK = 64
RET_DV = 128
CHUNK = 128
RET_ROPE_BASE = 10000.0
SWA_HEADS = 8
SWA_KV_HEADS = 2
SWA_HEAD_DIM = 64
SWA_GROUP = SWA_HEADS // SWA_KV_HEADS
NUM_BUCKETS = 32
MAX_DISTANCE = 128
NORM_EPS = 1e-6
GN_EPS = 1e-5
NEG_INF = -1e30

OFF_RQ, OFF_RK, OFF_RV, OFF_RG = 0, 256, 512, 1024
D_TOK = 1536
FT_Q, FT_K, FT_G = 0, 512, 768
D_FT = 1280
D_IN = D_TOK + D_FT
D_MIX = 1024

LANES = 128
SEQ_TILE = 512
N_CHUNKS = SEQ_TILE // CHUNK
ROW_BLOCK = 32
VMEM_LIMIT_BYTES = 56 * 1024 * 1024
MIX_ORDER = ((0, 1, 1, 1, 1, 1) + (0, 1, 1, 0, 1, 1, 1) + (0, 1, 1, 1, 0, 1, 1, 1) + (0, 1, 1, 0, 1, 1, 1, 1))
N_SLOT_BUFS = 10
SLOT_KBAND, SLOT_VT = 7, 8

BF16 = jnp.bfloat16
F32 = jnp.float32


def _t5_bucket_np(n):
    max_exact = NUM_BUCKETS // 2
    nf = np.maximum(n, 1).astype(np.float64)
    large = max_exact + (np.log(nf / max_exact) / math.log(MAX_DISTANCE / max_exact)
                         * (NUM_BUCKETS - max_exact)).astype(np.int32)
    large = np.minimum(large, NUM_BUCKETS - 1)
    return np.where(n < max_exact, n, large).astype(np.int32)


def _constant_tables(seq):
    half = RET_DK // 2
    inv_freq = RET_ROPE_BASE ** (-np.arange(half, dtype=np.float64) / half)
    ang = np.arange(seq, dtype=np.float64)[:, None] * inv_freq[None, :]
    cos, sin = np.cos(ang), np.sin(ang)
    zeros = np.zeros_like(sin)
    cos_t = np.tile(np.concatenate([cos, cos], axis=1), (1, 2))
    sin_a = np.tile(np.concatenate([-sin, zeros], axis=1), (1, 2))
    sin_b = np.tile(np.concatenate([zeros, sin], axis=1), (1, 2))

    gamma = 1.0 - np.exp2(-5.0 - np.arange(RET_HEADS, dtype=np.float64))
    log_g = np.log(gamma)
    i = np.arange(CHUNK, dtype=np.float64)
    diff = i[:, None] - i[None, :]
    k_scale = RET_DK ** -0.5
    decay = np.where(diff >= 0, np.exp(log_g[:, None, None] * np.maximum(diff, 0.0)), 0.0) * k_scale
    dec = np.stack([np.concatenate([decay[2 * p], decay[2 * p + 1]], axis=1) for p in range(2)])
    xi = np.exp(log_g[:, None] * (i + 1.0))
    zeta = np.exp(log_g[:, None] * (CHUNK - 1.0 - i)) * k_scale

    def per_lane(t):
        return np.stack([np.concatenate([np.repeat(t[2 * p][:, None], RET_DK, 1),
                                         np.repeat(t[2 * p + 1][:, None], RET_DK, 1)], axis=1)
                         for p in range(2)])

    lo = (np.arange(LANES) < RET_DK)[None, None, :]
    xi_l = per_lane(xi)
    xi_m = np.stack([np.where(lo, xi_l, 0.0), np.where(lo, 0.0, xi_l)], axis=1)

    chunk_decay = np.exp(log_g * CHUNK)
    cd = np.stack([np.concatenate([np.full((1, RET_DV), chunk_decay[2 * p]),
                                   np.full((1, RET_DV), chunk_decay[2 * p + 1])], axis=1)
                   for p in range(2)])

    j = np.arange(CHUNK)[:, None]
    q = np.arange(CHUNK)[None, :]
    dist = np.where(j <= q, q - j, q + CHUNK - j)
    bucket = _t5_bucket_np(dist)

    f = lambda a: jnp.asarray(a, dtype=F32)
    return dict(cos=f(cos_t), sin_a=f(sin_a), sin_b=f(sin_b), dec=f(dec), xi=f(xi_m),
                zeta=f(per_lane(zeta)), cd=f(cd), bucket=jnp.asarray(bucket, dtype=jnp.int32))


def _dot(a, b):
    return jnp.dot(a, b, preferred_element_type=F32)


def _dot_nt(a, b):
    return lax.dot_general(a, b, (((1,), (1,)), ((), ())), preferred_element_type=F32)


def _dot_tn(a, b):
    return lax.dot_general(a, b, (((0,), (0,)), ((), ())), preferred_element_type=F32)


def _silu(g):
    hg = 0.5 * g
    return hg + hg * jnp.tanh(hg)


def _row_blocks(n_rows):
    return [(r0, slice(r0, r0 + ROW_BLOCK)) for r0 in range(0, n_rows, ROW_BLOCK)]


def _layer_kernel(tiles_per_seq,
                  xin_ref, xres_ref, nw_ref, win_ref, wft_ref, wout_ref, cos_ref, sina_ref, sinb_ref, dec_ref,
                  xi_ref, zeta_ref, cd_ref, retw_ref, qnw_ref, knw_ref, bkt_ref, relb_ref,
                  sinks_ref, o_ref, h_s, *scratch):
    slots = (scratch[0:N_SLOT_BUFS], scratch[N_SLOT_BUFS:2 * N_SLOT_BUFS])
    pt_s, mixed_s, mixedt_s, state_s, bias_s = scratch[2 * N_SLOT_BUFS:]
    step = pl.program_id(0)
    mix_tile = jnp.maximum(step - 1, 0)
    first = (mix_tile % tiles_per_seq == 0) | (step == 0)
    first_i = first.astype(jnp.int32)
    keep_state = jnp.where(first, 0.0, 1.0).astype(F32)

    @pl.when(step == 0)
    def _init():
        bk = bkt_ref[...]
        key = lax.broadcasted_iota(jnp.int32, (CHUNK, CHUNK), 0)
        qry = lax.broadcasted_iota(jnp.int32, (CHUNK, CHUNK), 1)
        for h in range(SWA_HEADS):
            acc = jnp.zeros((CHUNK, CHUNK), F32)
            for u in range(NUM_BUCKETS):
                acc = jnp.where(bk == u, relb_ref[u, h], acc)
            bias_s[h] = acc
            bias_s[SWA_HEADS + h] = jnp.where(key <= qry, acc, NEG_INF)
        state_s[...] = jnp.zeros_like(state_s)
        for ref in slots[1]:
            ref[...] = jnp.zeros(ref.shape, ref.dtype)

    lo = lax.broadcasted_iota(jnp.int32, (ROW_BLOCK, LANES), 1) < RET_DK

    def project(slot):
        rq_s, rqx_s, rk_s, rkz_s, rv_s, gate_s, qt_s, kband_s, vt_s, gatet_s = slot
        for r0, rs in _row_blocks(SEQ_TILE):
            xb = xin_ref[0, rs, :]
            ms = jnp.mean(xb * xb, axis=-1, keepdims=True)
            h_s[rs, :] = (xb * lax.rsqrt(ms + NORM_EPS) * nw_ref[...]).astype(BF16)
        yield

        def proj(off, width=4 * LANES):
            return _dot(h_s[...], win_ref[:, off:off + width])

        def rot(v, rs):
            return (v * cos_ref[rs, :] + pltpu.roll(v, LANES - 32, 1) * sina_ref[rs, :]
                    + pltpu.roll(v, 32, 1) * sinb_ref[rs, :])

        r = proj(OFF_RQ)
        for r0, rs in _row_blocks(SEQ_TILE):
            c, cr = r0 // CHUNK, r0 % CHUNK
            cs = slice(cr, cr + ROW_BLOCK)
            for p in range(2):
                q = rot(r[rs, p * LANES:(p + 1) * LANES], rs)
                rq_s[p, rs, :] = q.astype(BF16)
                rqx_s[p, 0, rs, :] = (q * xi_ref[p, 0, cs, :]).astype(BF16)
                rqx_s[p, 1, rs, :] = (q * xi_ref[p, 1, cs, :]).astype(BF16)
                k = rot(r[rs, (2 + p) * LANES:(3 + p) * LANES], rs)
                rk_s[p, c, cr:cr + ROW_BLOCK, :] = jnp.where(lo, k, 0.0).astype(BF16)
                rk_s[p, c, CHUNK + cr:CHUNK + cr + ROW_BLOCK, :] = jnp.where(lo, 0.0, k).astype(BF16)
                rkz_s[p, rs, :] = (k * zeta_ref[p, cs, :]).astype(BF16)
        yield

        r = proj(OFF_RV)
        for r0, rs in _row_blocks(SEQ_TILE):
            rv_s[rs, :] = r[rs, :].astype(BF16)
        yield

        r = proj(OFF_RG)
        for r0, rs in _row_blocks(SEQ_TILE):
            gate_s[rs, :] = _silu(r[rs, :])
        yield

        def proj_t(row0, n_rows):
            return _dot_nt(wft_ref[row0:row0 + n_rows, :], h_s[...])

        def head_rms(xt, w_col):
            ms = jnp.mean(xt * xt, axis=0, keepdims=True)
            return xt * lax.rsqrt(ms + NORM_EPS) * w_col

        tok_blocks = [slice(c0, c0 + LANES) for c0 in range(0, SEQ_TILE, LANES)]

        rt = proj_t(FT_Q, SWA_HEADS * SWA_HEAD_DIM)
        for hd in range(SWA_HEADS):
            hs = slice(hd * SWA_HEAD_DIM, (hd + 1) * SWA_HEAD_DIM)
            for ts in tok_blocks:
                qt_s[hs, ts] = (head_rms(rt[hs, ts], qnw_ref[hs, :]) * (SWA_HEAD_DIM ** -0.5)).astype(BF16)
        yield

        rt = proj_t(FT_K, 2 * LANES)
        for ts in tok_blocks:
            knt = jnp.concatenate(
                [head_rms(rt[g * SWA_HEAD_DIM:(g + 1) * SWA_HEAD_DIM, ts],
                          knw_ref[g * SWA_HEAD_DIM:(g + 1) * SWA_HEAD_DIM, :]) for g in range(SWA_KV_HEADS)],
                axis=0)
            kband_s[CHUNK + ts.start:CHUNK + ts.stop, :] = knt.T.astype(BF16)
            vt_s[:, CHUNK + ts.start:CHUNK + ts.stop] = rt[LANES:2 * LANES, ts].astype(BF16)
        yield

        rt = proj_t(FT_G, SWA_HEADS * SWA_HEAD_DIM)
        for r0 in range(0, SWA_HEADS * SWA_HEAD_DIM, ROW_BLOCK):
            for ts in tok_blocks:
                gatet_s[r0:r0 + ROW_BLOCK, ts] = _silu(rt[r0:r0 + ROW_BLOCK, ts])
        yield

    def mix(slot, next_slot):
        rq_s, rqx_s, rk_s, rkz_s, rv_s, gate_s, qt_s, kband_s, vt_s, gatet_s = slot
        key = lax.broadcasted_iota(jnp.int32, (CHUNK, CHUNK), 0)
        qry = lax.broadcasted_iota(jnp.int32, (CHUNK, CHUNK), 1)
        tri_t = key <= qry
        zero_q = jnp.zeros((SWA_HEAD_DIM, SWA_GROUP * CHUNK), BF16)
        for c in range(N_CHUNKS):
            rows = slice(c * CHUNK, (c + 1) * CHUNK)
            band = slice(c * CHUNK, (c + 2) * CHUNK)

            for p in range(2):
                s2 = _dot_nt(rq_s[p, rows, :], rk_s[p, c])
                s2 = jnp.concatenate([(s2[bs, :] * dec_ref[p, bs, :]).astype(BF16)
                                      for _, bs in _row_blocks(CHUNK)], axis=0)
                full = state_s[p]
                if c == 0:
                    full = full * keep_state
                full_b = full.astype(BF16)
                v2 = rv_s[rows, p * 2 * RET_DV:(p + 1) * 2 * RET_DV]
                outs = [
                    _dot(jnp.concatenate([s2[:, j * CHUNK:(j + 1) * CHUNK], rqx_s[p, j, rows, :]], axis=1),
                         jnp.concatenate([v2[:, j * RET_DV:(j + 1) * RET_DV],
                                          full_b[:, j * RET_DV:(j + 1) * RET_DV]], axis=0))
                    for j in range(2)]
                state_s[p] = full * cd_ref[p] + _dot_tn(rkz_s[p, rows, :], v2)
                for j, o in enumerate(outs):
                    hd = 2 * p + j
                    cols = slice(hd * RET_DV, (hd + 1) * RET_DV)
                    for r0, bs in _row_blocks(CHUNK):
                        ob = o[bs, :]
                        rs = slice(c * CHUNK + r0, c * CHUNK + r0 + ROW_BLOCK)
                        mu = jnp.mean(ob, axis=-1, keepdims=True)
                        d = ob - mu
                        var = jnp.mean(d * d, axis=-1, keepdims=True)
                        y = d * lax.rsqrt(var + GN_EPS) * retw_ref[:, cols]
                        mixed_s[rs, cols] = (y * gate_s[rs, cols]).astype(BF16)
                yield

            qts = [jnp.concatenate([qt_s[(SWA_GROUP * g + hh) * SWA_HEAD_DIM:(SWA_GROUP * g + hh + 1) * SWA_HEAD_DIM,
                                         rows] for hh in range(SWA_GROUP)], axis=1)
                   for g in range(SWA_KV_HEADS)]
            rhs = jnp.concatenate([jnp.concatenate([qts[0], zero_q], axis=1),
                                   jnp.concatenate([zero_q, qts[1]], axis=1)], axis=0)
            lt = _dot(kband_s[band, :], rhs)
            yield
            for g in range(SWA_KV_HEADS):
                invs = []
                for hh in range(SWA_GROUP):
                    hd = SWA_GROUP * g + hh
                    hq = slice(hd * CHUNK, (hd + 1) * CHUNK)
                    sink = sinks_ref[hd]
                    bias = bias_s[hd + SWA_HEADS * first_i] if c == 0 else bias_s[hd]
                    lg = jnp.where(tri_t, lt[CHUNK:2 * CHUNK, hq], lt[0:CHUNK, hq]) + bias
                    m = jnp.maximum(jnp.max(lg, axis=0, keepdims=True), sink)
                    e = jnp.exp(lg - m)
                    den = jnp.sum(e, axis=0, keepdims=True) + jnp.exp(sink - m)
                    invs.append(1.0 / den)
                    pq = slice(hh * CHUNK, (hh + 1) * CHUNK)
                    pt_s[c % 2, g, 0:CHUNK, pq] = jnp.where(tri_t, 0.0, e).astype(BF16)
                    pt_s[c % 2, g, CHUNK:2 * CHUNK, pq] = jnp.where(tri_t, e, 0.0).astype(BF16)
                ot = _dot(vt_s[g * SWA_HEAD_DIM:(g + 1) * SWA_HEAD_DIM, band], pt_s[c % 2, g])
                for hh in range(SWA_GROUP):
                    hs = slice((SWA_GROUP * g + hh) * SWA_HEAD_DIM, (SWA_GROUP * g + hh + 1) * SWA_HEAD_DIM)
                    mixedt_s[hs, rows] = (ot[:, hh * CHUNK:(hh + 1) * CHUNK] * invs[hh]
                                          * gatet_s[hs, rows]).astype(BF16)
                yield

            if c % 2 == 1:
                pair = slice((c - 1) * CHUNK, (c + 1) * CHUNK)
                o_ref[0, pair, :] = (xres_ref[0, pair, :]
                                     + _dot(mixed_s[pair, :], wout_ref[0:RET_HEADS * RET_DV, :])
                                     + _dot_tn(mixedt_s[:, pair], wout_ref[RET_HEADS * RET_DV:D_MIX, :]))
                yield

        last = slice(SEQ_TILE, SEQ_TILE + CHUNK)
        next_slot[SLOT_KBAND][0:CHUNK, :] = kband_s[last, :]
        next_slot[SLOT_VT][:, 0:CHUNK] = vt_s[:, last]

    def run(parity):
        streams = [project(slots[parity]), mix(slots[1 - parity], slots[parity])]
        for which in MIX_ORDER + (0,) * len(MIX_ORDER) + (1,) * len(MIX_ORDER):
            next(streams[which], None)

    for parity in range(2):
        pl.when(step % 2 == parity)(functools.partial(run, parity))


def kernel(x, norm_w, w_in, ret_norm_w, q_norm_w, k_norm_w, sinks, rel_bias, w_out):
    batch, seq, d_model = x.shape
    assert d_model == D_MODEL and seq % SEQ_TILE == 0
    assert w_in.shape == (D_MODEL, D_IN) and w_out.shape == (D_MIX, D_MODEL)
    tb = _constant_tables(seq)
    tiles_per_seq = seq // SEQ_TILE
    n_tiles = batch * tiles_per_seq

    def proj_tile(s):
        return jnp.minimum(s, n_tiles - 1)

    def mix_tile(s):
        return jnp.maximum(s - 1, 0)

    const2 = lambda s: (0, 0)
    const3 = lambda s: (0, 0, 0)
    const4 = lambda s: (0, 0, 0, 0)
    once = pl.Buffered(1)
    pos_spec = pl.BlockSpec((SEQ_TILE, LANES), lambda s: (proj_tile(s) % tiles_per_seq, 0))
    smem = pl.BlockSpec(memory_space=pltpu.SMEM)
    x_block = (1, SEQ_TILE, D_MODEL)
    mix_map = lambda s: (mix_tile(s) // tiles_per_seq, mix_tile(s) % tiles_per_seq, 0)

    in_specs = [
        pl.BlockSpec(x_block, lambda s: (proj_tile(s) // tiles_per_seq, proj_tile(s) % tiles_per_seq, 0)),
        pl.BlockSpec(x_block, mix_map),
        pl.BlockSpec((1, D_MODEL), const2),
        pl.BlockSpec((D_MODEL, D_TOK), const2, pipeline_mode=once),
        pl.BlockSpec((D_FT, D_MODEL), const2, pipeline_mode=once),
        pl.BlockSpec((D_MIX, D_MODEL), const2, pipeline_mode=once),
        pos_spec, pos_spec, pos_spec,
        pl.BlockSpec((2, CHUNK, 2 * CHUNK), const3),
        pl.BlockSpec((2, 2, CHUNK, LANES), const4),
        pl.BlockSpec((2, CHUNK, LANES), const3),
        pl.BlockSpec((2, 1, 2 * RET_DV), const3),
        pl.BlockSpec((1, RET_HEADS * RET_DV), const2),
        pl.BlockSpec((SWA_HEADS * SWA_HEAD_DIM, 1), const2),
        pl.BlockSpec((SWA_KV_HEADS * SWA_HEAD_DIM, 1), const2),
        pl.BlockSpec((CHUNK, CHUNK), const2),
        smem, smem,
    ]
    slot_bufs = [
        pltpu.VMEM((2, SEQ_TILE, LANES), BF16),
        pltpu.VMEM((2, 2, SEQ_TILE, LANES), BF16),
        pltpu.VMEM((2, N_CHUNKS, 2 * CHUNK, LANES), BF16),
        pltpu.VMEM((2, SEQ_TILE, LANES), BF16),
        pltpu.VMEM((SEQ_TILE, RET_HEADS * RET_DV), BF16),
        pltpu.VMEM((SEQ_TILE, RET_HEADS * RET_DV), F32),
        pltpu.VMEM((SWA_HEADS * SWA_HEAD_DIM, SEQ_TILE), BF16),
        pltpu.VMEM((SEQ_TILE + CHUNK, LANES), BF16),
        pltpu.VMEM((SWA_KV_HEADS * SWA_HEAD_DIM, SEQ_TILE + CHUNK), BF16),
        pltpu.VMEM((SWA_HEADS * SWA_HEAD_DIM, SEQ_TILE), F32),
    ]
    assert len(slot_bufs) == N_SLOT_BUFS
    scratch = [pltpu.VMEM((SEQ_TILE, D_MODEL), BF16)] + slot_bufs + slot_bufs + [
        pltpu.VMEM((2, SWA_KV_HEADS, 2 * CHUNK, SWA_GROUP * CHUNK), BF16),
        pltpu.VMEM((SEQ_TILE, RET_HEADS * RET_DV), BF16),
        pltpu.VMEM((SWA_HEADS * SWA_HEAD_DIM, SEQ_TILE), BF16),
        pltpu.VMEM((2, 2 * RET_DK, 2 * RET_DV), F32),
        pltpu.VMEM((2 * SWA_HEADS, CHUNK, CHUNK), F32),
    ]
    return pl.pallas_call(
        functools.partial(_layer_kernel, tiles_per_seq),
        grid=(n_tiles + 1,),
        in_specs=in_specs,
        out_specs=pl.BlockSpec(x_block, mix_map),
        out_shape=jax.ShapeDtypeStruct(x.shape, x.dtype),
        scratch_shapes=scratch,
        compiler_params=pltpu.CompilerParams(
            dimension_semantics=("arbitrary",),
            vmem_limit_bytes=VMEM_LIMIT_BYTES),
    )(x, x, norm_w.reshape(1, D_MODEL), w_in[:, :D_TOK].astype(BF16), w_in[:, D_TOK:].T.astype(BF16),
      w_out.astype(BF16),
      tb["cos"], tb["sin_a"], tb["sin_b"], tb["dec"], tb["xi"], tb["zeta"], tb["cd"],
      ret_norm_w.reshape(1, -1), jnp.tile(q_norm_w, SWA_HEADS).reshape(-1, 1),
      jnp.tile(k_norm_w, SWA_KV_HEADS).reshape(-1, 1), tb["bucket"],
      rel_bias.astype(F32), sinks.astype(F32))
```

```python
import functools
import math

import numpy as np
import jax
import jax.numpy as jnp
from jax import lax
from jax.experimental import pallas as pl
from jax.experimental.pallas import tpu as pltpu

D_MODEL = 1024
RET_HEADS = 4
RET_DK = 64
RET_DV = 128
CHUNK = 128
RET_ROPE_BASE = 10000.0
SWA_HEADS = 8
SWA_KV_HEADS = 2
SWA_HEAD_DIM = 64
SWA_GROUP = SWA_HEADS // SWA_KV_HEADS
NUM_BUCKETS = 32
MAX_DISTANCE = 128
NORM_EPS = 1e-6
GN_EPS = 1e-5
NEG_INF = -1e30

OFF_RQ, OFF_RK, OFF_RV, OFF_RG = 0, 256, 512, 1024
D_TOK = 1536
FT_Q, FT_K, FT_G = 0, 512, 768
D_FT = 1280
D_IN = D_TOK + D_FT
D_MIX = 1024

LANES = 128
SEQ_TILE = 512
N_CHUNKS = SEQ_TILE // CHUNK
ROW_BLOCK = 32
VMEM_LIMIT_BYTES = 56 * 1024 * 1024
MIX_ORDER = ((0, 1, 1, 1, 1, 1) + (0, 1, 1, 0, 1, 1, 1) + (0, 1, 1, 1, 0, 1, 1, 1) + (0, 1, 1, 0, 1, 1, 1, 1))
N_SLOT_BUFS = 10
STAGE_ROWS = 128
SLOT_KBAND, SLOT_VT = 7, 8

BF16 = jnp.bfloat16
F32 = jnp.float32


def _t5_bucket_np(n):
    max_exact = NUM_BUCKETS // 2
    nf = np.maximum(n, 1).astype(np.float64)
    large = max_exact + (np.log(nf / max_exact) / math.log(MAX_DISTANCE / max_exact)
                         * (NUM_BUCKETS - max_exact)).astype(np.int32)
    large = np.minimum(large, NUM_BUCKETS - 1)
    return np.where(n < max_exact, n, large).astype(np.int32)


def _constant_tables(seq):
    half = RET_DK // 2
    inv_freq = RET_ROPE_BASE ** (-np.arange(half, dtype=np.float64) / half)
    ang = np.arange(seq, dtype=np.float64)[:, None] * inv_freq[None, :]
    cos, sin = np.cos(ang), np.sin(ang)
    zeros = np.zeros_like(sin)
    cos_t = np.tile(np.concatenate([cos, cos], axis=1), (1, 2))
    sin_a = np.tile(np.concatenate([-sin, zeros], axis=1), (1, 2))
    sin_b = np.tile(np.concatenate([zeros, sin], axis=1), (1, 2))

    gamma = 1.0 - np.exp2(-5.0 - np.arange(RET_HEADS, dtype=np.float64))
    log_g = np.log(gamma)
    i = np.arange(CHUNK, dtype=np.float64)
    diff = i[:, None] - i[None, :]
    k_scale = RET_DK ** -0.5
    decay = np.where(diff >= 0, np.exp(log_g[:, None, None] * np.maximum(diff, 0.0)), 0.0) * k_scale
    dec = np.stack([np.concatenate([decay[2 * p], decay[2 * p + 1]], axis=1) for p in range(2)])
    xi = np.exp(log_g[:, None] * (i + 1.0))
    zeta = np.exp(log_g[:, None] * (CHUNK - 1.0 - i)) * k_scale

    def per_lane(t):
        return np.stack([np.concatenate([np.repeat(t[2 * p][:, None], RET_DK, 1),
                                         np.repeat(t[2 * p + 1][:, None], RET_DK, 1)], axis=1)
                         for p in range(2)])

    lo = (np.arange(LANES) < RET_DK)[None, None, :]
    xi_l = per_lane(xi)
    xi_m = np.stack([np.where(lo, xi_l, 0.0), np.where(lo, 0.0, xi_l)], axis=1)

    chunk_decay = np.exp(log_g * CHUNK)
    cd = np.stack([np.concatenate([np.full((1, RET_DV), chunk_decay[2 * p]),
                                   np.full((1, RET_DV), chunk_decay[2 * p + 1])], axis=1)
                   for p in range(2)])

    j = np.arange(CHUNK)[:, None]
    q = np.arange(CHUNK)[None, :]
    dist = np.where(j <= q, q - j, q + CHUNK - j)
    bucket = _t5_bucket_np(dist)

    f = lambda a: jnp.asarray(a, dtype=F32)
    return dict(cos=f(cos_t), sin_a=f(sin_a), sin_b=f(sin_b), dec=f(dec), xi=f(xi_m),
                zeta=f(per_lane(zeta)), cd=f(cd), bucket=jnp.asarray(bucket, dtype=jnp.int32))


def _dot(a, b):
    return jnp.dot(a, b, preferred_element_type=F32)


def _dot_nt(a, b):
    return lax.dot_general(a, b, (((1,), (1,)), ((), ())), preferred_element_type=F32)


def _dot_tn(a, b):
    return lax.dot_general(a, b, (((0,), (0,)), ((), ())), preferred_element_type=F32)


def _silu(g):
    hg = 0.5 * g
    return hg + hg * jnp.tanh(hg)


def _row_blocks(n_rows):
    return [(r0, slice(r0, r0 + ROW_BLOCK)) for r0 in range(0, n_rows, ROW_BLOCK)]


def _layer_kernel(tiles_per_seq,
                  xin_ref, xres_ref, nw_ref, win_hbm, wout_hbm, cos_ref, sina_ref, sinb_ref, dec_ref,
                  xi_ref, zeta_ref, cd_ref, retw_ref, qnw_ref, knw_ref, bkt_ref, relb_ref,
                  sinks_ref, o_ref, h_s, *scratch):
    slots = (scratch[0:N_SLOT_BUFS], scratch[N_SLOT_BUFS:2 * N_SLOT_BUFS])
    (pt_s, mixed_s, mixedt_s, state_s, bias_s,
     win_ref, wft_ref, wout_ref, stage_in, stage_out, stage_sem) = scratch[2 * N_SLOT_BUFS:]
    step = pl.program_id(0)
    mix_tile = jnp.maximum(step - 1, 0)
    first = (mix_tile % tiles_per_seq == 0) | (step == 0)
    first_i = first.astype(jnp.int32)
    keep_state = jnp.where(first, 0.0, 1.0).astype(F32)

    @pl.when(step == 0)
    def _init():
        bk = bkt_ref[...]
        key = lax.broadcasted_iota(jnp.int32, (CHUNK, CHUNK), 0)
        qry = lax.broadcasted_iota(jnp.int32, (CHUNK, CHUNK), 1)
        for h in range(SWA_HEADS):
            acc = jnp.zeros((CHUNK, CHUNK), F32)
            for u in range(NUM_BUCKETS):
                acc = jnp.where(bk == u, relb_ref[u, h], acc)
            bias_s[h] = acc
            bias_s[SWA_HEADS + h] = jnp.where(key <= qry, acc, NEG_INF)
        state_s[...] = jnp.zeros_like(state_s)
        for ref in slots[1]:
            ref[...] = jnp.zeros(ref.shape, ref.dtype)

        def chunk_copies(hbm, stage, sem0, n_rows):
            return [pltpu.make_async_copy(hbm.at[pl.ds(i * STAGE_ROWS, STAGE_ROWS), :],
                                          stage.at[i % 2], stage_sem.at[sem0 + i % 2])
                    for i in range(n_rows // STAGE_ROWS)]

        def staged(copies, consume):
            copies[0].start()
            for i, cp in enumerate(copies):
                if i + 1 < len(copies):
                    copies[i + 1].start()
                cp.wait()
                consume(i)

        def consume_in(i):
            rows = slice(i * STAGE_ROWS, (i + 1) * STAGE_ROWS)
            win_ref[rows, :] = stage_in[i % 2, :, 0:D_TOK].astype(BF16)
            for f0 in range(0, D_FT, LANES):
                wft_ref[f0:f0 + LANES, rows] = stage_in[i % 2, :, D_TOK + f0:D_TOK + f0 + LANES].T.astype(BF16)

        def consume_out(i):
            rows = slice(i * STAGE_ROWS, (i + 1) * STAGE_ROWS)
            wout_ref[rows, :] = stage_out[i % 2].astype(BF16)

        staged(chunk_copies(win_hbm, stage_in, 0, D_MODEL), consume_in)
        staged(chunk_copies(wout_hbm, stage_out, 2, D_MIX), consume_out)

    lo = lax.broadcasted_iota(jnp.int32, (ROW_BLOCK, LANES), 1) < RET_DK

    def project(slot):
        rq_s, rqx_s, rk_s, rkz_s, rv_s, gate_s, qt_s, kband_s, vt_s, gatet_s = slot
        for r0, rs in _row_blocks(SEQ_TILE):
            xb = xin_ref[0, rs, :]
            ms = jnp.mean(xb * xb, axis=-1, keepdims=True)
            h_s[rs, :] = (xb * lax.rsqrt(ms + NORM_EPS) * nw_ref[...]).astype(BF16)
        yield

        def proj(off, width=4 * LANES):
            return _dot(h_s[...], win_ref[:, off:off + width])

        def rot(v, rs):
            return (v * cos_ref[rs, :] + pltpu.roll(v, LANES - 32, 1) * sina_ref[rs, :]
                    + pltpu.roll(v, 32, 1) * sinb_ref[rs, :])

        r = proj(OFF_RQ)
        for r0, rs in _row_blocks(SEQ_TILE):
            c, cr = r0 // CHUNK, r0 % CHUNK
            cs = slice(cr, cr + ROW_BLOCK)
            for p in range(2):
                q = rot(r[rs, p * LANES:(p + 1) * LANES], rs)
                rq_s[p, rs, :] = q.astype(BF16)
                rqx_s[p, 0, rs, :] = (q * xi_ref[p, 0, cs, :]).astype(BF16)
                rqx_s[p, 1, rs, :] = (q * xi_ref[p, 1, cs, :]).astype(BF16)
                k = rot(r[rs, (2 + p) * LANES:(3 + p) * LANES], rs)
                rk_s[p, c, cr:cr + ROW_BLOCK, :] = jnp.where(lo, k, 0.0).astype(BF16)
                rk_s[p, c, CHUNK + cr:CHUNK + cr + ROW_BLOCK, :] = jnp.where(lo, 0.0, k).astype(BF16)
                rkz_s[p, rs, :] = (k * zeta_ref[p, cs, :]).astype(BF16)
        yield

        r = proj(OFF_RV)
        for r0, rs in _row_blocks(SEQ_TILE):
            rv_s[rs, :] = r[rs, :].astype(BF16)
        yield

        r = proj(OFF_RG)
        for r0, rs in _row_blocks(SEQ_TILE):
            gate_s[rs, :] = _silu(r[rs, :])
        yield

        def proj_t(row0, n_rows):
            return _dot_nt(wft_ref[row0:row0 + n_rows, :], h_s[...])

        def head_rms(xt):
            ms = jnp.mean(xt * xt, axis=0, keepdims=True)
            return xt * lax.rsqrt(ms + NORM_EPS)

        tok_blocks = [slice(c0, c0 + LANES) for c0 in range(0, SEQ_TILE, LANES)]

        rt = proj_t(FT_Q, SWA_HEADS * SWA_HEAD_DIM)
        for hd in range(SWA_HEADS):
            hs = slice(hd * SWA_HEAD_DIM, (hd + 1) * SWA_HEAD_DIM)
            for ts in tok_blocks:
                qt_s[hs, ts] = head_rms(rt[hs, ts]).astype(BF16)
        yield

        qk_w = qnw_ref[...] * knw_ref[...] * (SWA_HEAD_DIM ** -0.5)
        rt = proj_t(FT_K, 2 * LANES)
        for ts in tok_blocks:
            knt = jnp.concatenate(
                [head_rms(rt[g * SWA_HEAD_DIM:(g + 1) * SWA_HEAD_DIM, ts]) for g in range(SWA_KV_HEADS)],
                axis=0)
            kband_s[CHUNK + ts.start:CHUNK + ts.stop, :] = (knt.T * qk_w).astype(BF16)
            vt_s[:, CHUNK + ts.start:CHUNK + ts.stop] = rt[LANES:2 * LANES, ts].astype(BF16)
        yield

        rt = proj_t(FT_G, SWA_HEADS * SWA_HEAD_DIM)
        for r0 in range(0, SWA_HEADS * SWA_HEAD_DIM, ROW_BLOCK):
            for ts in tok_blocks:
                gatet_s[r0:r0 + ROW_BLOCK, ts] = _silu(rt[r0:r0 + ROW_BLOCK, ts])
        yield

    def mix(slot, next_slot):
        rq_s, rqx_s, rk_s, rkz_s, rv_s, gate_s, qt_s, kband_s, vt_s, gatet_s = slot
        key = lax.broadcasted_iota(jnp.int32, (CHUNK, CHUNK), 0)
        qry = lax.broadcasted_iota(jnp.int32, (CHUNK, CHUNK), 1)
        tri_t = key <= qry
        zero_q = jnp.zeros((SWA_HEAD_DIM, SWA_GROUP * CHUNK), BF16)
        for c in range(N_CHUNKS):
            rows = slice(c * CHUNK, (c + 1) * CHUNK)
            band = slice(c * CHUNK, (c + 2) * CHUNK)

            for p in range(2):
                s2 = _dot_nt(rq_s[p, rows, :], rk_s[p, c])
                s2 = jnp.concatenate([(s2[bs, :] * dec_ref[p, bs, :]).astype(BF16)
                                      for _, bs in _row_blocks(CHUNK)], axis=0)
                full = state_s[p]
                if c == 0:
                    full = full * keep_state
                full_b = full.astype(BF16)
                v2 = rv_s[rows, p * 2 * RET_DV:(p + 1) * 2 * RET_DV]
                outs = [
                    _dot(jnp.concatenate([s2[:, j * CHUNK:(j + 1) * CHUNK], rqx_s[p, j, rows, :]], axis=1),
                         jnp.concatenate([v2[:, j * RET_DV:(j + 1) * RET_DV],
                                          full_b[:, j * RET_DV:(j + 1) * RET_DV]], axis=0))
                    for j in range(2)]
                state_s[p] = full * cd_ref[p] + _dot_tn(rkz_s[p, rows, :], v2)
                for j, o in enumerate(outs):
                    hd = 2 * p + j
                    cols = slice(hd * RET_DV, (hd + 1) * RET_DV)
                    for r0, bs in _row_blocks(CHUNK):
                        ob = o[bs, :]
                        rs = slice(c * CHUNK + r0, c * CHUNK + r0 + ROW_BLOCK)
                        mu = jnp.mean(ob, axis=-1, keepdims=True)
                        d = ob - mu
                        var = jnp.mean(d * d, axis=-1, keepdims=True)
                        y = d * lax.rsqrt(var + GN_EPS) * retw_ref[:, cols]
                        mixed_s[rs, cols] = (y * gate_s[rs, cols]).astype(BF16)
                yield

            qts = [jnp.concatenate([qt_s[(SWA_GROUP * g + hh) * SWA_HEAD_DIM:(SWA_GROUP * g + hh + 1) * SWA_HEAD_DIM,
                                         rows] for hh in range(SWA_GROUP)], axis=1)
                   for g in range(SWA_KV_HEADS)]
            rhs = jnp.concatenate([jnp.concatenate([qts[0], zero_q], axis=1),
                                   jnp.concatenate([zero_q, qts[1]], axis=1)], axis=0)
            lt = _dot(kband_s[band, :], rhs)
            yield
            for g in range(SWA_KV_HEADS):
                invs = []
                for hh in range(SWA_GROUP):
                    hd = SWA_GROUP * g + hh
                    hq = slice(hd * CHUNK, (hd + 1) * CHUNK)
                    sink = sinks_ref[hd]
                    bias = bias_s[hd + SWA_HEADS * first_i] if c == 0 else bias_s[hd]
                    lg = jnp.where(tri_t, lt[CHUNK:2 * CHUNK, hq], lt[0:CHUNK, hq]) + bias
                    m = jnp.maximum(jnp.max(lg, axis=0, keepdims=True), sink)
                    e = jnp.exp(lg - m)
                    den = jnp.sum(e, axis=0, keepdims=True) + jnp.exp(sink - m)
                    invs.append(1.0 / den)
                    pq = slice(hh * CHUNK, (hh + 1) * CHUNK)
                    pt_s[c % 2, g, 0:CHUNK, pq] = jnp.where(tri_t, 0.0, e).astype(BF16)
                    pt_s[c % 2, g, CHUNK:2 * CHUNK, pq] = jnp.where(tri_t, e, 0.0).astype(BF16)
                ot = _dot(vt_s[g * SWA_HEAD_DIM:(g + 1) * SWA_HEAD_DIM, band], pt_s[c % 2, g])
                for hh in range(SWA_GROUP):
                    hs = slice((SWA_GROUP * g + hh) * SWA_HEAD_DIM, (SWA_GROUP * g + hh + 1) * SWA_HEAD_DIM)
                    mixedt_s[hs, rows] = (ot[:, hh * CHUNK:(hh + 1) * CHUNK] * invs[hh]
                                          * gatet_s[hs, rows]).astype(BF16)
                yield

            if c % 2 == 1:
                pair = slice((c - 1) * CHUNK, (c + 1) * CHUNK)
                o_ref[0, pair, :] = (xres_ref[0, pair, :]
                                     + _dot(mixed_s[pair, :], wout_ref[0:RET_HEADS * RET_DV, :])
                                     + _dot_tn(mixedt_s[:, pair], wout_ref[RET_HEADS * RET_DV:D_MIX, :]))
                yield

        last = slice(SEQ_TILE, SEQ_TILE + CHUNK)
        next_slot[SLOT_KBAND][0:CHUNK, :] = kband_s[last, :]
        next_slot[SLOT_VT][:, 0:CHUNK] = vt_s[:, last]

    def run(parity):
        streams = [project(slots[parity]), mix(slots[1 - parity], slots[parity])]
        for which in MIX_ORDER + (0,) * len(MIX_ORDER) + (1,) * len(MIX_ORDER):
            next(streams[which], None)

    for parity in range(2):
        pl.when(step % 2 == parity)(functools.partial(run, parity))


def kernel(x, norm_w, w_in, ret_norm_w, q_norm_w, k_norm_w, sinks, rel_bias, w_out):
    batch, seq, d_model = x.shape
    assert d_model == D_MODEL and seq % SEQ_TILE == 0
    assert w_in.shape == (D_MODEL, D_IN) and w_out.shape == (D_MIX, D_MODEL)
    tb = _constant_tables(seq)
    tiles_per_seq = seq // SEQ_TILE
    n_tiles = batch * tiles_per_seq

    def proj_tile(s):
        return jnp.minimum(s, n_tiles - 1)

    def mix_tile(s):
        return jnp.maximum(s - 1, 0)

    const2 = lambda s: (0, 0)
    const3 = lambda s: (0, 0, 0)
    const4 = lambda s: (0, 0, 0, 0)
    pos_spec = pl.BlockSpec((SEQ_TILE, LANES), lambda s: (proj_tile(s) % tiles_per_seq, 0))
    smem = pl.BlockSpec(memory_space=pltpu.SMEM)
    x_block = (1, SEQ_TILE, D_MODEL)
    mix_map = lambda s: (mix_tile(s) // tiles_per_seq, mix_tile(s) % tiles_per_seq, 0)

    in_specs = [
        pl.BlockSpec(x_block, lambda s: (proj_tile(s) // tiles_per_seq, proj_tile(s) % tiles_per_seq, 0)),
        pl.BlockSpec(x_block, mix_map),
        pl.BlockSpec((1, D_MODEL), const2),
        pl.BlockSpec(memory_space=pl.ANY),
        pl.BlockSpec(memory_space=pl.ANY),
        pos_spec, pos_spec, pos_spec,
        pl.BlockSpec((2, CHUNK, 2 * CHUNK), const3),
        pl.BlockSpec((2, 2, CHUNK, LANES), const4),
        pl.BlockSpec((2, CHUNK, LANES), const3),
        pl.BlockSpec((2, 1, 2 * RET_DV), const3),
        pl.BlockSpec((1, RET_HEADS * RET_DV), const2),
        pl.BlockSpec((1, LANES), const2),
        pl.BlockSpec((1, LANES), const2),
        pl.BlockSpec((CHUNK, CHUNK), const2),
        smem, smem,
    ]
    slot_bufs = [
        pltpu.VMEM((2, SEQ_TILE, LANES), BF16),
        pltpu.VMEM((2, 2, SEQ_TILE, LANES), BF16),
        pltpu.VMEM((2, N_CHUNKS, 2 * CHUNK, LANES), BF16),
        pltpu.VMEM((2, SEQ_TILE, LANES), BF16),
        pltpu.VMEM((SEQ_TILE, RET_HEADS * RET_DV), BF16),
        pltpu.VMEM((SEQ_TILE, RET_HEADS * RET_DV), F32),
        pltpu.VMEM((SWA_HEADS * SWA_HEAD_DIM, SEQ_TILE), BF16),
        pltpu.VMEM((SEQ_TILE + CHUNK, LANES), BF16),
        pltpu.VMEM((SWA_KV_HEADS * SWA_HEAD_DIM, SEQ_TILE + CHUNK), BF16),
        pltpu.VMEM((SWA_HEADS * SWA_HEAD_DIM, SEQ_TILE), F32),
    ]
    assert len(slot_bufs) == N_SLOT_BUFS
    scratch = [pltpu.VMEM((SEQ_TILE, D_MODEL), BF16)] + slot_bufs + slot_bufs + [
        pltpu.VMEM((2, SWA_KV_HEADS, 2 * CHUNK, SWA_GROUP * CHUNK), BF16),
        pltpu.VMEM((SEQ_TILE, RET_HEADS * RET_DV), BF16),
        pltpu.VMEM((SWA_HEADS * SWA_HEAD_DIM, SEQ_TILE), BF16),
        pltpu.VMEM((2, 2 * RET_DK, 2 * RET_DV), F32),
        pltpu.VMEM((2 * SWA_HEADS, CHUNK, CHUNK), F32),
        pltpu.VMEM((D_MODEL, D_TOK), BF16),
        pltpu.VMEM((D_FT, D_MODEL), BF16),
        pltpu.VMEM((D_MIX, D_MODEL), BF16),
        pltpu.VMEM((2, STAGE_ROWS, D_IN), F32),
        pltpu.VMEM((2, STAGE_ROWS, D_MODEL), F32),
        pltpu.SemaphoreType.DMA((4,)),
    ]
    return pl.pallas_call(
        functools.partial(_layer_kernel, tiles_per_seq),
        grid=(n_tiles + 1,),
        in_specs=in_specs,
        out_specs=pl.BlockSpec(x_block, mix_map),
        out_shape=jax.ShapeDtypeStruct(x.shape, x.dtype),
        scratch_shapes=scratch,
        compiler_params=pltpu.CompilerParams(
            dimension_semantics=("arbitrary",),
            vmem_limit_bytes=VMEM_LIMIT_BYTES),
    )(x, x, norm_w.reshape(1, D_MODEL), w_in, w_out,
      tb["cos"], tb["sin_a"], tb["sin_b"], tb["dec"], tb["xi"], tb["zeta"], tb["cd"],
      ret_norm_w.reshape(1, -1), jnp.tile(q_norm_w, SWA_KV_HEADS).reshape(1, -1),
      jnp.tile(k_norm_w, SWA_KV_HEADS).reshape(1, -1), tb["bucket"],
      rel_bias.astype(F32), sinks.astype(F32))
```

```python
import functools
import math

import numpy as np
import jax
import jax.numpy as jnp
from jax import lax
from jax.experimental import pallas as pl
from jax.experimental.pallas import tpu as pltpu

D_MODEL = 1024
RET_HEADS = 4
RET_DK = 64
RET_DV = 128
CHUNK = 128
RET_ROPE_BASE = 10000.0
SWA_HEADS = 8
SWA_KV_HEADS = 2
SWA_HEAD_DIM = 64
SWA_GROUP = SWA_HEADS // SWA_KV_HEADS
NUM_BUCKETS = 32
MAX_DISTANCE = 128
NORM_EPS = 1e-6
GN_EPS = 1e-5
NEG_INF = -1e30

OFF_RQ, OFF_RK, OFF_RV, OFF_RG = 0, 256, 512, 1024
D_TOK = 1536
FT_Q, FT_K, FT_G = 0, 512, 768
D_FT = 1280
D_IN = D_TOK + D_FT
D_MIX = 1024

LANES = 128
SEQ_TILE = 512
N_CHUNKS = SEQ_TILE // CHUNK
ROW_BLOCK = 32
VMEM_LIMIT_BYTES = 56 * 1024 * 1024
MIX_ORDER = (0, 1, 0, 1, 1, 1, 0, 0, 1, 1, 1, 1, 1, 1, 0, 0, 1, 0, 0, 1, 1, 0, 0, 1, 0, 0, 0)
N_SLOT_BUFS = 10
STAGE_ROWS = 128
SLOT_KBAND, SLOT_VT = 7, 8

BF16 = jnp.bfloat16
F32 = jnp.float32


def _t5_bucket_np(n):
    max_exact = NUM_BUCKETS // 2
    nf = np.maximum(n, 1).astype(np.float64)
    large = max_exact + (np.log(nf / max_exact) / math.log(MAX_DISTANCE / max_exact)
                         * (NUM_BUCKETS - max_exact)).astype(np.int32)
    large = np.minimum(large, NUM_BUCKETS - 1)
    return np.where(n < max_exact, n, large).astype(np.int32)


def _constant_tables(seq):
    half = RET_DK // 2
    inv_freq = RET_ROPE_BASE ** (-np.arange(half, dtype=np.float64) / half)
    ang = np.arange(seq, dtype=np.float64)[:, None] * inv_freq[None, :]
    cos, sin = np.cos(ang), np.sin(ang)
    zeros = np.zeros_like(sin)
    cos_t = np.tile(np.concatenate([cos, cos], axis=1), (1, 2))
    sin_a = np.tile(np.concatenate([-sin, zeros], axis=1), (1, 2))
    sin_b = np.tile(np.concatenate([zeros, sin], axis=1), (1, 2))

    gamma = 1.0 - np.exp2(-5.0 - np.arange(RET_HEADS, dtype=np.float64))
    log_g = np.log(gamma)
    i = np.arange(CHUNK, dtype=np.float64)
    diff = i[:, None] - i[None, :]
    k_scale = RET_DK ** -0.5
    decay = np.where(diff >= 0, np.exp(log_g[:, None, None] * np.maximum(diff, 0.0)), 0.0) * k_scale
    dec = np.stack([np.concatenate([decay[2 * p], decay[2 * p + 1]], axis=1) for p in range(2)])
    xi = np.exp(log_g[:, None] * (i + 1.0))
    zeta = np.exp(log_g[:, None] * (CHUNK - 1.0 - i)) * k_scale

    def per_lane(t):
        return np.stack([np.concatenate([np.repeat(t[2 * p][:, None], RET_DK, 1),
                                         np.repeat(t[2 * p + 1][:, None], RET_DK, 1)], axis=1)
                         for p in range(2)])

    lo = (np.arange(LANES) < RET_DK)[None, None, :]
    xi_l = per_lane(xi)
    xi_m = np.stack([np.where(lo, xi_l, 0.0), np.where(lo, 0.0, xi_l)], axis=1)

    chunk_decay = np.exp(log_g * CHUNK)
    cd = np.stack([np.concatenate([np.full((1, RET_DV), chunk_decay[2 * p]),
                                   np.full((1, RET_DV), chunk_decay[2 * p + 1])], axis=1)
                   for p in range(2)])

    j = np.arange(CHUNK)[:, None]
    q = np.arange(CHUNK)[None, :]
    dist = np.where(j <= q, q - j, q + CHUNK - j)
    bucket = _t5_bucket_np(dist)

    f = lambda a: jnp.asarray(a, dtype=F32)
    return dict(cos=f(cos_t), sin_a=f(sin_a), sin_b=f(sin_b), dec=f(dec), xi=f(xi_m),
                zeta=f(per_lane(zeta)), cd=f(cd), bucket=jnp.asarray(bucket, dtype=jnp.int32))


def _dot(a, b):
    return jnp.dot(a, b, preferred_element_type=F32)


def _dot_nt(a, b):
    return lax.dot_general(a, b, (((1,), (1,)), ((), ())), preferred_element_type=F32)


def _dot_tn(a, b):
    return lax.dot_general(a, b, (((0,), (0,)), ((), ())), preferred_element_type=F32)


def _silu(g):
    hg = 0.5 * g
    return hg + hg * jnp.tanh(hg)


def _row_blocks(n_rows):
    return [(r0, slice(r0, r0 + ROW_BLOCK)) for r0 in range(0, n_rows, ROW_BLOCK)]


def _layer_kernel(tiles_per_seq,
                  xin_ref, xres_ref, nw_ref, win_hbm, wout_hbm, cos_ref, sina_ref, sinb_ref, dec_ref,
                  xi_ref, zeta_ref, cd_ref, retw_ref, qnw_ref, knw_ref, bkt_ref, relb_ref,
                  sinks_ref, o_ref, h_s, *scratch):
    slots = (scratch[0:N_SLOT_BUFS], scratch[N_SLOT_BUFS:2 * N_SLOT_BUFS])
    (pt_s, mixed_s, mixedt_s, state_s, bias_s,
     win_ref, wft_ref, wout_ref, stage_in, stage_out, stage_sem) = scratch[2 * N_SLOT_BUFS:]
    step = pl.program_id(0)
    mix_tile = jnp.maximum(step - 1, 0)
    first = (mix_tile % tiles_per_seq == 0) | (step == 0)
    first_i = first.astype(jnp.int32)
    keep_state = jnp.where(first, 0.0, 1.0).astype(F32)

    @pl.when(step == 0)
    def _init():
        bk = bkt_ref[...]
        key = lax.broadcasted_iota(jnp.int32, (CHUNK, CHUNK), 0)
        qry = lax.broadcasted_iota(jnp.int32, (CHUNK, CHUNK), 1)
        for h in range(SWA_HEADS):
            acc = jnp.zeros((CHUNK, CHUNK), F32)
            for u in range(NUM_BUCKETS):
                acc = jnp.where(bk == u, relb_ref[u, h], acc)
            bias_s[h] = acc
            bias_s[SWA_HEADS + h] = jnp.where(key <= qry, acc, NEG_INF)
        state_s[...] = jnp.zeros_like(state_s)
        for ref in slots[1]:
            ref[...] = jnp.zeros(ref.shape, ref.dtype)

        def chunk_copies(hbm, stage, sem0, n_rows):
            return [pltpu.make_async_copy(hbm.at[pl.ds(i * STAGE_ROWS, STAGE_ROWS), :],
                                          stage.at[i % 2], stage_sem.at[sem0 + i % 2])
                    for i in range(n_rows // STAGE_ROWS)]

        def staged(copies, consume):
            copies[0].start()
            for i, cp in enumerate(copies):
                if i + 1 < len(copies):
                    copies[i + 1].start()
                cp.wait()
                consume(i)

        def consume_in(i):
            rows = slice(i * STAGE_ROWS, (i + 1) * STAGE_ROWS)
            win_ref[rows, :] = stage_in[i % 2, :, 0:D_TOK].astype(BF16)
            for f0 in range(0, D_FT, LANES):
                wft_ref[f0:f0 + LANES, rows] = stage_in[i % 2, :, D_TOK + f0:D_TOK + f0 + LANES].T.astype(BF16)

        def consume_out(i):
            rows = slice(i * STAGE_ROWS, (i + 1) * STAGE_ROWS)
            wout_ref[rows, :] = stage_out[i % 2].astype(BF16)

        staged(chunk_copies(win_hbm, stage_in, 0, D_MODEL), consume_in)
        staged(chunk_copies(wout_hbm, stage_out, 2, D_MIX), consume_out)

    lo = lax.broadcasted_iota(jnp.int32, (ROW_BLOCK, LANES), 1) < RET_DK

    def project(slot):
        rq_s, rqx_s, rk_s, rkz_s, rv_s, gate_s, qt_s, kband_s, vt_s, gatet_s = slot
        for r0, rs in _row_blocks(SEQ_TILE):
            xb = xin_ref[0, rs, :]
            ms = jnp.mean(xb * xb, axis=-1, keepdims=True)
            h_s[rs, :] = (xb * lax.rsqrt(ms + NORM_EPS) * nw_ref[...]).astype(BF16)
        yield

        def proj(off):
            return _dot(h_s[...], win_ref[:, off:off + 4 * LANES])

        def proj_t(row0, n_rows):
            return _dot_nt(wft_ref[row0:row0 + n_rows, :], h_s[...])

        def rot(v, rs):
            return (v * cos_ref[rs, :] + pltpu.roll(v, LANES - 32, 1) * sina_ref[rs, :]
                    + pltpu.roll(v, 32, 1) * sinb_ref[rs, :])

        def head_rms(xt):
            ms = jnp.mean(xt * xt, axis=0, keepdims=True)
            return xt * lax.rsqrt(ms + NORM_EPS)

        tok_blocks = [slice(c0, c0 + LANES) for c0 in range(0, SEQ_TILE, LANES)]

        def post_rqk(r):
            for r0, rs in _row_blocks(SEQ_TILE):
                c, cr = r0 // CHUNK, r0 % CHUNK
                cs = slice(cr, cr + ROW_BLOCK)
                for p in range(2):
                    q = rot(r[rs, p * LANES:(p + 1) * LANES], rs)
                    rq_s[p, rs, :] = q.astype(BF16)
                    rqx_s[p, 0, rs, :] = (q * xi_ref[p, 0, cs, :]).astype(BF16)
                    rqx_s[p, 1, rs, :] = (q * xi_ref[p, 1, cs, :]).astype(BF16)
                    k = rot(r[rs, (2 + p) * LANES:(3 + p) * LANES], rs)
                    rk_s[p, c, cr:cr + ROW_BLOCK, :] = jnp.where(lo, k, 0.0).astype(BF16)
                    rk_s[p, c, CHUNK + cr:CHUNK + cr + ROW_BLOCK, :] = jnp.where(lo, 0.0, k).astype(BF16)
                    rkz_s[p, rs, :] = (k * zeta_ref[p, cs, :]).astype(BF16)

        def post_rv(r):
            for r0, rs in _row_blocks(SEQ_TILE):
                rv_s[rs, :] = r[rs, :].astype(BF16)

        def post_rg(r):
            for r0, rs in _row_blocks(SEQ_TILE):
                gate_s[rs, :] = _silu(r[rs, :])

        def post_qt(rt):
            for hd in range(SWA_HEADS):
                hs = slice(hd * SWA_HEAD_DIM, (hd + 1) * SWA_HEAD_DIM)
                for ts in tok_blocks:
                    qt_s[hs, ts] = head_rms(rt[hs, ts]).astype(BF16)

        def post_kvt(rt):
            qk_w = qnw_ref[...] * knw_ref[...] * (SWA_HEAD_DIM ** -0.5)
            for ts in tok_blocks:
                knt = jnp.concatenate(
                    [head_rms(rt[g * SWA_HEAD_DIM:(g + 1) * SWA_HEAD_DIM, ts]) for g in range(SWA_KV_HEADS)],
                    axis=0)
                kband_s[CHUNK + ts.start:CHUNK + ts.stop, :] = (knt.T * qk_w).astype(BF16)
                vt_s[:, CHUNK + ts.start:CHUNK + ts.stop] = rt[LANES:2 * LANES, ts].astype(BF16)

        def post_gt(rt):
            for r0 in range(0, SWA_HEADS * SWA_HEAD_DIM, ROW_BLOCK):
                for ts in tok_blocks:
                    gatet_s[r0:r0 + ROW_BLOCK, ts] = _silu(rt[r0:r0 + ROW_BLOCK, ts])

        groups = [
            (functools.partial(proj, OFF_RQ), post_rqk),
            (functools.partial(proj_t, FT_Q, SWA_HEADS * SWA_HEAD_DIM), post_qt),
            (functools.partial(proj, OFF_RG), post_rg),
            (functools.partial(proj_t, FT_G, SWA_HEADS * SWA_HEAD_DIM), post_gt),
            (functools.partial(proj_t, FT_K, 2 * LANES), post_kvt),
            (functools.partial(proj, OFF_RV), post_rv),
        ]
        pending = None
        for matmul, post in groups:
            result = matmul()
            yield
            if pending is not None:
                pending[0](pending[1])
                yield
            pending = (post, result)
        pending[0](pending[1])
        yield

    def mix(slot, next_slot):
        rq_s, rqx_s, rk_s, rkz_s, rv_s, gate_s, qt_s, kband_s, vt_s, gatet_s = slot
        key = lax.broadcasted_iota(jnp.int32, (CHUNK, CHUNK), 0)
        qry = lax.broadcasted_iota(jnp.int32, (CHUNK, CHUNK), 1)
        tri_t = key <= qry
        zero_q = jnp.zeros((SWA_HEAD_DIM, SWA_GROUP * CHUNK), BF16)

        def out_proj(c_hi):
            pair = slice((c_hi - 1) * CHUNK, (c_hi + 1) * CHUNK)
            o_ref[0, pair, :] = (xres_ref[0, pair, :]
                                 + _dot(mixed_s[pair, :], wout_ref[0:RET_HEADS * RET_DV, :])
                                 + _dot_tn(mixedt_s[:, pair], wout_ref[RET_HEADS * RET_DV:D_MIX, :]))

        for c in range(N_CHUNKS):
            rows = slice(c * CHUNK, (c + 1) * CHUNK)
            band = slice(c * CHUNK, (c + 2) * CHUNK)

            ret_outs = []
            for p in range(2):
                s2 = _dot_nt(rq_s[p, rows, :], rk_s[p, c])
                s2 = jnp.concatenate([(s2[bs, :] * dec_ref[p, bs, :]).astype(BF16)
                                      for _, bs in _row_blocks(CHUNK)], axis=0)
                full = state_s[p]
                if c == 0:
                    full = full * keep_state
                full_b = full.astype(BF16)
                v2 = rv_s[rows, p * 2 * RET_DV:(p + 1) * 2 * RET_DV]
                ret_outs += [
                    _dot(jnp.concatenate([s2[:, j * CHUNK:(j + 1) * CHUNK], rqx_s[p, j, rows, :]], axis=1),
                         jnp.concatenate([v2[:, j * RET_DV:(j + 1) * RET_DV],
                                          full_b[:, j * RET_DV:(j + 1) * RET_DV]], axis=0))
                    for j in range(2)]
                state_s[p] = full * cd_ref[p] + _dot_tn(rkz_s[p, rows, :], v2)
            qts = [jnp.concatenate([qt_s[(SWA_GROUP * g + hh) * SWA_HEAD_DIM:(SWA_GROUP * g + hh + 1) * SWA_HEAD_DIM,
                                         rows] for hh in range(SWA_GROUP)], axis=1)
                   for g in range(SWA_KV_HEADS)]
            rhs = jnp.concatenate([jnp.concatenate([qts[0], zero_q], axis=1),
                                   jnp.concatenate([zero_q, qts[1]], axis=1)], axis=0)
            lt = _dot(kband_s[band, :], rhs)
            yield
            if c == 2:
                out_proj(1)
                yield

            for hd, o in enumerate(ret_outs):
                cols = slice(hd * RET_DV, (hd + 1) * RET_DV)
                for r0, bs in _row_blocks(CHUNK):
                    ob = o[bs, :]
                    rs = slice(c * CHUNK + r0, c * CHUNK + r0 + ROW_BLOCK)
                    mu = jnp.mean(ob, axis=-1, keepdims=True)
                    d = ob - mu
                    var = jnp.mean(d * d, axis=-1, keepdims=True)
                    y = d * lax.rsqrt(var + GN_EPS) * retw_ref[:, cols]
                    mixed_s[rs, cols] = (y * gate_s[rs, cols]).astype(BF16)
            invs = []
            for hd in range(SWA_HEADS):
                g, hh = hd // SWA_GROUP, hd % SWA_GROUP
                hq = slice(hd * CHUNK, (hd + 1) * CHUNK)
                sink = sinks_ref[hd]
                bias = bias_s[hd + SWA_HEADS * first_i] if c == 0 else bias_s[hd]
                lg = jnp.where(tri_t, lt[CHUNK:2 * CHUNK, hq], lt[0:CHUNK, hq]) + bias
                m = jnp.maximum(jnp.max(lg, axis=0, keepdims=True), sink)
                e = jnp.exp(lg - m)
                den = jnp.sum(e, axis=0, keepdims=True) + jnp.exp(sink - m)
                invs.append(1.0 / den)
                pq = slice(hh * CHUNK, (hh + 1) * CHUNK)
                pt_s[c % 2, g, 0:CHUNK, pq] = jnp.where(tri_t, 0.0, e).astype(BF16)
                pt_s[c % 2, g, CHUNK:2 * CHUNK, pq] = jnp.where(tri_t, e, 0.0).astype(BF16)
            yield

            for g in range(SWA_KV_HEADS):
                ot = _dot(vt_s[g * SWA_HEAD_DIM:(g + 1) * SWA_HEAD_DIM, band], pt_s[c % 2, g])
                for hh in range(SWA_GROUP):
                    hd = SWA_GROUP * g + hh
                    hs = slice(hd * SWA_HEAD_DIM, (hd + 1) * SWA_HEAD_DIM)
                    mixedt_s[hs, rows] = (ot[:, hh * CHUNK:(hh + 1) * CHUNK] * invs[hd]
                                          * gatet_s[hs, rows]).astype(BF16)
            yield

        out_proj(N_CHUNKS - 1)
        yield

        last = slice(SEQ_TILE, SEQ_TILE + CHUNK)
        next_slot[SLOT_KBAND][0:CHUNK, :] = kband_s[last, :]
        next_slot[SLOT_VT][:, 0:CHUNK] = vt_s[:, last]

    def run(parity):
        streams = [project(slots[parity]), mix(slots[1 - parity], slots[parity])]
        for which in MIX_ORDER + (0,) * len(MIX_ORDER) + (1,) * len(MIX_ORDER):
            next(streams[which], None)

    for parity in range(2):
        pl.when(step % 2 == parity)(functools.partial(run, parity))


def kernel(x, norm_w, w_in, ret_norm_w, q_norm_w, k_norm_w, sinks, rel_bias, w_out):
    batch, seq, d_model = x.shape
    assert d_model == D_MODEL and seq % SEQ_TILE == 0
    assert w_in.shape == (D_MODEL, D_IN) and w_out.shape == (D_MIX, D_MODEL)
    tb = _constant_tables(seq)
    tiles_per_seq = seq // SEQ_TILE
    n_tiles = batch * tiles_per_seq

    def proj_tile(s):
        return jnp.minimum(s, n_tiles - 1)

    def mix_tile(s):
        return jnp.maximum(s - 1, 0)

    const2 = lambda s: (0, 0)
    const3 = lambda s: (0, 0, 0)
    const4 = lambda s: (0, 0, 0, 0)
    pos_spec = pl.BlockSpec((SEQ_TILE, LANES), lambda s: (proj_tile(s) % tiles_per_seq, 0))
    smem = pl.BlockSpec(memory_space=pltpu.SMEM)
    x_block = (1, SEQ_TILE, D_MODEL)
    mix_map = lambda s: (mix_tile(s) // tiles_per_seq, mix_tile(s) % tiles_per_seq, 0)

    in_specs = [
        pl.BlockSpec(x_block, lambda s: (proj_tile(s) // tiles_per_seq, proj_tile(s) % tiles_per_seq, 0)),
        pl.BlockSpec(x_block, mix_map),
        pl.BlockSpec((1, D_MODEL), const2),
        pl.BlockSpec(memory_space=pl.ANY),
        pl.BlockSpec(memory_space=pl.ANY),
        pos_spec, pos_spec, pos_spec,
        pl.BlockSpec((2, CHUNK, 2 * CHUNK), const3),
        pl.BlockSpec((2, 2, CHUNK, LANES), const4),
        pl.BlockSpec((2, CHUNK, LANES), const3),
        pl.BlockSpec((2, 1, 2 * RET_DV), const3),
        pl.BlockSpec((1, RET_HEADS * RET_DV), const2),
        pl.BlockSpec((1, LANES), const2),
        pl.BlockSpec((1, LANES), const2),
        pl.BlockSpec((CHUNK, CHUNK), const2),
        smem, smem,
    ]
    slot_bufs = [
        pltpu.VMEM((2, SEQ_TILE, LANES), BF16),
        pltpu.VMEM((2, 2, SEQ_TILE, LANES), BF16),
        pltpu.VMEM((2, N_CHUNKS, 2 * CHUNK, LANES), BF16),
        pltpu.VMEM((2, SEQ_TILE, LANES), BF16),
        pltpu.VMEM((SEQ_TILE, RET_HEADS * RET_DV), BF16),
        pltpu.VMEM((SEQ_TILE, RET_HEADS * RET_DV), F32),
        pltpu.VMEM((SWA_HEADS * SWA_HEAD_DIM, SEQ_TILE), BF16),
        pltpu.VMEM((SEQ_TILE + CHUNK, LANES), BF16),
        pltpu.VMEM((SWA_KV_HEADS * SWA_HEAD_DIM, SEQ_TILE + CHUNK), BF16),
        pltpu.VMEM((SWA_HEADS * SWA_HEAD_DIM, SEQ_TILE), F32),
    ]
    assert len(slot_bufs) == N_SLOT_BUFS
    scratch = [pltpu.VMEM((SEQ_TILE, D_MODEL), BF16)] + slot_bufs + slot_bufs + [
        pltpu.VMEM((2, SWA_KV_HEADS, 2 * CHUNK, SWA_GROUP * CHUNK), BF16),
        pltpu.VMEM((SEQ_TILE, RET_HEADS * RET_DV), BF16),
        pltpu.VMEM((SWA_HEADS * SWA_HEAD_DIM, SEQ_TILE), BF16),
        pltpu.VMEM((2, 2 * RET_DK, 2 * RET_DV), F32),
        pltpu.VMEM((2 * SWA_HEADS, CHUNK, CHUNK), F32),
        pltpu.VMEM((D_MODEL, D_TOK), BF16),
        pltpu.VMEM((D_FT, D_MODEL), BF16),
        pltpu.VMEM((D_MIX, D_MODEL), BF16),
        pltpu.VMEM((2, STAGE_ROWS, D_IN), F32),
        pltpu.VMEM((2, STAGE_ROWS, D_MODEL), F32),
        pltpu.SemaphoreType.DMA((4,)),
    ]
    return pl.pallas_call(
        functools.partial(_layer_kernel, tiles_per_seq),
        grid=(n_tiles + 1,),
        in_specs=in_specs,
        out_specs=pl.BlockSpec(x_block, mix_map),
        out_shape=jax.ShapeDtypeStruct(x.shape, x.dtype),
        scratch_shapes=scratch,
        compiler_params=pltpu.CompilerParams(
            dimension_semantics=("arbitrary",),
            vmem_limit_bytes=VMEM_LIMIT_BYTES),
    )(x, x, norm_w.reshape(1, D_MODEL), w_in, w_out,
      tb["cos"], tb["sin_a"], tb["sin_b"], tb["dec"], tb["xi"], tb["zeta"], tb["cd"],
      ret_norm_w.reshape(1, -1), jnp.tile(q_norm_w, SWA_KV_HEADS).reshape(1, -1),
      jnp.tile(k_norm_w, SWA_KV_HEADS).reshape(1, -1), tb["bucket"],
      rel_bias.astype(F32), sinks.astype(F32))
```

```python
import functools
import math

import numpy as np
import jax
import jax.numpy as jnp
from jax import lax
from jax.experimental import pallas as pl
from jax.experimental.pallas import tpu as pltpu

D_MODEL = 1024
RET_HEADS = 4
RET_DK = 64
RET_DV = 128
CHUNK = 128
RET_ROPE_BASE = 10000.0
SWA_HEADS = 8
SWA_KV_HEADS = 2
SWA_HEAD_DIM = 64
SWA_GROUP = SWA_HEADS // SWA_KV_HEADS
NUM_BUCKETS = 32
MAX_DISTANCE = 128
NORM_EPS = 1e-6
GN_EPS = 1e-5
NEG_INF = -1e30

OFF_RQ, OFF_RK, OFF_RV, OFF_RG = 0, 256, 512, 1024
D_TOK = 1536
FT_Q, FT_K, FT_G = 0, 512, 768
D_FT = 1280
D_IN = D_TOK + D_FT
D_MIX = 1024

LANES = 128
SEQ_TILE = 512
N_CHUNKS = SEQ_TILE // CHUNK
ROW_BLOCK = 32
VMEM_LIMIT_BYTES = 56 * 1024 * 1024
MIX_ORDER = (0, 1, 1, 0, 1, 1, 0, 0, 1, 1, 1, 0, 0, 1, 1, 0, 0, 1, 1, 0, 0, 1, 0, 0, 0)
N_SLOT_BUFS = 10
STAGE_ROWS = 128
SLOT_KBAND, SLOT_VT = 7, 8

BF16 = jnp.bfloat16
F32 = jnp.float32


def _t5_bucket_np(n):
    max_exact = NUM_BUCKETS // 2
    nf = np.maximum(n, 1).astype(np.float64)
    large = max_exact + (np.log(nf / max_exact) / math.log(MAX_DISTANCE / max_exact)
                         * (NUM_BUCKETS - max_exact)).astype(np.int32)
    large = np.minimum(large, NUM_BUCKETS - 1)
    return np.where(n < max_exact, n, large).astype(np.int32)


def _constant_tables(seq):
    half = RET_DK // 2
    inv_freq = RET_ROPE_BASE ** (-np.arange(half, dtype=np.float64) / half)
    ang = np.arange(seq, dtype=np.float64)[:, None] * inv_freq[None, :]
    cos, sin = np.cos(ang), np.sin(ang)
    zeros = np.zeros_like(sin)
    cos_t = np.tile(np.concatenate([cos, cos], axis=1), (1, 2))
    sin_a = np.tile(np.concatenate([-sin, zeros], axis=1), (1, 2))
    sin_b = np.tile(np.concatenate([zeros, sin], axis=1), (1, 2))

    gamma = 1.0 - np.exp2(-5.0 - np.arange(RET_HEADS, dtype=np.float64))
    log_g = np.log(gamma)
    i = np.arange(CHUNK, dtype=np.float64)
    diff = i[:, None] - i[None, :]
    k_scale = RET_DK ** -0.5
    decay = np.where(diff >= 0, np.exp(log_g[:, None, None] * np.maximum(diff, 0.0)), 0.0) * k_scale
    dec = np.stack([np.concatenate([decay[2 * p], decay[2 * p + 1]], axis=1) for p in range(2)])
    xi = np.exp(log_g[:, None] * (i + 1.0))
    zeta = np.exp(log_g[:, None] * (CHUNK - 1.0 - i)) * k_scale

    def per_lane(t):
        return np.stack([np.concatenate([np.repeat(t[2 * p][:, None], RET_DK, 1),
                                         np.repeat(t[2 * p + 1][:, None], RET_DK, 1)], axis=1)
                         for p in range(2)])

    lo = (np.arange(LANES) < RET_DK)[None, None, :]
    xi_l = per_lane(xi)
    xi_m = np.stack([np.where(lo, xi_l, 0.0), np.where(lo, 0.0, xi_l)], axis=1)

    chunk_decay = np.exp(log_g * CHUNK)
    cd = np.stack([np.concatenate([np.full((1, RET_DV), chunk_decay[2 * p]),
                                   np.full((1, RET_DV), chunk_decay[2 * p + 1])], axis=1)
                   for p in range(2)])

    j = np.arange(CHUNK)[:, None]
    q = np.arange(CHUNK)[None, :]
    dist = np.where(j <= q, q - j, q + CHUNK - j)
    bucket = _t5_bucket_np(dist)

    f = lambda a: jnp.asarray(a, dtype=F32)
    return dict(cos=f(cos_t), sin_a=f(sin_a), sin_b=f(sin_b), dec=f(dec), xi=f(xi_m),
                zeta=f(per_lane(zeta)), cd=f(cd), bucket=jnp.asarray(bucket, dtype=jnp.int32))


def _dot(a, b):
    return jnp.dot(a, b, preferred_element_type=F32)


def _dot_nt(a, b):
    return lax.dot_general(a, b, (((1,), (1,)), ((), ())), preferred_element_type=F32)


def _dot_tn(a, b):
    return lax.dot_general(a, b, (((0,), (0,)), ((), ())), preferred_element_type=F32)


def _silu(g):
    hg = 0.5 * g
    return hg + hg * jnp.tanh(hg)


def _row_blocks(n_rows):
    return [(r0, slice(r0, r0 + ROW_BLOCK)) for r0 in range(0, n_rows, ROW_BLOCK)]


def _layer_kernel(tiles_per_seq,
                  xin_ref, xres_ref, nw_ref, win_hbm, wout_hbm, cos_ref, sina_ref, sinb_ref, dec_ref,
                  xi_ref, zeta_ref, cd_ref, retw_ref, qnw_ref, knw_ref, bkt_ref, relb_ref,
                  sinks_ref, o_ref, h_s, *scratch):
    slots = (scratch[0:N_SLOT_BUFS], scratch[N_SLOT_BUFS:2 * N_SLOT_BUFS])
    (pt_s, mixed_s, mixedt_s, state_s, bias_s,
     win_ref, wft_ref, wout_ref, stage_in, stage_out, stage_sem) = scratch[2 * N_SLOT_BUFS:]
    step = pl.program_id(0)
    mix_tile = jnp.maximum(step - 1, 0)
    first = (mix_tile % tiles_per_seq == 0) | (step == 0)
    first_i = first.astype(jnp.int32)
    keep_state = jnp.where(first, 0.0, 1.0).astype(F32)

    @pl.when(step == 0)
    def _init():
        bk = bkt_ref[...]
        key = lax.broadcasted_iota(jnp.int32, (CHUNK, CHUNK), 0)
        qry = lax.broadcasted_iota(jnp.int32, (CHUNK, CHUNK), 1)
        for h in range(SWA_HEADS):
            acc = jnp.zeros((CHUNK, CHUNK), F32)
            for u in range(NUM_BUCKETS):
                acc = jnp.where(bk == u, relb_ref[u, h], acc)
            bias_s[h] = acc
            bias_s[SWA_HEADS + h] = jnp.where(key <= qry, acc, NEG_INF)
        state_s[...] = jnp.zeros_like(state_s)
        for ref in slots[1]:
            ref[...] = jnp.zeros(ref.shape, ref.dtype)

        def chunk_copies(hbm, stage, sem0, n_rows):
            return [pltpu.make_async_copy(hbm.at[pl.ds(i * STAGE_ROWS, STAGE_ROWS), :],
                                          stage.at[i % 2], stage_sem.at[sem0 + i % 2])
                    for i in range(n_rows // STAGE_ROWS)]

        def staged(copies, consume):
            copies[0].start()
            for i, cp in enumerate(copies):
                if i + 1 < len(copies):
                    copies[i + 1].start()
                cp.wait()
                consume(i)

        def consume_in(i):
            rows = slice(i * STAGE_ROWS, (i + 1) * STAGE_ROWS)
            win_ref[rows, :] = stage_in[i % 2, :, 0:D_TOK].astype(BF16)
            for f0 in range(0, D_FT, LANES):
                wft_ref[f0:f0 + LANES, rows] = stage_in[i % 2, :, D_TOK + f0:D_TOK + f0 + LANES].T.astype(BF16)

        def consume_out(i):
            rows = slice(i * STAGE_ROWS, (i + 1) * STAGE_ROWS)
            wout_ref[rows, :] = stage_out[i % 2].astype(BF16)

        staged(chunk_copies(win_hbm, stage_in, 0, D_MODEL), consume_in)
        staged(chunk_copies(wout_hbm, stage_out, 2, D_MIX), consume_out)

    lo = lax.broadcasted_iota(jnp.int32, (ROW_BLOCK, LANES), 1) < RET_DK

    def project(slot):
        rq_s, rqx_s, rk_s, rkz_s, rv_s, gate_s, qt_s, kband_s, vt_s, gatet_s = slot
        for r0, rs in _row_blocks(SEQ_TILE):
            xb = xin_ref[0, rs, :]
            ms = jnp.mean(xb * xb, axis=-1, keepdims=True)
            h_s[rs, :] = (xb * lax.rsqrt(ms + NORM_EPS) * nw_ref[...]).astype(BF16)
        yield

        def proj(off):
            return _dot(h_s[...], win_ref[:, off:off + 4 * LANES])

        def proj_t(row0, n_rows):
            return _dot_nt(wft_ref[row0:row0 + n_rows, :], h_s[...])

        def rot(v, rs):
            return (v * cos_ref[rs, :] + pltpu.roll(v, LANES - 32, 1) * sina_ref[rs, :]
                    + pltpu.roll(v, 32, 1) * sinb_ref[rs, :])

        def head_rms(xt):
            ms = jnp.mean(xt * xt, axis=0, keepdims=True)
            return xt * lax.rsqrt(ms + NORM_EPS)

        tok_blocks = [slice(c0, c0 + LANES) for c0 in range(0, SEQ_TILE, LANES)]

        def post_rqk(r):
            for r0, rs in _row_blocks(SEQ_TILE):
                c, cr = r0 // CHUNK, r0 % CHUNK
                cs = slice(cr, cr + ROW_BLOCK)
                for p in range(2):
                    q = rot(r[rs, p * LANES:(p + 1) * LANES], rs)
                    rq_s[p, rs, :] = q.astype(BF16)
                    rqx_s[p, 0, rs, :] = (q * xi_ref[p, 0, cs, :]).astype(BF16)
                    rqx_s[p, 1, rs, :] = (q * xi_ref[p, 1, cs, :]).astype(BF16)
                    k = rot(r[rs, (2 + p) * LANES:(3 + p) * LANES], rs)
                    rk_s[p, c, cr:cr + ROW_BLOCK, :] = jnp.where(lo, k, 0.0).astype(BF16)
                    rk_s[p, c, CHUNK + cr:CHUNK + cr + ROW_BLOCK, :] = jnp.where(lo, 0.0, k).astype(BF16)
                    rkz_s[p, rs, :] = (k * zeta_ref[p, cs, :]).astype(BF16)

        def post_rv(r):
            for r0, rs in _row_blocks(SEQ_TILE):
                rv_s[rs, :] = r[rs, :].astype(BF16)

        def post_rg(r):
            for r0, rs in _row_blocks(SEQ_TILE):
                gate_s[rs, :] = _silu(r[rs, :])

        def post_qt(rt):
            for hd in range(SWA_HEADS):
                hs = slice(hd * SWA_HEAD_DIM, (hd + 1) * SWA_HEAD_DIM)
                for ts in tok_blocks:
                    qt_s[hs, ts] = head_rms(rt[hs, ts]).astype(BF16)

        def post_kvt(rt):
            qk_w = qnw_ref[...] * knw_ref[...] * (SWA_HEAD_DIM ** -0.5)
            for ts in tok_blocks:
                knt = jnp.concatenate(
                    [head_rms(rt[g * SWA_HEAD_DIM:(g + 1) * SWA_HEAD_DIM, ts]) for g in range(SWA_KV_HEADS)],
                    axis=0)
                kband_s[CHUNK + ts.start:CHUNK + ts.stop, :] = (knt.T * qk_w).astype(BF16)
                vt_s[:, CHUNK + ts.start:CHUNK + ts.stop] = rt[LANES:2 * LANES, ts].astype(BF16)

        def post_gt(rt):
            for r0 in range(0, SWA_HEADS * SWA_HEAD_DIM, ROW_BLOCK):
                for ts in tok_blocks:
                    gatet_s[r0:r0 + ROW_BLOCK, ts] = _silu(rt[r0:r0 + ROW_BLOCK, ts])

        groups = [
            (functools.partial(proj, OFF_RQ), post_rqk),
            (functools.partial(proj_t, FT_Q, SWA_HEADS * SWA_HEAD_DIM), post_qt),
            (functools.partial(proj, OFF_RG), post_rg),
            (functools.partial(proj_t, FT_G, SWA_HEADS * SWA_HEAD_DIM), post_gt),
            (functools.partial(proj_t, FT_K, 2 * LANES), post_kvt),
            (functools.partial(proj, OFF_RV), post_rv),
        ]
        pending = None
        for matmul, post in groups:
            result = matmul()
            yield
            if pending is not None:
                pending[0](pending[1])
                yield
            pending = (post, result)
        pending[0](pending[1])
        yield

    def mix(slot, next_slot):
        rq_s, rqx_s, rk_s, rkz_s, rv_s, gate_s, qt_s, kband_s, vt_s, gatet_s = slot
        key = lax.broadcasted_iota(jnp.int32, (CHUNK, CHUNK), 0)
        qry = lax.broadcasted_iota(jnp.int32, (CHUNK, CHUNK), 1)
        tri_t = key <= qry
        zero_q = jnp.zeros((SWA_HEAD_DIM, SWA_GROUP * CHUNK), BF16)

        def out_proj(c_hi):
            pair = slice((c_hi - 1) * CHUNK, (c_hi + 1) * CHUNK)
            o_ref[0, pair, :] = (xres_ref[0, pair, :]
                                 + _dot(mixed_s[pair, :], wout_ref[0:RET_HEADS * RET_DV, :])
                                 + _dot_tn(mixedt_s[:, pair], wout_ref[RET_HEADS * RET_DV:D_MIX, :]))

        chunk_rows = [slice(c * CHUNK, (c + 1) * CHUNK) for c in range(N_CHUNKS)]
        chunk_band = [slice(c * CHUNK, (c + 2) * CHUNK) for c in range(N_CHUNKS)]

        scores, incs, logits_t = {}, {}, []
        for c, rows in enumerate(chunk_rows):
            for p in range(2):
                s2 = _dot_nt(rq_s[p, rows, :], rk_s[p, c])
                scores[c, p] = jnp.concatenate([(s2[bs, :] * dec_ref[p, bs, :]).astype(BF16)
                                                for _, bs in _row_blocks(CHUNK)], axis=0)
                incs[c, p] = _dot_tn(rkz_s[p, rows, :], rv_s[rows, p * 2 * RET_DV:(p + 1) * 2 * RET_DV])
            qts = [jnp.concatenate([qt_s[(SWA_GROUP * g + hh) * SWA_HEAD_DIM:(SWA_GROUP * g + hh + 1) * SWA_HEAD_DIM,
                                         rows] for hh in range(SWA_GROUP)], axis=1)
                   for g in range(SWA_KV_HEADS)]
            rhs = jnp.concatenate([jnp.concatenate([qts[0], zero_q], axis=1),
                                   jnp.concatenate([zero_q, qts[1]], axis=1)], axis=0)
            logits_t.append(_dot(kband_s[chunk_band[c], :], rhs))
        yield

        ret_outs = {}
        for p in range(2):
            full = state_s[p] * keep_state
            for c, rows in enumerate(chunk_rows):
                full_b = full.astype(BF16)
                v2 = rv_s[rows, p * 2 * RET_DV:(p + 1) * 2 * RET_DV]
                s2 = scores[c, p]
                for j in range(2):
                    ret_outs[c, 2 * p + j] = _dot(
                        jnp.concatenate([s2[:, j * CHUNK:(j + 1) * CHUNK], rqx_s[p, j, rows, :]], axis=1),
                        jnp.concatenate([v2[:, j * RET_DV:(j + 1) * RET_DV],
                                         full_b[:, j * RET_DV:(j + 1) * RET_DV]], axis=0))
                full = full * cd_ref[p] + incs[c, p]
            state_s[p] = full
        yield

        for c in range(N_CHUNKS):
            rows, band, lt = chunk_rows[c], chunk_band[c], logits_t[c]

            for hd in range(RET_HEADS):
                o = ret_outs[c, hd]
                cols = slice(hd * RET_DV, (hd + 1) * RET_DV)
                for r0, bs in _row_blocks(CHUNK):
                    ob = o[bs, :]
                    rs = slice(c * CHUNK + r0, c * CHUNK + r0 + ROW_BLOCK)
                    mu = jnp.mean(ob, axis=-1, keepdims=True)
                    d = ob - mu
                    var = jnp.mean(d * d, axis=-1, keepdims=True)
                    y = d * lax.rsqrt(var + GN_EPS) * retw_ref[:, cols]
                    mixed_s[rs, cols] = (y * gate_s[rs, cols]).astype(BF16)
            invs = []
            for hd in range(SWA_HEADS):
                g, hh = hd // SWA_GROUP, hd % SWA_GROUP
                hq = slice(hd * CHUNK, (hd + 1) * CHUNK)
                sink = sinks_ref[hd]
                bias = bias_s[hd + SWA_HEADS * first_i] if c == 0 else bias_s[hd]
                lg = jnp.where(tri_t, lt[CHUNK:2 * CHUNK, hq], lt[0:CHUNK, hq]) + bias
                m = jnp.maximum(jnp.max(lg, axis=0, keepdims=True), sink)
                e = jnp.exp(lg - m)
                den = jnp.sum(e, axis=0, keepdims=True) + jnp.exp(sink - m)
                invs.append(1.0 / den)
                pq = slice(hh * CHUNK, (hh + 1) * CHUNK)
                pt_s[c, g, 0:CHUNK, pq] = jnp.where(tri_t, 0.0, e).astype(BF16)
                pt_s[c, g, CHUNK:2 * CHUNK, pq] = jnp.where(tri_t, e, 0.0).astype(BF16)
            yield

            for g in range(SWA_KV_HEADS):
                ot = _dot(vt_s[g * SWA_HEAD_DIM:(g + 1) * SWA_HEAD_DIM, band], pt_s[c, g])
                for hh in range(SWA_GROUP):
                    hd = SWA_GROUP * g + hh
                    hs = slice(hd * SWA_HEAD_DIM, (hd + 1) * SWA_HEAD_DIM)
                    mixedt_s[hs, rows] = (ot[:, hh * CHUNK:(hh + 1) * CHUNK] * invs[hd]
                                          * gatet_s[hs, rows]).astype(BF16)
            yield
            if c % 2 == 1:
                out_proj(c)
                yield

        last = slice(SEQ_TILE, SEQ_TILE + CHUNK)
        next_slot[SLOT_KBAND][0:CHUNK, :] = kband_s[last, :]
        next_slot[SLOT_VT][:, 0:CHUNK] = vt_s[:, last]

    def run(parity):
        streams = [project(slots[parity]), mix(slots[1 - parity], slots[parity])]
        for which in MIX_ORDER + (0,) * len(MIX_ORDER) + (1,) * len(MIX_ORDER):
            next(streams[which], None)

    for parity in range(2):
        pl.when(step % 2 == parity)(functools.partial(run, parity))


def kernel(x, norm_w, w_in, ret_norm_w, q_norm_w, k_norm_w, sinks, rel_bias, w_out):
    batch, seq, d_model = x.shape
    assert d_model == D_MODEL and seq % SEQ_TILE == 0
    assert w_in.shape == (D_MODEL, D_IN) and w_out.shape == (D_MIX, D_MODEL)
    tb = _constant_tables(seq)
    tiles_per_seq = seq // SEQ_TILE
    n_tiles = batch * tiles_per_seq

    def proj_tile(s):
        return jnp.minimum(s, n_tiles - 1)

    def mix_tile(s):
        return jnp.maximum(s - 1, 0)

    const2 = lambda s: (0, 0)
    const3 = lambda s: (0, 0, 0)
    const4 = lambda s: (0, 0, 0, 0)
    pos_spec = pl.BlockSpec((SEQ_TILE, LANES), lambda s: (proj_tile(s) % tiles_per_seq, 0))
    smem = pl.BlockSpec(memory_space=pltpu.SMEM)
    x_block = (1, SEQ_TILE, D_MODEL)
    mix_map = lambda s: (mix_tile(s) // tiles_per_seq, mix_tile(s) % tiles_per_seq, 0)

    in_specs = [
        pl.BlockSpec(x_block, lambda s: (proj_tile(s) // tiles_per_seq, proj_tile(s) % tiles_per_seq, 0)),
        pl.BlockSpec(x_block, mix_map),
        pl.BlockSpec((1, D_MODEL), const2),
        pl.BlockSpec(memory_space=pl.ANY),
        pl.BlockSpec(memory_space=pl.ANY),
        pos_spec, pos_spec, pos_spec,
        pl.BlockSpec((2, CHUNK, 2 * CHUNK), const3),
        pl.BlockSpec((2, 2, CHUNK, LANES), const4),
        pl.BlockSpec((2, CHUNK, LANES), const3),
        pl.BlockSpec((2, 1, 2 * RET_DV), const3),
        pl.BlockSpec((1, RET_HEADS * RET_DV), const2),
        pl.BlockSpec((1, LANES), const2),
        pl.BlockSpec((1, LANES), const2),
        pl.BlockSpec((CHUNK, CHUNK), const2),
        smem, smem,
    ]
    slot_bufs = [
        pltpu.VMEM((2, SEQ_TILE, LANES), BF16),
        pltpu.VMEM((2, 2, SEQ_TILE, LANES), BF16),
        pltpu.VMEM((2, N_CHUNKS, 2 * CHUNK, LANES), BF16),
        pltpu.VMEM((2, SEQ_TILE, LANES), BF16),
        pltpu.VMEM((SEQ_TILE, RET_HEADS * RET_DV), BF16),
        pltpu.VMEM((SEQ_TILE, RET_HEADS * RET_DV), F32),
        pltpu.VMEM((SWA_HEADS * SWA_HEAD_DIM, SEQ_TILE), BF16),
        pltpu.VMEM((SEQ_TILE + CHUNK, LANES), BF16),
        pltpu.VMEM((SWA_KV_HEADS * SWA_HEAD_DIM, SEQ_TILE + CHUNK), BF16),
        pltpu.VMEM((SWA_HEADS * SWA_HEAD_DIM, SEQ_TILE), F32),
    ]
    assert len(slot_bufs) == N_SLOT_BUFS
    scratch = [pltpu.VMEM((SEQ_TILE, D_MODEL), BF16)] + slot_bufs + slot_bufs + [
        pltpu.VMEM((N_CHUNKS, SWA_KV_HEADS, 2 * CHUNK, SWA_GROUP * CHUNK), BF16),
        pltpu.VMEM((SEQ_TILE, RET_HEADS * RET_DV), BF16),
        pltpu.VMEM((SWA_HEADS * SWA_HEAD_DIM, SEQ_TILE), BF16),
        pltpu.VMEM((2, 2 * RET_DK, 2 * RET_DV), F32),
        pltpu.VMEM((2 * SWA_HEADS, CHUNK, CHUNK), F32),
        pltpu.VMEM((D_MODEL, D_TOK), BF16),
        pltpu.VMEM((D_FT, D_MODEL), BF16),
        pltpu.VMEM((D_MIX, D_MODEL), BF16),
        pltpu.VMEM((2, STAGE_ROWS, D_IN), F32),
        pltpu.VMEM((2, STAGE_ROWS, D_MODEL), F32),
        pltpu.SemaphoreType.DMA((4,)),
    ]
    return pl.pallas_call(
        functools.partial(_layer_kernel, tiles_per_seq),
        grid=(n_tiles + 1,),
        in_specs=in_specs,
        out_specs=pl.BlockSpec(x_block, mix_map),
        out_shape=jax.ShapeDtypeStruct(x.shape, x.dtype),
        scratch_shapes=scratch,
        compiler_params=pltpu.CompilerParams(
            dimension_semantics=("arbitrary",),
            vmem_limit_bytes=VMEM_LIMIT_BYTES),
    )(x, x, norm_w.reshape(1, D_MODEL), w_in, w_out,
      tb["cos"], tb["sin_a"], tb["sin_b"], tb["dec"], tb["xi"], tb["zeta"], tb["cd"],
      ret_norm_w.reshape(1, -1), jnp.tile(q_norm_w, SWA_KV_HEADS).reshape(1, -1),
      jnp.tile(k_norm_w, SWA_KV_HEADS).reshape(1, -1), tb["bucket"],
      rel_bias.astype(F32), sinks.astype(F32))
```

```python
import functools
import math

import numpy as np
import jax
import jax.numpy as jnp
from jax import lax
from jax.experimental import pallas as pl
from jax.experimental.pallas import tpu as pltpu

D_MODEL = 1024
RET_HEADS = 4
RET_DK = 64
RET_DV = 128
CHUNK = 128
RET_ROPE_BASE = 10000.0
SWA_HEADS = 8
SWA_KV_HEADS = 2
SWA_HEAD_DIM = 64
SWA_GROUP = SWA_HEADS // SWA_KV_HEADS
NUM_BUCKETS = 32
MAX_DISTANCE = 128
NORM_EPS = 1e-6
GN_EPS = 1e-5
NEG_INF = -1e30

OFF_RQ, OFF_RK, OFF_RV, OFF_RG = 0, 256, 512, 1024
D_TOK = 1536
FT_Q, FT_K, FT_G = 0, 512, 768
D_FT = 1280
D_IN = D_TOK + D_FT
D_MIX = 1024

LANES = 128
SEQ_TILE = 512
N_CHUNKS = SEQ_TILE // CHUNK
ROW_BLOCK = 32
VMEM_LIMIT_BYTES = 56 * 1024 * 1024
MIX_ORDER = (1, 1, 0, 1, 1, 0, 0, 1, 1, 1, 0, 0, 0, 1, 1, 0, 0, 1, 1, 0, 0, 1, 0, 0, 0)
NORM_AFTER_GROUP = 2
N_SLOT_BUFS = 10
STAGE_ROWS = 128
SLOT_KBAND, SLOT_VT = 7, 8

BF16 = jnp.bfloat16
F32 = jnp.float32


def _t5_bucket_np(n):
    max_exact = NUM_BUCKETS // 2
    nf = np.maximum(n, 1).astype(np.float64)
    large = max_exact + (np.log(nf / max_exact) / math.log(MAX_DISTANCE / max_exact)
                         * (NUM_BUCKETS - max_exact)).astype(np.int32)
    large = np.minimum(large, NUM_BUCKETS - 1)
    return np.where(n < max_exact, n, large).astype(np.int32)


def _constant_tables(seq):
    half = RET_DK // 2
    inv_freq = RET_ROPE_BASE ** (-np.arange(half, dtype=np.float64) / half)
    ang = np.arange(seq, dtype=np.float64)[:, None] * inv_freq[None, :]
    cos, sin = np.cos(ang), np.sin(ang)
    zeros = np.zeros_like(sin)
    cos_t = np.tile(np.concatenate([cos, cos], axis=1), (1, 2))
    sin_a = np.tile(np.concatenate([-sin, zeros], axis=1), (1, 2))
    sin_b = np.tile(np.concatenate([zeros, sin], axis=1), (1, 2))

    gamma = 1.0 - np.exp2(-5.0 - np.arange(RET_HEADS, dtype=np.float64))
    log_g = np.log(gamma)
    i = np.arange(CHUNK, dtype=np.float64)
    diff = i[:, None] - i[None, :]
    k_scale = RET_DK ** -0.5
    decay = np.where(diff >= 0, np.exp(log_g[:, None, None] * np.maximum(diff, 0.0)), 0.0) * k_scale
    dec = np.stack([np.concatenate([decay[2 * p], decay[2 * p + 1]], axis=1) for p in range(2)])
    xi = np.exp(log_g[:, None] * (i + 1.0))
    zeta = np.exp(log_g[:, None] * (CHUNK - 1.0 - i)) * k_scale

    def per_lane(t):
        return np.stack([np.concatenate([np.repeat(t[2 * p][:, None], RET_DK, 1),
                                         np.repeat(t[2 * p + 1][:, None], RET_DK, 1)], axis=1)
                         for p in range(2)])

    lo = (np.arange(LANES) < RET_DK)[None, None, :]
    xi_l = per_lane(xi)
    xi_m = np.stack([np.where(lo, xi_l, 0.0), np.where(lo, 0.0, xi_l)], axis=1)

    chunk_decay = np.exp(log_g * CHUNK)
    cd = np.stack([np.concatenate([np.full((1, RET_DV), chunk_decay[2 * p]),
                                   np.full((1, RET_DV), chunk_decay[2 * p + 1])], axis=1)
                   for p in range(2)])

    j = np.arange(CHUNK)[:, None]
    q = np.arange(CHUNK)[None, :]
    dist = np.where(j <= q, q - j, q + CHUNK - j)
    bucket = _t5_bucket_np(dist)

    f = lambda a: jnp.asarray(a, dtype=F32)
    return dict(cos=f(cos_t), sin_a=f(sin_a), sin_b=f(sin_b), dec=f(dec), xi=f(xi_m),
                zeta=f(per_lane(zeta)), cd=f(cd), bucket=jnp.asarray(bucket, dtype=jnp.int32))


def _dot(a, b):
    return jnp.dot(a, b, preferred_element_type=F32)


def _dot_nt(a, b):
    return lax.dot_general(a, b, (((1,), (1,)), ((), ())), preferred_element_type=F32)


def _dot_tn(a, b):
    return lax.dot_general(a, b, (((0,), (0,)), ((), ())), preferred_element_type=F32)


def _silu(g):
    hg = 0.5 * g
    return hg + hg * jnp.tanh(hg)


def _row_blocks(n_rows):
    return [(r0, slice(r0, r0 + ROW_BLOCK)) for r0 in range(0, n_rows, ROW_BLOCK)]


def _layer_kernel(tiles_per_seq,
                  xnext_ref, xres_ref, x_hbm, nw_ref, win_hbm, wout_hbm, cos_ref, sina_ref, sinb_ref, dec_ref,
                  xi_ref, zeta_ref, cd_ref, retw_ref, qnw_ref, knw_ref, bkt_ref, relb_ref,
                  sinks_ref, o_ref, h0_s, h1_s, *scratch):
    h_bufs = (h0_s, h1_s)
    slots = (scratch[0:N_SLOT_BUFS], scratch[N_SLOT_BUFS:2 * N_SLOT_BUFS])
    (pt_s, mixed_s, mixedt_s, state_s, bias_s,
     win_ref, wft_ref, wout_ref, stage_in, stage_out, stage_sem) = scratch[2 * N_SLOT_BUFS:]
    step = pl.program_id(0)
    mix_tile = jnp.maximum(step - 1, 0)
    first = (mix_tile % tiles_per_seq == 0) | (step == 0)
    first_i = first.astype(jnp.int32)
    keep_state = jnp.where(first, 0.0, 1.0).astype(F32)

    def rms_norm_rows(xb):
        ms = jnp.mean(xb * xb, axis=-1, keepdims=True)
        return (xb * lax.rsqrt(ms + NORM_EPS) * nw_ref[...]).astype(BF16)

    @pl.when(step == 0)
    def _init():
        bk = bkt_ref[...]
        key = lax.broadcasted_iota(jnp.int32, (CHUNK, CHUNK), 0)
        qry = lax.broadcasted_iota(jnp.int32, (CHUNK, CHUNK), 1)
        for h in range(SWA_HEADS):
            acc = jnp.zeros((CHUNK, CHUNK), F32)
            for u in range(NUM_BUCKETS):
                acc = jnp.where(bk == u, relb_ref[u, h], acc)
            bias_s[h] = acc
            bias_s[SWA_HEADS + h] = jnp.where(key <= qry, acc, NEG_INF)
        state_s[...] = jnp.zeros_like(state_s)
        for ref in slots[1]:
            ref[...] = jnp.zeros(ref.shape, ref.dtype)

        def chunk_copies(hbm, stage, sem0, n_rows):
            return [pltpu.make_async_copy(hbm.at[pl.ds(i * STAGE_ROWS, STAGE_ROWS), :],
                                          stage.at[i % 2], stage_sem.at[sem0 + i % 2])
                    for i in range(n_rows // STAGE_ROWS)]

        def staged(copies, consume):
            copies[0].start()
            for i, cp in enumerate(copies):
                if i + 1 < len(copies):
                    copies[i + 1].start()
                cp.wait()
                consume(i)

        def consume_in(i):
            rows = slice(i * STAGE_ROWS, (i + 1) * STAGE_ROWS)
            win_ref[rows, :] = stage_in[i % 2, :, 0:D_TOK].astype(BF16)
            for f0 in range(0, D_FT, LANES):
                wft_ref[f0:f0 + LANES, rows] = stage_in[i % 2, :, D_TOK + f0:D_TOK + f0 + LANES].T.astype(BF16)

        def consume_out(i):
            rows = slice(i * STAGE_ROWS, (i + 1) * STAGE_ROWS)
            wout_ref[rows, :] = stage_out[i % 2].astype(BF16)

        def consume_x0(i):
            for r0, rs in _row_blocks(STAGE_ROWS):
                h0_s[i * STAGE_ROWS + r0:i * STAGE_ROWS + r0 + ROW_BLOCK, :] = rms_norm_rows(stage_out[i % 2, rs, :])

        staged(chunk_copies(win_hbm, stage_in, 0, D_MODEL), consume_in)
        staged(chunk_copies(wout_hbm, stage_out, 2, D_MIX), consume_out)
        staged(chunk_copies(x_hbm.at[0], stage_out, 2, SEQ_TILE), consume_x0)

    lo = lax.broadcasted_iota(jnp.int32, (ROW_BLOCK, LANES), 1) < RET_DK

    def project(slot, h_s, h_next_s):
        rq_s, rqx_s, rk_s, rkz_s, rv_s, gate_s, qt_s, kband_s, vt_s, gatet_s = slot

        def proj(off):
            return _dot(h_s[...], win_ref[:, off:off + 4 * LANES])

        def proj_t(row0, n_rows):
            return _dot_nt(wft_ref[row0:row0 + n_rows, :], h_s[...])

        def rot(v, rs):
            return (v * cos_ref[rs, :] + pltpu.roll(v, LANES - 32, 1) * sina_ref[rs, :]
                    + pltpu.roll(v, 32, 1) * sinb_ref[rs, :])

        def head_rms(xt):
            ms = jnp.mean(xt * xt, axis=0, keepdims=True)
            return xt * lax.rsqrt(ms + NORM_EPS)

        tok_blocks = [slice(c0, c0 + LANES) for c0 in range(0, SEQ_TILE, LANES)]

        def post_rqk(r):
            for r0, rs in _row_blocks(SEQ_TILE):
                c, cr = r0 // CHUNK, r0 % CHUNK
                cs = slice(cr, cr + ROW_BLOCK)
                for p in range(2):
                    q = rot(r[rs, p * LANES:(p + 1) * LANES], rs)
                    rq_s[p, rs, :] = q.astype(BF16)
                    rqx_s[p, 0, rs, :] = (q * xi_ref[p, 0, cs, :]).astype(BF16)
                    rqx_s[p, 1, rs, :] = (q * xi_ref[p, 1, cs, :]).astype(BF16)
                    k = rot(r[rs, (2 + p) * LANES:(3 + p) * LANES], rs)
                    rk_s[p, c, cr:cr + ROW_BLOCK, :] = jnp.where(lo, k, 0.0).astype(BF16)
                    rk_s[p, c, CHUNK + cr:CHUNK + cr + ROW_BLOCK, :] = jnp.where(lo, 0.0, k).astype(BF16)
                    rkz_s[p, rs, :] = (k * zeta_ref[p, cs, :]).astype(BF16)

        def post_rv(r):
            for r0, rs in _row_blocks(SEQ_TILE):
                rv_s[rs, :] = r[rs, :].astype(BF16)

        def post_rg(r):
            for r0, rs in _row_blocks(SEQ_TILE):
                gate_s[rs, :] = _silu(r[rs, :])

        def post_qt(rt):
            for hd in range(SWA_HEADS):
                hs = slice(hd * SWA_HEAD_DIM, (hd + 1) * SWA_HEAD_DIM)
                for ts in tok_blocks:
                    qt_s[hs, ts] = head_rms(rt[hs, ts]).astype(BF16)

        def post_kvt(rt):
            qk_w = qnw_ref[...] * knw_ref[...] * (SWA_HEAD_DIM ** -0.5)
            for ts in tok_blocks:
                knt = jnp.concatenate(
                    [head_rms(rt[g * SWA_HEAD_DIM:(g + 1) * SWA_HEAD_DIM, ts]) for g in range(SWA_KV_HEADS)],
                    axis=0)
                kband_s[CHUNK + ts.start:CHUNK + ts.stop, :] = (knt.T * qk_w).astype(BF16)
                vt_s[:, CHUNK + ts.start:CHUNK + ts.stop] = rt[LANES:2 * LANES, ts].astype(BF16)

        def post_gt(rt):
            for r0 in range(0, SWA_HEADS * SWA_HEAD_DIM, ROW_BLOCK):
                for ts in tok_blocks:
                    gatet_s[r0:r0 + ROW_BLOCK, ts] = _silu(rt[r0:r0 + ROW_BLOCK, ts])

        groups = [
            (functools.partial(proj, OFF_RQ), post_rqk),
            (functools.partial(proj_t, FT_Q, SWA_HEADS * SWA_HEAD_DIM), post_qt),
            (functools.partial(proj, OFF_RG), post_rg),
            (functools.partial(proj_t, FT_G, SWA_HEADS * SWA_HEAD_DIM), post_gt),
            (functools.partial(proj_t, FT_K, 2 * LANES), post_kvt),
            (functools.partial(proj, OFF_RV), post_rv),
        ]
        pending = None
        for i, (matmul, post) in enumerate(groups):
            result = matmul()
            yield
            if pending is not None:
                pending[0](pending[1])
                yield
            pending = (post, result)
            if i == NORM_AFTER_GROUP:
                for r0, rs in _row_blocks(SEQ_TILE):
                    h_next_s[rs, :] = rms_norm_rows(xnext_ref[0, rs, :])
                yield
        pending[0](pending[1])
        yield

    def mix(slot, next_slot):
        rq_s, rqx_s, rk_s, rkz_s, rv_s, gate_s, qt_s, kband_s, vt_s, gatet_s = slot
        key = lax.broadcasted_iota(jnp.int32, (CHUNK, CHUNK), 0)
        qry = lax.broadcasted_iota(jnp.int32, (CHUNK, CHUNK), 1)
        tri_t = key <= qry
        zero_q = jnp.zeros((SWA_HEAD_DIM, SWA_GROUP * CHUNK), BF16)

        def out_proj(c_hi):
            pair = slice((c_hi - 1) * CHUNK, (c_hi + 1) * CHUNK)
            o_ref[0, pair, :] = (xres_ref[0, pair, :]
                                 + _dot(mixed_s[pair, :], wout_ref[0:RET_HEADS * RET_DV, :])
                                 + _dot_tn(mixedt_s[:, pair], wout_ref[RET_HEADS * RET_DV:D_MIX, :]))

        chunk_rows = [slice(c * CHUNK, (c + 1) * CHUNK) for c in range(N_CHUNKS)]
        chunk_band = [slice(c * CHUNK, (c + 2) * CHUNK) for c in range(N_CHUNKS)]

        scores, incs, logits_t = {}, {}, []
        for c, rows in enumerate(chunk_rows):
            for p in range(2):
                s2 = _dot_nt(rq_s[p, rows, :], rk_s[p, c])
                scores[c, p] = jnp.concatenate([(s2[bs, :] * dec_ref[p, bs, :]).astype(BF16)
                                                for _, bs in _row_blocks(CHUNK)], axis=0)
                incs[c, p] = _dot_tn(rkz_s[p, rows, :], rv_s[rows, p * 2 * RET_DV:(p + 1) * 2 * RET_DV])
            qts = [jnp.concatenate([qt_s[(SWA_GROUP * g + hh) * SWA_HEAD_DIM:(SWA_GROUP * g + hh + 1) * SWA_HEAD_DIM,
                                         rows] for hh in range(SWA_GROUP)], axis=1)
                   for g in range(SWA_KV_HEADS)]
            rhs = jnp.concatenate([jnp.concatenate([qts[0], zero_q], axis=1),
                                   jnp.concatenate([zero_q, qts[1]], axis=1)], axis=0)
            logits_t.append(_dot(kband_s[chunk_band[c], :], rhs))
        yield

        ret_outs = {}
        for p in range(2):
            full = state_s[p] * keep_state
            for c, rows in enumerate(chunk_rows):
                full_b = full.astype(BF16)
                v2 = rv_s[rows, p * 2 * RET_DV:(p + 1) * 2 * RET_DV]
                s2 = scores[c, p]
                for j in range(2):
                    ret_outs[c, 2 * p + j] = _dot(
                        jnp.concatenate([s2[:, j * CHUNK:(j + 1) * CHUNK], rqx_s[p, j, rows, :]], axis=1),
                        jnp.concatenate([v2[:, j * RET_DV:(j + 1) * RET_DV],
                                         full_b[:, j * RET_DV:(j + 1) * RET_DV]], axis=0))
                full = full * cd_ref[p] + incs[c, p]
            state_s[p] = full
        yield

        for c in range(N_CHUNKS):
            rows, band, lt = chunk_rows[c], chunk_band[c], logits_t[c]

            for hd in range(RET_HEADS):
                o = ret_outs[c, hd]
                cols = slice(hd * RET_DV, (hd + 1) * RET_DV)
                for r0, bs in _row_blocks(CHUNK):
                    ob = o[bs, :]
                    rs = slice(c * CHUNK + r0, c * CHUNK + r0 + ROW_BLOCK)
                    mu = jnp.mean(ob, axis=-1, keepdims=True)
                    d = ob - mu
                    var = jnp.mean(d * d, axis=-1, keepdims=True)
                    y = d * lax.rsqrt(var + GN_EPS) * retw_ref[:, cols]
                    mixed_s[rs, cols] = (y * gate_s[rs, cols]).astype(BF16)
            invs = []
            for hd in range(SWA_HEADS):
                g, hh = hd // SWA_GROUP, hd % SWA_GROUP
                hq = slice(hd * CHUNK, (hd + 1) * CHUNK)
                sink = sinks_ref[hd]
                bias = bias_s[hd + SWA_HEADS * first_i] if c == 0 else bias_s[hd]
                lg = jnp.where(tri_t, lt[CHUNK:2 * CHUNK, hq], lt[0:CHUNK, hq]) + bias
                m = jnp.maximum(jnp.max(lg, axis=0, keepdims=True), sink)
                e = jnp.exp(lg - m)
                den = jnp.sum(e, axis=0, keepdims=True) + jnp.exp(sink - m)
                invs.append(1.0 / den)
                pq = slice(hh * CHUNK, (hh + 1) * CHUNK)
                pt_s[c, g, 0:CHUNK, pq] = jnp.where(tri_t, 0.0, e).astype(BF16)
                pt_s[c, g, CHUNK:2 * CHUNK, pq] = jnp.where(tri_t, e, 0.0).astype(BF16)
            yield

            for g in range(SWA_KV_HEADS):
                ot = _dot(vt_s[g * SWA_HEAD_DIM:(g + 1) * SWA_HEAD_DIM, band], pt_s[c, g])
                for hh in range(SWA_GROUP):
                    hd = SWA_GROUP * g + hh
                    hs = slice(hd * SWA_HEAD_DIM, (hd + 1) * SWA_HEAD_DIM)
                    mixedt_s[hs, rows] = (ot[:, hh * CHUNK:(hh + 1) * CHUNK] * invs[hd]
                                          * gatet_s[hs, rows]).astype(BF16)
            yield
            if c % 2 == 1:
                out_proj(c)
                yield

        last = slice(SEQ_TILE, SEQ_TILE + CHUNK)
        next_slot[SLOT_KBAND][0:CHUNK, :] = kband_s[last, :]
        next_slot[SLOT_VT][:, 0:CHUNK] = vt_s[:, last]

    def run(parity):
        streams = [project(slots[parity], h_bufs[parity], h_bufs[1 - parity]),
                   mix(slots[1 - parity], slots[parity])]
        for which in MIX_ORDER + (0,) * len(MIX_ORDER) + (1,) * len(MIX_ORDER):
            next(streams[which], None)

    for parity in range(2):
        pl.when(step % 2 == parity)(functools.partial(run, parity))


def kernel(x, norm_w, w_in, ret_norm_w, q_norm_w, k_norm_w, sinks, rel_bias, w_out):
    batch, seq, d_model = x.shape
    assert d_model == D_MODEL and seq % SEQ_TILE == 0
    assert w_in.shape == (D_MODEL, D_IN) and w_out.shape == (D_MIX, D_MODEL)
    tb = _constant_tables(seq)
    tiles_per_seq = seq // SEQ_TILE
    n_tiles = batch * tiles_per_seq

    def proj_tile(s):
        return jnp.minimum(s, n_tiles - 1)

    def mix_tile(s):
        return jnp.maximum(s - 1, 0)

    const2 = lambda s: (0, 0)
    const3 = lambda s: (0, 0, 0)
    const4 = lambda s: (0, 0, 0, 0)
    pos_spec = pl.BlockSpec((SEQ_TILE, LANES), lambda s: (proj_tile(s) % tiles_per_seq, 0))
    smem = pl.BlockSpec(memory_space=pltpu.SMEM)
    x_block = (1, SEQ_TILE, D_MODEL)
    mix_map = lambda s: (mix_tile(s) // tiles_per_seq, mix_tile(s) % tiles_per_seq, 0)

    in_specs = [
        pl.BlockSpec(x_block, lambda s: (proj_tile(s + 1) // tiles_per_seq, proj_tile(s + 1) % tiles_per_seq, 0)),
        pl.BlockSpec(x_block, mix_map),
        pl.BlockSpec(memory_space=pl.ANY),
        pl.BlockSpec((1, D_MODEL), const2),
        pl.BlockSpec(memory_space=pl.ANY),
        pl.BlockSpec(memory_space=pl.ANY),
        pos_spec, pos_spec, pos_spec,
        pl.BlockSpec((2, CHUNK, 2 * CHUNK), const3),
        pl.BlockSpec((2, 2, CHUNK, LANES), const4),
        pl.BlockSpec((2, CHUNK, LANES), const3),
        pl.BlockSpec((2, 1, 2 * RET_DV), const3),
        pl.BlockSpec((1, RET_HEADS * RET_DV), const2),
        pl.BlockSpec((1, LANES), const2),
        pl.BlockSpec((1, LANES), const2),
        pl.BlockSpec((CHUNK, CHUNK), const2),
        smem, smem,
    ]
    slot_bufs = [
        pltpu.VMEM((2, SEQ_TILE, LANES), BF16),
        pltpu.VMEM((2, 2, SEQ_TILE, LANES), BF16),
        pltpu.VMEM((2, N_CHUNKS, 2 * CHUNK, LANES), BF16),
        pltpu.VMEM((2, SEQ_TILE, LANES), BF16),
        pltpu.VMEM((SEQ_TILE, RET_HEADS * RET_DV), BF16),
        pltpu.VMEM((SEQ_TILE, RET_HEADS * RET_DV), F32),
        pltpu.VMEM((SWA_HEADS * SWA_HEAD_DIM, SEQ_TILE), BF16),
        pltpu.VMEM((SEQ_TILE + CHUNK, LANES), BF16),
        pltpu.VMEM((SWA_KV_HEADS * SWA_HEAD_DIM, SEQ_TILE + CHUNK), BF16),
        pltpu.VMEM((SWA_HEADS * SWA_HEAD_DIM, SEQ_TILE), F32),
    ]
    assert len(slot_bufs) == N_SLOT_BUFS
    h_buf = pltpu.VMEM((SEQ_TILE, D_MODEL), BF16)
    scratch = [h_buf, h_buf] + slot_bufs + slot_bufs + [
        pltpu.VMEM((N_CHUNKS, SWA_KV_HEADS, 2 * CHUNK, SWA_GROUP * CHUNK), BF16),
        pltpu.VMEM((SEQ_TILE, RET_HEADS * RET_DV), BF16),
        pltpu.VMEM((SWA_HEADS * SWA_HEAD_DIM, SEQ_TILE), BF16),
        pltpu.VMEM((2, 2 * RET_DK, 2 * RET_DV), F32),
        pltpu.VMEM((2 * SWA_HEADS, CHUNK, CHUNK), F32),
        pltpu.VMEM((D_MODEL, D_TOK), BF16),
        pltpu.VMEM((D_FT, D_MODEL), BF16),
        pltpu.VMEM((D_MIX, D_MODEL), BF16),
        pltpu.VMEM((2, STAGE_ROWS, D_IN), F32),
        pltpu.VMEM((2, STAGE_ROWS, D_MODEL), F32),
        pltpu.SemaphoreType.DMA((4,)),
    ]
    return pl.pallas_call(
        functools.partial(_layer_kernel, tiles_per_seq),
        grid=(n_tiles + 1,),
        in_specs=in_specs,
        out_specs=pl.BlockSpec(x_block, mix_map),
        out_shape=jax.ShapeDtypeStruct(x.shape, x.dtype),
        scratch_shapes=scratch,
        compiler_params=pltpu.CompilerParams(
            dimension_semantics=("arbitrary",),
            vmem_limit_bytes=VMEM_LIMIT_BYTES),
    )(x, x, x, norm_w.reshape(1, D_MODEL), w_in, w_out,
      tb["cos"], tb["sin_a"], tb["sin_b"], tb["dec"], tb["xi"], tb["zeta"], tb["cd"],
      ret_norm_w.reshape(1, -1), jnp.tile(q_norm_w, SWA_KV_HEADS).reshape(1, -1),
      jnp.tile(k_norm_w, SWA_KV_HEADS).reshape(1, -1), tb["bucket"],
      rel_bias.astype(F32), sinks.astype(F32))
```

```python
import functools
import math

import numpy as np
import jax
import jax.numpy as jnp
from jax import lax
from jax.experimental import pallas as pl
from jax.experimental.pallas import tpu as pltpu

D_MODEL = 1024
RET_HEADS = 4
RET_DK = 64
RET_DV = 128
CHUNK = 128
RET_ROPE_BASE = 10000.0
SWA_HEADS = 8
SWA_KV_HEADS = 2
SWA_HEAD_DIM = 64
SWA_GROUP = SWA_HEADS // SWA_KV_HEADS
NUM_BUCKETS = 32
MAX_DISTANCE = 128
NORM_EPS = 1e-6
GN_EPS = 1e-5
NEG_INF = -1e30

OFF_RQ, OFF_RK, OFF_RV, OFF_RG = 0, 256, 512, 1024
D_TOK = 1536
FT_Q, FT_K, FT_G = 0, 512, 768
D_FT = 1280
D_IN = D_TOK + D_FT
D_MIX = 1024

LANES = 128
SEQ_TILE = 512
N_CHUNKS = SEQ_TILE // CHUNK
ROW_BLOCK = 32
VMEM_LIMIT_BYTES = 56 * 1024 * 1024
MIX_ORDER = (1, 1, 0, 1, 1, 0, 0, 1, 1, 1, 0, 0, 0, 1, 1, 0, 0, 1, 1, 0, 0, 1, 0, 0, 0)
NORM_AFTER_GROUP = 2
N_SLOT_BUFS = 10
STAGE_ROWS = 128
SLOT_KBAND, SLOT_VT = 7, 8

BF16 = jnp.bfloat16
F32 = jnp.float32


def _t5_bucket_np(n):
    max_exact = NUM_BUCKETS // 2
    nf = np.maximum(n, 1).astype(np.float64)
    large = max_exact + (np.log(nf / max_exact) / math.log(MAX_DISTANCE / max_exact)
                         * (NUM_BUCKETS - max_exact)).astype(np.int32)
    large = np.minimum(large, NUM_BUCKETS - 1)
    return np.where(n < max_exact, n, large).astype(np.int32)


def _constant_tables(seq):
    half = RET_DK // 2
    inv_freq = RET_ROPE_BASE ** (-np.arange(half, dtype=np.float64) / half)
    ang = np.arange(seq, dtype=np.float64)[:, None] * inv_freq[None, :]
    cos, sin = np.cos(ang), np.sin(ang)
    zeros = np.zeros_like(sin)
    cos_t = np.tile(np.concatenate([cos, cos], axis=1), (1, 2))
    sin_a = np.tile(np.concatenate([-sin, zeros], axis=1), (1, 2))
    sin_b = np.tile(np.concatenate([zeros, sin], axis=1), (1, 2))

    gamma = 1.0 - np.exp2(-5.0 - np.arange(RET_HEADS, dtype=np.float64))
    log_g = np.log(gamma)
    i = np.arange(CHUNK, dtype=np.float64)
    diff = i[:, None] - i[None, :]
    k_scale = RET_DK ** -0.5
    decay = np.where(diff >= 0, np.exp(log_g[:, None, None] * np.maximum(diff, 0.0)), 0.0) * k_scale
    dec = np.stack([np.concatenate([decay[2 * p], decay[2 * p + 1]], axis=1) for p in range(2)])
    xi = np.exp(log_g[:, None] * (i + 1.0))
    zeta = np.exp(log_g[:, None] * (CHUNK - 1.0 - i)) * k_scale

    def per_lane(t):
        return np.stack([np.concatenate([np.repeat(t[2 * p][:, None], RET_DK, 1),
                                         np.repeat(t[2 * p + 1][:, None], RET_DK, 1)], axis=1)
                         for p in range(2)])

    lo = (np.arange(LANES) < RET_DK)[None, None, :]
    xi_l = per_lane(xi)
    xi_m = np.stack([np.where(lo, xi_l, 0.0), np.where(lo, 0.0, xi_l)], axis=1)

    chunk_decay = np.exp(log_g * CHUNK)
    cd = np.stack([np.concatenate([np.full((1, RET_DV), chunk_decay[2 * p]),
                                   np.full((1, RET_DV), chunk_decay[2 * p + 1])], axis=1)
                   for p in range(2)])

    j = np.arange(CHUNK)[:, None]
    q = np.arange(CHUNK)[None, :]
    dist = np.where(j <= q, q - j, q + CHUNK - j)
    bucket = _t5_bucket_np(dist)

    f = lambda a: jnp.asarray(a, dtype=F32)
    return dict(cos=f(cos_t), sin_a=f(sin_a), sin_b=f(sin_b), dec=f(dec), xi=f(xi_m),
                zeta=f(per_lane(zeta)), cd=f(cd), bucket=jnp.asarray(bucket, dtype=jnp.int32))


def _dot(a, b):
    return jnp.dot(a, b, preferred_element_type=F32)


def _dot_nt(a, b):
    return lax.dot_general(a, b, (((1,), (1,)), ((), ())), preferred_element_type=F32)


def _dot_tn(a, b):
    return lax.dot_general(a, b, (((0,), (0,)), ((), ())), preferred_element_type=F32)


def _silu(g):
    hg = 0.5 * g
    return hg + hg * jnp.tanh(hg)


def _row_blocks(n_rows):
    return [(r0, slice(r0, r0 + ROW_BLOCK)) for r0 in range(0, n_rows, ROW_BLOCK)]


def _layer_kernel(tiles_per_seq, n_tiles,
                  xnext_ref, xres_ref, x_hbm, nw_ref, win_hbm, wout_hbm, cos_ref, sina_ref, sinb_ref, dec_ref,
                  xi_ref, zeta_ref, cd_ref, retw_ref, qnw_ref, knw_ref, bkt_ref, relb_ref,
                  sinks_ref, o_ref, h0_s, h1_s, *scratch):
    h_bufs = (h0_s, h1_s)
    slots = (scratch[0:N_SLOT_BUFS], scratch[N_SLOT_BUFS:2 * N_SLOT_BUFS])
    (pt_s, mixed_s, mixedt_s, state_s, bias_s,
     win_ref, wft_ref, wout_ref, stage_in, stage_out, stage_sem) = scratch[2 * N_SLOT_BUFS:]
    step = pl.program_id(0)
    mix_tile = jnp.maximum(step - 1, 0)
    first = mix_tile % tiles_per_seq == 0
    first_i = first.astype(jnp.int32)
    keep_state = jnp.where(first, 0.0, 1.0).astype(F32)

    def rms_norm_rows(xb):
        ms = jnp.mean(xb * xb, axis=-1, keepdims=True)
        return (xb * lax.rsqrt(ms + NORM_EPS) * nw_ref[...]).astype(BF16)

    @pl.when(step == 0)
    def _init():
        bk = bkt_ref[...]
        key = lax.broadcasted_iota(jnp.int32, (CHUNK, CHUNK), 0)
        qry = lax.broadcasted_iota(jnp.int32, (CHUNK, CHUNK), 1)
        for h in range(SWA_HEADS):
            acc = jnp.zeros((CHUNK, CHUNK), F32)
            for u in range(NUM_BUCKETS):
                acc = jnp.where(bk == u, relb_ref[u, h], acc)
            bias_s[h] = acc
            bias_s[SWA_HEADS + h] = jnp.where(key <= qry, acc, NEG_INF)
        state_s[...] = jnp.zeros_like(state_s)
        slots[0][SLOT_KBAND][0:CHUNK, :] = jnp.zeros((CHUNK, LANES), BF16)
        slots[0][SLOT_VT][:, 0:CHUNK] = jnp.zeros((SWA_KV_HEADS * SWA_HEAD_DIM, CHUNK), BF16)

        def chunk_copies(hbm, stage, sem0, n_rows):
            return [pltpu.make_async_copy(hbm.at[pl.ds(i * STAGE_ROWS, STAGE_ROWS), :],
                                          stage.at[i % 2], stage_sem.at[sem0 + i % 2])
                    for i in range(n_rows // STAGE_ROWS)]

        def staged(copies, consume):
            copies[0].start()
            for i, cp in enumerate(copies):
                if i + 1 < len(copies):
                    copies[i + 1].start()
                cp.wait()
                consume(i)

        def consume_in(i):
            rows = slice(i * STAGE_ROWS, (i + 1) * STAGE_ROWS)
            win_ref[rows, :] = stage_in[i % 2, :, 0:D_TOK].astype(BF16)
            for f0 in range(0, D_FT, LANES):
                wft_ref[f0:f0 + LANES, rows] = stage_in[i % 2, :, D_TOK + f0:D_TOK + f0 + LANES].T.astype(BF16)

        def consume_out(i):
            rows = slice(i * STAGE_ROWS, (i + 1) * STAGE_ROWS)
            wout_ref[rows, :] = stage_out[i % 2].astype(BF16)

        def consume_x0(i):
            for r0, rs in _row_blocks(STAGE_ROWS):
                h0_s[i * STAGE_ROWS + r0:i * STAGE_ROWS + r0 + ROW_BLOCK, :] = rms_norm_rows(stage_out[i % 2, rs, :])

        staged(chunk_copies(win_hbm, stage_in, 0, D_MODEL), consume_in)
        staged(chunk_copies(wout_hbm, stage_out, 2, D_MIX), consume_out)
        staged(chunk_copies(x_hbm.at[0], stage_out, 2, SEQ_TILE), consume_x0)

    lo = lax.broadcasted_iota(jnp.int32, (ROW_BLOCK, LANES), 1) < RET_DK

    def project(slot, h_s, h_next_s):
        rq_s, rqx_s, rk_s, rkz_s, rv_s, gate_s, qt_s, kband_s, vt_s, gatet_s = slot

        def proj(off):
            return _dot(h_s[...], win_ref[:, off:off + 4 * LANES])

        def proj_t(row0, n_rows):
            return _dot_nt(wft_ref[row0:row0 + n_rows, :], h_s[...])

        def rot(v, rs):
            return (v * cos_ref[rs, :] + pltpu.roll(v, LANES - 32, 1) * sina_ref[rs, :]
                    + pltpu.roll(v, 32, 1) * sinb_ref[rs, :])

        def head_rms(xt):
            ms = jnp.mean(xt * xt, axis=0, keepdims=True)
            return xt * lax.rsqrt(ms + NORM_EPS)

        tok_blocks = [slice(c0, c0 + LANES) for c0 in range(0, SEQ_TILE, LANES)]

        def post_rqk(r):
            for r0, rs in _row_blocks(SEQ_TILE):
                c, cr = r0 // CHUNK, r0 % CHUNK
                cs = slice(cr, cr + ROW_BLOCK)
                for p in range(2):
                    q = rot(r[rs, p * LANES:(p + 1) * LANES], rs)
                    rq_s[p, rs, :] = q.astype(BF16)
                    rqx_s[p, 0, rs, :] = (q * xi_ref[p, 0, cs, :]).astype(BF16)
                    rqx_s[p, 1, rs, :] = (q * xi_ref[p, 1, cs, :]).astype(BF16)
                    k = rot(r[rs, (2 + p) * LANES:(3 + p) * LANES], rs)
                    rk_s[p, c, cr:cr + ROW_BLOCK, :] = jnp.where(lo, k, 0.0).astype(BF16)
                    rk_s[p, c, CHUNK + cr:CHUNK + cr + ROW_BLOCK, :] = jnp.where(lo, 0.0, k).astype(BF16)
                    rkz_s[p, rs, :] = (k * zeta_ref[p, cs, :]).astype(BF16)

        def post_rv(r):
            for r0, rs in _row_blocks(SEQ_TILE):
                rv_s[rs, :] = r[rs, :].astype(BF16)

        def post_rg(r):
            for r0, rs in _row_blocks(SEQ_TILE):
                gate_s[rs, :] = _silu(r[rs, :])

        def post_qt(rt):
            for hd in range(SWA_HEADS):
                hs = slice(hd * SWA_HEAD_DIM, (hd + 1) * SWA_HEAD_DIM)
                for ts in tok_blocks:
                    qt_s[hs, ts] = head_rms(rt[hs, ts]).astype(BF16)

        def post_kvt(rt):
            qk_w = qnw_ref[...] * knw_ref[...] * (SWA_HEAD_DIM ** -0.5)
            for ts in tok_blocks:
                knt = jnp.concatenate(
                    [head_rms(rt[g * SWA_HEAD_DIM:(g + 1) * SWA_HEAD_DIM, ts]) for g in range(SWA_KV_HEADS)],
                    axis=0)
                kband_s[CHUNK + ts.start:CHUNK + ts.stop, :] = (knt.T * qk_w).astype(BF16)
                vt_s[:, CHUNK + ts.start:CHUNK + ts.stop] = rt[LANES:2 * LANES, ts].astype(BF16)

        def post_gt(rt):
            for r0 in range(0, SWA_HEADS * SWA_HEAD_DIM, ROW_BLOCK):
                for ts in tok_blocks:
                    gatet_s[r0:r0 + ROW_BLOCK, ts] = _silu(rt[r0:r0 + ROW_BLOCK, ts])

        groups = [
            (functools.partial(proj, OFF_RQ), post_rqk),
            (functools.partial(proj_t, FT_Q, SWA_HEADS * SWA_HEAD_DIM), post_qt),
            (functools.partial(proj, OFF_RG), post_rg),
            (functools.partial(proj_t, FT_G, SWA_HEADS * SWA_HEAD_DIM), post_gt),
            (functools.partial(proj_t, FT_K, 2 * LANES), post_kvt),
            (functools.partial(proj, OFF_RV), post_rv),
        ]
        pending = None
        for i, (matmul, post) in enumerate(groups):
            result = matmul()
            yield
            if pending is not None:
                pending[0](pending[1])
                yield
            pending = (post, result)
            if i == NORM_AFTER_GROUP:
                for r0, rs in _row_blocks(SEQ_TILE):
                    h_next_s[rs, :] = rms_norm_rows(xnext_ref[0, rs, :])
                yield
        pending[0](pending[1])
        yield

    def mix(slot, next_slot):
        rq_s, rqx_s, rk_s, rkz_s, rv_s, gate_s, qt_s, kband_s, vt_s, gatet_s = slot
        key = lax.broadcasted_iota(jnp.int32, (CHUNK, CHUNK), 0)
        qry = lax.broadcasted_iota(jnp.int32, (CHUNK, CHUNK), 1)
        tri_t = key <= qry
        zero_q = jnp.zeros((SWA_HEAD_DIM, SWA_GROUP * CHUNK), BF16)

        def out_proj(c_hi):
            pair = slice((c_hi - 1) * CHUNK, (c_hi + 1) * CHUNK)
            o_ref[0, pair, :] = (xres_ref[0, pair, :]
                                 + _dot(mixed_s[pair, :], wout_ref[0:RET_HEADS * RET_DV, :])
                                 + _dot_tn(mixedt_s[:, pair], wout_ref[RET_HEADS * RET_DV:D_MIX, :]))

        chunk_rows = [slice(c * CHUNK, (c + 1) * CHUNK) for c in range(N_CHUNKS)]
        chunk_band = [slice(c * CHUNK, (c + 2) * CHUNK) for c in range(N_CHUNKS)]

        scores, incs, logits_t = {}, {}, []
        for c, rows in enumerate(chunk_rows):
            for p in range(2):
                s2 = _dot_nt(rq_s[p, rows, :], rk_s[p, c])
                scores[c, p] = jnp.concatenate([(s2[bs, :] * dec_ref[p, bs, :]).astype(BF16)
                                                for _, bs in _row_blocks(CHUNK)], axis=0)
                incs[c, p] = _dot_tn(rkz_s[p, rows, :], rv_s[rows, p * 2 * RET_DV:(p + 1) * 2 * RET_DV])
            qts = [jnp.concatenate([qt_s[(SWA_GROUP * g + hh) * SWA_HEAD_DIM:(SWA_GROUP * g + hh + 1) * SWA_HEAD_DIM,
                                         rows] for hh in range(SWA_GROUP)], axis=1)
                   for g in range(SWA_KV_HEADS)]
            rhs = jnp.concatenate([jnp.concatenate([qts[0], zero_q], axis=1),
                                   jnp.concatenate([zero_q, qts[1]], axis=1)], axis=0)
            logits_t.append(_dot(kband_s[chunk_band[c], :], rhs))
        yield

        ret_outs = {}
        for p in range(2):
            full = state_s[p] * keep_state
            for c, rows in enumerate(chunk_rows):
                full_b = full.astype(BF16)
                v2 = rv_s[rows, p * 2 * RET_DV:(p + 1) * 2 * RET_DV]
                s2 = scores[c, p]
                for j in range(2):
                    ret_outs[c, 2 * p + j] = _dot(
                        jnp.concatenate([s2[:, j * CHUNK:(j + 1) * CHUNK], rqx_s[p, j, rows, :]], axis=1),
                        jnp.concatenate([v2[:, j * RET_DV:(j + 1) * RET_DV],
                                         full_b[:, j * RET_DV:(j + 1) * RET_DV]], axis=0))
                full = full * cd_ref[p] + incs[c, p]
            state_s[p] = full
        yield

        for c in range(N_CHUNKS):
            rows, band, lt = chunk_rows[c], chunk_band[c], logits_t[c]

            for hd in range(RET_HEADS):
                o = ret_outs[c, hd]
                cols = slice(hd * RET_DV, (hd + 1) * RET_DV)
                for r0, bs in _row_blocks(CHUNK):
                    ob = o[bs, :]
                    rs = slice(c * CHUNK + r0, c * CHUNK + r0 + ROW_BLOCK)
                    mu = jnp.mean(ob, axis=-1, keepdims=True)
                    d = ob - mu
                    var = jnp.mean(d * d, axis=-1, keepdims=True)
                    y = d * lax.rsqrt(var + GN_EPS) * retw_ref[:, cols]
                    mixed_s[rs, cols] = (y * gate_s[rs, cols]).astype(BF16)
            invs = []
            for hd in range(SWA_HEADS):
                g, hh = hd // SWA_GROUP, hd % SWA_GROUP
                hq = slice(hd * CHUNK, (hd + 1) * CHUNK)
                sink = sinks_ref[hd]
                bias = bias_s[hd + SWA_HEADS * first_i] if c == 0 else bias_s[hd]
                lg = jnp.where(tri_t, lt[CHUNK:2 * CHUNK, hq], lt[0:CHUNK, hq]) + bias
                m = jnp.maximum(jnp.max(lg, axis=0, keepdims=True), sink)
                e = jnp.exp(lg - m)
                den = jnp.sum(e, axis=0, keepdims=True) + jnp.exp(sink - m)
                invs.append(1.0 / den)
                pq = slice(hh * CHUNK, (hh + 1) * CHUNK)
                pt_s[c, g, 0:CHUNK, pq] = jnp.where(tri_t, 0.0, e).astype(BF16)
                pt_s[c, g, CHUNK:2 * CHUNK, pq] = jnp.where(tri_t, e, 0.0).astype(BF16)
            yield

            for g in range(SWA_KV_HEADS):
                ot = _dot(vt_s[g * SWA_HEAD_DIM:(g + 1) * SWA_HEAD_DIM, band], pt_s[c, g])
                for hh in range(SWA_GROUP):
                    hd = SWA_GROUP * g + hh
                    hs = slice(hd * SWA_HEAD_DIM, (hd + 1) * SWA_HEAD_DIM)
                    mixedt_s[hs, rows] = (ot[:, hh * CHUNK:(hh + 1) * CHUNK] * invs[hd]
                                          * gatet_s[hs, rows]).astype(BF16)
            yield
            if c % 2 == 1:
                out_proj(c)
                yield

        last = slice(SEQ_TILE, SEQ_TILE + CHUNK)
        next_slot[SLOT_KBAND][0:CHUNK, :] = kband_s[last, :]
        next_slot[SLOT_VT][:, 0:CHUNK] = vt_s[:, last]

    def run(parity, do_project=True, do_mix=True):
        streams = [project(slots[parity], h_bufs[parity], h_bufs[1 - parity]) if do_project else iter(()),
                   mix(slots[1 - parity], slots[parity]) if do_mix else iter(())]
        for which in MIX_ORDER + (0,) * len(MIX_ORDER) + (1,) * len(MIX_ORDER):
            next(streams[which], None)

    is_first, is_last = step == 0, step == n_tiles
    pl.when(is_first)(functools.partial(run, 0, do_mix=False))
    pl.when(is_last)(functools.partial(run, n_tiles % 2, do_project=False))
    for parity in range(2):
        pl.when((step % 2 == parity) & jnp.logical_not(is_first | is_last))(functools.partial(run, parity))


def kernel(x, norm_w, w_in, ret_norm_w, q_norm_w, k_norm_w, sinks, rel_bias, w_out):
    batch, seq, d_model = x.shape
    assert d_model == D_MODEL and seq % SEQ_TILE == 0
    assert w_in.shape == (D_MODEL, D_IN) and w_out.shape == (D_MIX, D_MODEL)
    tb = _constant_tables(seq)
    tiles_per_seq = seq // SEQ_TILE
    n_tiles = batch * tiles_per_seq

    def proj_tile(s):
        return jnp.minimum(s, n_tiles - 1)

    def mix_tile(s):
        return jnp.maximum(s - 1, 0)

    const2 = lambda s: (0, 0)
    const3 = lambda s: (0, 0, 0)
    const4 = lambda s: (0, 0, 0, 0)
    pos_spec = pl.BlockSpec((SEQ_TILE, LANES), lambda s: (proj_tile(s) % tiles_per_seq, 0))
    smem = pl.BlockSpec(memory_space=pltpu.SMEM)
    x_block = (1, SEQ_TILE, D_MODEL)
    mix_map = lambda s: (mix_tile(s) // tiles_per_seq, mix_tile(s) % tiles_per_seq, 0)

    in_specs = [
        pl.BlockSpec(x_block, lambda s: (proj_tile(s + 1) // tiles_per_seq, proj_tile(s + 1) % tiles_per_seq, 0)),
        pl.BlockSpec(x_block, mix_map),
        pl.BlockSpec(memory_space=pl.ANY),
        pl.BlockSpec((1, D_MODEL), const2),
        pl.BlockSpec(memory_space=pl.ANY),
        pl.BlockSpec(memory_space=pl.ANY),
        pos_spec, pos_spec, pos_spec,
        pl.BlockSpec((2, CHUNK, 2 * CHUNK), const3),
        pl.BlockSpec((2, 2, CHUNK, LANES), const4),
        pl.BlockSpec((2, CHUNK, LANES), const3),
        pl.BlockSpec((2, 1, 2 * RET_DV), const3),
        pl.BlockSpec((1, RET_HEADS * RET_DV), const2),
        pl.BlockSpec((1, LANES), const2),
        pl.BlockSpec((1, LANES), const2),
        pl.BlockSpec((CHUNK, CHUNK), const2),
        smem, smem,
    ]
    slot_bufs = [
        pltpu.VMEM((2, SEQ_TILE, LANES), BF16),
        pltpu.VMEM((2, 2, SEQ_TILE, LANES), BF16),
        pltpu.VMEM((2, N_CHUNKS, 2 * CHUNK, LANES), BF16),
        pltpu.VMEM((2, SEQ_TILE, LANES), BF16),
        pltpu.VMEM((SEQ_TILE, RET_HEADS * RET_DV), BF16),
        pltpu.VMEM((SEQ_TILE, RET_HEADS * RET_DV), F32),
        pltpu.VMEM((SWA_HEADS * SWA_HEAD_DIM, SEQ_TILE), BF16),
        pltpu.VMEM((SEQ_TILE + CHUNK, LANES), BF16),
        pltpu.VMEM((SWA_KV_HEADS * SWA_HEAD_DIM, SEQ_TILE + CHUNK), BF16),
        pltpu.VMEM((SWA_HEADS * SWA_HEAD_DIM, SEQ_TILE), F32),
    ]
    assert len(slot_bufs) == N_SLOT_BUFS
    h_buf = pltpu.VMEM((SEQ_TILE, D_MODEL), BF16)
    scratch = [h_buf, h_buf] + slot_bufs + slot_bufs + [
        pltpu.VMEM((N_CHUNKS, SWA_KV_HEADS, 2 * CHUNK, SWA_GROUP * CHUNK), BF16),
        pltpu.VMEM((SEQ_TILE, RET_HEADS * RET_DV), BF16),
        pltpu.VMEM((SWA_HEADS * SWA_HEAD_DIM, SEQ_TILE), BF16),
        pltpu.VMEM((2, 2 * RET_DK, 2 * RET_DV), F32),
        pltpu.VMEM((2 * SWA_HEADS, CHUNK, CHUNK), F32),
        pltpu.VMEM((D_MODEL, D_TOK), BF16),
        pltpu.VMEM((D_FT, D_MODEL), BF16),
        pltpu.VMEM((D_MIX, D_MODEL), BF16),
        pltpu.VMEM((2, STAGE_ROWS, D_IN), F32),
        pltpu.VMEM((2, STAGE_ROWS, D_MODEL), F32),
        pltpu.SemaphoreType.DMA((4,)),
    ]
    return pl.pallas_call(
        functools.partial(_layer_kernel, tiles_per_seq, n_tiles),
        grid=(n_tiles + 1,),
        in_specs=in_specs,
        out_specs=pl.BlockSpec(x_block, mix_map),
        out_shape=jax.ShapeDtypeStruct(x.shape, x.dtype),
        scratch_shapes=scratch,
        compiler_params=pltpu.CompilerParams(
            dimension_semantics=("arbitrary",),
            vmem_limit_bytes=VMEM_LIMIT_BYTES),
    )(x, x, x, norm_w.reshape(1, D_MODEL), w_in, w_out,
      tb["cos"], tb["sin_a"], tb["sin_b"], tb["dec"], tb["xi"], tb["zeta"], tb["cd"],
      ret_norm_w.reshape(1, -1), jnp.tile(q_norm_w, SWA_KV_HEADS).reshape(1, -1),
      jnp.tile(k_norm_w, SWA_KV_HEADS).reshape(1, -1), tb["bucket"],
      rel_bias.astype(F32), sinks.astype(F32))
```

```python
import functools
import math

import numpy as np
import jax
import jax.numpy as jnp
from jax import lax
from jax.experimental import pallas as pl
from jax.experimental.pallas import tpu as pltpu

D_MODEL = 1024
RET_HEADS = 4
RET_DK = 64
RET_DV = 128
CHUNK = 128
RET_ROPE_BASE = 10000.0
SWA_HEADS = 8
SWA_KV_HEADS = 2
SWA_HEAD_DIM = 64
SWA_GROUP = SWA_HEADS // SWA_KV_HEADS
NUM_BUCKETS = 32
MAX_DISTANCE = 128
NORM_EPS = 1e-6
GN_EPS = 1e-5
NEG_INF = -1e30

OFF_RQ, OFF_RK, OFF_RV, OFF_RG = 0, 256, 512, 1024
D_TOK = 1536
FT_Q, FT_K, FT_G = 0, 512, 768
D_FT = 1280
D_IN = D_TOK + D_FT
D_MIX = 1024

LANES = 128
SEQ_TILE = 512
N_CHUNKS = SEQ_TILE // CHUNK
ROW_BLOCK = 32
VMEM_LIMIT_BYTES = 56 * 1024 * 1024
MIX_ORDER = (1, 1, 0, 0, 1, 1, 0, 0, 1, 1, 0, 0, 1, 0, 1, 1, 0, 0, 1, 1, 0, 0, 1, 0, 0)
NORM_AFTER_GROUP = 2
N_SLOT_BUFS = 10
STAGE_ROWS = 128
SLOT_KBAND, SLOT_VT = 7, 8

BF16 = jnp.bfloat16
F32 = jnp.float32


def _t5_bucket_np(n):
    max_exact = NUM_BUCKETS // 2
    nf = np.maximum(n, 1).astype(np.float64)
    large = max_exact + (np.log(nf / max_exact) / math.log(MAX_DISTANCE / max_exact)
                         * (NUM_BUCKETS - max_exact)).astype(np.int32)
    large = np.minimum(large, NUM_BUCKETS - 1)
    return np.where(n < max_exact, n, large).astype(np.int32)


def _constant_tables(seq):
    half = RET_DK // 2
    inv_freq = RET_ROPE_BASE ** (-np.arange(half, dtype=np.float64) / half)
    ang = np.arange(seq, dtype=np.float64)[:, None] * inv_freq[None, :]
    cos, sin = np.cos(ang), np.sin(ang)
    zeros = np.zeros_like(sin)
    cos_t = np.tile(np.concatenate([cos, cos], axis=1), (1, 2))
    sin_a = np.tile(np.concatenate([-sin, zeros], axis=1), (1, 2))
    sin_b = np.tile(np.concatenate([zeros, sin], axis=1), (1, 2))

    gamma = 1.0 - np.exp2(-5.0 - np.arange(RET_HEADS, dtype=np.float64))
    log_g = np.log(gamma)
    i = np.arange(CHUNK, dtype=np.float64)
    diff = i[:, None] - i[None, :]
    k_scale = RET_DK ** -0.5
    decay = np.where(diff >= 0, np.exp(log_g[:, None, None] * np.maximum(diff, 0.0)), 0.0) * k_scale
    dec = np.stack([np.concatenate([decay[2 * p], decay[2 * p + 1]], axis=1) for p in range(2)])
    xi = np.exp(log_g[:, None] * (i + 1.0))
    zeta = np.exp(log_g[:, None] * (CHUNK - 1.0 - i)) * k_scale

    def per_lane(t):
        return np.stack([np.concatenate([np.repeat(t[2 * p][:, None], RET_DK, 1),
                                         np.repeat(t[2 * p + 1][:, None], RET_DK, 1)], axis=1)
                         for p in range(2)])

    lo = (np.arange(LANES) < RET_DK)[None, None, :]
    xi_l = per_lane(xi)
    xi_m = np.stack([np.where(lo, xi_l, 0.0), np.where(lo, 0.0, xi_l)], axis=1)

    chunk_decay = np.exp(log_g * CHUNK)
    cd = np.stack([np.concatenate([np.full((1, RET_DV), chunk_decay[2 * p]),
                                   np.full((1, RET_DV), chunk_decay[2 * p + 1])], axis=1)
                   for p in range(2)])

    j = np.arange(CHUNK)[:, None]
    q = np.arange(CHUNK)[None, :]
    dist = np.where(j <= q, q - j, q + CHUNK - j)
    bucket = _t5_bucket_np(dist)

    f = lambda a: jnp.asarray(a, dtype=F32)
    return dict(cos=f(cos_t), sin_a=f(sin_a), sin_b=f(sin_b), dec=f(dec), xi=f(xi_m),
                zeta=f(per_lane(zeta)), cd=f(cd), bucket=jnp.asarray(bucket, dtype=jnp.int32))


def _dot(a, b):
    return jnp.dot(a, b, preferred_element_type=F32)


def _dot_nt(a, b):
    return lax.dot_general(a, b, (((1,), (1,)), ((), ())), preferred_element_type=F32)


def _dot_tn(a, b):
    return lax.dot_general(a, b, (((0,), (0,)), ((), ())), preferred_element_type=F32)


def _silu(g):
    hg = 0.5 * g
    return hg + hg * jnp.tanh(hg)


def _row_blocks(n_rows):
    return [(r0, slice(r0, r0 + ROW_BLOCK)) for r0 in range(0, n_rows, ROW_BLOCK)]


def _layer_kernel(tiles_per_seq, n_tiles,
                  xnext_ref, xres_ref, x_hbm, nw_ref, win_hbm, wout_hbm, cos_ref, sina_ref, sinb_ref, dec_ref,
                  xi_ref, zeta_ref, cd_ref, retw_ref, qnw_ref, knw_ref, bkt_ref, relb_ref,
                  sinks_ref, o_ref, h0_s, h1_s, *scratch):
    h_bufs = (h0_s, h1_s)
    slots = (scratch[0:N_SLOT_BUFS], scratch[N_SLOT_BUFS:2 * N_SLOT_BUFS])
    (pt_s, mixed_s, mixedt_s, state_s, bias_s,
     win_ref, wft_ref, wout_ref, stage_in, stage_out, stage_sem) = scratch[2 * N_SLOT_BUFS:]
    step = pl.program_id(0)
    mix_tile = jnp.maximum(step - 1, 0)
    first = mix_tile % tiles_per_seq == 0
    first_i = first.astype(jnp.int32)
    keep_state = jnp.where(first, 0.0, 1.0).astype(F32)

    def rms_norm_rows(xb):
        ms = jnp.mean(xb * xb, axis=-1, keepdims=True)
        return (xb * lax.rsqrt(ms + NORM_EPS) * nw_ref[...]).astype(BF16)

    @pl.when(step == 0)
    def _init():
        bk = bkt_ref[...]
        key = lax.broadcasted_iota(jnp.int32, (CHUNK, CHUNK), 0)
        qry = lax.broadcasted_iota(jnp.int32, (CHUNK, CHUNK), 1)
        for h in range(SWA_HEADS):
            acc = jnp.zeros((CHUNK, CHUNK), F32)
            for u in range(NUM_BUCKETS):
                acc = jnp.where(bk == u, relb_ref[u, h], acc)
            bias_s[h] = acc
            bias_s[SWA_HEADS + h] = jnp.where(key <= qry, acc, NEG_INF)
        state_s[...] = jnp.zeros_like(state_s)
        slots[0][SLOT_KBAND][0:CHUNK, :] = jnp.zeros((CHUNK, LANES), BF16)
        slots[0][SLOT_VT][:, 0:CHUNK] = jnp.zeros((SWA_KV_HEADS * SWA_HEAD_DIM, CHUNK), BF16)

        def chunk_copies(hbm, stage, sem0, n_rows):
            return [pltpu.make_async_copy(hbm.at[pl.ds(i * STAGE_ROWS, STAGE_ROWS), :],
                                          stage.at[i % 2], stage_sem.at[sem0 + i % 2])
                    for i in range(n_rows // STAGE_ROWS)]

        def staged(copies, consume):
            copies[0].start()
            for i, cp in enumerate(copies):
                if i + 1 < len(copies):
                    copies[i + 1].start()
                cp.wait()
                consume(i)

        def consume_in(i):
            rows = slice(i * STAGE_ROWS, (i + 1) * STAGE_ROWS)
            win_ref[rows, :] = stage_in[i % 2, :, 0:D_TOK].astype(BF16)
            for f0 in range(0, D_FT, LANES):
                wft_ref[f0:f0 + LANES, rows] = stage_in[i % 2, :, D_TOK + f0:D_TOK + f0 + LANES].T.astype(BF16)

        def consume_out(i):
            rows = slice(i * STAGE_ROWS, (i + 1) * STAGE_ROWS)
            wout_ref[rows, :] = stage_out[i % 2].astype(BF16)

        def consume_x0(i):
            for r0, rs in _row_blocks(STAGE_ROWS):
                h0_s[i * STAGE_ROWS + r0:i * STAGE_ROWS + r0 + ROW_BLOCK, :] = rms_norm_rows(stage_out[i % 2, rs, :])

        staged(chunk_copies(win_hbm, stage_in, 0, D_MODEL), consume_in)
        staged(chunk_copies(wout_hbm, stage_out, 2, D_MIX), consume_out)
        staged(chunk_copies(x_hbm.at[0], stage_out, 2, SEQ_TILE), consume_x0)

    lo = lax.broadcasted_iota(jnp.int32, (ROW_BLOCK, LANES), 1) < RET_DK

    def project(slot, h_s, h_next_s):
        rq_s, rqx_s, rk_s, rkz_s, rv_s, gate_s, qt_s, kband_s, vt_s, gatet_s = slot

        def proj(off):
            return _dot(h_s[...], win_ref[:, off:off + 4 * LANES])

        def proj_t(row0, n_rows):
            return _dot_nt(wft_ref[row0:row0 + n_rows, :], h_s[...])

        def rot(v, rs):
            return (v * cos_ref[rs, :] + pltpu.roll(v, LANES - 32, 1) * sina_ref[rs, :]
                    + pltpu.roll(v, 32, 1) * sinb_ref[rs, :])

        def head_rms(xt):
            ms = jnp.mean(xt * xt, axis=0, keepdims=True)
            return xt * lax.rsqrt(ms + NORM_EPS)

        tok_blocks = [slice(c0, c0 + LANES) for c0 in range(0, SEQ_TILE, LANES)]

        def post_rqk(r):
            for r0, rs in _row_blocks(SEQ_TILE):
                c, cr = r0 // CHUNK, r0 % CHUNK
                cs = slice(cr, cr + ROW_BLOCK)
                for p in range(2):
                    q = rot(r[rs, p * LANES:(p + 1) * LANES], rs)
                    rq_s[p, rs, :] = q.astype(BF16)
                    rqx_s[p, 0, rs, :] = (q * xi_ref[p, 0, cs, :]).astype(BF16)
                    rqx_s[p, 1, rs, :] = (q * xi_ref[p, 1, cs, :]).astype(BF16)
                    k = rot(r[rs, (2 + p) * LANES:(3 + p) * LANES], rs)
                    rk_s[p, c, cr:cr + ROW_BLOCK, :] = jnp.where(lo, k, 0.0).astype(BF16)
                    rk_s[p, c, CHUNK + cr:CHUNK + cr + ROW_BLOCK, :] = jnp.where(lo, 0.0, k).astype(BF16)
                    rkz_s[p, rs, :] = (k * zeta_ref[p, cs, :]).astype(BF16)

        def post_rv(r):
            for r0, rs in _row_blocks(SEQ_TILE):
                rv_s[rs, :] = r[rs, :].astype(BF16)

        def post_rg(r):
            for r0, rs in _row_blocks(SEQ_TILE):
                gate_s[rs, :] = _silu(r[rs, :])

        def post_qt(rt):
            for hd in range(SWA_HEADS):
                hs = slice(hd * SWA_HEAD_DIM, (hd + 1) * SWA_HEAD_DIM)
                for ts in tok_blocks:
                    qt_s[hs, ts] = head_rms(rt[hs, ts]).astype(BF16)

        def post_kvt(rt):
            qk_w = qnw_ref[...] * knw_ref[...] * (SWA_HEAD_DIM ** -0.5)
            for ts in tok_blocks:
                knt = jnp.concatenate(
                    [head_rms(rt[g * SWA_HEAD_DIM:(g + 1) * SWA_HEAD_DIM, ts]) for g in range(SWA_KV_HEADS)],
                    axis=0)
                kband_s[CHUNK + ts.start:CHUNK + ts.stop, :] = (knt.T * qk_w).astype(BF16)
                vt_s[:, CHUNK + ts.start:CHUNK + ts.stop] = rt[LANES:2 * LANES, ts].astype(BF16)

        def post_gt(rt):
            for r0 in range(0, SWA_HEADS * SWA_HEAD_DIM, ROW_BLOCK):
                for ts in tok_blocks:
                    gatet_s[r0:r0 + ROW_BLOCK, ts] = _silu(rt[r0:r0 + ROW_BLOCK, ts])

        groups = [
            (functools.partial(proj, OFF_RQ), post_rqk),
            (functools.partial(proj_t, FT_Q, SWA_HEADS * SWA_HEAD_DIM), post_qt),
            (functools.partial(proj, OFF_RG), post_rg),
            (functools.partial(proj_t, FT_G, SWA_HEADS * SWA_HEAD_DIM), post_gt),
            (functools.partial(proj_t, FT_K, 2 * LANES), post_kvt),
            (functools.partial(proj, OFF_RV), post_rv),
        ]
        pending = None
        for i, (matmul, post) in enumerate(groups):
            result = matmul()
            yield
            if pending is not None:
                pending[0](pending[1])
                yield
            pending = (post, result)
            if i == NORM_AFTER_GROUP:
                for r0, rs in _row_blocks(SEQ_TILE):
                    h_next_s[rs, :] = rms_norm_rows(xnext_ref[0, rs, :])
                yield
        pending[0](pending[1])
        yield

    def mix(slot, next_slot):
        rq_s, rqx_s, rk_s, rkz_s, rv_s, gate_s, qt_s, kband_s, vt_s, gatet_s = slot
        key = lax.broadcasted_iota(jnp.int32, (CHUNK, CHUNK), 0)
        qry = lax.broadcasted_iota(jnp.int32, (CHUNK, CHUNK), 1)
        tri_t = key <= qry
        zero_q = jnp.zeros((SWA_HEAD_DIM, SWA_GROUP * CHUNK), BF16)

        def out_proj(c_hi):
            pair = slice((c_hi - 1) * CHUNK, (c_hi + 1) * CHUNK)
            o_ref[0, pair, :] = (xres_ref[0, pair, :]
                                 + _dot(mixed_s[pair, :], wout_ref[0:RET_HEADS * RET_DV, :])
                                 + _dot_tn(mixedt_s[:, pair], wout_ref[RET_HEADS * RET_DV:D_MIX, :]))

        chunk_rows = [slice(c * CHUNK, (c + 1) * CHUNK) for c in range(N_CHUNKS)]
        chunk_band = [slice(c * CHUNK, (c + 2) * CHUNK) for c in range(N_CHUNKS)]

        scores, incs, logits_t = {}, {}, []
        for c, rows in enumerate(chunk_rows):
            for p in range(2):
                s2 = _dot_nt(rq_s[p, rows, :], rk_s[p, c])
                scores[c, p] = jnp.concatenate([(s2[bs, :] * dec_ref[p, bs, :]).astype(BF16)
                                                for _, bs in _row_blocks(CHUNK)], axis=0)
                incs[c, p] = _dot_tn(rkz_s[p, rows, :], rv_s[rows, p * 2 * RET_DV:(p + 1) * 2 * RET_DV])
            qts = [jnp.concatenate([qt_s[(SWA_GROUP * g + hh) * SWA_HEAD_DIM:(SWA_GROUP * g + hh + 1) * SWA_HEAD_DIM,
                                         rows] for hh in range(SWA_GROUP)], axis=1)
                   for g in range(SWA_KV_HEADS)]
            rhs = jnp.concatenate([jnp.concatenate([qts[0], zero_q], axis=1),
                                   jnp.concatenate([zero_q, qts[1]], axis=1)], axis=0)
            logits_t.append(_dot(kband_s[chunk_band[c], :], rhs))
        yield

        ret_outs = {}
        for p in range(2):
            full = state_s[p] * keep_state
            for c, rows in enumerate(chunk_rows):
                full_b = full.astype(BF16)
                v2 = rv_s[rows, p * 2 * RET_DV:(p + 1) * 2 * RET_DV]
                s2 = scores[c, p]
                for j in range(2):
                    ret_outs[c, 2 * p + j] = _dot(
                        jnp.concatenate([s2[:, j * CHUNK:(j + 1) * CHUNK], rqx_s[p, j, rows, :]], axis=1),
                        jnp.concatenate([v2[:, j * RET_DV:(j + 1) * RET_DV],
                                         full_b[:, j * RET_DV:(j + 1) * RET_DV]], axis=0))
                full = full * cd_ref[p] + incs[c, p]
            state_s[p] = full
        yield

        for c in range(N_CHUNKS):
            rows, band, lt = chunk_rows[c], chunk_band[c], logits_t[c]

            for hd in range(RET_HEADS):
                o = ret_outs[c, hd]
                cols = slice(hd * RET_DV, (hd + 1) * RET_DV)
                for r0, bs in _row_blocks(CHUNK):
                    ob = o[bs, :]
                    rs = slice(c * CHUNK + r0, c * CHUNK + r0 + ROW_BLOCK)
                    mu = jnp.mean(ob, axis=-1, keepdims=True)
                    d = ob - mu
                    var = jnp.mean(d * d, axis=-1, keepdims=True)
                    y = d * lax.rsqrt(var + GN_EPS) * retw_ref[:, cols]
                    mixed_s[rs, cols] = (y * gate_s[rs, cols]).astype(BF16)
            invs = []
            for hd in range(SWA_HEADS):
                g, hh = hd // SWA_GROUP, hd % SWA_GROUP
                hq = slice(hd * CHUNK, (hd + 1) * CHUNK)
                sink = sinks_ref[hd]
                bias = bias_s[hd + SWA_HEADS * first_i] if c == 0 else bias_s[hd]
                lg = jnp.where(tri_t, lt[CHUNK:2 * CHUNK, hq], lt[0:CHUNK, hq]) + bias
                m = jnp.maximum(jnp.max(lg, axis=0, keepdims=True), sink)
                e = jnp.exp(lg - m)
                den = jnp.sum(e, axis=0, keepdims=True) + jnp.exp(sink - m)
                invs.append(1.0 / den)
                pq = slice(hh * CHUNK, (hh + 1) * CHUNK)
                pt_s[c, g, 0:CHUNK, pq] = jnp.where(tri_t, 0.0, e).astype(BF16)
                pt_s[c, g, CHUNK:2 * CHUNK, pq] = jnp.where(tri_t, e, 0.0).astype(BF16)
            yield

            for g in range(SWA_KV_HEADS):
                ot = _dot(vt_s[g * SWA_HEAD_DIM:(g + 1) * SWA_HEAD_DIM, band], pt_s[c, g])
                for hh in range(SWA_GROUP):
                    hd = SWA_GROUP * g + hh
                    hs = slice(hd * SWA_HEAD_DIM, (hd + 1) * SWA_HEAD_DIM)
                    mixedt_s[hs, rows] = (ot[:, hh * CHUNK:(hh + 1) * CHUNK] * invs[hd]
                                          * gatet_s[hs, rows]).astype(BF16)
            yield
            if c % 2 == 1:
                out_proj(c)
                yield

        last = slice(SEQ_TILE, SEQ_TILE + CHUNK)
        next_slot[SLOT_KBAND][0:CHUNK, :] = kband_s[last, :]
        next_slot[SLOT_VT][:, 0:CHUNK] = vt_s[:, last]

    def run(parity, do_project=True, do_mix=True):
        streams = [project(slots[parity], h_bufs[parity], h_bufs[1 - parity]) if do_project else iter(()),
                   mix(slots[1 - parity], slots[parity]) if do_mix else iter(())]
        for which in MIX_ORDER + (0,) * len(MIX_ORDER) + (1,) * len(MIX_ORDER):
            next(streams[which], None)

    is_first, is_last = step == 0, step == n_tiles
    pl.when(is_first)(functools.partial(run, 0, do_mix=False))
    pl.when(is_last)(functools.partial(run, n_tiles % 2, do_project=False))
    for parity in range(2):
        pl.when((step % 2 == parity) & jnp.logical_not(is_first | is_last))(functools.partial(run, parity))


def kernel(x, norm_w, w_in, ret_norm_w, q_norm_w, k_norm_w, sinks, rel_bias, w_out):
    batch, seq, d_model = x.shape
    assert d_model == D_MODEL and seq % SEQ_TILE == 0
    assert w_in.shape == (D_MODEL, D_IN) and w_out.shape == (D_MIX, D_MODEL)
    tb = _constant_tables(seq)
    tiles_per_seq = seq // SEQ_TILE
    n_tiles = batch * tiles_per_seq

    def proj_tile(s):
        return jnp.minimum(s, n_tiles - 1)

    def mix_tile(s):
        return jnp.maximum(s - 1, 0)

    const2 = lambda s: (0, 0)
    const3 = lambda s: (0, 0, 0)
    const4 = lambda s: (0, 0, 0, 0)
    pos_spec = pl.BlockSpec((SEQ_TILE, LANES), lambda s: (proj_tile(s) % tiles_per_seq, 0))
    smem = pl.BlockSpec(memory_space=pltpu.SMEM)
    x_block = (1, SEQ_TILE, D_MODEL)
    mix_map = lambda s: (mix_tile(s) // tiles_per_seq, mix_tile(s) % tiles_per_seq, 0)

    in_specs = [
        pl.BlockSpec(x_block, lambda s: (proj_tile(s + 1) // tiles_per_seq, proj_tile(s + 1) % tiles_per_seq, 0)),
        pl.BlockSpec(x_block, mix_map),
        pl.BlockSpec(memory_space=pl.ANY),
        pl.BlockSpec((1, D_MODEL), const2),
        pl.BlockSpec(memory_space=pl.ANY),
        pl.BlockSpec(memory_space=pl.ANY),
        pos_spec, pos_spec, pos_spec,
        pl.BlockSpec((2, CHUNK, 2 * CHUNK), const3),
        pl.BlockSpec((2, 2, CHUNK, LANES), const4),
        pl.BlockSpec((2, CHUNK, LANES), const3),
        pl.BlockSpec((2, 1, 2 * RET_DV), const3),
        pl.BlockSpec((1, RET_HEADS * RET_DV), const2),
        pl.BlockSpec((1, LANES), const2),
        pl.BlockSpec((1, LANES), const2),
        pl.BlockSpec((CHUNK, CHUNK), const2),
        smem, smem,
    ]
    slot_bufs = [
        pltpu.VMEM((2, SEQ_TILE, LANES), BF16),
        pltpu.VMEM((2, 2, SEQ_TILE, LANES), BF16),
        pltpu.VMEM((2, N_CHUNKS, 2 * CHUNK, LANES), BF16),
        pltpu.VMEM((2, SEQ_TILE, LANES), BF16),
        pltpu.VMEM((SEQ_TILE, RET_HEADS * RET_DV), BF16),
        pltpu.VMEM((SEQ_TILE, RET_HEADS * RET_DV), F32),
        pltpu.VMEM((SWA_HEADS * SWA_HEAD_DIM, SEQ_TILE), BF16),
        pltpu.VMEM((SEQ_TILE + CHUNK, LANES), BF16),
        pltpu.VMEM((SWA_KV_HEADS * SWA_HEAD_DIM, SEQ_TILE + CHUNK), BF16),
        pltpu.VMEM((SWA_HEADS * SWA_HEAD_DIM, SEQ_TILE), F32),
    ]
    assert len(slot_bufs) == N_SLOT_BUFS
    h_buf = pltpu.VMEM((SEQ_TILE, D_MODEL), BF16)
    scratch = [h_buf, h_buf] + slot_bufs + slot_bufs + [
        pltpu.VMEM((N_CHUNKS, SWA_KV_HEADS, 2 * CHUNK, SWA_GROUP * CHUNK), BF16),
        pltpu.VMEM((SEQ_TILE, RET_HEADS * RET_DV), BF16),
        pltpu.VMEM((SWA_HEADS * SWA_HEAD_DIM, SEQ_TILE), BF16),
        pltpu.VMEM((2, 2 * RET_DK, 2 * RET_DV), F32),
        pltpu.VMEM((2 * SWA_HEADS, CHUNK, CHUNK), F32),
        pltpu.VMEM((D_MODEL, D_TOK), BF16),
        pltpu.VMEM((D_FT, D_MODEL), BF16),
        pltpu.VMEM((D_MIX, D_MODEL), BF16),
        pltpu.VMEM((2, STAGE_ROWS, D_IN), F32),
        pltpu.VMEM((2, STAGE_ROWS, D_MODEL), F32),
        pltpu.SemaphoreType.DMA((4,)),
    ]
    return pl.pallas_call(
        functools.partial(_layer_kernel, tiles_per_seq, n_tiles),
        grid=(n_tiles + 1,),
        in_specs=in_specs,
        out_specs=pl.BlockSpec(x_block, mix_map),
        out_shape=jax.ShapeDtypeStruct(x.shape, x.dtype),
        scratch_shapes=scratch,
        compiler_params=pltpu.CompilerParams(
            dimension_semantics=("arbitrary",),
            vmem_limit_bytes=VMEM_LIMIT_BYTES),
    )(x, x, x, norm_w.reshape(1, D_MODEL), w_in, w_out,
      tb["cos"], tb["sin_a"], tb["sin_b"], tb["dec"], tb["xi"], tb["zeta"], tb["cd"],
      ret_norm_w.reshape(1, -1), jnp.tile(q_norm_w, SWA_KV_HEADS).reshape(1, -1),
      jnp.tile(k_norm_w, SWA_KV_HEADS).reshape(1, -1), tb["bucket"],
      rel_bias.astype(F32), sinks.astype(F32))
```

```python
import functools
import math

import numpy as np
import jax
import jax.numpy as jnp
from jax import lax
from jax.experimental import pallas as pl
from jax.experimental.pallas import tpu as pltpu

D_MODEL = 1024
RET_HEADS = 4
RET_DK = 64
RET_DV = 128
CHUNK = 128
RET_ROPE_BASE = 10000.0
SWA_HEADS = 8
SWA_KV_HEADS = 2
SWA_HEAD_DIM = 64
SWA_GROUP = SWA_HEADS // SWA_KV_HEADS
NUM_BUCKETS = 32
MAX_DISTANCE = 128
NORM_EPS = 1e-6
GN_EPS = 1e-5
NEG_INF = -1e30
LOG2_E = math.log2(math.e)

OFF_RQ, OFF_RK, OFF_RV, OFF_RG = 0, 256, 512, 1024
D_TOK = 1536
FT_Q, FT_K, FT_G = 0, 512, 768
D_FT = 1280
D_IN = D_TOK + D_FT
D_MIX = 1024

LANES = 128
SEQ_TILE = 512
N_CHUNKS = SEQ_TILE // CHUNK
ROW_BLOCK = 32
VMEM_LIMIT_BYTES = 56 * 1024 * 1024
MIX_ORDER = (1, 1, 0, 0, 1, 1, 0, 0, 1, 1, 0, 0, 1, 0, 1, 1, 0, 0, 1, 1, 0, 0, 1, 0, 0)
NORM_AFTER_GROUP = 2
N_SLOT_BUFS = 10
STAGE_ROWS = 128
SLOT_KBAND, SLOT_VT = 7, 8

BF16 = jnp.bfloat16
F32 = jnp.float32


def _t5_bucket_np(n):
    max_exact = NUM_BUCKETS // 2
    nf = np.maximum(n, 1).astype(np.float64)
    large = max_exact + (np.log(nf / max_exact) / math.log(MAX_DISTANCE / max_exact)
                         * (NUM_BUCKETS - max_exact)).astype(np.int32)
    large = np.minimum(large, NUM_BUCKETS - 1)
    return np.where(n < max_exact, n, large).astype(np.int32)


def _constant_tables(seq):
    half = RET_DK // 2
    inv_freq = RET_ROPE_BASE ** (-np.arange(half, dtype=np.float64) / half)
    ang = np.arange(seq, dtype=np.float64)[:, None] * inv_freq[None, :]
    cos, sin = np.cos(ang), np.sin(ang)
    zeros = np.zeros_like(sin)
    cos_t = np.tile(np.concatenate([cos, cos], axis=1), (1, 2))
    sin_a = np.tile(np.concatenate([-sin, zeros], axis=1), (1, 2))
    sin_b = np.tile(np.concatenate([zeros, sin], axis=1), (1, 2))

    gamma = 1.0 - np.exp2(-5.0 - np.arange(RET_HEADS, dtype=np.float64))
    log_g = np.log(gamma)
    i = np.arange(CHUNK, dtype=np.float64)
    diff = i[:, None] - i[None, :]
    k_scale = RET_DK ** -0.5
    decay = np.where(diff >= 0, np.exp(log_g[:, None, None] * np.maximum(diff, 0.0)), 0.0) * k_scale
    dec = np.stack([np.concatenate([decay[2 * p], decay[2 * p + 1]], axis=1) for p in range(2)])
    xi = np.exp(log_g[:, None] * (i + 1.0))
    zeta = np.exp(log_g[:, None] * (CHUNK - 1.0 - i)) * k_scale

    def per_lane(t):
        return np.stack([np.concatenate([np.repeat(t[2 * p][:, None], RET_DK, 1),
                                         np.repeat(t[2 * p + 1][:, None], RET_DK, 1)], axis=1)
                         for p in range(2)])

    lo = (np.arange(LANES) < RET_DK)[None, None, :]
    xi_l = per_lane(xi)
    xi_m = np.stack([np.where(lo, xi_l, 0.0), np.where(lo, 0.0, xi_l)], axis=1)

    chunk_decay = np.exp(log_g * CHUNK)
    cd = np.stack([np.concatenate([np.full((1, RET_DV), chunk_decay[2 * p]),
                                   np.full((1, RET_DV), chunk_decay[2 * p + 1])], axis=1)
                   for p in range(2)])

    j = np.arange(CHUNK)[:, None]
    q = np.arange(CHUNK)[None, :]
    dist = np.where(j <= q, q - j, q + CHUNK - j)
    bucket = _t5_bucket_np(dist)

    f = lambda a: jnp.asarray(a, dtype=F32)
    return dict(cos=f(cos_t), sin_a=f(sin_a), sin_b=f(sin_b), dec=f(dec), xi=f(xi_m),
                zeta=f(per_lane(zeta)), cd=f(cd), bucket=jnp.asarray(bucket, dtype=jnp.int32))


def _dot(a, b):
    return jnp.dot(a, b, preferred_element_type=F32)


def _dot_nt(a, b):
    return lax.dot_general(a, b, (((1,), (1,)), ((), ())), preferred_element_type=F32)


def _dot_tn(a, b):
    return lax.dot_general(a, b, (((0,), (0,)), ((), ())), preferred_element_type=F32)


def _silu(g):
    hg = 0.5 * g
    return hg + hg * jnp.tanh(hg)


def _row_blocks(n_rows):
    return [(r0, slice(r0, r0 + ROW_BLOCK)) for r0 in range(0, n_rows, ROW_BLOCK)]


def _layer_kernel(tiles_per_seq, n_tiles,
                  xnext_ref, xres_ref, x_hbm, nw_ref, win_hbm, wout_hbm, cos_ref, sina_ref, sinb_ref, dec_ref,
                  xi_ref, zeta_ref, cd_ref, retw_ref, qnw_ref, knw_ref, bkt_ref, relb_ref,
                  sinks_ref, o_ref, h0_s, h1_s, *scratch):
    h_bufs = (h0_s, h1_s)
    slots = (scratch[0:N_SLOT_BUFS], scratch[N_SLOT_BUFS:2 * N_SLOT_BUFS])
    (pt_s, mixed_s, mixedt_s, state_s, bias_s,
     win_ref, wft_ref, wout_ref, stage_in, stage_out, stage_sem) = scratch[2 * N_SLOT_BUFS:]
    step = pl.program_id(0)
    mix_tile = jnp.maximum(step - 1, 0)
    first = mix_tile % tiles_per_seq == 0
    first_i = first.astype(jnp.int32)
    keep_state = jnp.where(first, 0.0, 1.0).astype(F32)

    def rms_norm_rows(xb):
        ms = jnp.mean(xb * xb, axis=-1, keepdims=True)
        return (xb * lax.rsqrt(ms + NORM_EPS) * nw_ref[...]).astype(BF16)

    @pl.when(step == 0)
    def _init():
        bk = bkt_ref[...]
        key = lax.broadcasted_iota(jnp.int32, (CHUNK, CHUNK), 0)
        qry = lax.broadcasted_iota(jnp.int32, (CHUNK, CHUNK), 1)
        for h in range(SWA_HEADS):
            acc = jnp.zeros((CHUNK, CHUNK), F32)
            for u in range(NUM_BUCKETS):
                acc = jnp.where(bk == u, relb_ref[u, h] * LOG2_E, acc)
            bias_s[h] = acc
            bias_s[SWA_HEADS + h] = jnp.where(key <= qry, acc, NEG_INF)
        state_s[...] = jnp.zeros_like(state_s)
        slots[0][SLOT_KBAND][0:CHUNK, :] = jnp.zeros((CHUNK, LANES), BF16)
        slots[0][SLOT_VT][:, 0:CHUNK] = jnp.zeros((SWA_KV_HEADS * SWA_HEAD_DIM, CHUNK), BF16)

        def chunk_copies(hbm, stage, sem0, n_rows):
            return [pltpu.make_async_copy(hbm.at[pl.ds(i * STAGE_ROWS, STAGE_ROWS), :],
                                          stage.at[i % 2], stage_sem.at[sem0 + i % 2])
                    for i in range(n_rows // STAGE_ROWS)]

        def staged(copies, consume):
            copies[0].start()
            for i, cp in enumerate(copies):
                if i + 1 < len(copies):
                    copies[i + 1].start()
                cp.wait()
                consume(i)

        def consume_in(i):
            rows = slice(i * STAGE_ROWS, (i + 1) * STAGE_ROWS)
            win_ref[rows, :] = stage_in[i % 2, :, 0:D_TOK].astype(BF16)
            for f0 in range(0, D_FT, LANES):
                wft_ref[f0:f0 + LANES, rows] = stage_in[i % 2, :, D_TOK + f0:D_TOK + f0 + LANES].T.astype(BF16)

        def consume_out(i):
            rows = slice(i * STAGE_ROWS, (i + 1) * STAGE_ROWS)
            wout_ref[rows, :] = stage_out[i % 2].astype(BF16)

        def consume_x0(i):
            for r0, rs in _row_blocks(STAGE_ROWS):
                h0_s[i * STAGE_ROWS + r0:i * STAGE_ROWS + r0 + ROW_BLOCK, :] = rms_norm_rows(stage_out[i % 2, rs, :])

        staged(chunk_copies(win_hbm, stage_in, 0, D_MODEL), consume_in)
        staged(chunk_copies(wout_hbm, stage_out, 2, D_MIX), consume_out)
        staged(chunk_copies(x_hbm.at[0], stage_out, 2, SEQ_TILE), consume_x0)

    lo = lax.broadcasted_iota(jnp.int32, (ROW_BLOCK, LANES), 1) < RET_DK

    def project(slot, h_s, h_next_s):
        rq_s, rqx_s, rk_s, rkz_s, rv_s, gate_s, qt_s, kband_s, vt_s, gatet_s = slot

        def proj(off):
            return _dot(h_s[...], win_ref[:, off:off + 4 * LANES])

        def proj_t(row0, n_rows):
            return _dot_nt(wft_ref[row0:row0 + n_rows, :], h_s[...])

        def rot(v, rs):
            return (v * cos_ref[rs, :] + pltpu.roll(v, LANES - 32, 1) * sina_ref[rs, :]
                    + pltpu.roll(v, 32, 1) * sinb_ref[rs, :])

        def head_rms(xt):
            ms = jnp.mean(xt * xt, axis=0, keepdims=True)
            return xt * lax.rsqrt(ms + NORM_EPS)

        tok_blocks = [slice(c0, c0 + LANES) for c0 in range(0, SEQ_TILE, LANES)]

        def post_rqk(r):
            for r0, rs in _row_blocks(SEQ_TILE):
                c, cr = r0 // CHUNK, r0 % CHUNK
                cs = slice(cr, cr + ROW_BLOCK)
                for p in range(2):
                    q = rot(r[rs, p * LANES:(p + 1) * LANES], rs)
                    rq_s[p, rs, :] = q.astype(BF16)
                    rqx_s[p, 0, rs, :] = (q * xi_ref[p, 0, cs, :]).astype(BF16)
                    rqx_s[p, 1, rs, :] = (q * xi_ref[p, 1, cs, :]).astype(BF16)
                    k = rot(r[rs, (2 + p) * LANES:(3 + p) * LANES], rs)
                    rk_s[p, c, cr:cr + ROW_BLOCK, :] = jnp.where(lo, k, 0.0).astype(BF16)
                    rk_s[p, c, CHUNK + cr:CHUNK + cr + ROW_BLOCK, :] = jnp.where(lo, 0.0, k).astype(BF16)
                    rkz_s[p, rs, :] = (k * zeta_ref[p, cs, :]).astype(BF16)

        def post_rv(r):
            for r0, rs in _row_blocks(SEQ_TILE):
                rv_s[rs, :] = r[rs, :].astype(BF16)

        def post_rg(r):
            for r0, rs in _row_blocks(SEQ_TILE):
                gate_s[rs, :] = _silu(r[rs, :])

        def post_qt(rt):
            for hd in range(SWA_HEADS):
                hs = slice(hd * SWA_HEAD_DIM, (hd + 1) * SWA_HEAD_DIM)
                for ts in tok_blocks:
                    qt_s[hs, ts] = head_rms(rt[hs, ts]).astype(BF16)

        def post_kvt(rt):
            qk_w = qnw_ref[...] * knw_ref[...] * (SWA_HEAD_DIM ** -0.5 * LOG2_E)
            for ts in tok_blocks:
                knt = jnp.concatenate(
                    [head_rms(rt[g * SWA_HEAD_DIM:(g + 1) * SWA_HEAD_DIM, ts]) for g in range(SWA_KV_HEADS)],
                    axis=0)
                kband_s[CHUNK + ts.start:CHUNK + ts.stop, :] = (knt.T * qk_w).astype(BF16)
                vt_s[:, CHUNK + ts.start:CHUNK + ts.stop] = rt[LANES:2 * LANES, ts].astype(BF16)

        def post_gt(rt):
            for r0 in range(0, SWA_HEADS * SWA_HEAD_DIM, ROW_BLOCK):
                for ts in tok_blocks:
                    gatet_s[r0:r0 + ROW_BLOCK, ts] = _silu(rt[r0:r0 + ROW_BLOCK, ts])

        groups = [
            (functools.partial(proj, OFF_RQ), post_rqk),
            (functools.partial(proj_t, FT_Q, SWA_HEADS * SWA_HEAD_DIM), post_qt),
            (functools.partial(proj, OFF_RG), post_rg),
            (functools.partial(proj_t, FT_G, SWA_HEADS * SWA_HEAD_DIM), post_gt),
            (functools.partial(proj_t, FT_K, 2 * LANES), post_kvt),
            (functools.partial(proj, OFF_RV), post_rv),
        ]
        pending = None
        for i, (matmul, post) in enumerate(groups):
            result = matmul()
            yield
            if pending is not None:
                pending[0](pending[1])
                yield
            pending = (post, result)
            if i == NORM_AFTER_GROUP:
                for r0, rs in _row_blocks(SEQ_TILE):
                    h_next_s[rs, :] = rms_norm_rows(xnext_ref[0, rs, :])
                yield
        pending[0](pending[1])
        yield

    def mix(slot, next_slot):
        rq_s, rqx_s, rk_s, rkz_s, rv_s, gate_s, qt_s, kband_s, vt_s, gatet_s = slot
        key = lax.broadcasted_iota(jnp.int32, (CHUNK, CHUNK), 0)
        qry = lax.broadcasted_iota(jnp.int32, (CHUNK, CHUNK), 1)
        tri_t = key <= qry
        cur_keep = jnp.where(tri_t, 1.0, 0.0).astype(BF16)
        prev_keep = jnp.where(tri_t, 0.0, 1.0).astype(BF16)
        zero_q = jnp.zeros((SWA_HEAD_DIM, SWA_GROUP * CHUNK), BF16)
        ones_rows = jnp.ones((8, 2 * CHUNK), BF16)

        def out_proj(c_hi):
            pair = slice((c_hi - 1) * CHUNK, (c_hi + 1) * CHUNK)
            o_ref[0, pair, :] = (xres_ref[0, pair, :]
                                 + _dot(mixed_s[pair, :], wout_ref[0:RET_HEADS * RET_DV, :])
                                 + _dot_tn(mixedt_s[:, pair], wout_ref[RET_HEADS * RET_DV:D_MIX, :]))

        chunk_rows = [slice(c * CHUNK, (c + 1) * CHUNK) for c in range(N_CHUNKS)]
        chunk_band = [slice(c * CHUNK, (c + 2) * CHUNK) for c in range(N_CHUNKS)]

        scores, incs, logits_t = {}, {}, []
        for c, rows in enumerate(chunk_rows):
            for p in range(2):
                s2 = _dot_nt(rq_s[p, rows, :], rk_s[p, c])
                scores[c, p] = jnp.concatenate([(s2[bs, :] * dec_ref[p, bs, :]).astype(BF16)
                                                for _, bs in _row_blocks(CHUNK)], axis=0)
                incs[c, p] = _dot_tn(rkz_s[p, rows, :], rv_s[rows, p * 2 * RET_DV:(p + 1) * 2 * RET_DV])
            qts = [jnp.concatenate([qt_s[(SWA_GROUP * g + hh) * SWA_HEAD_DIM:(SWA_GROUP * g + hh + 1) * SWA_HEAD_DIM,
                                         rows] for hh in range(SWA_GROUP)], axis=1)
                   for g in range(SWA_KV_HEADS)]
            rhs = jnp.concatenate([jnp.concatenate([qts[0], zero_q], axis=1),
                                   jnp.concatenate([zero_q, qts[1]], axis=1)], axis=0)
            logits_t.append(_dot(kband_s[chunk_band[c], :], rhs))
        yield

        ret_outs = {}
        for p in range(2):
            full = state_s[p] * keep_state
            for c, rows in enumerate(chunk_rows):
                full_b = full.astype(BF16)
                v2 = rv_s[rows, p * 2 * RET_DV:(p + 1) * 2 * RET_DV]
                s2 = scores[c, p]
                for j in range(2):
                    ret_outs[c, 2 * p + j] = _dot(
                        jnp.concatenate([s2[:, j * CHUNK:(j + 1) * CHUNK], rqx_s[p, j, rows, :]], axis=1),
                        jnp.concatenate([v2[:, j * RET_DV:(j + 1) * RET_DV],
                                         full_b[:, j * RET_DV:(j + 1) * RET_DV]], axis=0))
                full = full * cd_ref[p] + incs[c, p]
            state_s[p] = full
        yield

        for c in range(N_CHUNKS):
            rows, band, lt = chunk_rows[c], chunk_band[c], logits_t[c]

            for hd in range(RET_HEADS):
                o = ret_outs[c, hd]
                cols = slice(hd * RET_DV, (hd + 1) * RET_DV)
                for r0, bs in _row_blocks(CHUNK):
                    ob = o[bs, :]
                    rs = slice(c * CHUNK + r0, c * CHUNK + r0 + ROW_BLOCK)
                    mu = jnp.mean(ob, axis=-1, keepdims=True)
                    d = ob - mu
                    var = jnp.mean(d * d, axis=-1, keepdims=True)
                    y = d * lax.rsqrt(var + GN_EPS) * retw_ref[:, cols]
                    mixed_s[rs, cols] = (y * gate_s[rs, cols]).astype(BF16)
            sink_terms = []
            for hd in range(SWA_HEADS):
                g, hh = hd // SWA_GROUP, hd % SWA_GROUP
                hq = slice(hd * CHUNK, (hd + 1) * CHUNK)
                sink = sinks_ref[hd] * LOG2_E
                bias = bias_s[hd + SWA_HEADS * first_i] if c == 0 else bias_s[hd]
                lg = jnp.where(tri_t, lt[CHUNK:2 * CHUNK, hq], lt[0:CHUNK, hq]) + bias
                m = jnp.maximum(jnp.max(lg, axis=0, keepdims=True), sink)
                e = jnp.exp2(lg - m).astype(BF16)
                sink_terms.append(jnp.exp2(sink - m))
                pq = slice(hh * CHUNK, (hh + 1) * CHUNK)
                pt_s[c, g, 0:CHUNK, pq] = e * prev_keep
                pt_s[c, g, CHUNK:2 * CHUNK, pq] = e * cur_keep
            yield

            for g in range(SWA_KV_HEADS):
                vt1 = jnp.concatenate([vt_s[g * SWA_HEAD_DIM:(g + 1) * SWA_HEAD_DIM, band], ones_rows], axis=0)
                ot = _dot(vt1, pt_s[c, g])
                for hh in range(SWA_GROUP):
                    hd = SWA_GROUP * g + hh
                    hs = slice(hd * SWA_HEAD_DIM, (hd + 1) * SWA_HEAD_DIM)
                    pq = slice(hh * CHUNK, (hh + 1) * CHUNK)
                    inv = 1.0 / (ot[SWA_HEAD_DIM:SWA_HEAD_DIM + 1, pq] + sink_terms[hd])
                    mixedt_s[hs, rows] = (ot[0:SWA_HEAD_DIM, pq] * inv * gatet_s[hs, rows]).astype(BF16)
            yield
            if c % 2 == 1:
                out_proj(c)
                yield

        last = slice(SEQ_TILE, SEQ_TILE + CHUNK)
        next_slot[SLOT_KBAND][0:CHUNK, :] = kband_s[last, :]
        next_slot[SLOT_VT][:, 0:CHUNK] = vt_s[:, last]

    def run(parity, do_project=True, do_mix=True):
        streams = [project(slots[parity], h_bufs[parity], h_bufs[1 - parity]) if do_project else iter(()),
                   mix(slots[1 - parity], slots[parity]) if do_mix else iter(())]
        for which in MIX_ORDER + (0,) * len(MIX_ORDER) + (1,) * len(MIX_ORDER):
            next(streams[which], None)

    is_first, is_last = step == 0, step == n_tiles
    pl.when(is_first)(functools.partial(run, 0, do_mix=False))
    pl.when(is_last)(functools.partial(run, n_tiles % 2, do_project=False))
    for parity in range(2):
        pl.when((step % 2 == parity) & jnp.logical_not(is_first | is_last))(functools.partial(run, parity))


def kernel(x, norm_w, w_in, ret_norm_w, q_norm_w, k_norm_w, sinks, rel_bias, w_out):
    batch, seq, d_model = x.shape
    assert d_model == D_MODEL and seq % SEQ_TILE == 0
    assert w_in.shape == (D_MODEL, D_IN) and w_out.shape == (D_MIX, D_MODEL)
    tb = _constant_tables(seq)
    tiles_per_seq = seq // SEQ_TILE
    n_tiles = batch * tiles_per_seq

    def proj_tile(s):
        return jnp.minimum(s, n_tiles - 1)

    def mix_tile(s):
        return jnp.maximum(s - 1, 0)

    const2 = lambda s: (0, 0)
    const3 = lambda s: (0, 0, 0)
    const4 = lambda s: (0, 0, 0, 0)
    pos_spec = pl.BlockSpec((SEQ_TILE, LANES), lambda s: (proj_tile(s) % tiles_per_seq, 0))
    smem = pl.BlockSpec(memory_space=pltpu.SMEM)
    x_block = (1, SEQ_TILE, D_MODEL)
    mix_map = lambda s: (mix_tile(s) // tiles_per_seq, mix_tile(s) % tiles_per_seq, 0)

    in_specs = [
        pl.BlockSpec(x_block, lambda s: (proj_tile(s + 1) // tiles_per_seq, proj_tile(s + 1) % tiles_per_seq, 0)),
        pl.BlockSpec(x_block, mix_map),
        pl.BlockSpec(memory_space=pl.ANY),
        pl.BlockSpec((1, D_MODEL), const2),
        pl.BlockSpec(memory_space=pl.ANY),
        pl.BlockSpec(memory_space=pl.ANY),
        pos_spec, pos_spec, pos_spec,
        pl.BlockSpec((2, CHUNK, 2 * CHUNK), const3),
        pl.BlockSpec((2, 2, CHUNK, LANES), const4),
        pl.BlockSpec((2, CHUNK, LANES), const3),
        pl.BlockSpec((2, 1, 2 * RET_DV), const3),
        pl.BlockSpec((1, RET_HEADS * RET_DV), const2),
        pl.BlockSpec((1, LANES), const2),
        pl.BlockSpec((1, LANES), const2),
        pl.BlockSpec((CHUNK, CHUNK), const2),
        smem, smem,
    ]
    slot_bufs = [
        pltpu.VMEM((2, SEQ_TILE, LANES), BF16),
        pltpu.VMEM((2, 2, SEQ_TILE, LANES), BF16),
        pltpu.VMEM((2, N_CHUNKS, 2 * CHUNK, LANES), BF16),
        pltpu.VMEM((2, SEQ_TILE, LANES), BF16),
        pltpu.VMEM((SEQ_TILE, RET_HEADS * RET_DV), BF16),
        pltpu.VMEM((SEQ_TILE, RET_HEADS * RET_DV), F32),
        pltpu.VMEM((SWA_HEADS * SWA_HEAD_DIM, SEQ_TILE), BF16),
        pltpu.VMEM((SEQ_TILE + CHUNK, LANES), BF16),
        pltpu.VMEM((SWA_KV_HEADS * SWA_HEAD_DIM, SEQ_TILE + CHUNK), BF16),
        pltpu.VMEM((SWA_HEADS * SWA_HEAD_DIM, SEQ_TILE), F32),
    ]
    assert len(slot_bufs) == N_SLOT_BUFS
    h_buf = pltpu.VMEM((SEQ_TILE, D_MODEL), BF16)
    scratch = [h_buf, h_buf] + slot_bufs + slot_bufs + [
        pltpu.VMEM((N_CHUNKS, SWA_KV_HEADS, 2 * CHUNK, SWA_GROUP * CHUNK), BF16),
        pltpu.VMEM((SEQ_TILE, RET_HEADS * RET_DV), BF16),
        pltpu.VMEM((SWA_HEADS * SWA_HEAD_DIM, SEQ_TILE), BF16),
        pltpu.VMEM((2, 2 * RET_DK, 2 * RET_DV), F32),
        pltpu.VMEM((2 * SWA_HEADS, CHUNK, CHUNK), F32),
        pltpu.VMEM((D_MODEL, D_TOK), BF16),
        pltpu.VMEM((D_FT, D_MODEL), BF16),
        pltpu.VMEM((D_MIX, D_MODEL), BF16),
        pltpu.VMEM((2, STAGE_ROWS, D_IN), F32),
        pltpu.VMEM((2, STAGE_ROWS, D_MODEL), F32),
        pltpu.SemaphoreType.DMA((4,)),
    ]
    return pl.pallas_call(
        functools.partial(_layer_kernel, tiles_per_seq, n_tiles),
        grid=(n_tiles + 1,),
        in_specs=in_specs,
        out_specs=pl.BlockSpec(x_block, mix_map),
        out_shape=jax.ShapeDtypeStruct(x.shape, x.dtype),
        scratch_shapes=scratch,
        compiler_params=pltpu.CompilerParams(
            dimension_semantics=("arbitrary",),
            vmem_limit_bytes=VMEM_LIMIT_BYTES),
    )(x, x, x, norm_w.reshape(1, D_MODEL), w_in, w_out,
      tb["cos"], tb["sin_a"], tb["sin_b"], tb["dec"], tb["xi"], tb["zeta"], tb["cd"],
      ret_norm_w.reshape(1, -1), jnp.tile(q_norm_w, SWA_KV_HEADS).reshape(1, -1),
      jnp.tile(k_norm_w, SWA_KV_HEADS).reshape(1, -1), tb["bucket"],
      rel_bias.astype(F32), sinks.astype(F32))
```

```python
import functools
import math

import numpy as np
import jax
import jax.numpy as jnp
from jax import lax
from jax.experimental import pallas as pl
from jax.experimental.pallas import tpu as pltpu

D_MODEL = 1024
RET_HEADS = 4
RET_DK = 64
RET_DV = 128
CHUNK = 128
RET_ROPE_BASE = 10000.0
SWA_HEADS = 8
SWA_KV_HEADS = 2
SWA_HEAD_DIM = 64
SWA_GROUP = SWA_HEADS // SWA_KV_HEADS
NUM_BUCKETS = 32
MAX_DISTANCE = 128
NORM_EPS = 1e-6
GN_EPS = 1e-5
NEG_INF = -1e30
LOG2_E = math.log2(math.e)

OFF_RQ, OFF_RK, OFF_RV, OFF_RG = 0, 256, 512, 1024
D_TOK = 1536
FT_Q, FT_K, FT_G = 0, 512, 768
D_FT = 1280
D_IN = D_TOK + D_FT
D_MIX = 1024

LANES = 128
SEQ_TILE = 512
N_CHUNKS = SEQ_TILE // CHUNK
ROW_BLOCK = 32
VMEM_LIMIT_BYTES = 56 * 1024 * 1024
MIX_ORDER = (1, 1, 0, 0, 1, 1, 0, 0, 1, 1, 0, 0, 1, 0, 1, 1, 0, 0, 1, 1, 0, 0, 1, 0, 0)
NORM_AFTER_GROUP = 2
N_SLOT_BUFS = 10
STAGE_ROWS = 128
SLOT_KBAND, SLOT_VT = 7, 8

BF16 = jnp.bfloat16
F32 = jnp.float32


def _t5_bucket_np(n):
    max_exact = NUM_BUCKETS // 2
    nf = np.maximum(n, 1).astype(np.float64)
    large = max_exact + (np.log(nf / max_exact) / math.log(MAX_DISTANCE / max_exact)
                         * (NUM_BUCKETS - max_exact)).astype(np.int32)
    large = np.minimum(large, NUM_BUCKETS - 1)
    return np.where(n < max_exact, n, large).astype(np.int32)


def _constant_tables(seq):
    half = RET_DK // 2
    inv_freq = RET_ROPE_BASE ** (-np.arange(half, dtype=np.float64) / half)
    ang = np.arange(seq, dtype=np.float64)[:, None] * inv_freq[None, :]
    cos, sin = np.cos(ang), np.sin(ang)
    cos_t = np.tile(np.concatenate([cos, cos], axis=1), (1, 2))
    sin_s = np.tile(np.concatenate([-sin, sin], axis=1), (1, 2))

    gamma = 1.0 - np.exp2(-5.0 - np.arange(RET_HEADS, dtype=np.float64))
    log_g = np.log(gamma)
    i = np.arange(CHUNK, dtype=np.float64)
    diff = i[:, None] - i[None, :]
    k_scale = RET_DK ** -0.5
    decay = np.where(diff >= 0, np.exp(log_g[:, None, None] * np.maximum(diff, 0.0)), 0.0) * k_scale
    dec = np.stack([np.concatenate([decay[2 * p], decay[2 * p + 1]], axis=1) for p in range(2)])
    xi = np.exp(log_g[:, None] * (i + 1.0))
    zeta = np.exp(log_g[:, None] * (CHUNK - 1.0 - i)) * k_scale

    def per_lane(t):
        return np.stack([np.concatenate([np.repeat(t[2 * p][:, None], RET_DK, 1),
                                         np.repeat(t[2 * p + 1][:, None], RET_DK, 1)], axis=1)
                         for p in range(2)])

    lo = (np.arange(LANES) < RET_DK)[None, None, :]
    xi_l = per_lane(xi)
    xi_m = np.stack([np.where(lo, xi_l, 0.0), np.where(lo, 0.0, xi_l)], axis=1)

    chunk_decay = np.exp(log_g * CHUNK)
    cd = np.stack([np.concatenate([np.full((1, RET_DV), chunk_decay[2 * p]),
                                   np.full((1, RET_DV), chunk_decay[2 * p + 1])], axis=1)
                   for p in range(2)])

    j = np.arange(CHUNK)[:, None]
    q = np.arange(CHUNK)[None, :]
    dist = np.where(j <= q, q - j, q + CHUNK - j)
    bucket = _t5_bucket_np(dist)

    f = lambda a: jnp.asarray(a, dtype=F32)
    col_scale = np.ones((1, D_IN))
    col_scale[:, OFF_RG:OFF_RG + RET_HEADS * RET_DV] = 0.5
    col_scale[:, D_TOK + FT_G:D_TOK + FT_G + SWA_HEADS * SWA_HEAD_DIM] = 0.5

    return dict(cos=f(cos_t), sin_s=f(sin_s), col_scale=f(col_scale), dec=f(dec), xi=f(xi_m),
                zeta=f(per_lane(zeta)), cd=f(cd), bucket=jnp.asarray(bucket, dtype=jnp.int32))


def _dot(a, b):
    return jnp.dot(a, b, preferred_element_type=F32)


def _dot_nt(a, b):
    return lax.dot_general(a, b, (((1,), (1,)), ((), ())), preferred_element_type=F32)


def _dot_tn(a, b):
    return lax.dot_general(a, b, (((0,), (0,)), ((), ())), preferred_element_type=F32)


def _silu_of_half(hg):
    return hg + hg * jnp.tanh(hg)


def _row_blocks(n_rows):
    return [(r0, slice(r0, r0 + ROW_BLOCK)) for r0 in range(0, n_rows, ROW_BLOCK)]


def _layer_kernel(tiles_per_seq, n_tiles,
                  xnext_ref, xres_ref, x_hbm, nwcol_ref, colscale_ref, win_hbm, wout_hbm, cos_ref, sins_ref, dec_ref,
                  xi_ref, zeta_ref, cd_ref, retw_ref, qnw_ref, knw_ref, bkt_ref, relb_ref,
                  sinks_ref, o_ref, h0_s, h1_s, *scratch):
    h_bufs = (h0_s, h1_s)
    slots = (scratch[0:N_SLOT_BUFS], scratch[N_SLOT_BUFS:2 * N_SLOT_BUFS])
    (pt_s, mixed_s, mixedt_s, state_s, bias_s,
     win_ref, wft_ref, wout_ref, stage_in, stage_out, stage_sem) = scratch[2 * N_SLOT_BUFS:]
    step = pl.program_id(0)
    mix_tile = jnp.maximum(step - 1, 0)
    first = mix_tile % tiles_per_seq == 0
    first_i = first.astype(jnp.int32)
    keep_state = jnp.where(first, 0.0, 1.0).astype(F32)

    def rms_norm_rows(xb):
        ms = jnp.mean(xb * xb, axis=-1, keepdims=True)
        return (xb * lax.rsqrt(ms + NORM_EPS)).astype(BF16)

    @pl.when(step == 0)
    def _init():
        bk = bkt_ref[...]
        key = lax.broadcasted_iota(jnp.int32, (CHUNK, CHUNK), 0)
        qry = lax.broadcasted_iota(jnp.int32, (CHUNK, CHUNK), 1)
        for h in range(SWA_HEADS):
            acc = jnp.zeros((CHUNK, CHUNK), F32)
            for u in range(NUM_BUCKETS):
                acc = jnp.where(bk == u, relb_ref[u, h] * LOG2_E, acc)
            bias_s[h] = acc
            bias_s[SWA_HEADS + h] = jnp.where(key <= qry, acc, NEG_INF)
        state_s[...] = jnp.zeros_like(state_s)
        slots[0][SLOT_KBAND][0:CHUNK, :] = jnp.zeros((CHUNK, LANES), BF16)
        slots[0][SLOT_VT][:, 0:CHUNK] = jnp.zeros((SWA_KV_HEADS * SWA_HEAD_DIM, CHUNK), BF16)

        def chunk_copies(hbm, stage, sem0, n_rows):
            return [pltpu.make_async_copy(hbm.at[pl.ds(i * STAGE_ROWS, STAGE_ROWS), :],
                                          stage.at[i % 2], stage_sem.at[sem0 + i % 2])
                    for i in range(n_rows // STAGE_ROWS)]

        def staged(copies, consume):
            copies[0].start()
            for i, cp in enumerate(copies):
                if i + 1 < len(copies):
                    copies[i + 1].start()
                cp.wait()
                consume(i)

        def consume_in(i):
            rows = slice(i * STAGE_ROWS, (i + 1) * STAGE_ROWS)
            row_w = nwcol_ref[rows, :]
            for c0 in range(0, D_TOK, LANES):
                win_ref[rows, c0:c0 + LANES] = (stage_in[i % 2, :, c0:c0 + LANES] * row_w
                                                * colscale_ref[:, c0:c0 + LANES]).astype(BF16)
            for f0 in range(0, D_FT, LANES):
                cols = slice(D_TOK + f0, D_TOK + f0 + LANES)
                wft_ref[f0:f0 + LANES, rows] = (stage_in[i % 2, :, cols] * row_w * colscale_ref[:, cols]).T.astype(BF16)

        def consume_out(i):
            rows = slice(i * STAGE_ROWS, (i + 1) * STAGE_ROWS)
            wout_ref[rows, :] = stage_out[i % 2].astype(BF16)

        def consume_x0(i):
            for r0, rs in _row_blocks(STAGE_ROWS):
                h0_s[i * STAGE_ROWS + r0:i * STAGE_ROWS + r0 + ROW_BLOCK, :] = rms_norm_rows(stage_out[i % 2, rs, :])

        staged(chunk_copies(win_hbm, stage_in, 0, D_MODEL), consume_in)
        staged(chunk_copies(wout_hbm, stage_out, 2, D_MIX), consume_out)
        staged(chunk_copies(x_hbm.at[0], stage_out, 2, SEQ_TILE), consume_x0)

    lane = lax.broadcasted_iota(jnp.int32, (ROW_BLOCK, LANES), 1)
    lo = lane < RET_DK
    first_half = lane % RET_DK < RET_DK // 2

    def project(slot, h_s, h_next_s):
        rq_s, rqx_s, rk_s, rkz_s, rv_s, gate_s, qt_s, kband_s, vt_s, gatet_s = slot

        def proj(off):
            return _dot(h_s[...], win_ref[:, off:off + 4 * LANES])

        def proj_t(row0, n_rows):
            return _dot_nt(wft_ref[row0:row0 + n_rows, :], h_s[...])

        def rot(v, rs):
            partner = jnp.where(first_half, pltpu.roll(v, LANES - 32, 1), pltpu.roll(v, 32, 1))
            return v * cos_ref[rs, :] + partner * sins_ref[rs, :]

        def head_rms(xt):
            ms = jnp.mean(xt * xt, axis=0, keepdims=True)
            return xt * lax.rsqrt(ms + NORM_EPS)

        tok_blocks = [slice(c0, c0 + LANES) for c0 in range(0, SEQ_TILE, LANES)]

        def post_rqk(r):
            for r0, rs in _row_blocks(SEQ_TILE):
                c, cr = r0 // CHUNK, r0 % CHUNK
                cs = slice(cr, cr + ROW_BLOCK)
                for p in range(2):
                    q = rot(r[rs, p * LANES:(p + 1) * LANES], rs)
                    rq_s[p, rs, :] = q.astype(BF16)
                    rqx_s[p, 0, rs, :] = (q * xi_ref[p, 0, cs, :]).astype(BF16)
                    rqx_s[p, 1, rs, :] = (q * xi_ref[p, 1, cs, :]).astype(BF16)
                    k = rot(r[rs, (2 + p) * LANES:(3 + p) * LANES], rs)
                    rk_s[p, c, cr:cr + ROW_BLOCK, :] = jnp.where(lo, k, 0.0).astype(BF16)
                    rk_s[p, c, CHUNK + cr:CHUNK + cr + ROW_BLOCK, :] = jnp.where(lo, 0.0, k).astype(BF16)
                    rkz_s[p, rs, :] = (k * zeta_ref[p, cs, :]).astype(BF16)

        def post_rv(r):
            for r0, rs in _row_blocks(SEQ_TILE):
                rv_s[rs, :] = r[rs, :].astype(BF16)

        def post_rg(r):
            for r0, rs in _row_blocks(SEQ_TILE):
                gate_s[rs, :] = _silu_of_half(r[rs, :])

        def post_qt(rt):
            for hd in range(SWA_HEADS):
                hs = slice(hd * SWA_HEAD_DIM, (hd + 1) * SWA_HEAD_DIM)
                for ts in tok_blocks:
                    qt_s[hs, ts] = head_rms(rt[hs, ts]).astype(BF16)

        def post_kvt(rt):
            qk_w = qnw_ref[...] * knw_ref[...] * (SWA_HEAD_DIM ** -0.5 * LOG2_E)
            for ts in tok_blocks:
                knt = jnp.concatenate(
                    [head_rms(rt[g * SWA_HEAD_DIM:(g + 1) * SWA_HEAD_DIM, ts]) for g in range(SWA_KV_HEADS)],
                    axis=0)
                kband_s[CHUNK + ts.start:CHUNK + ts.stop, :] = (knt.T * qk_w).astype(BF16)
                vt_s[:, CHUNK + ts.start:CHUNK + ts.stop] = rt[LANES:2 * LANES, ts].astype(BF16)

        def post_gt(rt):
            for r0 in range(0, SWA_HEADS * SWA_HEAD_DIM, ROW_BLOCK):
                for ts in tok_blocks:
                    gatet_s[r0:r0 + ROW_BLOCK, ts] = _silu_of_half(rt[r0:r0 + ROW_BLOCK, ts])

        groups = [
            (functools.partial(proj, OFF_RQ), post_rqk),
            (functools.partial(proj_t, FT_Q, SWA_HEADS * SWA_HEAD_DIM), post_qt),
            (functools.partial(proj, OFF_RG), post_rg),
            (functools.partial(proj_t, FT_G, SWA_HEADS * SWA_HEAD_DIM), post_gt),
            (functools.partial(proj_t, FT_K, 2 * LANES), post_kvt),
            (functools.partial(proj, OFF_RV), post_rv),
        ]
        pending = None
        for i, (matmul, post) in enumerate(groups):
            result = matmul()
            yield
            if pending is not None:
                pending[0](pending[1])
                yield
            pending = (post, result)
            if i == NORM_AFTER_GROUP:
                for r0, rs in _row_blocks(SEQ_TILE):
                    h_next_s[rs, :] = rms_norm_rows(xnext_ref[0, rs, :])
                yield
        pending[0](pending[1])
        yield

    def mix(slot, next_slot):
        rq_s, rqx_s, rk_s, rkz_s, rv_s, gate_s, qt_s, kband_s, vt_s, gatet_s = slot
        key = lax.broadcasted_iota(jnp.int32, (CHUNK, CHUNK), 0)
        qry = lax.broadcasted_iota(jnp.int32, (CHUNK, CHUNK), 1)
        tri_t = key <= qry
        cur_keep = jnp.where(tri_t, 1.0, 0.0).astype(BF16)
        prev_keep = jnp.where(tri_t, 0.0, 1.0).astype(BF16)
        zero_q = jnp.zeros((SWA_HEAD_DIM, SWA_GROUP * CHUNK), BF16)
        ones_rows = jnp.ones((8, 2 * CHUNK), BF16)

        def out_proj(c_hi):
            pair = slice((c_hi - 1) * CHUNK, (c_hi + 1) * CHUNK)
            o_ref[0, pair, :] = (xres_ref[0, pair, :]
                                 + _dot(mixed_s[pair, :], wout_ref[0:RET_HEADS * RET_DV, :])
                                 + _dot_tn(mixedt_s[:, pair], wout_ref[RET_HEADS * RET_DV:D_MIX, :]))

        chunk_rows = [slice(c * CHUNK, (c + 1) * CHUNK) for c in range(N_CHUNKS)]
        chunk_band = [slice(c * CHUNK, (c + 2) * CHUNK) for c in range(N_CHUNKS)]

        def logits_for(c):
            rows = chunk_rows[c]
            qts = [jnp.concatenate([qt_s[(SWA_GROUP * g + hh) * SWA_HEAD_DIM:(SWA_GROUP * g + hh + 1) * SWA_HEAD_DIM,
                                         rows] for hh in range(SWA_GROUP)], axis=1)
                   for g in range(SWA_KV_HEADS)]
            rhs = jnp.concatenate([jnp.concatenate([qts[0], zero_q], axis=1),
                                   jnp.concatenate([zero_q, qts[1]], axis=1)], axis=0)
            return _dot(kband_s[chunk_band[c], :], rhs)

        scores, incs, logits_t = {}, {}, []
        for c, rows in enumerate(chunk_rows):
            for p in range(2):
                s2 = _dot_nt(rq_s[p, rows, :], rk_s[p, c])
                scores[c, p] = jnp.concatenate([(s2[bs, :] * dec_ref[p, bs, :]).astype(BF16)
                                                for _, bs in _row_blocks(CHUNK)], axis=0)
                incs[c, p] = _dot_tn(rkz_s[p, rows, :], rv_s[rows, p * 2 * RET_DV:(p + 1) * 2 * RET_DV])
            logits_t.append(logits_for(c))
        yield

        ret_outs = {}
        for p in range(2):
            full = state_s[p] * keep_state
            for c, rows in enumerate(chunk_rows):
                full_b = full.astype(BF16)
                v2 = rv_s[rows, p * 2 * RET_DV:(p + 1) * 2 * RET_DV]
                s2 = scores[c, p]
                for j in range(2):
                    ret_outs[c, 2 * p + j] = _dot(
                        jnp.concatenate([s2[:, j * CHUNK:(j + 1) * CHUNK], rqx_s[p, j, rows, :]], axis=1),
                        jnp.concatenate([v2[:, j * RET_DV:(j + 1) * RET_DV],
                                         full_b[:, j * RET_DV:(j + 1) * RET_DV]], axis=0))
                full = full * cd_ref[p] + incs[c, p]
            state_s[p] = full
        yield

        for c in range(N_CHUNKS):
            rows, band, lt = chunk_rows[c], chunk_band[c], logits_t[c]

            for hd in range(RET_HEADS):
                o = ret_outs[c, hd]
                cols = slice(hd * RET_DV, (hd + 1) * RET_DV)
                for r0, bs in _row_blocks(CHUNK):
                    ob = o[bs, :]
                    rs = slice(c * CHUNK + r0, c * CHUNK + r0 + ROW_BLOCK)
                    mu = jnp.mean(ob, axis=-1, keepdims=True)
                    d = ob - mu
                    var = jnp.mean(d * d, axis=-1, keepdims=True)
                    y = d * lax.rsqrt(var + GN_EPS) * retw_ref[:, cols]
                    mixed_s[rs, cols] = (y * gate_s[rs, cols]).astype(BF16)
            sink_terms = []
            for hd in range(SWA_HEADS):
                g, hh = hd // SWA_GROUP, hd % SWA_GROUP
                hq = slice(hd * CHUNK, (hd + 1) * CHUNK)
                sink = sinks_ref[hd] * LOG2_E
                bias = bias_s[hd + SWA_HEADS * first_i] if c == 0 else bias_s[hd]
                lg = jnp.where(tri_t, lt[CHUNK:2 * CHUNK, hq], lt[0:CHUNK, hq]) + bias
                m = jnp.maximum(jnp.max(lg, axis=0, keepdims=True), sink)
                e = jnp.exp2(lg - m).astype(BF16)
                sink_terms.append(jnp.exp2(sink - m))
                pq = slice(hh * CHUNK, (hh + 1) * CHUNK)
                pt_s[c, g, 0:CHUNK, pq] = e * prev_keep
                pt_s[c, g, CHUNK:2 * CHUNK, pq] = e * cur_keep
            yield

            for g in range(SWA_KV_HEADS):
                vt1 = jnp.concatenate([vt_s[g * SWA_HEAD_DIM:(g + 1) * SWA_HEAD_DIM, band], ones_rows], axis=0)
                ot = _dot(vt1, pt_s[c, g])
                for hh in range(SWA_GROUP):
                    hd = SWA_GROUP * g + hh
                    hs = slice(hd * SWA_HEAD_DIM, (hd + 1) * SWA_HEAD_DIM)
                    pq = slice(hh * CHUNK, (hh + 1) * CHUNK)
                    inv = 1.0 / (ot[SWA_HEAD_DIM:SWA_HEAD_DIM + 1, pq] + sink_terms[hd])
                    mixedt_s[hs, rows] = (ot[0:SWA_HEAD_DIM, pq] * inv * gatet_s[hs, rows]).astype(BF16)
            yield
            if c % 2 == 1:
                out_proj(c)
                yield

        last = slice(SEQ_TILE, SEQ_TILE + CHUNK)
        next_slot[SLOT_KBAND][0:CHUNK, :] = kband_s[last, :]
        next_slot[SLOT_VT][:, 0:CHUNK] = vt_s[:, last]

    def run(parity, do_project=True, do_mix=True):
        streams = [project(slots[parity], h_bufs[parity], h_bufs[1 - parity]) if do_project else iter(()),
                   mix(slots[1 - parity], slots[parity]) if do_mix else iter(())]
        for which in MIX_ORDER + (0,) * len(MIX_ORDER) + (1,) * len(MIX_ORDER):
            next(streams[which], None)

    is_first, is_last = step == 0, step == n_tiles
    pl.when(is_first)(functools.partial(run, 0, do_mix=False))
    pl.when(is_last)(functools.partial(run, n_tiles % 2, do_project=False))
    for parity in range(2):
        pl.when((step % 2 == parity) & jnp.logical_not(is_first | is_last))(functools.partial(run, parity))


def kernel(x, norm_w, w_in, ret_norm_w, q_norm_w, k_norm_w, sinks, rel_bias, w_out):
    batch, seq, d_model = x.shape
    assert d_model == D_MODEL and seq % SEQ_TILE == 0
    assert w_in.shape == (D_MODEL, D_IN) and w_out.shape == (D_MIX, D_MODEL)
    tb = _constant_tables(seq)
    tiles_per_seq = seq // SEQ_TILE
    n_tiles = batch * tiles_per_seq

    def proj_tile(s):
        return jnp.minimum(s, n_tiles - 1)

    def mix_tile(s):
        return jnp.maximum(s - 1, 0)

    const2 = lambda s: (0, 0)
    const3 = lambda s: (0, 0, 0)
    const4 = lambda s: (0, 0, 0, 0)
    pos_spec = pl.BlockSpec((SEQ_TILE, LANES), lambda s: (proj_tile(s) % tiles_per_seq, 0))
    smem = pl.BlockSpec(memory_space=pltpu.SMEM)
    x_block = (1, SEQ_TILE, D_MODEL)
    mix_map = lambda s: (mix_tile(s) // tiles_per_seq, mix_tile(s) % tiles_per_seq, 0)

    in_specs = [
        pl.BlockSpec(x_block, lambda s: (proj_tile(s + 1) // tiles_per_seq, proj_tile(s + 1) % tiles_per_seq, 0)),
        pl.BlockSpec(x_block, mix_map),
        pl.BlockSpec(memory_space=pl.ANY),
        pl.BlockSpec((D_MODEL, 1), const2),
        pl.BlockSpec((1, D_IN), const2),
        pl.BlockSpec(memory_space=pl.ANY),
        pl.BlockSpec(memory_space=pl.ANY),
        pos_spec, pos_spec,
        pl.BlockSpec((2, CHUNK, 2 * CHUNK), const3),
        pl.BlockSpec((2, 2, CHUNK, LANES), const4),
        pl.BlockSpec((2, CHUNK, LANES), const3),
        pl.BlockSpec((2, 1, 2 * RET_DV), const3),
        pl.BlockSpec((1, RET_HEADS * RET_DV), const2),
        pl.BlockSpec((1, LANES), const2),
        pl.BlockSpec((1, LANES), const2),
        pl.BlockSpec((CHUNK, CHUNK), const2),
        smem, smem,
    ]
    slot_bufs = [
        pltpu.VMEM((2, SEQ_TILE, LANES), BF16),
        pltpu.VMEM((2, 2, SEQ_TILE, LANES), BF16),
        pltpu.VMEM((2, N_CHUNKS, 2 * CHUNK, LANES), BF16),
        pltpu.VMEM((2, SEQ_TILE, LANES), BF16),
        pltpu.VMEM((SEQ_TILE, RET_HEADS * RET_DV), BF16),
        pltpu.VMEM((SEQ_TILE, RET_HEADS * RET_DV), F32),
        pltpu.VMEM((SWA_HEADS * SWA_HEAD_DIM, SEQ_TILE), BF16),
        pltpu.VMEM((SEQ_TILE + CHUNK, LANES), BF16),
        pltpu.VMEM((SWA_KV_HEADS * SWA_HEAD_DIM, SEQ_TILE + CHUNK), BF16),
        pltpu.VMEM((SWA_HEADS * SWA_HEAD_DIM, SEQ_TILE), F32),
    ]
    assert len(slot_bufs) == N_SLOT_BUFS
    h_buf = pltpu.VMEM((SEQ_TILE, D_MODEL), BF16)
    scratch = [h_buf, h_buf] + slot_bufs + slot_bufs + [
        pltpu.VMEM((N_CHUNKS, SWA_KV_HEADS, 2 * CHUNK, SWA_GROUP * CHUNK), BF16),
        pltpu.VMEM((SEQ_TILE, RET_HEADS * RET_DV), BF16),
        pltpu.VMEM((SWA_HEADS * SWA_HEAD_DIM, SEQ_TILE), BF16),
        pltpu.VMEM((2, 2 * RET_DK, 2 * RET_DV), F32),
        pltpu.VMEM((2 * SWA_HEADS, CHUNK, CHUNK), F32),
        pltpu.VMEM((D_MODEL, D_TOK), BF16),
        pltpu.VMEM((D_FT, D_MODEL), BF16),
        pltpu.VMEM((D_MIX, D_MODEL), BF16),
        pltpu.VMEM((2, STAGE_ROWS, D_IN), F32),
        pltpu.VMEM((2, STAGE_ROWS, D_MODEL), F32),
        pltpu.SemaphoreType.DMA((4,)),
    ]
    return pl.pallas_call(
        functools.partial(_layer_kernel, tiles_per_seq, n_tiles),
        grid=(n_tiles + 1,),
        in_specs=in_specs,
        out_specs=pl.BlockSpec(x_block, mix_map),
        out_shape=jax.ShapeDtypeStruct(x.shape, x.dtype),
        scratch_shapes=scratch,
        compiler_params=pltpu.CompilerParams(
            dimension_semantics=("arbitrary",),
            vmem_limit_bytes=VMEM_LIMIT_BYTES),
    )(x, x, x, norm_w.reshape(D_MODEL, 1), tb["col_scale"], w_in, w_out,
      tb["cos"], tb["sin_s"], tb["dec"], tb["xi"], tb["zeta"], tb["cd"],
      ret_norm_w.reshape(1, -1), jnp.tile(q_norm_w, SWA_KV_HEADS).reshape(1, -1),
      jnp.tile(k_norm_w, SWA_KV_HEADS).reshape(1, -1), tb["bucket"],
      rel_bias.astype(F32), sinks.astype(F32))
```

```python
import functools
import math

import numpy as np
import jax
import jax.numpy as jnp
from jax import lax
from jax.experimental import pallas as pl
from jax.experimental.pallas import tpu as pltpu

D_MODEL = 1024
RET_HEADS = 4
RET_DK = 64
RET_DV = 128
CHUNK = 128
RET_ROPE_BASE = 10000.0
SWA_HEADS = 8
SWA_KV_HEADS = 2
SWA_HEAD_DIM = 64
SWA_GROUP = SWA_HEADS // SWA_KV_HEADS
NUM_BUCKETS = 32
MAX_DISTANCE = 128
NORM_EPS = 1e-6
GN_EPS = 1e-5
NEG_INF = -1e30
LOG2_E = math.log2(math.e)

OFF_RQ, OFF_RK, OFF_RV, OFF_RG = 0, 256, 512, 1024
D_TOK = 1536
FT_Q, FT_K, FT_G = 0, 512, 768
D_FT = 1280
D_IN = D_TOK + D_FT
D_MIX = 1024

LANES = 128
SEQ_TILE = 512
N_CHUNKS = SEQ_TILE // CHUNK
ROW_BLOCK = 32
VMEM_LIMIT_BYTES = 56 * 1024 * 1024
MIX_ORDER = (1, 1, 0, 0, 1, 1, 0, 0, 1, 1, 0, 0, 1, 0, 1, 1, 0, 0, 1, 1, 0, 0, 1, 0, 0)
NORM_AFTER_GROUP = 2
N_SLOT_BUFS = 10
STAGE_ROWS = 128
SLOT_KBAND, SLOT_VT = 7, 8

BF16 = jnp.bfloat16
F32 = jnp.float32


def _t5_bucket_np(n):
    max_exact = NUM_BUCKETS // 2
    nf = np.maximum(n, 1).astype(np.float64)
    large = max_exact + (np.log(nf / max_exact) / math.log(MAX_DISTANCE / max_exact)
                         * (NUM_BUCKETS - max_exact)).astype(np.int32)
    large = np.minimum(large, NUM_BUCKETS - 1)
    return np.where(n < max_exact, n, large).astype(np.int32)


def _constant_tables(seq):
    half = RET_DK // 2
    inv_freq = RET_ROPE_BASE ** (-np.arange(half, dtype=np.float64) / half)
    ang = np.arange(seq, dtype=np.float64)[:, None] * inv_freq[None, :]
    cos, sin = np.cos(ang), np.sin(ang)
    cos_t = np.tile(np.concatenate([cos, cos], axis=1), (1, 2))
    sin_s = np.tile(np.concatenate([-sin, sin], axis=1), (1, 2))

    gamma = 1.0 - np.exp2(-5.0 - np.arange(RET_HEADS, dtype=np.float64))
    log_g = np.log(gamma)
    i = np.arange(CHUNK, dtype=np.float64)
    diff = i[:, None] - i[None, :]
    k_scale = RET_DK ** -0.5
    decay = np.where(diff >= 0, np.exp(log_g[:, None, None] * np.maximum(diff, 0.0)), 0.0) * k_scale
    dec = np.stack([np.concatenate([decay[2 * p], decay[2 * p + 1]], axis=1) for p in range(2)])
    xi = np.exp(log_g[:, None] * (i + 1.0))
    zeta = np.exp(log_g[:, None] * (CHUNK - 1.0 - i)) * k_scale

    def per_lane(t):
        return np.stack([np.concatenate([np.repeat(t[2 * p][:, None], RET_DK, 1),
                                         np.repeat(t[2 * p + 1][:, None], RET_DK, 1)], axis=1)
                         for p in range(2)])

    lo = (np.arange(LANES) < RET_DK)[None, None, :]
    xi_l = per_lane(xi)
    xi_m = np.stack([np.where(lo, xi_l, 0.0), np.where(lo, 0.0, xi_l)], axis=1)

    chunk_decay = np.exp(log_g * CHUNK)
    cd = np.stack([np.concatenate([np.full((1, RET_DV), chunk_decay[2 * p]),
                                   np.full((1, RET_DV), chunk_decay[2 * p + 1])], axis=1)
                   for p in range(2)])

    j = np.arange(CHUNK)[:, None]
    q = np.arange(CHUNK)[None, :]
    dist = np.where(j <= q, q - j, q + CHUNK - j)
    bucket = _t5_bucket_np(dist)

    f = lambda a: jnp.asarray(a, dtype=F32)
    col_scale = np.ones((1, D_IN))
    col_scale[:, OFF_RG:OFF_RG + RET_HEADS * RET_DV] = 0.5
    col_scale[:, D_TOK + FT_G:D_TOK + FT_G + SWA_HEADS * SWA_HEAD_DIM] = 0.5

    return dict(cos=f(cos_t), sin_s=f(sin_s), col_scale=f(col_scale), dec=f(dec), xi=f(xi_m),
                zeta=f(per_lane(zeta)), cd=f(cd), bucket=jnp.asarray(bucket, dtype=jnp.int32))


def _dot(a, b):
    return jnp.dot(a, b, preferred_element_type=F32)


def _dot_nt(a, b):
    return lax.dot_general(a, b, (((1,), (1,)), ((), ())), preferred_element_type=F32)


def _dot_tn(a, b):
    return lax.dot_general(a, b, (((0,), (0,)), ((), ())), preferred_element_type=F32)


def _silu_of_half(hg):
    return hg + hg * jnp.tanh(hg)


def _row_blocks(n_rows):
    return [(r0, slice(r0, r0 + ROW_BLOCK)) for r0 in range(0, n_rows, ROW_BLOCK)]


def _layer_kernel(tiles_per_seq, n_tiles,
                  xnext_ref, xres_ref, x_hbm, nwrow_ref, colscale_ref, win_hbm, wout_hbm, cos_ref, sins_ref, dec_ref,
                  xi_ref, zeta_ref, cd_ref, retw_ref, qnw_ref, knw_ref, bkt_ref, relb_ref,
                  sinks_ref, o_ref, h0_s, h1_s, *scratch):
    h_bufs = (h0_s, h1_s)
    slots = (scratch[0:N_SLOT_BUFS], scratch[N_SLOT_BUFS:2 * N_SLOT_BUFS])
    (pt_s, mixed_s, mixedt_s, state_s, bias_s,
     win_ref, wft_ref, wout_ref, stage_in, stage_out, stage_sem) = scratch[2 * N_SLOT_BUFS:]
    step = pl.program_id(0)
    mix_tile = jnp.maximum(step - 1, 0)
    first = mix_tile % tiles_per_seq == 0
    first_i = first.astype(jnp.int32)
    keep_state = jnp.where(first, 0.0, 1.0).astype(F32)

    def rms_norm_rows(xb):
        ms = jnp.mean(xb * xb, axis=-1, keepdims=True)
        return (xb * lax.rsqrt(ms + NORM_EPS)).astype(BF16)

    @pl.when(step == 0)
    def _init():
        def build_tables():
            bk = bkt_ref[...]
            key = lax.broadcasted_iota(jnp.int32, (CHUNK, CHUNK), 0)
            qry = lax.broadcasted_iota(jnp.int32, (CHUNK, CHUNK), 1)
            for h in range(SWA_HEADS):
                acc = jnp.zeros((CHUNK, CHUNK), F32)
                for u in range(NUM_BUCKETS):
                    acc = jnp.where(bk == u, relb_ref[u, h] * LOG2_E, acc)
                bias_s[h] = acc
                bias_s[SWA_HEADS + h] = jnp.where(key <= qry, acc, NEG_INF)
            state_s[...] = jnp.zeros_like(state_s)
            slots[0][SLOT_KBAND][0:CHUNK, :] = jnp.zeros((CHUNK, LANES), BF16)
            slots[0][SLOT_VT][:, 0:CHUNK] = jnp.zeros((SWA_KV_HEADS * SWA_HEAD_DIM, CHUNK), BF16)

        def chunk_copies(hbm, stage, sem0, n_rows):
            return [pltpu.make_async_copy(hbm.at[pl.ds(i * STAGE_ROWS, STAGE_ROWS), :],
                                          stage.at[i % 2], stage_sem.at[sem0 + i % 2])
                    for i in range(n_rows // STAGE_ROWS)]

        def staged(copies, consume, meanwhile=None):
            for cp in copies[:2]:
                cp.start()
            if meanwhile is not None:
                meanwhile()
            for i, cp in enumerate(copies):
                cp.wait()
                consume(i)
                if i + 2 < len(copies):
                    copies[i + 2].start()

        def consume_in(i):
            rows = slice(i * STAGE_ROWS, (i + 1) * STAGE_ROWS)
            row_w = jnp.broadcast_to(nwrow_ref[:, rows], (STAGE_ROWS, LANES)).T
            for c0 in range(0, D_TOK, LANES):
                win_ref[rows, c0:c0 + LANES] = (stage_in[i % 2, :, c0:c0 + LANES] * row_w
                                                * colscale_ref[:, c0:c0 + LANES]).astype(BF16)
            for f0 in range(0, D_FT, LANES):
                cols = slice(D_TOK + f0, D_TOK + f0 + LANES)
                wft_ref[f0:f0 + LANES, rows] = (stage_in[i % 2, :, cols] * row_w * colscale_ref[:, cols]).T.astype(BF16)

        def consume_out(i):
            rows = slice(i * STAGE_ROWS, (i + 1) * STAGE_ROWS)
            wout_ref[rows, :] = stage_out[i % 2].astype(BF16)

        def consume_x0(i):
            for r0, rs in _row_blocks(STAGE_ROWS):
                h0_s[i * STAGE_ROWS + r0:i * STAGE_ROWS + r0 + ROW_BLOCK, :] = rms_norm_rows(stage_out[i % 2, rs, :])

        staged(chunk_copies(win_hbm, stage_in, 0, D_MODEL), consume_in, meanwhile=build_tables)
        staged(chunk_copies(wout_hbm, stage_out, 2, D_MIX), consume_out)
        staged(chunk_copies(x_hbm.at[0], stage_out, 2, SEQ_TILE), consume_x0)

    lane = lax.broadcasted_iota(jnp.int32, (ROW_BLOCK, LANES), 1)
    lo = lane < RET_DK
    first_half = lane % RET_DK < RET_DK // 2

    def project(slot, h_s, h_next_s):
        rq_s, rqx_s, rk_s, rkz_s, rv_s, gate_s, qt_s, kband_s, vt_s, gatet_s = slot

        def proj(off):
            return _dot(h_s[...], win_ref[:, off:off + 4 * LANES])

        def proj_t(row0, n_rows):
            return _dot_nt(wft_ref[row0:row0 + n_rows, :], h_s[...])

        def rot(v, rs):
            partner = jnp.where(first_half, pltpu.roll(v, LANES - 32, 1), pltpu.roll(v, 32, 1))
            return v * cos_ref[rs, :] + partner * sins_ref[rs, :]

        def head_rms(xt):
            ms = jnp.mean(xt * xt, axis=0, keepdims=True)
            return xt * lax.rsqrt(ms + NORM_EPS)

        tok_blocks = [slice(c0, c0 + LANES) for c0 in range(0, SEQ_TILE, LANES)]

        def post_rqk(r):
            for r0, rs in _row_blocks(SEQ_TILE):
                c, cr = r0 // CHUNK, r0 % CHUNK
                cs = slice(cr, cr + ROW_BLOCK)
                for p in range(2):
                    q = rot(r[rs, p * LANES:(p + 1) * LANES], rs)
                    rq_s[p, rs, :] = q.astype(BF16)
                    rqx_s[p, 0, rs, :] = (q * xi_ref[p, 0, cs, :]).astype(BF16)
                    rqx_s[p, 1, rs, :] = (q * xi_ref[p, 1, cs, :]).astype(BF16)
                    k = rot(r[rs, (2 + p) * LANES:(3 + p) * LANES], rs)
                    rk_s[p, c, cr:cr + ROW_BLOCK, :] = jnp.where(lo, k, 0.0).astype(BF16)
                    rk_s[p, c, CHUNK + cr:CHUNK + cr + ROW_BLOCK, :] = jnp.where(lo, 0.0, k).astype(BF16)
                    rkz_s[p, rs, :] = (k * zeta_ref[p, cs, :]).astype(BF16)

        def post_rv(r):
            for r0, rs in _row_blocks(SEQ_TILE):
                rv_s[rs, :] = r[rs, :].astype(BF16)

        def post_rg(r):
            for r0, rs in _row_blocks(SEQ_TILE):
                gate_s[rs, :] = _silu_of_half(r[rs, :])

        def post_qt(rt):
            for hd in range(SWA_HEADS):
                hs = slice(hd * SWA_HEAD_DIM, (hd + 1) * SWA_HEAD_DIM)
                for ts in tok_blocks:
                    qt_s[hs, ts] = head_rms(rt[hs, ts]).astype(BF16)

        def post_kvt(rt):
            qk_head = qnw_ref[...] * knw_ref[...] * (SWA_HEAD_DIM ** -0.5 * LOG2_E)
            qk_w = jnp.concatenate([qk_head] * SWA_KV_HEADS, axis=1)
            for ts in tok_blocks:
                knt = jnp.concatenate(
                    [head_rms(rt[g * SWA_HEAD_DIM:(g + 1) * SWA_HEAD_DIM, ts]) for g in range(SWA_KV_HEADS)],
                    axis=0)
                kband_s[CHUNK + ts.start:CHUNK + ts.stop, :] = (knt.T * qk_w).astype(BF16)
                vt_s[:, CHUNK + ts.start:CHUNK + ts.stop] = rt[LANES:2 * LANES, ts].astype(BF16)

        def post_gt(rt):
            for r0 in range(0, SWA_HEADS * SWA_HEAD_DIM, ROW_BLOCK):
                for ts in tok_blocks:
                    gatet_s[r0:r0 + ROW_BLOCK, ts] = _silu_of_half(rt[r0:r0 + ROW_BLOCK, ts])

        groups = [
            (functools.partial(proj, OFF_RQ), post_rqk),
            (functools.partial(proj_t, FT_Q, SWA_HEADS * SWA_HEAD_DIM), post_qt),
            (functools.partial(proj, OFF_RG), post_rg),
            (functools.partial(proj_t, FT_G, SWA_HEADS * SWA_HEAD_DIM), post_gt),
            (functools.partial(proj_t, FT_K, 2 * LANES), post_kvt),
            (functools.partial(proj, OFF_RV), post_rv),
        ]
        pending = None
        for i, (matmul, post) in enumerate(groups):
            result = matmul()
            yield
            if pending is not None:
                pending[0](pending[1])
                yield
            pending = (post, result)
            if i == NORM_AFTER_GROUP:
                for r0, rs in _row_blocks(SEQ_TILE):
                    h_next_s[rs, :] = rms_norm_rows(xnext_ref[0, rs, :])
                yield
        pending[0](pending[1])
        yield

    def mix(slot, next_slot):
        rq_s, rqx_s, rk_s, rkz_s, rv_s, gate_s, qt_s, kband_s, vt_s, gatet_s = slot
        key = lax.broadcasted_iota(jnp.int32, (CHUNK, CHUNK), 0)
        qry = lax.broadcasted_iota(jnp.int32, (CHUNK, CHUNK), 1)
        tri_t = key <= qry
        cur_keep = jnp.where(tri_t, 1.0, 0.0).astype(BF16)
        prev_keep = jnp.where(tri_t, 0.0, 1.0).astype(BF16)
        zero_q = jnp.zeros((SWA_HEAD_DIM, SWA_GROUP * CHUNK), BF16)
        ones_rows = jnp.ones((8, 2 * CHUNK), BF16)

        def out_proj(c_hi):
            pair = slice((c_hi - 1) * CHUNK, (c_hi + 1) * CHUNK)
            o_ref[0, pair, :] = (xres_ref[0, pair, :]
                                 + _dot(mixed_s[pair, :], wout_ref[0:RET_HEADS * RET_DV, :])
                                 + _dot_tn(mixedt_s[:, pair], wout_ref[RET_HEADS * RET_DV:D_MIX, :]))

        chunk_rows = [slice(c * CHUNK, (c + 1) * CHUNK) for c in range(N_CHUNKS)]
        chunk_band = [slice(c * CHUNK, (c + 2) * CHUNK) for c in range(N_CHUNKS)]

        def logits_for(c):
            rows = chunk_rows[c]
            qts = [jnp.concatenate([qt_s[(SWA_GROUP * g + hh) * SWA_HEAD_DIM:(SWA_GROUP * g + hh + 1) * SWA_HEAD_DIM,
                                         rows] for hh in range(SWA_GROUP)], axis=1)
                   for g in range(SWA_KV_HEADS)]
            rhs = jnp.concatenate([jnp.concatenate([qts[0], zero_q], axis=1),
                                   jnp.concatenate([zero_q, qts[1]], axis=1)], axis=0)
            return _dot(kband_s[chunk_band[c], :], rhs)

        scores, incs, logits_t = {}, {}, []
        for c, rows in enumerate(chunk_rows):
            for p in range(2):
                s2 = _dot_nt(rq_s[p, rows, :], rk_s[p, c])
                scores[c, p] = jnp.concatenate([(s2[bs, :] * dec_ref[p, bs, :]).astype(BF16)
                                                for _, bs in _row_blocks(CHUNK)], axis=0)
                incs[c, p] = _dot_tn(rkz_s[p, rows, :], rv_s[rows, p * 2 * RET_DV:(p + 1) * 2 * RET_DV])
            logits_t.append(logits_for(c))
        yield

        ret_outs = {}
        for p in range(2):
            full = state_s[p] * keep_state
            for c, rows in enumerate(chunk_rows):
                full_b = full.astype(BF16)
                v2 = rv_s[rows, p * 2 * RET_DV:(p + 1) * 2 * RET_DV]
                s2 = scores[c, p]
                for j in range(2):
                    ret_outs[c, 2 * p + j] = _dot(
                        jnp.concatenate([s2[:, j * CHUNK:(j + 1) * CHUNK], rqx_s[p, j, rows, :]], axis=1),
                        jnp.concatenate([v2[:, j * RET_DV:(j + 1) * RET_DV],
                                         full_b[:, j * RET_DV:(j + 1) * RET_DV]], axis=0))
                full = full * cd_ref[p] + incs[c, p]
            state_s[p] = full
        yield

        for c in range(N_CHUNKS):
            rows, band, lt = chunk_rows[c], chunk_band[c], logits_t[c]

            for hd in range(RET_HEADS):
                o = ret_outs[c, hd]
                cols = slice(hd * RET_DV, (hd + 1) * RET_DV)
                for r0, bs in _row_blocks(CHUNK):
                    ob = o[bs, :]
                    rs = slice(c * CHUNK + r0, c * CHUNK + r0 + ROW_BLOCK)
                    mu = jnp.mean(ob, axis=-1, keepdims=True)
                    d = ob - mu
                    var = jnp.mean(d * d, axis=-1, keepdims=True)
                    y = d * lax.rsqrt(var + GN_EPS) * retw_ref[:, cols]
                    mixed_s[rs, cols] = (y * gate_s[rs, cols]).astype(BF16)
            sink_terms = []
            for hd in range(SWA_HEADS):
                g, hh = hd // SWA_GROUP, hd % SWA_GROUP
                hq = slice(hd * CHUNK, (hd + 1) * CHUNK)
                sink = sinks_ref[hd] * LOG2_E
                bias = bias_s[hd + SWA_HEADS * first_i] if c == 0 else bias_s[hd]
                lg = jnp.where(tri_t, lt[CHUNK:2 * CHUNK, hq], lt[0:CHUNK, hq]) + bias
                m = jnp.maximum(jnp.max(lg, axis=0, keepdims=True), sink)
                e = jnp.exp2(lg - m).astype(BF16)
                sink_terms.append(jnp.exp2(sink - m))
                pq = slice(hh * CHUNK, (hh + 1) * CHUNK)
                pt_s[c, g, 0:CHUNK, pq] = e * prev_keep
                pt_s[c, g, CHUNK:2 * CHUNK, pq] = e * cur_keep
            yield

            for g in range(SWA_KV_HEADS):
                vt1 = jnp.concatenate([vt_s[g * SWA_HEAD_DIM:(g + 1) * SWA_HEAD_DIM, band], ones_rows], axis=0)
                ot = _dot(vt1, pt_s[c, g])
                for hh in range(SWA_GROUP):
                    hd = SWA_GROUP * g + hh
                    hs = slice(hd * SWA_HEAD_DIM, (hd + 1) * SWA_HEAD_DIM)
                    pq = slice(hh * CHUNK, (hh + 1) * CHUNK)
                    inv = 1.0 / (ot[SWA_HEAD_DIM:SWA_HEAD_DIM + 1, pq] + sink_terms[hd])
                    mixedt_s[hs, rows] = (ot[0:SWA_HEAD_DIM, pq] * inv * gatet_s[hs, rows]).astype(BF16)
            yield
            if c % 2 == 1:
                out_proj(c)
                yield

        last = slice(SEQ_TILE, SEQ_TILE + CHUNK)
        next_slot[SLOT_KBAND][0:CHUNK, :] = kband_s[last, :]
        next_slot[SLOT_VT][:, 0:CHUNK] = vt_s[:, last]

    def run(parity, do_project=True, do_mix=True):
        streams = [project(slots[parity], h_bufs[parity], h_bufs[1 - parity]) if do_project else iter(()),
                   mix(slots[1 - parity], slots[parity]) if do_mix else iter(())]
        for which in MIX_ORDER + (0,) * len(MIX_ORDER) + (1,) * len(MIX_ORDER):
            next(streams[which], None)

    is_first, is_last = step == 0, step == n_tiles
    pl.when(is_first)(functools.partial(run, 0, do_mix=False))
    pl.when(is_last)(functools.partial(run, n_tiles % 2, do_project=False))
    for parity in range(2):
        pl.when((step % 2 == parity) & jnp.logical_not(is_first | is_last))(functools.partial(run, parity))


def kernel(x, norm_w, w_in, ret_norm_w, q_norm_w, k_norm_w, sinks, rel_bias, w_out):
    batch, seq, d_model = x.shape
    assert d_model == D_MODEL and seq % SEQ_TILE == 0
    assert w_in.shape == (D_MODEL, D_IN) and w_out.shape == (D_MIX, D_MODEL)
    tb = _constant_tables(seq)
    tiles_per_seq = seq // SEQ_TILE
    n_tiles = batch * tiles_per_seq

    def proj_tile(s):
        return jnp.minimum(s, n_tiles - 1)

    def mix_tile(s):
        return jnp.maximum(s - 1, 0)

    const2 = lambda s: (0, 0)
    const3 = lambda s: (0, 0, 0)
    const4 = lambda s: (0, 0, 0, 0)
    pos_spec = pl.BlockSpec((SEQ_TILE, LANES), lambda s: (proj_tile(s) % tiles_per_seq, 0))
    smem = pl.BlockSpec(memory_space=pltpu.SMEM)
    x_block = (1, SEQ_TILE, D_MODEL)
    mix_map = lambda s: (mix_tile(s) // tiles_per_seq, mix_tile(s) % tiles_per_seq, 0)

    in_specs = [
        pl.BlockSpec(x_block, lambda s: (proj_tile(s + 1) // tiles_per_seq, proj_tile(s + 1) % tiles_per_seq, 0)),
        pl.BlockSpec(x_block, mix_map),
        pl.BlockSpec(memory_space=pl.ANY),
        pl.BlockSpec((1, D_MODEL), const2),
        pl.BlockSpec((1, D_IN), const2),
        pl.BlockSpec(memory_space=pl.ANY),
        pl.BlockSpec(memory_space=pl.ANY),
        pos_spec, pos_spec,
        pl.BlockSpec((2, CHUNK, 2 * CHUNK), const3),
        pl.BlockSpec((2, 2, CHUNK, LANES), const4),
        pl.BlockSpec((2, CHUNK, LANES), const3),
        pl.BlockSpec((2, 1, 2 * RET_DV), const3),
        pl.BlockSpec((1, RET_HEADS * RET_DV), const2),
        pl.BlockSpec((1, SWA_HEAD_DIM), const2),
        pl.BlockSpec((1, SWA_HEAD_DIM), const2),
        pl.BlockSpec((CHUNK, CHUNK), const2),
        smem, smem,
    ]
    slot_bufs = [
        pltpu.VMEM((2, SEQ_TILE, LANES), BF16),
        pltpu.VMEM((2, 2, SEQ_TILE, LANES), BF16),
        pltpu.VMEM((2, N_CHUNKS, 2 * CHUNK, LANES), BF16),
        pltpu.VMEM((2, SEQ_TILE, LANES), BF16),
        pltpu.VMEM((SEQ_TILE, RET_HEADS * RET_DV), BF16),
        pltpu.VMEM((SEQ_TILE, RET_HEADS * RET_DV), F32),
        pltpu.VMEM((SWA_HEADS * SWA_HEAD_DIM, SEQ_TILE), BF16),
        pltpu.VMEM((SEQ_TILE + CHUNK, LANES), BF16),
        pltpu.VMEM((SWA_KV_HEADS * SWA_HEAD_DIM, SEQ_TILE + CHUNK), BF16),
        pltpu.VMEM((SWA_HEADS * SWA_HEAD_DIM, SEQ_TILE), F32),
    ]
    assert len(slot_bufs) == N_SLOT_BUFS
    h_buf = pltpu.VMEM((SEQ_TILE, D_MODEL), BF16)
    scratch = [h_buf, h_buf] + slot_bufs + slot_bufs + [
        pltpu.VMEM((N_CHUNKS, SWA_KV_HEADS, 2 * CHUNK, SWA_GROUP * CHUNK), BF16),
        pltpu.VMEM((SEQ_TILE, RET_HEADS * RET_DV), BF16),
        pltpu.VMEM((SWA_HEADS * SWA_HEAD_DIM, SEQ_TILE), BF16),
        pltpu.VMEM((2, 2 * RET_DK, 2 * RET_DV), F32),
        pltpu.VMEM((2 * SWA_HEADS, CHUNK, CHUNK), F32),
        pltpu.VMEM((D_MODEL, D_TOK), BF16),
        pltpu.VMEM((D_FT, D_MODEL), BF16),
        pltpu.VMEM((D_MIX, D_MODEL), BF16),
        pltpu.VMEM((2, STAGE_ROWS, D_IN), F32),
        pltpu.VMEM((2, STAGE_ROWS, D_MODEL), F32),
        pltpu.SemaphoreType.DMA((4,)),
    ]
    return pl.pallas_call(
        functools.partial(_layer_kernel, tiles_per_seq, n_tiles),
        grid=(n_tiles + 1,),
        in_specs=in_specs,
        out_specs=pl.BlockSpec(x_block, mix_map),
        out_shape=jax.ShapeDtypeStruct(x.shape, x.dtype),
        scratch_shapes=scratch,
        compiler_params=pltpu.CompilerParams(
            dimension_semantics=("arbitrary",),
            vmem_limit_bytes=VMEM_LIMIT_BYTES),
    )(x, x, x, norm_w.reshape(1, D_MODEL), tb["col_scale"], w_in, w_out,
      tb["cos"], tb["sin_s"], tb["dec"], tb["xi"], tb["zeta"], tb["cd"],
      ret_norm_w.reshape(1, -1), q_norm_w.reshape(1, -1), k_norm_w.reshape(1, -1), tb["bucket"],
      rel_bias.astype(F32), sinks.astype(F32))
```

```python
import functools
import math

import numpy as np
import jax
import jax.numpy as jnp
from jax import lax
from jax.experimental import pallas as pl
from jax.experimental.pallas import tpu as pltpu

D_MODEL = 1024
RET_HEADS = 4
RET_DK = 64
RET_DV = 128
CHUNK = 128
RET_ROPE_BASE = 10000.0
SWA_HEADS = 8
SWA_KV_HEADS = 2
SWA_HEAD_DIM = 64
SWA_GROUP = SWA_HEADS // SWA_KV_HEADS
NUM_BUCKETS = 32
MAX_DISTANCE = 128
NORM_EPS = 1e-6
GN_EPS = 1e-5
NEG_INF = -1e30
LOG2_E = math.log2(math.e)

OFF_RQ, OFF_RK, OFF_RV, OFF_RG = 0, 256, 512, 1024
D_TOK = 1536
FT_Q, FT_K, FT_G = 0, 512, 768
D_FT = 1280
D_IN = D_TOK + D_FT
D_MIX = 1024

LANES = 128
SEQ_TILE = 512
N_CHUNKS = SEQ_TILE // CHUNK
ROW_BLOCK = 32
VMEM_LIMIT_BYTES = 56 * 1024 * 1024
MIX_ORDER = (1, 1, 0, 0, 1, 1, 0, 0, 1, 1, 0, 0, 1, 0, 1, 1, 0, 0, 1, 1, 0, 0, 1, 0, 0)
NORM_AFTER_GROUP = 2
N_SLOT_BUFS = 10
STAGE_ROWS = 128
SLOT_KBAND, SLOT_VT = 7, 8

BF16 = jnp.bfloat16
F32 = jnp.float32


def _t5_bucket_np(n):
    max_exact = NUM_BUCKETS // 2
    nf = np.maximum(n, 1).astype(np.float64)
    large = max_exact + (np.log(nf / max_exact) / math.log(MAX_DISTANCE / max_exact)
                         * (NUM_BUCKETS - max_exact)).astype(np.int32)
    large = np.minimum(large, NUM_BUCKETS - 1)
    return np.where(n < max_exact, n, large).astype(np.int32)


def _constant_tables(seq):
    half = RET_DK // 2
    inv_freq = RET_ROPE_BASE ** (-np.arange(half, dtype=np.float64) / half)
    ang = np.arange(seq, dtype=np.float64)[:, None] * inv_freq[None, :]
    cos, sin = np.cos(ang), np.sin(ang)
    cos_t = np.tile(np.concatenate([cos, cos], axis=1), (1, 2))
    sin_s = np.tile(np.concatenate([-sin, sin], axis=1), (1, 2))

    gamma = 1.0 - np.exp2(-5.0 - np.arange(RET_HEADS, dtype=np.float64))
    log_g = np.log(gamma)
    i = np.arange(CHUNK, dtype=np.float64)
    diff = i[:, None] - i[None, :]
    k_scale = RET_DK ** -0.5
    decay = np.where(diff >= 0, np.exp(log_g[:, None, None] * np.maximum(diff, 0.0)), 0.0) * k_scale
    dec = np.stack([np.concatenate([decay[2 * p], decay[2 * p + 1]], axis=1) for p in range(2)])
    xi = np.exp(log_g[:, None] * (i + 1.0))
    zeta = np.exp(log_g[:, None] * (CHUNK - 1.0 - i)) * k_scale

    def per_lane(t):
        return np.stack([np.concatenate([np.repeat(t[2 * p][:, None], RET_DK, 1),
                                         np.repeat(t[2 * p + 1][:, None], RET_DK, 1)], axis=1)
                         for p in range(2)])

    lo = (np.arange(LANES) < RET_DK)[None, None, :]
    xi_l = per_lane(xi)
    xi_m = np.stack([np.where(lo, xi_l, 0.0), np.where(lo, 0.0, xi_l)], axis=1)

    chunk_decay = np.exp(log_g * CHUNK)
    cd = np.stack([np.concatenate([np.full((1, RET_DV), chunk_decay[2 * p]),
                                   np.full((1, RET_DV), chunk_decay[2 * p + 1])], axis=1)
                   for p in range(2)])

    j = np.arange(CHUNK)[:, None]
    q = np.arange(CHUNK)[None, :]
    dist = np.where(j <= q, q - j, q + CHUNK - j)
    bucket = _t5_bucket_np(dist)

    f = lambda a: jnp.asarray(a, dtype=F32)
    col_scale = np.ones((1, D_IN))
    col_scale[:, OFF_RG:OFF_RG + RET_HEADS * RET_DV] = 0.5
    col_scale[:, D_TOK + FT_G:D_TOK + FT_G + SWA_HEADS * SWA_HEAD_DIM] = 0.5

    return dict(cos=f(cos_t), sin_s=f(sin_s), col_scale=f(col_scale), dec=f(dec), xi=f(xi_m),
                zeta=f(per_lane(zeta)), cd=f(cd), bucket=jnp.asarray(bucket, dtype=jnp.int32))


def _dot(a, b):
    return jnp.dot(a, b, preferred_element_type=F32)


def _dot_nt(a, b):
    return lax.dot_general(a, b, (((1,), (1,)), ((), ())), preferred_element_type=F32)


def _dot_tn(a, b):
    return lax.dot_general(a, b, (((0,), (0,)), ((), ())), preferred_element_type=F32)


def _silu_of_half(hg):
    return hg + hg * jnp.tanh(hg)


def _row_blocks(n_rows):
    return [(r0, slice(r0, r0 + ROW_BLOCK)) for r0 in range(0, n_rows, ROW_BLOCK)]


def _layer_kernel(tiles_per_seq, n_tiles,
                  xnext_ref, xres_ref, x_hbm, nwrow_ref, colscale_ref, win_hbm, wout_hbm, cos_ref, sins_ref, dec_ref,
                  xi_ref, zeta_ref, cd_ref, retw_ref, qnw_ref, knw_ref, bkt_ref, relb_ref,
                  sinks_ref, o_ref, h0_s, h1_s, *scratch):
    h_bufs = (h0_s, h1_s)
    slots = (scratch[0:N_SLOT_BUFS], scratch[N_SLOT_BUFS:2 * N_SLOT_BUFS])
    (pt_s, mixed_s, mixedt_s, state_s, bias_s,
     win_ref, wft_ref, wout_ref, stage_in, stage_out, stage_sem) = scratch[2 * N_SLOT_BUFS:]
    step = pl.program_id(0)
    mix_tile = jnp.maximum(step - 1, 0)
    first = mix_tile % tiles_per_seq == 0
    first_i = first.astype(jnp.int32)
    keep_state = jnp.where(first, 0.0, 1.0).astype(F32)

    def rms_norm_rows(xb):
        ms = jnp.mean(xb * xb, axis=-1, keepdims=True)
        return (xb * lax.rsqrt(ms + NORM_EPS) * nwrow_ref[...]).astype(BF16)

    @pl.when(step == 0)
    def _init():
        def build_tables():
            bk = bkt_ref[...]
            key = lax.broadcasted_iota(jnp.int32, (CHUNK, CHUNK), 0)
            qry = lax.broadcasted_iota(jnp.int32, (CHUNK, CHUNK), 1)
            for h in range(SWA_HEADS):
                acc = jnp.zeros((CHUNK, CHUNK), F32)
                for u in range(NUM_BUCKETS):
                    acc = jnp.where(bk == u, relb_ref[u, h] * LOG2_E, acc)
                bias_s[h] = acc
                bias_s[SWA_HEADS + h] = jnp.where(key <= qry, acc, NEG_INF)
            state_s[...] = jnp.zeros_like(state_s)
            slots[0][SLOT_KBAND][0:CHUNK, :] = jnp.zeros((CHUNK, LANES), BF16)
            slots[0][SLOT_VT][:, 0:CHUNK] = jnp.zeros((SWA_KV_HEADS * SWA_HEAD_DIM, CHUNK), BF16)

        def chunk_copies(hbm, stage, sem0, n_rows):
            return [pltpu.make_async_copy(hbm.at[pl.ds(i * STAGE_ROWS, STAGE_ROWS), :],
                                          stage.at[i % 2], stage_sem.at[sem0 + i % 2])
                    for i in range(n_rows // STAGE_ROWS)]

        def staged(copies, consume, meanwhile=None):
            for cp in copies[:2]:
                cp.start()
            if meanwhile is not None:
                meanwhile()
            for i, cp in enumerate(copies):
                cp.wait()
                consume(i)
                if i + 2 < len(copies):
                    copies[i + 2].start()

        def consume_in(i):
            rows = slice(i * STAGE_ROWS, (i + 1) * STAGE_ROWS)
            for c0 in range(0, D_TOK, LANES):
                win_ref[rows, c0:c0 + LANES] = (stage_in[i % 2, :, c0:c0 + LANES]
                                                * colscale_ref[:, c0:c0 + LANES]).astype(BF16)
            for f0 in range(0, D_FT, LANES):
                cols = slice(D_TOK + f0, D_TOK + f0 + LANES)
                wft_ref[f0:f0 + LANES, rows] = (stage_in[i % 2, :, cols] * colscale_ref[:, cols]).T.astype(BF16)

        def consume_out(i):
            rows = slice(i * STAGE_ROWS, (i + 1) * STAGE_ROWS)
            wout_ref[rows, :] = stage_out[i % 2].astype(BF16)

        def consume_x0(i):
            for r0, rs in _row_blocks(STAGE_ROWS):
                h0_s[i * STAGE_ROWS + r0:i * STAGE_ROWS + r0 + ROW_BLOCK, :] = rms_norm_rows(stage_out[i % 2, rs, :])

        staged(chunk_copies(win_hbm, stage_in, 0, D_MODEL), consume_in, meanwhile=build_tables)
        staged(chunk_copies(wout_hbm, stage_out, 2, D_MIX), consume_out)
        staged(chunk_copies(x_hbm.at[0], stage_out, 2, SEQ_TILE), consume_x0)

    lane = lax.broadcasted_iota(jnp.int32, (ROW_BLOCK, LANES), 1)
    lo = lane < RET_DK
    first_half = lane % RET_DK < RET_DK // 2

    def project(slot, h_s, h_next_s):
        rq_s, rqx_s, rk_s, rkz_s, rv_s, gate_s, qt_s, kband_s, vt_s, gatet_s = slot

        def proj(off):
            return _dot(h_s[...], win_ref[:, off:off + 4 * LANES])

        def proj_t(row0, n_rows):
            return _dot_nt(wft_ref[row0:row0 + n_rows, :], h_s[...])

        def rot(v, rs):
            partner = jnp.where(first_half, pltpu.roll(v, LANES - 32, 1), pltpu.roll(v, 32, 1))
            return v * cos_ref[rs, :] + partner * sins_ref[rs, :]

        def head_rms(xt):
            ms = jnp.mean(xt * xt, axis=0, keepdims=True)
            return xt * lax.rsqrt(ms + NORM_EPS)

        tok_blocks = [slice(c0, c0 + LANES) for c0 in range(0, SEQ_TILE, LANES)]

        def post_rqk(r):
            for r0, rs in _row_blocks(SEQ_TILE):
                c, cr = r0 // CHUNK, r0 % CHUNK
                cs = slice(cr, cr + ROW_BLOCK)
                for p in range(2):
                    q = rot(r[rs, p * LANES:(p + 1) * LANES], rs)
                    rq_s[p, rs, :] = q.astype(BF16)
                    rqx_s[p, 0, rs, :] = (q * xi_ref[p, 0, cs, :]).astype(BF16)
                    rqx_s[p, 1, rs, :] = (q * xi_ref[p, 1, cs, :]).astype(BF16)
                    k = rot(r[rs, (2 + p) * LANES:(3 + p) * LANES], rs)
                    rk_s[p, c, cr:cr + ROW_BLOCK, :] = jnp.where(lo, k, 0.0).astype(BF16)
                    rk_s[p, c, CHUNK + cr:CHUNK + cr + ROW_BLOCK, :] = jnp.where(lo, 0.0, k).astype(BF16)
                    rkz_s[p, rs, :] = (k * zeta_ref[p, cs, :]).astype(BF16)

        def post_rv(r):
            for r0, rs in _row_blocks(SEQ_TILE):
                rv_s[rs, :] = r[rs, :].astype(BF16)

        def post_rg(r):
            for r0, rs in _row_blocks(SEQ_TILE):
                gate_s[rs, :] = _silu_of_half(r[rs, :])

        def post_qt(rt):
            for hd in range(SWA_HEADS):
                hs = slice(hd * SWA_HEAD_DIM, (hd + 1) * SWA_HEAD_DIM)
                for ts in tok_blocks:
                    qt_s[hs, ts] = head_rms(rt[hs, ts]).astype(BF16)

        def post_kvt(rt):
            qk_head = qnw_ref[...] * knw_ref[...] * (SWA_HEAD_DIM ** -0.5 * LOG2_E)
            qk_w = jnp.concatenate([qk_head] * SWA_KV_HEADS, axis=1)
            for ts in tok_blocks:
                knt = jnp.concatenate(
                    [head_rms(rt[g * SWA_HEAD_DIM:(g + 1) * SWA_HEAD_DIM, ts]) for g in range(SWA_KV_HEADS)],
                    axis=0)
                kband_s[CHUNK + ts.start:CHUNK + ts.stop, :] = (knt.T * qk_w).astype(BF16)
                vt_s[:, CHUNK + ts.start:CHUNK + ts.stop] = rt[LANES:2 * LANES, ts].astype(BF16)

        def post_gt(rt):
            for r0 in range(0, SWA_HEADS * SWA_HEAD_DIM, ROW_BLOCK):
                for ts in tok_blocks:
                    gatet_s[r0:r0 + ROW_BLOCK, ts] = _silu_of_half(rt[r0:r0 + ROW_BLOCK, ts])

        groups = [
            (functools.partial(proj, OFF_RQ), post_rqk),
            (functools.partial(proj_t, FT_Q, SWA_HEADS * SWA_HEAD_DIM), post_qt),
            (functools.partial(proj, OFF_RG), post_rg),
            (functools.partial(proj_t, FT_G, SWA_HEADS * SWA_HEAD_DIM), post_gt),
            (functools.partial(proj_t, FT_K, 2 * LANES), post_kvt),
            (functools.partial(proj, OFF_RV), post_rv),
        ]
        pending = None
        for i, (matmul, post) in enumerate(groups):
            result = matmul()
            yield
            if pending is not None:
                pending[0](pending[1])
                yield
            pending = (post, result)
            if i == NORM_AFTER_GROUP:
                for r0, rs in _row_blocks(SEQ_TILE):
                    h_next_s[rs, :] = rms_norm_rows(xnext_ref[0, rs, :])
                yield
        pending[0](pending[1])
        yield

    def mix(slot, next_slot):
        rq_s, rqx_s, rk_s, rkz_s, rv_s, gate_s, qt_s, kband_s, vt_s, gatet_s = slot
        key = lax.broadcasted_iota(jnp.int32, (CHUNK, CHUNK), 0)
        qry = lax.broadcasted_iota(jnp.int32, (CHUNK, CHUNK), 1)
        tri_t = key <= qry
        cur_keep = jnp.where(tri_t, 1.0, 0.0).astype(BF16)
        prev_keep = jnp.where(tri_t, 0.0, 1.0).astype(BF16)
        zero_q = jnp.zeros((SWA_HEAD_DIM, SWA_GROUP * CHUNK), BF16)
        ones_rows = jnp.ones((8, 2 * CHUNK), BF16)

        def out_proj(c_hi):
            pair = slice((c_hi - 1) * CHUNK, (c_hi + 1) * CHUNK)
            o_ref[0, pair, :] = (xres_ref[0, pair, :]
                                 + _dot(mixed_s[pair, :], wout_ref[0:RET_HEADS * RET_DV, :])
                                 + _dot_tn(mixedt_s[:, pair], wout_ref[RET_HEADS * RET_DV:D_MIX, :]))

        chunk_rows = [slice(c * CHUNK, (c + 1) * CHUNK) for c in range(N_CHUNKS)]
        chunk_band = [slice(c * CHUNK, (c + 2) * CHUNK) for c in range(N_CHUNKS)]

        def logits_for(c):
            rows = chunk_rows[c]
            qts = [jnp.concatenate([qt_s[(SWA_GROUP * g + hh) * SWA_HEAD_DIM:(SWA_GROUP * g + hh + 1) * SWA_HEAD_DIM,
                                         rows] for hh in range(SWA_GROUP)], axis=1)
                   for g in range(SWA_KV_HEADS)]
            rhs = jnp.concatenate([jnp.concatenate([qts[0], zero_q], axis=1),
                                   jnp.concatenate([zero_q, qts[1]], axis=1)], axis=0)
            return _dot(kband_s[chunk_band[c], :], rhs)

        scores, incs, logits_t = {}, {}, []
        for c, rows in enumerate(chunk_rows):
            for p in range(2):
                s2 = _dot_nt(rq_s[p, rows, :], rk_s[p, c])
                scores[c, p] = jnp.concatenate([(s2[bs, :] * dec_ref[p, bs, :]).astype(BF16)
                                                for _, bs in _row_blocks(CHUNK)], axis=0)
                incs[c, p] = _dot_tn(rkz_s[p, rows, :], rv_s[rows, p * 2 * RET_DV:(p + 1) * 2 * RET_DV])
            logits_t.append(logits_for(c))
        yield

        ret_outs = {}
        for p in range(2):
            full = state_s[p] * keep_state
            for c, rows in enumerate(chunk_rows):
                full_b = full.astype(BF16)
                v2 = rv_s[rows, p * 2 * RET_DV:(p + 1) * 2 * RET_DV]
                s2 = scores[c, p]
                for j in range(2):
                    ret_outs[c, 2 * p + j] = _dot(
                        jnp.concatenate([s2[:, j * CHUNK:(j + 1) * CHUNK], rqx_s[p, j, rows, :]], axis=1),
                        jnp.concatenate([v2[:, j * RET_DV:(j + 1) * RET_DV],
                                         full_b[:, j * RET_DV:(j + 1) * RET_DV]], axis=0))
                full = full * cd_ref[p] + incs[c, p]
            state_s[p] = full
        yield

        for c in range(N_CHUNKS):
            rows, band, lt = chunk_rows[c], chunk_band[c], logits_t[c]

            for hd in range(RET_HEADS):
                o = ret_outs[c, hd]
                cols = slice(hd * RET_DV, (hd + 1) * RET_DV)
                for r0, bs in _row_blocks(CHUNK):
                    ob = o[bs, :]
                    rs = slice(c * CHUNK + r0, c * CHUNK + r0 + ROW_BLOCK)
                    mu = jnp.mean(ob, axis=-1, keepdims=True)
                    d = ob - mu
                    var = jnp.mean(d * d, axis=-1, keepdims=True)
                    y = d * lax.rsqrt(var + GN_EPS) * retw_ref[:, cols]
                    mixed_s[rs, cols] = (y * gate_s[rs, cols]).astype(BF16)
            sink_terms = []
            for hd in range(SWA_HEADS):
                g, hh = hd // SWA_GROUP, hd % SWA_GROUP
                hq = slice(hd * CHUNK, (hd + 1) * CHUNK)
                sink = sinks_ref[hd] * LOG2_E
                bias = bias_s[hd + SWA_HEADS * first_i] if c == 0 else bias_s[hd]
                lg = jnp.where(tri_t, lt[CHUNK:2 * CHUNK, hq], lt[0:CHUNK, hq]) + bias
                m = jnp.maximum(jnp.max(lg, axis=0, keepdims=True), sink)
                e = jnp.exp2(lg - m).astype(BF16)
                sink_terms.append(jnp.exp2(sink - m))
                pq = slice(hh * CHUNK, (hh + 1) * CHUNK)
                pt_s[c, g, 0:CHUNK, pq] = e * prev_keep
                pt_s[c, g, CHUNK:2 * CHUNK, pq] = e * cur_keep
            yield

            for g in range(SWA_KV_HEADS):
                vt1 = jnp.concatenate([vt_s[g * SWA_HEAD_DIM:(g + 1) * SWA_HEAD_DIM, band], ones_rows], axis=0)
                ot = _dot(vt1, pt_s[c, g])
                for hh in range(SWA_GROUP):
                    hd = SWA_GROUP * g + hh
                    hs = slice(hd * SWA_HEAD_DIM, (hd + 1) * SWA_HEAD_DIM)
                    pq = slice(hh * CHUNK, (hh + 1) * CHUNK)
                    inv = 1.0 / (ot[SWA_HEAD_DIM:SWA_HEAD_DIM + 1, pq] + sink_terms[hd])
                    mixedt_s[hs, rows] = (ot[0:SWA_HEAD_DIM, pq] * inv * gatet_s[hs, rows]).astype(BF16)
            yield
            if c % 2 == 1:
                out_proj(c)
                yield

        last = slice(SEQ_TILE, SEQ_TILE + CHUNK)
        next_slot[SLOT_KBAND][0:CHUNK, :] = kband_s[last, :]
        next_slot[SLOT_VT][:, 0:CHUNK] = vt_s[:, last]

    def run(parity, do_project=True, do_mix=True):
        streams = [project(slots[parity], h_bufs[parity], h_bufs[1 - parity]) if do_project else iter(()),
                   mix(slots[1 - parity], slots[parity]) if do_mix else iter(())]
        for which in MIX_ORDER + (0,) * len(MIX_ORDER) + (1,) * len(MIX_ORDER):
            next(streams[which], None)

    is_first, is_last = step == 0, step == n_tiles
    pl.when(is_first)(functools.partial(run, 0, do_mix=False))
    pl.when(is_last)(functools.partial(run, n_tiles % 2, do_project=False))
    for parity in range(2):
        pl.when((step % 2 == parity) & jnp.logical_not(is_first | is_last))(functools.partial(run, parity))


def kernel(x, norm_w, w_in, ret_norm_w, q_norm_w, k_norm_w, sinks, rel_bias, w_out):
    batch, seq, d_model = x.shape
    assert d_model == D_MODEL and seq % SEQ_TILE == 0
    assert w_in.shape == (D_MODEL, D_IN) and w_out.shape == (D_MIX, D_MODEL)
    tb = _constant_tables(seq)
    tiles_per_seq = seq // SEQ_TILE
    n_tiles = batch * tiles_per_seq

    def proj_tile(s):
        return jnp.minimum(s, n_tiles - 1)

    def mix_tile(s):
        return jnp.maximum(s - 1, 0)

    const2 = lambda s: (0, 0)
    const3 = lambda s: (0, 0, 0)
    const4 = lambda s: (0, 0, 0, 0)
    pos_spec = pl.BlockSpec((SEQ_TILE, LANES), lambda s: (proj_tile(s) % tiles_per_seq, 0))
    smem = pl.BlockSpec(memory_space=pltpu.SMEM)
    x_block = (1, SEQ_TILE, D_MODEL)
    mix_map = lambda s: (mix_tile(s) // tiles_per_seq, mix_tile(s) % tiles_per_seq, 0)

    in_specs = [
        pl.BlockSpec(x_block, lambda s: (proj_tile(s + 1) // tiles_per_seq, proj_tile(s + 1) % tiles_per_seq, 0)),
        pl.BlockSpec(x_block, mix_map),
        pl.BlockSpec(memory_space=pl.ANY),
        pl.BlockSpec((1, D_MODEL), const2),
        pl.BlockSpec((1, D_IN), const2),
        pl.BlockSpec(memory_space=pl.ANY),
        pl.BlockSpec(memory_space=pl.ANY),
        pos_spec, pos_spec,
        pl.BlockSpec((2, CHUNK, 2 * CHUNK), const3),
        pl.BlockSpec((2, 2, CHUNK, LANES), const4),
        pl.BlockSpec((2, CHUNK, LANES), const3),
        pl.BlockSpec((2, 1, 2 * RET_DV), const3),
        pl.BlockSpec((1, RET_HEADS * RET_DV), const2),
        pl.BlockSpec((1, SWA_HEAD_DIM), const2),
        pl.BlockSpec((1, SWA_HEAD_DIM), const2),
        pl.BlockSpec((CHUNK, CHUNK), const2),
        smem, smem,
    ]
    slot_bufs = [
        pltpu.VMEM((2, SEQ_TILE, LANES), BF16),
        pltpu.VMEM((2, 2, SEQ_TILE, LANES), BF16),
        pltpu.VMEM((2, N_CHUNKS, 2 * CHUNK, LANES), BF16),
        pltpu.VMEM((2, SEQ_TILE, LANES), BF16),
        pltpu.VMEM((SEQ_TILE, RET_HEADS * RET_DV), BF16),
        pltpu.VMEM((SEQ_TILE, RET_HEADS * RET_DV), F32),
        pltpu.VMEM((SWA_HEADS * SWA_HEAD_DIM, SEQ_TILE), BF16),
        pltpu.VMEM((SEQ_TILE + CHUNK, LANES), BF16),
        pltpu.VMEM((SWA_KV_HEADS * SWA_HEAD_DIM, SEQ_TILE + CHUNK), BF16),
        pltpu.VMEM((SWA_HEADS * SWA_HEAD_DIM, SEQ_TILE), F32),
    ]
    assert len(slot_bufs) == N_SLOT_BUFS
    h_buf = pltpu.VMEM((SEQ_TILE, D_MODEL), BF16)
    scratch = [h_buf, h_buf] + slot_bufs + slot_bufs + [
        pltpu.VMEM((N_CHUNKS, SWA_KV_HEADS, 2 * CHUNK, SWA_GROUP * CHUNK), BF16),
        pltpu.VMEM((SEQ_TILE, RET_HEADS * RET_DV), BF16),
        pltpu.VMEM((SWA_HEADS * SWA_HEAD_DIM, SEQ_TILE), BF16),
        pltpu.VMEM((2, 2 * RET_DK, 2 * RET_DV), F32),
        pltpu.VMEM((2 * SWA_HEADS, CHUNK, CHUNK), F32),
        pltpu.VMEM((D_MODEL, D_TOK), BF16),
        pltpu.VMEM((D_FT, D_MODEL), BF16),
        pltpu.VMEM((D_MIX, D_MODEL), BF16),
        pltpu.VMEM((2, STAGE_ROWS, D_IN), F32),
        pltpu.VMEM((2, STAGE_ROWS, D_MODEL), F32),
        pltpu.SemaphoreType.DMA((4,)),
    ]
    return pl.pallas_call(
        functools.partial(_layer_kernel, tiles_per_seq, n_tiles),
        grid=(n_tiles + 1,),
        in_specs=in_specs,
        out_specs=pl.BlockSpec(x_block, mix_map),
        out_shape=jax.ShapeDtypeStruct(x.shape, x.dtype),
        scratch_shapes=scratch,
        compiler_params=pltpu.CompilerParams(
            dimension_semantics=("arbitrary",),
            vmem_limit_bytes=VMEM_LIMIT_BYTES),
    )(x, x, x, norm_w.reshape(1, D_MODEL), tb["col_scale"], w_in, w_out,
      tb["cos"], tb["sin_s"], tb["dec"], tb["xi"], tb["zeta"], tb["cd"],
      ret_norm_w.reshape(1, -1), q_norm_w.reshape(1, -1), k_norm_w.reshape(1, -1), tb["bucket"],
      rel_bias.astype(F32), sinks.astype(F32))
```

```python
import functools
import math

import numpy as np
import jax
import jax.numpy as jnp
from jax import lax
from jax.experimental import pallas as pl
from jax.experimental.pallas import tpu as pltpu

D_MODEL = 1024
RET_HEADS = 4
RET_DK = 64
RET_DV = 128
CHUNK = 128
RET_ROPE_BASE = 10000.0
SWA_HEADS = 8
SWA_KV_HEADS = 2
SWA_HEAD_DIM = 64
SWA_GROUP = SWA_HEADS // SWA_KV_HEADS
NUM_BUCKETS = 32
MAX_DISTANCE = 128
NORM_EPS = 1e-6
GN_EPS = 1e-5
NEG_INF = -1e30
LOG2_E = math.log2(math.e)

OFF_RQ, OFF_RK, OFF_RV, OFF_RG = 0, 256, 512, 1024
D_TOK = 1536
FT_Q, FT_K, FT_G = 0, 512, 768
D_FT = 1280
D_IN = D_TOK + D_FT
D_MIX = 1024

LANES = 128
SEQ_TILE = 512
N_CHUNKS = SEQ_TILE // CHUNK
ROW_BLOCK = 32
VMEM_LIMIT_BYTES = 56 * 1024 * 1024
MIX_ORDER = (1, 1, 0, 0, 1, 1, 0, 0, 1, 1, 0, 0, 1, 0, 1, 1, 0, 0, 1, 1, 0, 0, 1, 0, 0)
NORM_AFTER_GROUP = 2
N_SLOT_BUFS = 10
STAGE_ROWS = 128
STAGE_DEPTH = 4
SLOT_KBAND, SLOT_VT = 7, 8

BF16 = jnp.bfloat16
F32 = jnp.float32


def _t5_bucket_np(n):
    max_exact = NUM_BUCKETS // 2
    nf = np.maximum(n, 1).astype(np.float64)
    large = max_exact + (np.log(nf / max_exact) / math.log(MAX_DISTANCE / max_exact)
                         * (NUM_BUCKETS - max_exact)).astype(np.int32)
    large = np.minimum(large, NUM_BUCKETS - 1)
    return np.where(n < max_exact, n, large).astype(np.int32)


def _constant_tables(seq):
    half = RET_DK // 2
    inv_freq = RET_ROPE_BASE ** (-np.arange(half, dtype=np.float64) / half)
    ang = np.arange(seq, dtype=np.float64)[:, None] * inv_freq[None, :]
    cos, sin = np.cos(ang), np.sin(ang)
    cos_t = np.tile(np.concatenate([cos, cos], axis=1), (1, 2))
    sin_s = np.tile(np.concatenate([-sin, sin], axis=1), (1, 2))

    gamma = 1.0 - np.exp2(-5.0 - np.arange(RET_HEADS, dtype=np.float64))
    log_g = np.log(gamma)
    i = np.arange(CHUNK, dtype=np.float64)
    diff = i[:, None] - i[None, :]
    k_scale = RET_DK ** -0.5
    decay = np.where(diff >= 0, np.exp(log_g[:, None, None] * np.maximum(diff, 0.0)), 0.0) * k_scale
    dec = np.stack([np.concatenate([decay[2 * p], decay[2 * p + 1]], axis=1) for p in range(2)])
    xi = np.exp(log_g[:, None] * (i + 1.0))
    zeta = np.exp(log_g[:, None] * (CHUNK - 1.0 - i)) * k_scale

    def per_lane(t):
        return np.stack([np.concatenate([np.repeat(t[2 * p][:, None], RET_DK, 1),
                                         np.repeat(t[2 * p + 1][:, None], RET_DK, 1)], axis=1)
                         for p in range(2)])

    lo = (np.arange(LANES) < RET_DK)[None, None, :]
    xi_l = per_lane(xi)
    xi_m = np.stack([np.where(lo, xi_l, 0.0), np.where(lo, 0.0, xi_l)], axis=1)

    chunk_decay = np.exp(log_g * CHUNK)
    cd = np.stack([np.concatenate([np.full((1, RET_DV), chunk_decay[2 * p]),
                                   np.full((1, RET_DV), chunk_decay[2 * p + 1])], axis=1)
                   for p in range(2)])

    j = np.arange(CHUNK)[:, None]
    q = np.arange(CHUNK)[None, :]
    dist = np.where(j <= q, q - j, q + CHUNK - j)
    bucket = _t5_bucket_np(dist)

    f = lambda a: jnp.asarray(a, dtype=F32)
    col_scale = np.ones((1, D_IN))
    col_scale[:, OFF_RG:OFF_RG + RET_HEADS * RET_DV] = 0.5
    col_scale[:, D_TOK + FT_G:D_TOK + FT_G + SWA_HEADS * SWA_HEAD_DIM] = 0.5

    return dict(cos=f(cos_t), sin_s=f(sin_s), col_scale=f(col_scale), dec=f(dec), xi=f(xi_m),
                zeta=f(per_lane(zeta)), cd=f(cd), bucket=jnp.asarray(bucket, dtype=jnp.int32))


def _dot(a, b):
    return jnp.dot(a, b, preferred_element_type=F32)


def _dot_nt(a, b):
    return lax.dot_general(a, b, (((1,), (1,)), ((), ())), preferred_element_type=F32)


def _dot_tn(a, b):
    return lax.dot_general(a, b, (((0,), (0,)), ((), ())), preferred_element_type=F32)


def _silu_of_half(hg):
    return hg + hg * jnp.tanh(hg)


def _row_blocks(n_rows):
    return [(r0, slice(r0, r0 + ROW_BLOCK)) for r0 in range(0, n_rows, ROW_BLOCK)]


def _layer_kernel(tiles_per_seq, n_tiles,
                  xnext_ref, xres_ref, x_hbm, nwrow_ref, colscale_ref, win_hbm, wout_hbm, cos_ref, sins_ref, dec_ref,
                  xi_ref, zeta_ref, cd_ref, retw_ref, qnw_ref, knw_ref, bkt_ref, relb_ref,
                  sinks_ref, o_ref, h0_s, h1_s, *scratch):
    h_bufs = (h0_s, h1_s)
    slots = (scratch[0:N_SLOT_BUFS], scratch[N_SLOT_BUFS:2 * N_SLOT_BUFS])
    (pt_s, mixed_s, mixedt_s, state_s, bias_s,
     win_ref, wft_ref, wout_ref, stage_in, stage_out, stage_sem) = scratch[2 * N_SLOT_BUFS:]
    step = pl.program_id(0)
    mix_tile = jnp.maximum(step - 1, 0)
    first = mix_tile % tiles_per_seq == 0
    first_i = first.astype(jnp.int32)
    keep_state = jnp.where(first, 0.0, 1.0).astype(F32)

    def rms_norm_rows(xb):
        ms = jnp.mean(xb * xb, axis=-1, keepdims=True)
        return (xb * lax.rsqrt(ms + NORM_EPS) * nwrow_ref[...]).astype(BF16)

    @pl.when(step == 0)
    def _init():
        def build_tables():
            bk = bkt_ref[...]
            key = lax.broadcasted_iota(jnp.int32, (CHUNK, CHUNK), 0)
            qry = lax.broadcasted_iota(jnp.int32, (CHUNK, CHUNK), 1)
            for h in range(SWA_HEADS):
                acc = jnp.zeros((CHUNK, CHUNK), F32)
                for u in range(NUM_BUCKETS):
                    acc = jnp.where(bk == u, relb_ref[u, h] * LOG2_E, acc)
                bias_s[h] = acc
                bias_s[SWA_HEADS + h] = jnp.where(key <= qry, acc, NEG_INF)
            state_s[...] = jnp.zeros_like(state_s)
            slots[0][SLOT_KBAND][0:CHUNK, :] = jnp.zeros((CHUNK, LANES), BF16)
            slots[0][SLOT_VT][:, 0:CHUNK] = jnp.zeros((SWA_KV_HEADS * SWA_HEAD_DIM, CHUNK), BF16)

        def chunk_copies(hbm, stage, sem0, n_rows):
            return [pltpu.make_async_copy(hbm.at[pl.ds(i * STAGE_ROWS, STAGE_ROWS), :],
                                          stage.at[i % STAGE_DEPTH], stage_sem.at[sem0 + i % STAGE_DEPTH])
                    for i in range(n_rows // STAGE_ROWS)]

        def staged(copies, consume, meanwhile=None):
            for cp in copies[:STAGE_DEPTH]:
                cp.start()
            if meanwhile is not None:
                meanwhile()
            for i, cp in enumerate(copies):
                cp.wait()
                consume(i)
                if i + STAGE_DEPTH < len(copies):
                    copies[i + STAGE_DEPTH].start()

        def consume_in(i):
            rows = slice(i * STAGE_ROWS, (i + 1) * STAGE_ROWS)
            for c0 in range(0, D_TOK, LANES):
                win_ref[rows, c0:c0 + LANES] = (stage_in[i % STAGE_DEPTH, :, c0:c0 + LANES]
                                                * colscale_ref[:, c0:c0 + LANES]).astype(BF16)
            for f0 in range(0, D_FT, LANES):
                cols = slice(D_TOK + f0, D_TOK + f0 + LANES)
                wft_ref[f0:f0 + LANES, rows] = (stage_in[i % STAGE_DEPTH, :, cols]
                                                * colscale_ref[:, cols]).T.astype(BF16)

        def consume_out(i):
            rows = slice(i * STAGE_ROWS, (i + 1) * STAGE_ROWS)
            wout_ref[rows, :] = stage_out[i % STAGE_DEPTH].astype(BF16)

        def consume_x0(i):
            for r0, rs in _row_blocks(STAGE_ROWS):
                h0_s[i * STAGE_ROWS + r0:i * STAGE_ROWS + r0 + ROW_BLOCK, :] = rms_norm_rows(
                    stage_out[i % STAGE_DEPTH, rs, :])

        staged(chunk_copies(win_hbm, stage_in, 0, D_MODEL), consume_in, meanwhile=build_tables)
        staged(chunk_copies(wout_hbm, stage_out, STAGE_DEPTH, D_MIX), consume_out)
        staged(chunk_copies(x_hbm.at[0], stage_out, STAGE_DEPTH, SEQ_TILE), consume_x0)

    lane = lax.broadcasted_iota(jnp.int32, (ROW_BLOCK, LANES), 1)
    lo = lane < RET_DK
    first_half = lane % RET_DK < RET_DK // 2

    def project(slot, h_s, h_next_s):
        rq_s, rqx_s, rk_s, rkz_s, rv_s, gate_s, qt_s, kband_s, vt_s, gatet_s = slot

        def proj(off):
            return _dot(h_s[...], win_ref[:, off:off + 4 * LANES])

        def proj_t(row0, n_rows):
            return _dot_nt(wft_ref[row0:row0 + n_rows, :], h_s[...])

        def rot(v, rs):
            partner = jnp.where(first_half, pltpu.roll(v, LANES - 32, 1), pltpu.roll(v, 32, 1))
            return v * cos_ref[rs, :] + partner * sins_ref[rs, :]

        def head_rms(xt):
            ms = jnp.mean(xt * xt, axis=0, keepdims=True)
            return xt * lax.rsqrt(ms + NORM_EPS)

        tok_blocks = [slice(c0, c0 + LANES) for c0 in range(0, SEQ_TILE, LANES)]

        def post_rqk(r):
            for r0, rs in _row_blocks(SEQ_TILE):
                c, cr = r0 // CHUNK, r0 % CHUNK
                cs = slice(cr, cr + ROW_BLOCK)
                for p in range(2):
                    q = rot(r[rs, p * LANES:(p + 1) * LANES], rs)
                    rq_s[p, rs, :] = q.astype(BF16)
                    rqx_s[p, 0, rs, :] = (q * xi_ref[p, 0, cs, :]).astype(BF16)
                    rqx_s[p, 1, rs, :] = (q * xi_ref[p, 1, cs, :]).astype(BF16)
                    k = rot(r[rs, (2 + p) * LANES:(3 + p) * LANES], rs)
                    rk_s[p, c, cr:cr + ROW_BLOCK, :] = jnp.where(lo, k, 0.0).astype(BF16)
                    rk_s[p, c, CHUNK + cr:CHUNK + cr + ROW_BLOCK, :] = jnp.where(lo, 0.0, k).astype(BF16)
                    rkz_s[p, rs, :] = (k * zeta_ref[p, cs, :]).astype(BF16)

        def post_rv(r):
            for r0, rs in _row_blocks(SEQ_TILE):
                rv_s[rs, :] = r[rs, :].astype(BF16)

        def post_rg(r):
            for r0, rs in _row_blocks(SEQ_TILE):
                gate_s[rs, :] = _silu_of_half(r[rs, :])

        def post_qt(rt):
            for hd in range(SWA_HEADS):
                hs = slice(hd * SWA_HEAD_DIM, (hd + 1) * SWA_HEAD_DIM)
                for ts in tok_blocks:
                    qt_s[hs, ts] = head_rms(rt[hs, ts]).astype(BF16)

        def post_kvt(rt):
            qk_head = qnw_ref[...] * knw_ref[...] * (SWA_HEAD_DIM ** -0.5 * LOG2_E)
            qk_w = jnp.concatenate([qk_head] * SWA_KV_HEADS, axis=1)
            for ts in tok_blocks:
                knt = jnp.concatenate(
                    [head_rms(rt[g * SWA_HEAD_DIM:(g + 1) * SWA_HEAD_DIM, ts]) for g in range(SWA_KV_HEADS)],
                    axis=0)
                kband_s[CHUNK + ts.start:CHUNK + ts.stop, :] = (knt.T * qk_w).astype(BF16)
                vt_s[:, CHUNK + ts.start:CHUNK + ts.stop] = rt[LANES:2 * LANES, ts].astype(BF16)

        def post_gt(rt):
            for r0 in range(0, SWA_HEADS * SWA_HEAD_DIM, ROW_BLOCK):
                for ts in tok_blocks:
                    gatet_s[r0:r0 + ROW_BLOCK, ts] = _silu_of_half(rt[r0:r0 + ROW_BLOCK, ts])

        groups = [
            (functools.partial(proj, OFF_RQ), post_rqk),
            (functools.partial(proj_t, FT_Q, SWA_HEADS * SWA_HEAD_DIM), post_qt),
            (functools.partial(proj, OFF_RG), post_rg),
            (functools.partial(proj_t, FT_G, SWA_HEADS * SWA_HEAD_DIM), post_gt),
            (functools.partial(proj_t, FT_K, 2 * LANES), post_kvt),
            (functools.partial(proj, OFF_RV), post_rv),
        ]
        pending = None
        for i, (matmul, post) in enumerate(groups):
            result = matmul()
            yield
            if pending is not None:
                pending[0](pending[1])
                yield
            pending = (post, result)
            if i == NORM_AFTER_GROUP:
                for r0, rs in _row_blocks(SEQ_TILE):
                    h_next_s[rs, :] = rms_norm_rows(xnext_ref[0, rs, :])
                yield
        pending[0](pending[1])
        yield

    def mix(slot, next_slot):
        rq_s, rqx_s, rk_s, rkz_s, rv_s, gate_s, qt_s, kband_s, vt_s, gatet_s = slot
        key = lax.broadcasted_iota(jnp.int32, (CHUNK, CHUNK), 0)
        qry = lax.broadcasted_iota(jnp.int32, (CHUNK, CHUNK), 1)
        tri_t = key <= qry
        cur_keep = jnp.where(tri_t, 1.0, 0.0).astype(BF16)
        prev_keep = jnp.where(tri_t, 0.0, 1.0).astype(BF16)
        zero_q = jnp.zeros((SWA_HEAD_DIM, SWA_GROUP * CHUNK), BF16)
        ones_rows = jnp.ones((8, 2 * CHUNK), BF16)

        def out_proj(c_hi):
            pair = slice((c_hi - 1) * CHUNK, (c_hi + 1) * CHUNK)
            o_ref[0, pair, :] = (xres_ref[0, pair, :]
                                 + _dot(mixed_s[pair, :], wout_ref[0:RET_HEADS * RET_DV, :])
                                 + _dot_tn(mixedt_s[:, pair], wout_ref[RET_HEADS * RET_DV:D_MIX, :]))

        chunk_rows = [slice(c * CHUNK, (c + 1) * CHUNK) for c in range(N_CHUNKS)]
        chunk_band = [slice(c * CHUNK, (c + 2) * CHUNK) for c in range(N_CHUNKS)]

        def logits_for(c):
            rows = chunk_rows[c]
            qts = [jnp.concatenate([qt_s[(SWA_GROUP * g + hh) * SWA_HEAD_DIM:(SWA_GROUP * g + hh + 1) * SWA_HEAD_DIM,
                                         rows] for hh in range(SWA_GROUP)], axis=1)
                   for g in range(SWA_KV_HEADS)]
            rhs = jnp.concatenate([jnp.concatenate([qts[0], zero_q], axis=1),
                                   jnp.concatenate([zero_q, qts[1]], axis=1)], axis=0)
            return _dot(kband_s[chunk_band[c], :], rhs)

        scores, incs, logits_t = {}, {}, []
        for c, rows in enumerate(chunk_rows):
            for p in range(2):
                s2 = _dot_nt(rq_s[p, rows, :], rk_s[p, c])
                scores[c, p] = jnp.concatenate([(s2[bs, :] * dec_ref[p, bs, :]).astype(BF16)
                                                for _, bs in _row_blocks(CHUNK)], axis=0)
                incs[c, p] = _dot_tn(rkz_s[p, rows, :], rv_s[rows, p * 2 * RET_DV:(p + 1) * 2 * RET_DV])
            logits_t.append(logits_for(c))
        yield

        ret_outs = {}
        for p in range(2):
            full = state_s[p] * keep_state
            for c, rows in enumerate(chunk_rows):
                full_b = full.astype(BF16)
                v2 = rv_s[rows, p * 2 * RET_DV:(p + 1) * 2 * RET_DV]
                s2 = scores[c, p]
                for j in range(2):
                    ret_outs[c, 2 * p + j] = _dot(
                        jnp.concatenate([s2[:, j * CHUNK:(j + 1) * CHUNK], rqx_s[p, j, rows, :]], axis=1),
                        jnp.concatenate([v2[:, j * RET_DV:(j + 1) * RET_DV],
                                         full_b[:, j * RET_DV:(j + 1) * RET_DV]], axis=0))
                full = full * cd_ref[p] + incs[c, p]
            state_s[p] = full
        yield

        for c in range(N_CHUNKS):
            rows, band, lt = chunk_rows[c], chunk_band[c], logits_t[c]

            for hd in range(RET_HEADS):
                o = ret_outs[c, hd]
                cols = slice(hd * RET_DV, (hd + 1) * RET_DV)
                for r0, bs in _row_blocks(CHUNK):
                    ob = o[bs, :]
                    rs = slice(c * CHUNK + r0, c * CHUNK + r0 + ROW_BLOCK)
                    mu = jnp.mean(ob, axis=-1, keepdims=True)
                    d = ob - mu
                    var = jnp.mean(d * d, axis=-1, keepdims=True)
                    y = d * lax.rsqrt(var + GN_EPS) * retw_ref[:, cols]
                    mixed_s[rs, cols] = (y * gate_s[rs, cols]).astype(BF16)
            sink_terms = []
            for hd in range(SWA_HEADS):
                g, hh = hd // SWA_GROUP, hd % SWA_GROUP
                hq = slice(hd * CHUNK, (hd + 1) * CHUNK)
                sink = sinks_ref[hd] * LOG2_E
                bias = bias_s[hd + SWA_HEADS * first_i] if c == 0 else bias_s[hd]
                lg = jnp.where(tri_t, lt[CHUNK:2 * CHUNK, hq], lt[0:CHUNK, hq]) + bias
                m = jnp.maximum(jnp.max(lg, axis=0, keepdims=True), sink)
                e = jnp.exp2(lg - m).astype(BF16)
                sink_terms.append(jnp.exp2(sink - m))
                pq = slice(hh * CHUNK, (hh + 1) * CHUNK)
                pt_s[c, g, 0:CHUNK, pq] = e * prev_keep
                pt_s[c, g, CHUNK:2 * CHUNK, pq] = e * cur_keep
            yield

            for g in range(SWA_KV_HEADS):
                vt1 = jnp.concatenate([vt_s[g * SWA_HEAD_DIM:(g + 1) * SWA_HEAD_DIM, band], ones_rows], axis=0)
                ot = _dot(vt1, pt_s[c, g])
                for hh in range(SWA_GROUP):
                    hd = SWA_GROUP * g + hh
                    hs = slice(hd * SWA_HEAD_DIM, (hd + 1) * SWA_HEAD_DIM)
                    pq = slice(hh * CHUNK, (hh + 1) * CHUNK)
                    inv = 1.0 / (ot[SWA_HEAD_DIM:SWA_HEAD_DIM + 1, pq] + sink_terms[hd])
                    mixedt_s[hs, rows] = (ot[0:SWA_HEAD_DIM, pq] * inv * gatet_s[hs, rows]).astype(BF16)
            yield
            if c % 2 == 1:
                out_proj(c)
                yield

        last = slice(SEQ_TILE, SEQ_TILE + CHUNK)
        next_slot[SLOT_KBAND][0:CHUNK, :] = kband_s[last, :]
        next_slot[SLOT_VT][:, 0:CHUNK] = vt_s[:, last]

    def run(parity, do_project=True, do_mix=True):
        streams = [project(slots[parity], h_bufs[parity], h_bufs[1 - parity]) if do_project else iter(()),
                   mix(slots[1 - parity], slots[parity]) if do_mix else iter(())]
        for which in MIX_ORDER + (0,) * len(MIX_ORDER) + (1,) * len(MIX_ORDER):
            next(streams[which], None)

    is_first, is_last = step == 0, step == n_tiles
    pl.when(is_first)(functools.partial(run, 0, do_mix=False))
    pl.when(is_last)(functools.partial(run, n_tiles % 2, do_project=False))
    for parity in range(2):
        pl.when((step % 2 == parity) & jnp.logical_not(is_first | is_last))(functools.partial(run, parity))


def kernel(x, norm_w, w_in, ret_norm_w, q_norm_w, k_norm_w, sinks, rel_bias, w_out):
    batch, seq, d_model = x.shape
    assert d_model == D_MODEL and seq % SEQ_TILE == 0
    assert w_in.shape == (D_MODEL, D_IN) and w_out.shape == (D_MIX, D_MODEL)
    tb = _constant_tables(seq)
    tiles_per_seq = seq // SEQ_TILE
    n_tiles = batch * tiles_per_seq

    def proj_tile(s):
        return jnp.minimum(s, n_tiles - 1)

    def mix_tile(s):
        return jnp.maximum(s - 1, 0)

    const2 = lambda s: (0, 0)
    const3 = lambda s: (0, 0, 0)
    const4 = lambda s: (0, 0, 0, 0)
    pos_spec = pl.BlockSpec((SEQ_TILE, LANES), lambda s: (proj_tile(s) % tiles_per_seq, 0))
    smem = pl.BlockSpec(memory_space=pltpu.SMEM)
    x_block = (1, SEQ_TILE, D_MODEL)
    mix_map = lambda s: (mix_tile(s) // tiles_per_seq, mix_tile(s) % tiles_per_seq, 0)

    in_specs = [
        pl.BlockSpec(x_block, lambda s: (proj_tile(s + 1) // tiles_per_seq, proj_tile(s + 1) % tiles_per_seq, 0)),
        pl.BlockSpec(x_block, mix_map),
        pl.BlockSpec(memory_space=pl.ANY),
        pl.BlockSpec((1, D_MODEL), const2),
        pl.BlockSpec((1, D_IN), const2),
        pl.BlockSpec(memory_space=pl.ANY),
        pl.BlockSpec(memory_space=pl.ANY),
        pos_spec, pos_spec,
        pl.BlockSpec((2, CHUNK, 2 * CHUNK), const3),
        pl.BlockSpec((2, 2, CHUNK, LANES), const4),
        pl.BlockSpec((2, CHUNK, LANES), const3),
        pl.BlockSpec((2, 1, 2 * RET_DV), const3),
        pl.BlockSpec((1, RET_HEADS * RET_DV), const2),
        pl.BlockSpec((1, SWA_HEAD_DIM), const2),
        pl.BlockSpec((1, SWA_HEAD_DIM), const2),
        pl.BlockSpec((CHUNK, CHUNK), const2),
        smem, smem,
    ]
    slot_bufs = [
        pltpu.VMEM((2, SEQ_TILE, LANES), BF16),
        pltpu.VMEM((2, 2, SEQ_TILE, LANES), BF16),
        pltpu.VMEM((2, N_CHUNKS, 2 * CHUNK, LANES), BF16),
        pltpu.VMEM((2, SEQ_TILE, LANES), BF16),
        pltpu.VMEM((SEQ_TILE, RET_HEADS * RET_DV), BF16),
        pltpu.VMEM((SEQ_TILE, RET_HEADS * RET_DV), F32),
        pltpu.VMEM((SWA_HEADS * SWA_HEAD_DIM, SEQ_TILE), BF16),
        pltpu.VMEM((SEQ_TILE + CHUNK, LANES), BF16),
        pltpu.VMEM((SWA_KV_HEADS * SWA_HEAD_DIM, SEQ_TILE + CHUNK), BF16),
        pltpu.VMEM((SWA_HEADS * SWA_HEAD_DIM, SEQ_TILE), F32),
    ]
    assert len(slot_bufs) == N_SLOT_BUFS
    h_buf = pltpu.VMEM((SEQ_TILE, D_MODEL), BF16)
    scratch = [h_buf, h_buf] + slot_bufs + slot_bufs + [
        pltpu.VMEM((N_CHUNKS, SWA_KV_HEADS, 2 * CHUNK, SWA_GROUP * CHUNK), BF16),
        pltpu.VMEM((SEQ_TILE, RET_HEADS * RET_DV), BF16),
        pltpu.VMEM((SWA_HEADS * SWA_HEAD_DIM, SEQ_TILE), BF16),
        pltpu.VMEM((2, 2 * RET_DK, 2 * RET_DV), F32),
        pltpu.VMEM((2 * SWA_HEADS, CHUNK, CHUNK), F32),
        pltpu.VMEM((D_MODEL, D_TOK), BF16),
        pltpu.VMEM((D_FT, D_MODEL), BF16),
        pltpu.VMEM((D_MIX, D_MODEL), BF16),
        pltpu.VMEM((STAGE_DEPTH, STAGE_ROWS, D_IN), F32),
        pltpu.VMEM((STAGE_DEPTH, STAGE_ROWS, D_MODEL), F32),
        pltpu.SemaphoreType.DMA((2 * STAGE_DEPTH,)),
    ]
    return pl.pallas_call(
        functools.partial(_layer_kernel, tiles_per_seq, n_tiles),
        grid=(n_tiles + 1,),
        in_specs=in_specs,
        out_specs=pl.BlockSpec(x_block, mix_map),
        out_shape=jax.ShapeDtypeStruct(x.shape, x.dtype),
        scratch_shapes=scratch,
        compiler_params=pltpu.CompilerParams(
            dimension_semantics=("arbitrary",),
            vmem_limit_bytes=VMEM_LIMIT_BYTES),
    )(x, x, x, norm_w.reshape(1, D_MODEL), tb["col_scale"], w_in, w_out,
      tb["cos"], tb["sin_s"], tb["dec"], tb["xi"], tb["zeta"], tb["cd"],
      ret_norm_w.reshape(1, -1), q_norm_w.reshape(1, -1), k_norm_w.reshape(1, -1), tb["bucket"],
      rel_bias.astype(F32), sinks.astype(F32))
```

```python
import functools
import math

import numpy as np
import jax
import jax.numpy as jnp
from jax import lax
from jax.experimental import pallas as pl
from jax.experimental.pallas import tpu as pltpu

D_MODEL = 1024
RET_HEADS = 4
RET_DK = 64
RET_DV = 128
CHUNK = 128
RET_ROPE_BASE = 10000.0
SWA_HEADS = 8
SWA_KV_HEADS = 2
SWA_HEAD_DIM = 64
SWA_GROUP = SWA_HEADS // SWA_KV_HEADS
NUM_BUCKETS = 32
MAX_DISTANCE = 128
NORM_EPS = 1e-6
GN_EPS = 1e-5
NEG_INF = -1e30
LOG2_E = math.log2(math.e)

OFF_RQ, OFF_RK, OFF_RV, OFF_RG = 0, 256, 512, 1024
D_TOK = 1536
FT_Q, FT_K, FT_G = 0, 512, 768
D_FT = 1280
D_IN = D_TOK + D_FT
D_MIX = 1024

LANES = 128
SEQ_TILE = 512
N_CHUNKS = SEQ_TILE // CHUNK
ROW_BLOCK = 32
VMEM_LIMIT_BYTES = 56 * 1024 * 1024
MIX_ORDER = (1, 1, 0, 0, 1, 1, 0, 0, 1, 1, 0, 0, 1, 0, 1, 1, 0, 0, 1, 1, 0, 0, 1, 0, 0)
NORM_AFTER_GROUP = 2
N_SLOT_BUFS = 10
STAGE_ROWS = 128
STAGE_DEPTH = 4
SLOT_KBAND, SLOT_VT = 7, 8

BF16 = jnp.bfloat16
F32 = jnp.float32


def _t5_bucket_np(n):
    max_exact = NUM_BUCKETS // 2
    nf = np.maximum(n, 1).astype(np.float64)
    large = max_exact + (np.log(nf / max_exact) / math.log(MAX_DISTANCE / max_exact)
                         * (NUM_BUCKETS - max_exact)).astype(np.int32)
    large = np.minimum(large, NUM_BUCKETS - 1)
    return np.where(n < max_exact, n, large).astype(np.int32)


def _constant_tables(seq):
    half = RET_DK // 2
    inv_freq = RET_ROPE_BASE ** (-np.arange(half, dtype=np.float64) / half)
    ang = np.arange(seq, dtype=np.float64)[:, None] * inv_freq[None, :]
    cos, sin = np.cos(ang), np.sin(ang)
    cos_t = np.tile(np.concatenate([cos, cos], axis=1), (1, 2))
    sin_s = np.tile(np.concatenate([-sin, sin], axis=1), (1, 2))

    gamma = 1.0 - np.exp2(-5.0 - np.arange(RET_HEADS, dtype=np.float64))
    log_g = np.log(gamma)
    i = np.arange(CHUNK, dtype=np.float64)
    diff = i[:, None] - i[None, :]
    k_scale = RET_DK ** -0.5
    decay = np.where(diff >= 0, np.exp(log_g[:, None, None] * np.maximum(diff, 0.0)), 0.0) * k_scale
    dec = np.stack([np.concatenate([decay[2 * p], decay[2 * p + 1]], axis=1) for p in range(2)])
    xi = np.exp(log_g[:, None] * (i + 1.0))
    zeta = np.exp(log_g[:, None] * (CHUNK - 1.0 - i)) * k_scale

    def per_lane(t):
        return np.stack([np.concatenate([np.repeat(t[2 * p][:, None], RET_DK, 1),
                                         np.repeat(t[2 * p + 1][:, None], RET_DK, 1)], axis=1)
                         for p in range(2)])

    lo = (np.arange(LANES) < RET_DK)[None, None, :]
    xi_l = per_lane(xi)
    xi_m = np.stack([np.where(lo, xi_l, 0.0), np.where(lo, 0.0, xi_l)], axis=1)

    chunk_decay = np.exp(log_g * CHUNK)
    cd = np.stack([np.concatenate([np.full((1, RET_DV), chunk_decay[2 * p]),
                                   np.full((1, RET_DV), chunk_decay[2 * p + 1])], axis=1)
                   for p in range(2)])

    j = np.arange(CHUNK)[:, None]
    q = np.arange(CHUNK)[None, :]
    dist = np.where(j <= q, q - j, q + CHUNK - j)
    bucket = _t5_bucket_np(dist)

    f = lambda a: jnp.asarray(a, dtype=F32)
    col_scale = np.ones((1, D_IN))
    col_scale[:, OFF_RG:OFF_RG + RET_HEADS * RET_DV] = 0.5
    col_scale[:, D_TOK + FT_G:D_TOK + FT_G + SWA_HEADS * SWA_HEAD_DIM] = 0.5

    return dict(cos=f(cos_t), sin_s=f(sin_s), col_scale=f(col_scale), dec=f(dec), xi=f(xi_m),
                zeta=f(per_lane(zeta)), cd=f(cd), bucket=jnp.asarray(bucket, dtype=jnp.int32))


def _dot(a, b):
    return jnp.dot(a, b, preferred_element_type=F32)


def _dot_nt(a, b):
    return lax.dot_general(a, b, (((1,), (1,)), ((), ())), preferred_element_type=F32)


def _dot_tn(a, b):
    return lax.dot_general(a, b, (((0,), (0,)), ((), ())), preferred_element_type=F32)


def _silu_of_half(hg):
    return hg + hg * jnp.tanh(hg)


def _row_blocks(n_rows):
    return [(r0, slice(r0, r0 + ROW_BLOCK)) for r0 in range(0, n_rows, ROW_BLOCK)]


def _layer_kernel(tiles_per_seq, n_tiles,
                  xnext_ref, xres_ref, x_hbm, nwrow_ref, colscale_ref, win_hbm, wout_hbm, cos_ref, sins_ref, dec_ref,
                  xi_ref, zeta_ref, cd_ref, retw_ref, qnw_ref, knw_ref, bkt_ref, relb_ref,
                  sinks_ref, o_ref, h0_s, h1_s, *scratch):
    h_bufs = (h0_s, h1_s)
    slots = (scratch[0:N_SLOT_BUFS], scratch[N_SLOT_BUFS:2 * N_SLOT_BUFS])
    (pt_s, mixed_s, mixedt_s, state_s, bias_s,
     win_ref, wft_ref, wout_ref, stage_in, stage_out, stage_x, stage_sem) = scratch[2 * N_SLOT_BUFS:]
    step = pl.program_id(0)
    mix_tile = jnp.maximum(step - 1, 0)
    first = mix_tile % tiles_per_seq == 0
    first_i = first.astype(jnp.int32)
    keep_state = jnp.where(first, 0.0, 1.0).astype(F32)

    def rms_norm_rows(xb):
        ms = jnp.mean(xb * xb, axis=-1, keepdims=True)
        return (xb * lax.rsqrt(ms + NORM_EPS) * nwrow_ref[...]).astype(BF16)

    @pl.when(step == 0)
    def _init():
        def build_tables():
            bk = bkt_ref[...]
            key = lax.broadcasted_iota(jnp.int32, (CHUNK, CHUNK), 0)
            qry = lax.broadcasted_iota(jnp.int32, (CHUNK, CHUNK), 1)
            for h in range(SWA_HEADS):
                acc = jnp.zeros((CHUNK, CHUNK), F32)
                for u in range(NUM_BUCKETS):
                    acc = jnp.where(bk == u, relb_ref[u, h] * LOG2_E, acc)
                bias_s[h] = acc
                bias_s[SWA_HEADS + h] = jnp.where(key <= qry, acc, NEG_INF)
            state_s[...] = jnp.zeros_like(state_s)
            slots[0][SLOT_KBAND][0:CHUNK, :] = jnp.zeros((CHUNK, LANES), BF16)
            slots[0][SLOT_VT][:, 0:CHUNK] = jnp.zeros((SWA_KV_HEADS * SWA_HEAD_DIM, CHUNK), BF16)

        def chunk_copies(hbm, stage, sem0, n_rows):
            return [pltpu.make_async_copy(hbm.at[pl.ds(i * STAGE_ROWS, STAGE_ROWS), :],
                                          stage.at[i % STAGE_DEPTH], stage_sem.at[sem0 + i % STAGE_DEPTH])
                    for i in range(n_rows // STAGE_ROWS)]

        def drain(copies, consume):
            for i, cp in enumerate(copies):
                cp.wait()
                consume(i)
                if i + STAGE_DEPTH < len(copies):
                    copies[i + STAGE_DEPTH].start()

        def consume_in(i):
            rows = slice(i * STAGE_ROWS, (i + 1) * STAGE_ROWS)
            for c0 in range(0, D_TOK, LANES):
                win_ref[rows, c0:c0 + LANES] = (stage_in[i % STAGE_DEPTH, :, c0:c0 + LANES]
                                                * colscale_ref[:, c0:c0 + LANES]).astype(BF16)
            for f0 in range(0, D_FT, LANES):
                cols = slice(D_TOK + f0, D_TOK + f0 + LANES)
                wft_ref[f0:f0 + LANES, rows] = (stage_in[i % STAGE_DEPTH, :, cols]
                                                * colscale_ref[:, cols]).T.astype(BF16)

        def consume_out(i):
            rows = slice(i * STAGE_ROWS, (i + 1) * STAGE_ROWS)
            wout_ref[rows, :] = stage_out[i % STAGE_DEPTH].astype(BF16)

        def consume_x0(i):
            for r0, rs in _row_blocks(STAGE_ROWS):
                h0_s[i * STAGE_ROWS + r0:i * STAGE_ROWS + r0 + ROW_BLOCK, :] = rms_norm_rows(
                    stage_x[i % STAGE_DEPTH, rs, :])

        streams = [(chunk_copies(win_hbm, stage_in, 0, D_MODEL), consume_in),
                   (chunk_copies(wout_hbm, stage_out, STAGE_DEPTH, D_MIX), consume_out),
                   (chunk_copies(x_hbm.at[0], stage_x, 2 * STAGE_DEPTH, SEQ_TILE), consume_x0)]
        for copies, _ in streams:
            for cp in copies[:STAGE_DEPTH]:
                cp.start()
        build_tables()
        for copies, consume in streams:
            drain(copies, consume)

    lane = lax.broadcasted_iota(jnp.int32, (ROW_BLOCK, LANES), 1)
    lo = lane < RET_DK
    first_half = lane % RET_DK < RET_DK // 2

    def project(slot, h_s, h_next_s):
        rq_s, rqx_s, rk_s, rkz_s, rv_s, gate_s, qt_s, kband_s, vt_s, gatet_s = slot

        def proj(off):
            return _dot(h_s[...], win_ref[:, off:off + 4 * LANES])

        def proj_t(row0, n_rows):
            return _dot_nt(wft_ref[row0:row0 + n_rows, :], h_s[...])

        def rot(v, rs):
            partner = jnp.where(first_half, pltpu.roll(v, LANES - 32, 1), pltpu.roll(v, 32, 1))
            return v * cos_ref[rs, :] + partner * sins_ref[rs, :]

        def head_rms(xt):
            ms = jnp.mean(xt * xt, axis=0, keepdims=True)
            return xt * lax.rsqrt(ms + NORM_EPS)

        tok_blocks = [slice(c0, c0 + LANES) for c0 in range(0, SEQ_TILE, LANES)]

        def post_rqk(r):
            for r0, rs in _row_blocks(SEQ_TILE):
                c, cr = r0 // CHUNK, r0 % CHUNK
                cs = slice(cr, cr + ROW_BLOCK)
                for p in range(2):
                    q = rot(r[rs, p * LANES:(p + 1) * LANES], rs)
                    rq_s[p, rs, :] = q.astype(BF16)
                    rqx_s[p, 0, rs, :] = (q * xi_ref[p, 0, cs, :]).astype(BF16)
                    rqx_s[p, 1, rs, :] = (q * xi_ref[p, 1, cs, :]).astype(BF16)
                    k = rot(r[rs, (2 + p) * LANES:(3 + p) * LANES], rs)
                    rk_s[p, c, cr:cr + ROW_BLOCK, :] = jnp.where(lo, k, 0.0).astype(BF16)
                    rk_s[p, c, CHUNK + cr:CHUNK + cr + ROW_BLOCK, :] = jnp.where(lo, 0.0, k).astype(BF16)
                    rkz_s[p, rs, :] = (k * zeta_ref[p, cs, :]).astype(BF16)

        def post_rv(r):
            for r0, rs in _row_blocks(SEQ_TILE):
                rv_s[rs, :] = r[rs, :].astype(BF16)

        def post_rg(r):
            for r0, rs in _row_blocks(SEQ_TILE):
                gate_s[rs, :] = _silu_of_half(r[rs, :])

        def post_qt(rt):
            for hd in range(SWA_HEADS):
                hs = slice(hd * SWA_HEAD_DIM, (hd + 1) * SWA_HEAD_DIM)
                for ts in tok_blocks:
                    qt_s[hs, ts] = head_rms(rt[hs, ts]).astype(BF16)

        def post_kvt(rt):
            qk_head = qnw_ref[...] * knw_ref[...] * (SWA_HEAD_DIM ** -0.5 * LOG2_E)
            qk_w = jnp.concatenate([qk_head] * SWA_KV_HEADS, axis=1)
            for ts in tok_blocks:
                knt = jnp.concatenate(
                    [head_rms(rt[g * SWA_HEAD_DIM:(g + 1) * SWA_HEAD_DIM, ts]) for g in range(SWA_KV_HEADS)],
                    axis=0)
                kband_s[CHUNK + ts.start:CHUNK + ts.stop, :] = (knt.T * qk_w).astype(BF16)
                vt_s[:, CHUNK + ts.start:CHUNK + ts.stop] = rt[LANES:2 * LANES, ts].astype(BF16)

        def post_gt(rt):
            for r0 in range(0, SWA_HEADS * SWA_HEAD_DIM, ROW_BLOCK):
                for ts in tok_blocks:
                    gatet_s[r0:r0 + ROW_BLOCK, ts] = _silu_of_half(rt[r0:r0 + ROW_BLOCK, ts])

        groups = [
            (functools.partial(proj, OFF_RQ), post_rqk),
            (functools.partial(proj_t, FT_Q, SWA_HEADS * SWA_HEAD_DIM), post_qt),
            (functools.partial(proj, OFF_RG), post_rg),
            (functools.partial(proj_t, FT_G, SWA_HEADS * SWA_HEAD_DIM), post_gt),
            (functools.partial(proj_t, FT_K, 2 * LANES), post_kvt),
            (functools.partial(proj, OFF_RV), post_rv),
        ]
        pending = None
        for i, (matmul, post) in enumerate(groups):
            result = matmul()
            yield
            if pending is not None:
                pending[0](pending[1])
                yield
            pending = (post, result)
            if i == NORM_AFTER_GROUP:
                for r0, rs in _row_blocks(SEQ_TILE):
                    h_next_s[rs, :] = rms_norm_rows(xnext_ref[0, rs, :])
                yield
        pending[0](pending[1])
        yield

    def mix(slot, next_slot):
        rq_s, rqx_s, rk_s, rkz_s, rv_s, gate_s, qt_s, kband_s, vt_s, gatet_s = slot
        key = lax.broadcasted_iota(jnp.int32, (CHUNK, CHUNK), 0)
        qry = lax.broadcasted_iota(jnp.int32, (CHUNK, CHUNK), 1)
        tri_t = key <= qry
        cur_keep = jnp.where(tri_t, 1.0, 0.0).astype(BF16)
        prev_keep = jnp.where(tri_t, 0.0, 1.0).astype(BF16)
        zero_q = jnp.zeros((SWA_HEAD_DIM, SWA_GROUP * CHUNK), BF16)
        ones_rows = jnp.ones((8, 2 * CHUNK), BF16)

        def out_proj(c_hi):
            pair = slice((c_hi - 1) * CHUNK, (c_hi + 1) * CHUNK)
            o_ref[0, pair, :] = (xres_ref[0, pair, :]
                                 + _dot(mixed_s[pair, :], wout_ref[0:RET_HEADS * RET_DV, :])
                                 + _dot_tn(mixedt_s[:, pair], wout_ref[RET_HEADS * RET_DV:D_MIX, :]))

        chunk_rows = [slice(c * CHUNK, (c + 1) * CHUNK) for c in range(N_CHUNKS)]
        chunk_band = [slice(c * CHUNK, (c + 2) * CHUNK) for c in range(N_CHUNKS)]

        def logits_for(c):
            rows = chunk_rows[c]
            qts = [jnp.concatenate([qt_s[(SWA_GROUP * g + hh) * SWA_HEAD_DIM:(SWA_GROUP * g + hh + 1) * SWA_HEAD_DIM,
                                         rows] for hh in range(SWA_GROUP)], axis=1)
                   for g in range(SWA_KV_HEADS)]
            rhs = jnp.concatenate([jnp.concatenate([qts[0], zero_q], axis=1),
                                   jnp.concatenate([zero_q, qts[1]], axis=1)], axis=0)
            return _dot(kband_s[chunk_band[c], :], rhs)

        scores, incs, logits_t = {}, {}, []
        for c, rows in enumerate(chunk_rows):
            for p in range(2):
                s2 = _dot_nt(rq_s[p, rows, :], rk_s[p, c])
                scores[c, p] = jnp.concatenate([(s2[bs, :] * dec_ref[p, bs, :]).astype(BF16)
                                                for _, bs in _row_blocks(CHUNK)], axis=0)
                incs[c, p] = _dot_tn(rkz_s[p, rows, :], rv_s[rows, p * 2 * RET_DV:(p + 1) * 2 * RET_DV])
            logits_t.append(logits_for(c))
        yield

        ret_outs = {}
        for p in range(2):
            full = state_s[p] * keep_state
            for c, rows in enumerate(chunk_rows):
                full_b = full.astype(BF16)
                v2 = rv_s[rows, p * 2 * RET_DV:(p + 1) * 2 * RET_DV]
                s2 = scores[c, p]
                for j in range(2):
                    ret_outs[c, 2 * p + j] = _dot(
                        jnp.concatenate([s2[:, j * CHUNK:(j + 1) * CHUNK], rqx_s[p, j, rows, :]], axis=1),
                        jnp.concatenate([v2[:, j * RET_DV:(j + 1) * RET_DV],
                                         full_b[:, j * RET_DV:(j + 1) * RET_DV]], axis=0))
                full = full * cd_ref[p] + incs[c, p]
            state_s[p] = full
        yield

        for c in range(N_CHUNKS):
            rows, band, lt = chunk_rows[c], chunk_band[c], logits_t[c]

            for hd in range(RET_HEADS):
                o = ret_outs[c, hd]
                cols = slice(hd * RET_DV, (hd + 1) * RET_DV)
                for r0, bs in _row_blocks(CHUNK):
                    ob = o[bs, :]
                    rs = slice(c * CHUNK + r0, c * CHUNK + r0 + ROW_BLOCK)
                    mu = jnp.mean(ob, axis=-1, keepdims=True)
                    d = ob - mu
                    var = jnp.mean(d * d, axis=-1, keepdims=True)
                    y = d * lax.rsqrt(var + GN_EPS) * retw_ref[:, cols]
                    mixed_s[rs, cols] = (y * gate_s[rs, cols]).astype(BF16)
            sink_terms = []
            for hd in range(SWA_HEADS):
                g, hh = hd // SWA_GROUP, hd % SWA_GROUP
                hq = slice(hd * CHUNK, (hd + 1) * CHUNK)
                sink = sinks_ref[hd] * LOG2_E
                bias = bias_s[hd + SWA_HEADS * first_i] if c == 0 else bias_s[hd]
                lg = jnp.where(tri_t, lt[CHUNK:2 * CHUNK, hq], lt[0:CHUNK, hq]) + bias
                m = jnp.maximum(jnp.max(lg, axis=0, keepdims=True), sink)
                e = jnp.exp2(lg - m).astype(BF16)
                sink_terms.append(jnp.exp2(sink - m))
                pq = slice(hh * CHUNK, (hh + 1) * CHUNK)
                pt_s[c, g, 0:CHUNK, pq] = e * prev_keep
                pt_s[c, g, CHUNK:2 * CHUNK, pq] = e * cur_keep
            yield

            for g in range(SWA_KV_HEADS):
                vt1 = jnp.concatenate([vt_s[g * SWA_HEAD_DIM:(g + 1) * SWA_HEAD_DIM, band], ones_rows], axis=0)
                ot = _dot(vt1, pt_s[c, g])
                for hh in range(SWA_GROUP):
                    hd = SWA_GROUP * g + hh
                    hs = slice(hd * SWA_HEAD_DIM, (hd + 1) * SWA_HEAD_DIM)
                    pq = slice(hh * CHUNK, (hh + 1) * CHUNK)
                    inv = 1.0 / (ot[SWA_HEAD_DIM:SWA_HEAD_DIM + 1, pq] + sink_terms[hd])
                    mixedt_s[hs, rows] = (ot[0:SWA_HEAD_DIM, pq] * inv * gatet_s[hs, rows]).astype(BF16)
            yield
            if c % 2 == 1:
                out_proj(c)
                yield

        last = slice(SEQ_TILE, SEQ_TILE + CHUNK)
        next_slot[SLOT_KBAND][0:CHUNK, :] = kband_s[last, :]
        next_slot[SLOT_VT][:, 0:CHUNK] = vt_s[:, last]

    def run(parity, do_project=True, do_mix=True):
        streams = [project(slots[parity], h_bufs[parity], h_bufs[1 - parity]) if do_project else iter(()),
                   mix(slots[1 - parity], slots[parity]) if do_mix else iter(())]
        for which in MIX_ORDER + (0,) * len(MIX_ORDER) + (1,) * len(MIX_ORDER):
            next(streams[which], None)

    is_first, is_last = step == 0, step == n_tiles
    pl.when(is_first)(functools.partial(run, 0, do_mix=False))
    pl.when(is_last)(functools.partial(run, n_tiles % 2, do_project=False))
    for parity in range(2):
        pl.when((step % 2 == parity) & jnp.logical_not(is_first | is_last))(functools.partial(run, parity))


def kernel(x, norm_w, w_in, ret_norm_w, q_norm_w, k_norm_w, sinks, rel_bias, w_out):
    batch, seq, d_model = x.shape
    assert d_model == D_MODEL and seq % SEQ_TILE == 0
    assert w_in.shape == (D_MODEL, D_IN) and w_out.shape == (D_MIX, D_MODEL)
    tb = _constant_tables(seq)
    tiles_per_seq = seq // SEQ_TILE
    n_tiles = batch * tiles_per_seq

    def proj_tile(s):
        return jnp.minimum(s, n_tiles - 1)

    def mix_tile(s):
        return jnp.maximum(s - 1, 0)

    const2 = lambda s: (0, 0)
    const3 = lambda s: (0, 0, 0)
    const4 = lambda s: (0, 0, 0, 0)
    pos_spec = pl.BlockSpec((SEQ_TILE, LANES), lambda s: (proj_tile(s) % tiles_per_seq, 0))
    smem = pl.BlockSpec(memory_space=pltpu.SMEM)
    x_block = (1, SEQ_TILE, D_MODEL)
    mix_map = lambda s: (mix_tile(s) // tiles_per_seq, mix_tile(s) % tiles_per_seq, 0)

    in_specs = [
        pl.BlockSpec(x_block, lambda s: (proj_tile(s + 1) // tiles_per_seq, proj_tile(s + 1) % tiles_per_seq, 0)),
        pl.BlockSpec(x_block, mix_map),
        pl.BlockSpec(memory_space=pl.ANY),
        pl.BlockSpec((1, D_MODEL), const2),
        pl.BlockSpec((1, D_IN), const2),
        pl.BlockSpec(memory_space=pl.ANY),
        pl.BlockSpec(memory_space=pl.ANY),
        pos_spec, pos_spec,
        pl.BlockSpec((2, CHUNK, 2 * CHUNK), const3),
        pl.BlockSpec((2, 2, CHUNK, LANES), const4),
        pl.BlockSpec((2, CHUNK, LANES), const3),
        pl.BlockSpec((2, 1, 2 * RET_DV), const3),
        pl.BlockSpec((1, RET_HEADS * RET_DV), const2),
        pl.BlockSpec((1, SWA_HEAD_DIM), const2),
        pl.BlockSpec((1, SWA_HEAD_DIM), const2),
        pl.BlockSpec((CHUNK, CHUNK), const2),
        smem, smem,
    ]
    slot_bufs = [
        pltpu.VMEM((2, SEQ_TILE, LANES), BF16),
        pltpu.VMEM((2, 2, SEQ_TILE, LANES), BF16),
        pltpu.VMEM((2, N_CHUNKS, 2 * CHUNK, LANES), BF16),
        pltpu.VMEM((2, SEQ_TILE, LANES), BF16),
        pltpu.VMEM((SEQ_TILE, RET_HEADS * RET_DV), BF16),
        pltpu.VMEM((SEQ_TILE, RET_HEADS * RET_DV), F32),
        pltpu.VMEM((SWA_HEADS * SWA_HEAD_DIM, SEQ_TILE), BF16),
        pltpu.VMEM((SEQ_TILE + CHUNK, LANES), BF16),
        pltpu.VMEM((SWA_KV_HEADS * SWA_HEAD_DIM, SEQ_TILE + CHUNK), BF16),
        pltpu.VMEM((SWA_HEADS * SWA_HEAD_DIM, SEQ_TILE), F32),
    ]
    assert len(slot_bufs) == N_SLOT_BUFS
    h_buf = pltpu.VMEM((SEQ_TILE, D_MODEL), BF16)
    scratch = [h_buf, h_buf] + slot_bufs + slot_bufs + [
        pltpu.VMEM((N_CHUNKS, SWA_KV_HEADS, 2 * CHUNK, SWA_GROUP * CHUNK), BF16),
        pltpu.VMEM((SEQ_TILE, RET_HEADS * RET_DV), BF16),
        pltpu.VMEM((SWA_HEADS * SWA_HEAD_DIM, SEQ_TILE), BF16),
        pltpu.VMEM((2, 2 * RET_DK, 2 * RET_DV), F32),
        pltpu.VMEM((2 * SWA_HEADS, CHUNK, CHUNK), F32),
        pltpu.VMEM((D_MODEL, D_TOK), BF16),
        pltpu.VMEM((D_FT, D_MODEL), BF16),
        pltpu.VMEM((D_MIX, D_MODEL), BF16),
        pltpu.VMEM((STAGE_DEPTH, STAGE_ROWS, D_IN), F32),
        pltpu.VMEM((STAGE_DEPTH, STAGE_ROWS, D_MODEL), F32),
        pltpu.VMEM((STAGE_DEPTH, STAGE_ROWS, D_MODEL), F32),
        pltpu.SemaphoreType.DMA((3 * STAGE_DEPTH,)),
    ]
    return pl.pallas_call(
        functools.partial(_layer_kernel, tiles_per_seq, n_tiles),
        grid=(n_tiles + 1,),
        in_specs=in_specs,
        out_specs=pl.BlockSpec(x_block, mix_map),
        out_shape=jax.ShapeDtypeStruct(x.shape, x.dtype),
        scratch_shapes=scratch,
        compiler_params=pltpu.CompilerParams(
            dimension_semantics=("arbitrary",),
            vmem_limit_bytes=VMEM_LIMIT_BYTES),
    )(x, x, x, norm_w.reshape(1, D_MODEL), tb["col_scale"], w_in, w_out,
      tb["cos"], tb["sin_s"], tb["dec"], tb["xi"], tb["zeta"], tb["cd"],
      ret_norm_w.reshape(1, -1), q_norm_w.reshape(1, -1), k_norm_w.reshape(1, -1), tb["bucket"],
      rel_bias.astype(F32), sinks.astype(F32))
```

```python
import functools
import math

import numpy as np
import jax
import jax.numpy as jnp
from jax import lax
from jax.experimental import pallas as pl
from jax.experimental.pallas import tpu as pltpu

D_MODEL = 1024
RET_HEADS = 4
RET_DK = 64
RET_DV = 128
CHUNK = 128
Q_HALF = CHUNK // 2
RET_ROPE_BASE = 10000.0
SWA_HEADS = 8
SWA_KV_HEADS = 2
SWA_HEAD_DIM = 64
SWA_GROUP = SWA_HEADS // SWA_KV_HEADS
NUM_BUCKETS = 32
MAX_DISTANCE = 128
NORM_EPS = 1e-6
GN_EPS = 1e-5
NEG_INF = -1e30
LOG2_E = math.log2(math.e)

OFF_RQ, OFF_RK, OFF_RV, OFF_RG = 0, 256, 512, 1024
D_TOK = 1536
FT_Q, FT_K, FT_G = 0, 512, 768
D_FT = 1280
D_IN = D_TOK + D_FT
D_MIX = 1024

LANES = 128
SEQ_TILE = 512
N_CHUNKS = SEQ_TILE // CHUNK
ROW_BLOCK = 32
VMEM_LIMIT_BYTES = 56 * 1024 * 1024
MIX_ORDER = (1, 1, 0, 0, 1, 1, 0, 0, 1, 1, 0, 0, 1, 0, 1, 1, 0, 0, 1, 1, 0, 0, 1, 0, 0)
NORM_AFTER_GROUP = 2
N_SLOT_BUFS = 10
STAGE_ROWS = 128
STAGE_DEPTH = 4
SLOT_KBAND, SLOT_VT = 7, 8

BF16 = jnp.bfloat16
F32 = jnp.float32


def _t5_bucket_np(n):
    max_exact = NUM_BUCKETS // 2
    nf = np.maximum(n, 1).astype(np.float64)
    large = max_exact + (np.log(nf / max_exact) / math.log(MAX_DISTANCE / max_exact)
                         * (NUM_BUCKETS - max_exact)).astype(np.int32)
    large = np.minimum(large, NUM_BUCKETS - 1)
    return np.where(n < max_exact, n, large).astype(np.int32)


def _constant_tables(seq):
    half = RET_DK // 2
    inv_freq = RET_ROPE_BASE ** (-np.arange(half, dtype=np.float64) / half)
    ang = np.arange(seq, dtype=np.float64)[:, None] * inv_freq[None, :]
    cos, sin = np.cos(ang), np.sin(ang)
    cos_t = np.tile(np.concatenate([cos, cos], axis=1), (1, 2))
    sin_s = np.tile(np.concatenate([-sin, sin], axis=1), (1, 2))

    gamma = 1.0 - np.exp2(-5.0 - np.arange(RET_HEADS, dtype=np.float64))
    log_g = np.log(gamma)
    i = np.arange(CHUNK, dtype=np.float64)
    diff = i[:, None] - i[None, :]
    k_scale = RET_DK ** -0.5
    decay = np.where(diff >= 0, np.exp(log_g[:, None, None] * np.maximum(diff, 0.0)), 0.0) * k_scale
    dec = np.stack([np.concatenate([decay[2 * p], decay[2 * p + 1]], axis=1) for p in range(2)])
    xi = np.exp(log_g[:, None] * (i + 1.0))
    zeta = np.exp(log_g[:, None] * (CHUNK - 1.0 - i)) * k_scale

    def per_lane(t):
        return np.stack([np.concatenate([np.repeat(t[2 * p][:, None], RET_DK, 1),
                                         np.repeat(t[2 * p + 1][:, None], RET_DK, 1)], axis=1)
                         for p in range(2)])

    lo = (np.arange(LANES) < RET_DK)[None, None, :]
    xi_l = per_lane(xi)
    xi_m = np.stack([np.where(lo, xi_l, 0.0), np.where(lo, 0.0, xi_l)], axis=1)

    chunk_decay = np.exp(log_g * CHUNK)
    cd = np.stack([np.concatenate([np.full((1, RET_DV), chunk_decay[2 * p]),
                                   np.full((1, RET_DV), chunk_decay[2 * p + 1])], axis=1)
                   for p in range(2)])

    j = np.arange(CHUNK)[None, :, None]
    q = (np.arange(2)[:, None, None] * Q_HALF + np.arange(LANES)[None, None, :] % Q_HALF)
    dist = np.where(j <= q, q - j, q + CHUNK - j)
    bucket = _t5_bucket_np(dist)

    f = lambda a: jnp.asarray(a, dtype=F32)
    col_scale = np.ones((1, D_IN))
    col_scale[:, OFF_RG:OFF_RG + RET_HEADS * RET_DV] = 0.5
    col_scale[:, D_TOK + FT_G:D_TOK + FT_G + SWA_HEADS * SWA_HEAD_DIM] = 0.5

    return dict(cos=f(cos_t), sin_s=f(sin_s), col_scale=f(col_scale), dec=f(dec), xi=f(xi_m),
                zeta=f(per_lane(zeta)), cd=f(cd), bucket=jnp.asarray(bucket, dtype=jnp.int32))


def _dot(a, b):
    return jnp.dot(a, b, preferred_element_type=F32)


def _dot_nt(a, b):
    return lax.dot_general(a, b, (((1,), (1,)), ((), ())), preferred_element_type=F32)


def _dot_tn(a, b):
    return lax.dot_general(a, b, (((0,), (0,)), ((), ())), preferred_element_type=F32)


def _silu_of_half(hg):
    return hg + hg * jnp.tanh(hg)


def _row_blocks(n_rows):
    return [(r0, slice(r0, r0 + ROW_BLOCK)) for r0 in range(0, n_rows, ROW_BLOCK)]


def _layer_kernel(tiles_per_seq, n_tiles,
                  xnext_ref, xres_ref, x_hbm, nwrow_ref, colscale_ref, win_hbm, wout_hbm, cos_ref, sins_ref, dec_ref,
                  xi_ref, zeta_ref, cd_ref, retw_ref, qnw_ref, knw_ref, bkt_ref, relb_ref,
                  sinks_ref, o_ref, h0_s, h1_s, *scratch):
    h_bufs = (h0_s, h1_s)
    slots = (scratch[0:N_SLOT_BUFS], scratch[N_SLOT_BUFS:2 * N_SLOT_BUFS])
    (pt_s, mixed_s, mixedt_s, state_s, bias_s,
     win_ref, wft_ref, wout_ref, stage_in, stage_out, stage_x, stage_sem) = scratch[2 * N_SLOT_BUFS:]
    step = pl.program_id(0)
    mix_tile = jnp.maximum(step - 1, 0)
    first = mix_tile % tiles_per_seq == 0
    first_i = first.astype(jnp.int32)
    keep_state = jnp.where(first, 0.0, 1.0).astype(F32)

    def rms_norm_rows(xb):
        ms = jnp.mean(xb * xb, axis=-1, keepdims=True)
        return (xb * lax.rsqrt(ms + NORM_EPS) * nwrow_ref[...]).astype(BF16)

    @pl.when(step == 0)
    def _init():
        def build_tables():
            key = lax.broadcasted_iota(jnp.int32, (CHUNK, LANES), 0)
            lane_c = lax.broadcasted_iota(jnp.int32, (CHUNK, LANES), 1)
            for half in range(2):
                bk = bkt_ref[half]
                is_cur = key <= half * Q_HALF + lane_c % Q_HALF
                for pair in range(SWA_HEADS // 2):
                    acc = jnp.zeros((CHUNK, LANES), F32)
                    for u in range(NUM_BUCKETS):
                        val = jnp.where(lane_c < Q_HALF, relb_ref[u, 2 * pair], relb_ref[u, 2 * pair + 1]) * LOG2_E
                        acc = jnp.where(bk == u, val, acc)
                    bias_s[half * 4 + pair] = acc
                    bias_s[SWA_HEADS + half * 4 + pair] = jnp.where(is_cur, acc, NEG_INF)
            zeros_blk = jnp.zeros((N_CHUNKS, SWA_KV_HEADS, Q_HALF, 2 * LANES), BF16)
            pt_s[:, :, 3 * Q_HALF:4 * Q_HALF, 0:2 * LANES] = zeros_blk
            pt_s[:, :, 0:Q_HALF, 2 * LANES:4 * LANES] = zeros_blk
            state_s[...] = jnp.zeros_like(state_s)
            slots[0][SLOT_KBAND][0:CHUNK, :] = jnp.zeros((CHUNK, LANES), BF16)
            slots[0][SLOT_VT][:, 0:CHUNK] = jnp.zeros((SWA_KV_HEADS * SWA_HEAD_DIM, CHUNK), BF16)

        def chunk_copies(hbm, stage, sem0, n_rows):
            return [pltpu.make_async_copy(hbm.at[pl.ds(i * STAGE_ROWS, STAGE_ROWS), :],
                                          stage.at[i % STAGE_DEPTH], stage_sem.at[sem0 + i % STAGE_DEPTH])
                    for i in range(n_rows // STAGE_ROWS)]

        def drain(copies, consume):
            for i, cp in enumerate(copies):
                cp.wait()
                consume(i)
                if i + STAGE_DEPTH < len(copies):
                    copies[i + STAGE_DEPTH].start()

        def consume_in(i):
            rows = slice(i * STAGE_ROWS, (i + 1) * STAGE_ROWS)
            for c0 in range(0, D_TOK, LANES):
                win_ref[rows, c0:c0 + LANES] = (stage_in[i % STAGE_DEPTH, :, c0:c0 + LANES]
                                                * colscale_ref[:, c0:c0 + LANES]).astype(BF16)
            for f0 in range(0, D_FT, LANES):
                cols = slice(D_TOK + f0, D_TOK + f0 + LANES)
                wft_ref[f0:f0 + LANES, rows] = (stage_in[i % STAGE_DEPTH, :, cols]
                                                * colscale_ref[:, cols]).T.astype(BF16)

        def consume_out(i):
            rows = slice(i * STAGE_ROWS, (i + 1) * STAGE_ROWS)
            wout_ref[rows, :] = stage_out[i % STAGE_DEPTH].astype(BF16)

        def consume_x0(i):
            for r0, rs in _row_blocks(STAGE_ROWS):
                h0_s[i * STAGE_ROWS + r0:i * STAGE_ROWS + r0 + ROW_BLOCK, :] = rms_norm_rows(
                    stage_x[i % STAGE_DEPTH, rs, :])

        streams = [(chunk_copies(win_hbm, stage_in, 0, D_MODEL), consume_in),
                   (chunk_copies(wout_hbm, stage_out, STAGE_DEPTH, D_MIX), consume_out),
                   (chunk_copies(x_hbm.at[0], stage_x, 2 * STAGE_DEPTH, SEQ_TILE), consume_x0)]
        for copies, _ in streams:
            for cp in copies[:STAGE_DEPTH]:
                cp.start()
        build_tables()
        for copies, consume in streams:
            drain(copies, consume)

    lane = lax.broadcasted_iota(jnp.int32, (ROW_BLOCK, LANES), 1)
    lo = lane < RET_DK
    first_half = lane % RET_DK < RET_DK // 2

    def project(slot, h_s, h_next_s):
        rq_s, rqx_s, rk_s, rkz_s, rv_s, gate_s, qt_s, kband_s, vt_s, gatet_s = slot

        def proj(off):
            return _dot(h_s[...], win_ref[:, off:off + 4 * LANES])

        def proj_t(row0, n_rows):
            return _dot_nt(wft_ref[row0:row0 + n_rows, :], h_s[...])

        def rot(v, rs):
            partner = jnp.where(first_half, pltpu.roll(v, LANES - 32, 1), pltpu.roll(v, 32, 1))
            return v * cos_ref[rs, :] + partner * sins_ref[rs, :]

        def head_rms(xt):
            ms = jnp.mean(xt * xt, axis=0, keepdims=True)
            return xt * lax.rsqrt(ms + NORM_EPS)

        tok_blocks = [slice(c0, c0 + LANES) for c0 in range(0, SEQ_TILE, LANES)]

        def post_rqk(r):
            for r0, rs in _row_blocks(SEQ_TILE):
                c, cr = r0 // CHUNK, r0 % CHUNK
                cs = slice(cr, cr + ROW_BLOCK)
                for p in range(2):
                    q = rot(r[rs, p * LANES:(p + 1) * LANES], rs)
                    rq_s[p, rs, :] = q.astype(BF16)
                    rqx_s[p, 0, rs, :] = (q * xi_ref[p, 0, cs, :]).astype(BF16)
                    rqx_s[p, 1, rs, :] = (q * xi_ref[p, 1, cs, :]).astype(BF16)
                    k = rot(r[rs, (2 + p) * LANES:(3 + p) * LANES], rs)
                    rk_s[p, c, cr:cr + ROW_BLOCK, :] = jnp.where(lo, k, 0.0).astype(BF16)
                    rk_s[p, c, CHUNK + cr:CHUNK + cr + ROW_BLOCK, :] = jnp.where(lo, 0.0, k).astype(BF16)
                    rkz_s[p, rs, :] = (k * zeta_ref[p, cs, :]).astype(BF16)

        def post_rv(r):
            for r0, rs in _row_blocks(SEQ_TILE):
                rv_s[rs, :] = r[rs, :].astype(BF16)

        def post_rg(r):
            for r0, rs in _row_blocks(SEQ_TILE):
                gate_s[rs, :] = _silu_of_half(r[rs, :])

        def post_qt(rt):
            for pair in range(SWA_HEADS // 2):
                g, gp = pair // 2, pair % 2
                for c, ts in enumerate(tok_blocks):
                    xe, xo = (head_rms(rt[hd * SWA_HEAD_DIM:(hd + 1) * SWA_HEAD_DIM, ts])
                              for hd in (2 * pair, 2 * pair + 1))
                    for half in range(2):
                        hq = slice(half * Q_HALF, (half + 1) * Q_HALF)
                        qt_s[c, half, g, :, gp * LANES:(gp + 1) * LANES] = jnp.concatenate(
                            [xe[:, hq], xo[:, hq]], axis=1).astype(BF16)

        def post_kvt(rt):
            qk_head = qnw_ref[...] * knw_ref[...] * (SWA_HEAD_DIM ** -0.5 * LOG2_E)
            qk_w = jnp.concatenate([qk_head] * SWA_KV_HEADS, axis=1)
            for ts in tok_blocks:
                knt = jnp.concatenate(
                    [head_rms(rt[g * SWA_HEAD_DIM:(g + 1) * SWA_HEAD_DIM, ts]) for g in range(SWA_KV_HEADS)],
                    axis=0)
                kband_s[CHUNK + ts.start:CHUNK + ts.stop, :] = (knt.T * qk_w).astype(BF16)
                vt_s[:, CHUNK + ts.start:CHUNK + ts.stop] = rt[LANES:2 * LANES, ts].astype(BF16)

        def post_gt(rt):
            for r0 in range(0, SWA_HEADS * SWA_HEAD_DIM, ROW_BLOCK):
                for ts in tok_blocks:
                    gatet_s[r0:r0 + ROW_BLOCK, ts] = _silu_of_half(rt[r0:r0 + ROW_BLOCK, ts])

        groups = [
            (functools.partial(proj, OFF_RQ), post_rqk),
            (functools.partial(proj_t, FT_Q, SWA_HEADS * SWA_HEAD_DIM), post_qt),
            (functools.partial(proj, OFF_RG), post_rg),
            (functools.partial(proj_t, FT_G, SWA_HEADS * SWA_HEAD_DIM), post_gt),
            (functools.partial(proj_t, FT_K, 2 * LANES), post_kvt),
            (functools.partial(proj, OFF_RV), post_rv),
        ]
        pending = None
        for i, (matmul, post) in enumerate(groups):
            result = matmul()
            yield
            if pending is not None:
                pending[0](pending[1])
                yield
            pending = (post, result)
            if i == NORM_AFTER_GROUP:
                for r0, rs in _row_blocks(SEQ_TILE):
                    h_next_s[rs, :] = rms_norm_rows(xnext_ref[0, rs, :])
                yield
        pending[0](pending[1])
        yield

    def mix(slot, next_slot):
        rq_s, rqx_s, rk_s, rkz_s, rv_s, gate_s, qt_s, kband_s, vt_s, gatet_s = slot
        key = lax.broadcasted_iota(jnp.int32, (Q_HALF, LANES), 0)
        lane_h = lax.broadcasted_iota(jnp.int32, (Q_HALF, LANES), 1)
        tri_t = key <= lane_h % Q_HALF
        cur_keep = jnp.where(tri_t, 1.0, 0.0).astype(BF16)
        prev_keep = jnp.where(tri_t, 0.0, 1.0).astype(BF16)
        first_head_lanes = lax.broadcasted_iota(jnp.int32, (1, LANES), 1) < Q_HALF
        zero_q = jnp.zeros((SWA_HEAD_DIM, 2 * LANES), BF16)
        ones_rows = jnp.ones((8, 2 * CHUNK), BF16)

        def out_proj(c_hi):
            pair = slice((c_hi - 1) * CHUNK, (c_hi + 1) * CHUNK)
            o_ref[0, pair, :] = (xres_ref[0, pair, :]
                                 + _dot(mixed_s[pair, :], wout_ref[0:RET_HEADS * RET_DV, :])
                                 + _dot_tn(mixedt_s[:, pair], wout_ref[RET_HEADS * RET_DV:D_MIX, :]))

        chunk_rows = [slice(c * CHUNK, (c + 1) * CHUNK) for c in range(N_CHUNKS)]
        chunk_band = [slice(c * CHUNK, (c + 2) * CHUNK) for c in range(N_CHUNKS)]

        def logits_for(c):
            rhs_a, rhs_b = (jnp.concatenate([jnp.concatenate([qt_s[c, half, 0], zero_q], axis=1),
                                             jnp.concatenate([zero_q, qt_s[c, half, 1]], axis=1)], axis=0)
                            for half in range(2))
            k0 = c * CHUNK
            p0_a = _dot(kband_s[k0:k0 + Q_HALF, :], rhs_a)
            mid = _dot(kband_s[k0 + Q_HALF:k0 + 3 * Q_HALF, :], jnp.concatenate([rhs_a, rhs_b], axis=1))
            c1_b = _dot(kband_s[k0 + 3 * Q_HALF:k0 + 4 * Q_HALF, :], rhs_b)
            return p0_a, mid, c1_b

        scores, incs, logits_t = {}, {}, []
        for c, rows in enumerate(chunk_rows):
            for p in range(2):
                s2 = _dot_nt(rq_s[p, rows, :], rk_s[p, c])
                scores[c, p] = jnp.concatenate([(s2[bs, :] * dec_ref[p, bs, :]).astype(BF16)
                                                for _, bs in _row_blocks(CHUNK)], axis=0)
                incs[c, p] = _dot_tn(rkz_s[p, rows, :], rv_s[rows, p * 2 * RET_DV:(p + 1) * 2 * RET_DV])
            logits_t.append(logits_for(c))
        yield

        ret_outs = {}
        for p in range(2):
            full = state_s[p] * keep_state
            for c, rows in enumerate(chunk_rows):
                full_b = full.astype(BF16)
                v2 = rv_s[rows, p * 2 * RET_DV:(p + 1) * 2 * RET_DV]
                s2 = scores[c, p]
                for j in range(2):
                    ret_outs[c, 2 * p + j] = _dot(
                        jnp.concatenate([s2[:, j * CHUNK:(j + 1) * CHUNK], rqx_s[p, j, rows, :]], axis=1),
                        jnp.concatenate([v2[:, j * RET_DV:(j + 1) * RET_DV],
                                         full_b[:, j * RET_DV:(j + 1) * RET_DV]], axis=0))
                full = full * cd_ref[p] + incs[c, p]
            state_s[p] = full
        yield

        for c in range(N_CHUNKS):
            rows, band, lt = chunk_rows[c], chunk_band[c], logits_t[c]

            for hd in range(RET_HEADS):
                o = ret_outs[c, hd]
                cols = slice(hd * RET_DV, (hd + 1) * RET_DV)
                for r0, bs in _row_blocks(CHUNK):
                    ob = o[bs, :]
                    rs = slice(c * CHUNK + r0, c * CHUNK + r0 + ROW_BLOCK)
                    mu = jnp.mean(ob, axis=-1, keepdims=True)
                    d = ob - mu
                    var = jnp.mean(d * d, axis=-1, keepdims=True)
                    y = d * lax.rsqrt(var + GN_EPS) * retw_ref[:, cols]
                    mixed_s[rs, cols] = (y * gate_s[rs, cols]).astype(BF16)
            p0_a, mid, c1_b = lt
            p_rows = [slice(i * Q_HALF, (i + 1) * Q_HALF) for i in range(4)]
            sink_terms = {}
            for half in range(2):
                for pair in range(SWA_HEADS // 2):
                    g, gp = pair // 2, pair % 2
                    cs = slice(pair * LANES, (pair + 1) * LANES)
                    cs_b = slice(4 * LANES + pair * LANES, 4 * LANES + (pair + 1) * LANES)
                    sink = jnp.where(first_head_lanes, sinks_ref[2 * pair], sinks_ref[2 * pair + 1]) * LOG2_E
                    bi = half * 4 + pair
                    bias = bias_s[bi + SWA_HEADS * first_i] if c == 0 else bias_s[bi]
                    if half == 0:
                        lg0 = jnp.where(tri_t, mid[Q_HALF:CHUNK, cs], p0_a[:, cs]) + bias[0:Q_HALF]
                        lg1 = mid[0:Q_HALF, cs] + bias[Q_HALF:CHUNK]
                    else:
                        lg0 = mid[Q_HALF:CHUNK, cs_b] + bias[0:Q_HALF]
                        lg1 = jnp.where(tri_t, c1_b[:, cs], mid[0:Q_HALF, cs_b]) + bias[Q_HALF:CHUNK]
                    m = jnp.maximum(jnp.maximum(jnp.max(lg0, axis=0, keepdims=True),
                                                jnp.max(lg1, axis=0, keepdims=True)), sink)
                    e0 = jnp.exp2(lg0 - m).astype(BF16)
                    e1 = jnp.exp2(lg1 - m).astype(BF16)
                    sink_terms[half, pair] = jnp.exp2(sink - m)
                    pq = slice(half * 2 * LANES + gp * LANES, half * 2 * LANES + (gp + 1) * LANES)
                    if half == 0:
                        pt_s[c, g, p_rows[0], pq] = e0 * prev_keep
                        pt_s[c, g, p_rows[2], pq] = e0 * cur_keep
                        pt_s[c, g, p_rows[1], pq] = e1
                    else:
                        pt_s[c, g, p_rows[2], pq] = e0
                        pt_s[c, g, p_rows[1], pq] = e1 * prev_keep
                        pt_s[c, g, p_rows[3], pq] = e1 * cur_keep
            yield

            for g in range(SWA_KV_HEADS):
                vt1 = jnp.concatenate([vt_s[g * SWA_HEAD_DIM:(g + 1) * SWA_HEAD_DIM, band], ones_rows], axis=0)
                ot = _dot(vt1, pt_s[c, g])
                den = ot[SWA_HEAD_DIM:SWA_HEAD_DIM + 1, :] + jnp.concatenate(
                    [sink_terms[half, 2 * g + gp] for half in range(2) for gp in range(2)], axis=1)
                otn = ot[0:SWA_HEAD_DIM, :] * (1.0 / den)
                for hh in range(SWA_GROUP):
                    hd = SWA_GROUP * g + hh
                    hs = slice(hd * SWA_HEAD_DIM, (hd + 1) * SWA_HEAD_DIM)
                    o_h = jnp.concatenate([otn[:, hh * Q_HALF:(hh + 1) * Q_HALF],
                                           otn[:, 2 * LANES + hh * Q_HALF:2 * LANES + (hh + 1) * Q_HALF]], axis=1)
                    mixedt_s[hs, rows] = (o_h * gatet_s[hs, rows]).astype(BF16)
            yield
            if c % 2 == 1:
                out_proj(c)
                yield

        last = slice(SEQ_TILE, SEQ_TILE + CHUNK)
        next_slot[SLOT_KBAND][0:CHUNK, :] = kband_s[last, :]
        next_slot[SLOT_VT][:, 0:CHUNK] = vt_s[:, last]

    def run(parity, do_project=True, do_mix=True):
        streams = [project(slots[parity], h_bufs[parity], h_bufs[1 - parity]) if do_project else iter(()),
                   mix(slots[1 - parity], slots[parity]) if do_mix else iter(())]
        for which in MIX_ORDER + (0,) * len(MIX_ORDER) + (1,) * len(MIX_ORDER):
            next(streams[which], None)

    is_first, is_last = step == 0, step == n_tiles
    pl.when(is_first)(functools.partial(run, 0, do_mix=False))
    pl.when(is_last)(functools.partial(run, n_tiles % 2, do_project=False))
    for parity in range(2):
        pl.when((step % 2 == parity) & jnp.logical_not(is_first | is_last))(functools.partial(run, parity))


def kernel(x, norm_w, w_in, ret_norm_w, q_norm_w, k_norm_w, sinks, rel_bias, w_out):
    batch, seq, d_model = x.shape
    assert d_model == D_MODEL and seq % SEQ_TILE == 0
    assert w_in.shape == (D_MODEL, D_IN) and w_out.shape == (D_MIX, D_MODEL)
    tb = _constant_tables(seq)
    tiles_per_seq = seq // SEQ_TILE
    n_tiles = batch * tiles_per_seq

    def proj_tile(s):
        return jnp.minimum(s, n_tiles - 1)

    def mix_tile(s):
        return jnp.maximum(s - 1, 0)

    const2 = lambda s: (0, 0)
    const3 = lambda s: (0, 0, 0)
    const4 = lambda s: (0, 0, 0, 0)
    pos_spec = pl.BlockSpec((SEQ_TILE, LANES), lambda s: (proj_tile(s) % tiles_per_seq, 0))
    smem = pl.BlockSpec(memory_space=pltpu.SMEM)
    x_block = (1, SEQ_TILE, D_MODEL)
    mix_map = lambda s: (mix_tile(s) // tiles_per_seq, mix_tile(s) % tiles_per_seq, 0)

    in_specs = [
        pl.BlockSpec(x_block, lambda s: (proj_tile(s + 1) // tiles_per_seq, proj_tile(s + 1) % tiles_per_seq, 0)),
        pl.BlockSpec(x_block, mix_map),
        pl.BlockSpec(memory_space=pl.ANY),
        pl.BlockSpec((1, D_MODEL), const2),
        pl.BlockSpec((1, D_IN), const2),
        pl.BlockSpec(memory_space=pl.ANY),
        pl.BlockSpec(memory_space=pl.ANY),
        pos_spec, pos_spec,
        pl.BlockSpec((2, CHUNK, 2 * CHUNK), const3),
        pl.BlockSpec((2, 2, CHUNK, LANES), const4),
        pl.BlockSpec((2, CHUNK, LANES), const3),
        pl.BlockSpec((2, 1, 2 * RET_DV), const3),
        pl.BlockSpec((1, RET_HEADS * RET_DV), const2),
        pl.BlockSpec((1, SWA_HEAD_DIM), const2),
        pl.BlockSpec((1, SWA_HEAD_DIM), const2),
        pl.BlockSpec((2, CHUNK, LANES), const3),
        smem, smem,
    ]
    slot_bufs = [
        pltpu.VMEM((2, SEQ_TILE, LANES), BF16),
        pltpu.VMEM((2, 2, SEQ_TILE, LANES), BF16),
        pltpu.VMEM((2, N_CHUNKS, 2 * CHUNK, LANES), BF16),
        pltpu.VMEM((2, SEQ_TILE, LANES), BF16),
        pltpu.VMEM((SEQ_TILE, RET_HEADS * RET_DV), BF16),
        pltpu.VMEM((SEQ_TILE, RET_HEADS * RET_DV), F32),
        pltpu.VMEM((N_CHUNKS, 2, SWA_KV_HEADS, SWA_HEAD_DIM, 2 * LANES), BF16),
        pltpu.VMEM((SEQ_TILE + CHUNK, LANES), BF16),
        pltpu.VMEM((SWA_KV_HEADS * SWA_HEAD_DIM, SEQ_TILE + CHUNK), BF16),
        pltpu.VMEM((SWA_HEADS * SWA_HEAD_DIM, SEQ_TILE), F32),
    ]
    assert len(slot_bufs) == N_SLOT_BUFS
    h_buf = pltpu.VMEM((SEQ_TILE, D_MODEL), BF16)
    scratch = [h_buf, h_buf] + slot_bufs + slot_bufs + [
        pltpu.VMEM((N_CHUNKS, SWA_KV_HEADS, 2 * CHUNK, SWA_GROUP * CHUNK), BF16),
        pltpu.VMEM((SEQ_TILE, RET_HEADS * RET_DV), BF16),
        pltpu.VMEM((SWA_HEADS * SWA_HEAD_DIM, SEQ_TILE), BF16),
        pltpu.VMEM((2, 2 * RET_DK, 2 * RET_DV), F32),
        pltpu.VMEM((2 * SWA_HEADS, CHUNK, LANES), F32),
        pltpu.VMEM((D_MODEL, D_TOK), BF16),
        pltpu.VMEM((D_FT, D_MODEL), BF16),
        pltpu.VMEM((D_MIX, D_MODEL), BF16),
        pltpu.VMEM((STAGE_DEPTH, STAGE_ROWS, D_IN), F32),
        pltpu.VMEM((STAGE_DEPTH, STAGE_ROWS, D_MODEL), F32),
        pltpu.VMEM((STAGE_DEPTH, STAGE_ROWS, D_MODEL), F32),
        pltpu.SemaphoreType.DMA((3 * STAGE_DEPTH,)),
    ]
    return pl.pallas_call(
        functools.partial(_layer_kernel, tiles_per_seq, n_tiles),
        grid=(n_tiles + 1,),
        in_specs=in_specs,
        out_specs=pl.BlockSpec(x_block, mix_map),
        out_shape=jax.ShapeDtypeStruct(x.shape, x.dtype),
        scratch_shapes=scratch,
        compiler_params=pltpu.CompilerParams(
            dimension_semantics=("arbitrary",),
            vmem_limit_bytes=VMEM_LIMIT_BYTES),
    )(x, x, x, norm_w.reshape(1, D_MODEL), tb["col_scale"], w_in, w_out,
      tb["cos"], tb["sin_s"], tb["dec"], tb["xi"], tb["zeta"], tb["cd"],
      ret_norm_w.reshape(1, -1), q_norm_w.reshape(1, -1), k_norm_w.reshape(1, -1), tb["bucket"],
      rel_bias.astype(F32), sinks.astype(F32))
```

```python
import functools
import math

import numpy as np
import jax
import jax.numpy as jnp
from jax import lax
from jax.experimental import pallas as pl
from jax.experimental.pallas import tpu as pltpu

D_MODEL = 1024
RET_HEADS = 4
RET_DK = 64
RET_DV = 128
CHUNK = 128
Q_HALF = CHUNK // 2
RET_ROPE_BASE = 10000.0
SWA_HEADS = 8
SWA_KV_HEADS = 2
SWA_HEAD_DIM = 64
SWA_GROUP = SWA_HEADS // SWA_KV_HEADS
NUM_BUCKETS = 32
MAX_DISTANCE = 128
NORM_EPS = 1e-6
GN_EPS = 1e-5
NEG_INF = -1e30
LOG2_E = math.log2(math.e)

OFF_RQ, OFF_RK, OFF_RV, OFF_RG = 0, 256, 512, 1024
D_TOK = 1536
FT_Q, FT_K, FT_G = 0, 512, 768
D_FT = 1280
D_IN = D_TOK + D_FT
D_MIX = 1024

LANES = 128
SEQ_TILE = 512
N_CHUNKS = SEQ_TILE // CHUNK
ROW_BLOCK = 32
VMEM_LIMIT_BYTES = 56 * 1024 * 1024
MIX_ORDER = (1, 1, 0, 0, 1, 1, 0, 0, 1, 1, 0, 0, 1, 0, 1, 1, 0, 0, 1, 1, 0, 0, 1, 0, 0)
NORM_AFTER_GROUP = 2
N_SLOT_BUFS = 10
STAGE_ROWS = 128
SLAB_GROUPS = ((OFF_RQ, 512), (D_TOK + FT_Q, 512), (OFF_RG, 512), (D_TOK + FT_G, 512), (D_TOK + FT_K, 256),
               (OFF_RV, 512))
SLOT_KBAND, SLOT_VT = 7, 8

BF16 = jnp.bfloat16
F32 = jnp.float32


def _t5_bucket_np(n):
    max_exact = NUM_BUCKETS // 2
    nf = np.maximum(n, 1).astype(np.float64)
    large = max_exact + (np.log(nf / max_exact) / math.log(MAX_DISTANCE / max_exact)
                         * (NUM_BUCKETS - max_exact)).astype(np.int32)
    large = np.minimum(large, NUM_BUCKETS - 1)
    return np.where(n < max_exact, n, large).astype(np.int32)


def _constant_tables(seq):
    half = RET_DK // 2
    inv_freq = RET_ROPE_BASE ** (-np.arange(half, dtype=np.float64) / half)
    ang = np.arange(seq, dtype=np.float64)[:, None] * inv_freq[None, :]
    cos, sin = np.cos(ang), np.sin(ang)
    cos_t = np.tile(np.concatenate([cos, cos], axis=1), (1, 2))
    sin_s = np.tile(np.concatenate([-sin, sin], axis=1), (1, 2))

    gamma = 1.0 - np.exp2(-5.0 - np.arange(RET_HEADS, dtype=np.float64))
    log_g = np.log(gamma)
    i = np.arange(CHUNK, dtype=np.float64)
    diff = i[:, None] - i[None, :]
    k_scale = RET_DK ** -0.5
    decay = np.where(diff >= 0, np.exp(log_g[:, None, None] * np.maximum(diff, 0.0)), 0.0) * k_scale
    dec = np.stack([np.concatenate([decay[2 * p], decay[2 * p + 1]], axis=1) for p in range(2)])
    xi = np.exp(log_g[:, None] * (i + 1.0))
    zeta = np.exp(log_g[:, None] * (CHUNK - 1.0 - i)) * k_scale

    def per_lane(t):
        return np.stack([np.concatenate([np.repeat(t[2 * p][:, None], RET_DK, 1),
                                         np.repeat(t[2 * p + 1][:, None], RET_DK, 1)], axis=1)
                         for p in range(2)])

    lo = (np.arange(LANES) < RET_DK)[None, None, :]
    xi_l = per_lane(xi)
    xi_m = np.stack([np.where(lo, xi_l, 0.0), np.where(lo, 0.0, xi_l)], axis=1)

    chunk_decay = np.exp(log_g * CHUNK)
    cd = np.stack([np.concatenate([np.full((1, RET_DV), chunk_decay[2 * p]),
                                   np.full((1, RET_DV), chunk_decay[2 * p + 1])], axis=1)
                   for p in range(2)])

    j = np.arange(CHUNK)[None, :, None]
    q = (np.arange(2)[:, None, None] * Q_HALF + np.arange(LANES)[None, None, :] % Q_HALF)
    dist = np.where(j <= q, q - j, q + CHUNK - j)
    bucket = _t5_bucket_np(dist)

    f = lambda a: jnp.asarray(a, dtype=F32)
    col_scale = np.ones((1, D_IN))
    col_scale[:, OFF_RG:OFF_RG + RET_HEADS * RET_DV] = 0.5
    col_scale[:, D_TOK + FT_G:D_TOK + FT_G + SWA_HEADS * SWA_HEAD_DIM] = 0.5

    return dict(cos=f(cos_t), sin_s=f(sin_s), col_scale=f(col_scale), dec=f(dec), xi=f(xi_m),
                zeta=f(per_lane(zeta)), cd=f(cd), bucket=jnp.asarray(bucket, dtype=jnp.int32))


def _dot(a, b):
    return jnp.dot(a, b, preferred_element_type=F32)


def _dot_nt(a, b):
    return lax.dot_general(a, b, (((1,), (1,)), ((), ())), preferred_element_type=F32)


def _dot_tn(a, b):
    return lax.dot_general(a, b, (((0,), (0,)), ((), ())), preferred_element_type=F32)


def _silu_of_half(hg):
    return hg + hg * jnp.tanh(hg)


def _row_blocks(n_rows):
    return [(r0, slice(r0, r0 + ROW_BLOCK)) for r0 in range(0, n_rows, ROW_BLOCK)]


def _layer_kernel(tiles_per_seq, n_tiles,
                  xnext_ref, xres_ref, x_hbm, nwrow_ref, colscale_ref, win_hbm, wout_hbm, cos_ref, sins_ref, dec_ref,
                  xi_ref, zeta_ref, cd_ref, retw_ref, qnw_ref, knw_ref, bkt_ref, relb_ref,
                  sinks_ref, o_ref, h0_s, h1_s, *scratch):
    h_bufs = (h0_s, h1_s)
    slots = (scratch[0:N_SLOT_BUFS], scratch[N_SLOT_BUFS:2 * N_SLOT_BUFS])
    (pt_s, mixed_s, mixedt_s, state_s, bias_s,
     win_ref, wft_ref, wout_ref, slab_s, stage_out, stage_x, stage_sem, slab_sem) = scratch[2 * N_SLOT_BUFS:]
    step = pl.program_id(0)
    mix_tile = jnp.maximum(step - 1, 0)
    first = mix_tile % tiles_per_seq == 0
    first_i = first.astype(jnp.int32)
    keep_state = jnp.where(first, 0.0, 1.0).astype(F32)

    def rms_norm_rows(xb):
        ms = jnp.mean(xb * xb, axis=-1, keepdims=True)
        return (xb * lax.rsqrt(ms + NORM_EPS) * nwrow_ref[...]).astype(BF16)

    def row_copies(hbm, stage, sem0, n_rows):
        return [pltpu.make_async_copy(hbm.at[pl.ds(i * STAGE_ROWS, STAGE_ROWS), :], stage.at[i],
                                      stage_sem.at[sem0 + i]) for i in range(n_rows // STAGE_ROWS)]

    def slab_copy(i):
        col0, width = SLAB_GROUPS[i]
        return pltpu.make_async_copy(win_hbm.at[:, pl.ds(col0, width)], slab_s.at[i % 2, :, pl.ds(0, width)],
                                     slab_sem.at[i % 2])

    def stage_group(i):
        col0, width = SLAB_GROUPS[i]
        slab_copy(i).wait()
        for r0 in range(0, D_MODEL, STAGE_ROWS):
            rows = slice(r0, r0 + STAGE_ROWS)
            for c0 in range(0, width, LANES):
                blk = slab_s[i % 2, rows, c0:c0 + LANES] * colscale_ref[:, col0 + c0:col0 + c0 + LANES]
                if col0 < D_TOK:
                    win_ref[rows, col0 + c0:col0 + c0 + LANES] = blk.astype(BF16)
                else:
                    wft_ref[col0 - D_TOK + c0:col0 - D_TOK + c0 + LANES, rows] = blk.T.astype(BF16)
        if i + 2 < len(SLAB_GROUPS):
            slab_copy(i + 2).start()

    def stage_w_out():
        for i, cp in enumerate(row_copies(wout_hbm, stage_out, SEQ_TILE // STAGE_ROWS, D_MIX)):
            cp.wait()
            wout_ref[i * STAGE_ROWS:(i + 1) * STAGE_ROWS, :] = stage_out[i].astype(BF16)

    @pl.when(step == 0)
    def _init():
        def build_tables():
            key = lax.broadcasted_iota(jnp.int32, (CHUNK, LANES), 0)
            lane_c = lax.broadcasted_iota(jnp.int32, (CHUNK, LANES), 1)
            for half in range(2):
                bk = bkt_ref[half]
                is_cur = key <= half * Q_HALF + lane_c % Q_HALF
                for pair in range(SWA_HEADS // 2):
                    acc = jnp.zeros((CHUNK, LANES), F32)
                    for u in range(NUM_BUCKETS):
                        val = jnp.where(lane_c < Q_HALF, relb_ref[u, 2 * pair], relb_ref[u, 2 * pair + 1]) * LOG2_E
                        acc = jnp.where(bk == u, val, acc)
                    bias_s[half * 4 + pair] = acc
                    bias_s[SWA_HEADS + half * 4 + pair] = jnp.where(is_cur, acc, NEG_INF)
            zeros_blk = jnp.zeros((N_CHUNKS, SWA_KV_HEADS, Q_HALF, 2 * LANES), BF16)
            pt_s[:, :, 3 * Q_HALF:4 * Q_HALF, 0:2 * LANES] = zeros_blk
            pt_s[:, :, 0:Q_HALF, 2 * LANES:4 * LANES] = zeros_blk
            state_s[...] = jnp.zeros_like(state_s)
            slots[0][SLOT_KBAND][0:CHUNK, :] = jnp.zeros((CHUNK, LANES), BF16)
            slots[0][SLOT_VT][:, 0:CHUNK] = jnp.zeros((SWA_KV_HEADS * SWA_HEAD_DIM, CHUNK), BF16)

        x_copies = row_copies(x_hbm.at[0], stage_x, 0, SEQ_TILE)
        for cp in x_copies + row_copies(wout_hbm, stage_out, len(x_copies), D_MIX) + [slab_copy(0), slab_copy(1)]:
            cp.start()
        build_tables()
        for i, cp in enumerate(x_copies):
            cp.wait()
            for r0, rs in _row_blocks(STAGE_ROWS):
                h0_s[i * STAGE_ROWS + r0:i * STAGE_ROWS + r0 + ROW_BLOCK, :] = rms_norm_rows(stage_x[i, rs, :])

    lane = lax.broadcasted_iota(jnp.int32, (ROW_BLOCK, LANES), 1)
    lo = lane < RET_DK
    first_half = lane % RET_DK < RET_DK // 2

    def project(slot, h_s, h_next_s, stage_weights=False):
        rq_s, rqx_s, rk_s, rkz_s, rv_s, gate_s, qt_s, kband_s, vt_s, gatet_s = slot

        def proj(off):
            return _dot(h_s[...], win_ref[:, off:off + 4 * LANES])

        def proj_t(row0, n_rows):
            return _dot_nt(wft_ref[row0:row0 + n_rows, :], h_s[...])

        def rot(v, rs):
            partner = jnp.where(first_half, pltpu.roll(v, LANES - 32, 1), pltpu.roll(v, 32, 1))
            return v * cos_ref[rs, :] + partner * sins_ref[rs, :]

        def head_rms(xt):
            ms = jnp.mean(xt * xt, axis=0, keepdims=True)
            return xt * lax.rsqrt(ms + NORM_EPS)

        tok_blocks = [slice(c0, c0 + LANES) for c0 in range(0, SEQ_TILE, LANES)]

        def post_rqk(r):
            for r0, rs in _row_blocks(SEQ_TILE):
                c, cr = r0 // CHUNK, r0 % CHUNK
                cs = slice(cr, cr + ROW_BLOCK)
                for p in range(2):
                    q = rot(r[rs, p * LANES:(p + 1) * LANES], rs)
                    rq_s[p, rs, :] = q.astype(BF16)
                    rqx_s[p, 0, rs, :] = (q * xi_ref[p, 0, cs, :]).astype(BF16)
                    rqx_s[p, 1, rs, :] = (q * xi_ref[p, 1, cs, :]).astype(BF16)
                    k = rot(r[rs, (2 + p) * LANES:(3 + p) * LANES], rs)
                    rk_s[p, c, cr:cr + ROW_BLOCK, :] = jnp.where(lo, k, 0.0).astype(BF16)
                    rk_s[p, c, CHUNK + cr:CHUNK + cr + ROW_BLOCK, :] = jnp.where(lo, 0.0, k).astype(BF16)
                    rkz_s[p, rs, :] = (k * zeta_ref[p, cs, :]).astype(BF16)

        def post_rv(r):
            for r0, rs in _row_blocks(SEQ_TILE):
                rv_s[rs, :] = r[rs, :].astype(BF16)

        def post_rg(r):
            for r0, rs in _row_blocks(SEQ_TILE):
                gate_s[rs, :] = _silu_of_half(r[rs, :])

        def post_qt(rt):
            for pair in range(SWA_HEADS // 2):
                g, gp = pair // 2, pair % 2
                for c, ts in enumerate(tok_blocks):
                    xe, xo = (head_rms(rt[hd * SWA_HEAD_DIM:(hd + 1) * SWA_HEAD_DIM, ts])
                              for hd in (2 * pair, 2 * pair + 1))
                    for half in range(2):
                        hq = slice(half * Q_HALF, (half + 1) * Q_HALF)
                        qt_s[c, half, g, :, gp * LANES:(gp + 1) * LANES] = jnp.concatenate(
                            [xe[:, hq], xo[:, hq]], axis=1).astype(BF16)

        def post_kvt(rt):
            qk_head = qnw_ref[...] * knw_ref[...] * (SWA_HEAD_DIM ** -0.5 * LOG2_E)
            qk_w = jnp.concatenate([qk_head] * SWA_KV_HEADS, axis=1)
            for ts in tok_blocks:
                knt = jnp.concatenate(
                    [head_rms(rt[g * SWA_HEAD_DIM:(g + 1) * SWA_HEAD_DIM, ts]) for g in range(SWA_KV_HEADS)],
                    axis=0)
                kband_s[CHUNK + ts.start:CHUNK + ts.stop, :] = (knt.T * qk_w).astype(BF16)
                vt_s[:, CHUNK + ts.start:CHUNK + ts.stop] = rt[LANES:2 * LANES, ts].astype(BF16)

        def post_gt(rt):
            for r0 in range(0, SWA_HEADS * SWA_HEAD_DIM, ROW_BLOCK):
                for ts in tok_blocks:
                    gatet_s[r0:r0 + ROW_BLOCK, ts] = _silu_of_half(rt[r0:r0 + ROW_BLOCK, ts])

        groups = [
            (functools.partial(proj, OFF_RQ), post_rqk),
            (functools.partial(proj_t, FT_Q, SWA_HEADS * SWA_HEAD_DIM), post_qt),
            (functools.partial(proj, OFF_RG), post_rg),
            (functools.partial(proj_t, FT_G, SWA_HEADS * SWA_HEAD_DIM), post_gt),
            (functools.partial(proj_t, FT_K, 2 * LANES), post_kvt),
            (functools.partial(proj, OFF_RV), post_rv),
        ]
        pending = None
        for i, (matmul, post) in enumerate(groups):
            if stage_weights:
                stage_group(i)
            result = matmul()
            yield
            if pending is not None:
                pending[0](pending[1])
                yield
            pending = (post, result)
            if i == NORM_AFTER_GROUP:
                for r0, rs in _row_blocks(SEQ_TILE):
                    h_next_s[rs, :] = rms_norm_rows(xnext_ref[0, rs, :])
                yield
        pending[0](pending[1])
        yield

    def mix(slot, next_slot):
        rq_s, rqx_s, rk_s, rkz_s, rv_s, gate_s, qt_s, kband_s, vt_s, gatet_s = slot
        key = lax.broadcasted_iota(jnp.int32, (Q_HALF, LANES), 0)
        lane_h = lax.broadcasted_iota(jnp.int32, (Q_HALF, LANES), 1)
        tri_t = key <= lane_h % Q_HALF
        cur_keep = jnp.where(tri_t, 1.0, 0.0).astype(BF16)
        prev_keep = jnp.where(tri_t, 0.0, 1.0).astype(BF16)
        first_head_lanes = lax.broadcasted_iota(jnp.int32, (1, LANES), 1) < Q_HALF
        zero_q = jnp.zeros((SWA_HEAD_DIM, 2 * LANES), BF16)
        ones_rows = jnp.ones((8, 2 * CHUNK), BF16)

        def out_proj(c_hi):
            pair = slice((c_hi - 1) * CHUNK, (c_hi + 1) * CHUNK)
            o_ref[0, pair, :] = (xres_ref[0, pair, :]
                                 + _dot(mixed_s[pair, :], wout_ref[0:RET_HEADS * RET_DV, :])
                                 + _dot_tn(mixedt_s[:, pair], wout_ref[RET_HEADS * RET_DV:D_MIX, :]))

        chunk_rows = [slice(c * CHUNK, (c + 1) * CHUNK) for c in range(N_CHUNKS)]
        chunk_band = [slice(c * CHUNK, (c + 2) * CHUNK) for c in range(N_CHUNKS)]

        def logits_for(c):
            rhs_a, rhs_b = (jnp.concatenate([jnp.concatenate([qt_s[c, half, 0], zero_q], axis=1),
                                             jnp.concatenate([zero_q, qt_s[c, half, 1]], axis=1)], axis=0)
                            for half in range(2))
            k0 = c * CHUNK
            p0_a = _dot(kband_s[k0:k0 + Q_HALF, :], rhs_a)
            mid = _dot(kband_s[k0 + Q_HALF:k0 + 3 * Q_HALF, :], jnp.concatenate([rhs_a, rhs_b], axis=1))
            c1_b = _dot(kband_s[k0 + 3 * Q_HALF:k0 + 4 * Q_HALF, :], rhs_b)
            return p0_a, mid, c1_b

        scores, incs, logits_t = {}, {}, []
        for c, rows in enumerate(chunk_rows):
            for p in range(2):
                s2 = _dot_nt(rq_s[p, rows, :], rk_s[p, c])
                scores[c, p] = jnp.concatenate([(s2[bs, :] * dec_ref[p, bs, :]).astype(BF16)
                                                for _, bs in _row_blocks(CHUNK)], axis=0)
                incs[c, p] = _dot_tn(rkz_s[p, rows, :], rv_s[rows, p * 2 * RET_DV:(p + 1) * 2 * RET_DV])
            logits_t.append(logits_for(c))
        yield

        ret_outs = {}
        for p in range(2):
            full = state_s[p] * keep_state
            for c, rows in enumerate(chunk_rows):
                full_b = full.astype(BF16)
                v2 = rv_s[rows, p * 2 * RET_DV:(p + 1) * 2 * RET_DV]
                s2 = scores[c, p]
                for j in range(2):
                    ret_outs[c, 2 * p + j] = _dot(
                        jnp.concatenate([s2[:, j * CHUNK:(j + 1) * CHUNK], rqx_s[p, j, rows, :]], axis=1),
                        jnp.concatenate([v2[:, j * RET_DV:(j + 1) * RET_DV],
                                         full_b[:, j * RET_DV:(j + 1) * RET_DV]], axis=0))
                full = full * cd_ref[p] + incs[c, p]
            state_s[p] = full
        yield

        for c in range(N_CHUNKS):
            rows, band, lt = chunk_rows[c], chunk_band[c], logits_t[c]

            for hd in range(RET_HEADS):
                o = ret_outs[c, hd]
                cols = slice(hd * RET_DV, (hd + 1) * RET_DV)
                for r0, bs in _row_blocks(CHUNK):
                    ob = o[bs, :]
                    rs = slice(c * CHUNK + r0, c * CHUNK + r0 + ROW_BLOCK)
                    mu = jnp.mean(ob, axis=-1, keepdims=True)
                    d = ob - mu
                    var = jnp.mean(d * d, axis=-1, keepdims=True)
                    y = d * lax.rsqrt(var + GN_EPS) * retw_ref[:, cols]
                    mixed_s[rs, cols] = (y * gate_s[rs, cols]).astype(BF16)
            p0_a, mid, c1_b = lt
            p_rows = [slice(i * Q_HALF, (i + 1) * Q_HALF) for i in range(4)]
            sink_terms = {}
            for half in range(2):
                for pair in range(SWA_HEADS // 2):
                    g, gp = pair // 2, pair % 2
                    cs = slice(pair * LANES, (pair + 1) * LANES)
                    cs_b = slice(4 * LANES + pair * LANES, 4 * LANES + (pair + 1) * LANES)
                    sink = jnp.where(first_head_lanes, sinks_ref[2 * pair], sinks_ref[2 * pair + 1]) * LOG2_E
                    bi = half * 4 + pair
                    bias = bias_s[bi + SWA_HEADS * first_i] if c == 0 else bias_s[bi]
                    if half == 0:
                        lg0 = jnp.where(tri_t, mid[Q_HALF:CHUNK, cs], p0_a[:, cs]) + bias[0:Q_HALF]
                        lg1 = mid[0:Q_HALF, cs] + bias[Q_HALF:CHUNK]
                    else:
                        lg0 = mid[Q_HALF:CHUNK, cs_b] + bias[0:Q_HALF]
                        lg1 = jnp.where(tri_t, c1_b[:, cs], mid[0:Q_HALF, cs_b]) + bias[Q_HALF:CHUNK]
                    m = jnp.maximum(jnp.maximum(jnp.max(lg0, axis=0, keepdims=True),
                                                jnp.max(lg1, axis=0, keepdims=True)), sink)
                    e0 = jnp.exp2(lg0 - m).astype(BF16)
                    e1 = jnp.exp2(lg1 - m).astype(BF16)
                    sink_terms[half, pair] = jnp.exp2(sink - m)
                    pq = slice(half * 2 * LANES + gp * LANES, half * 2 * LANES + (gp + 1) * LANES)
                    if half == 0:
                        pt_s[c, g, p_rows[0], pq] = e0 * prev_keep
                        pt_s[c, g, p_rows[2], pq] = e0 * cur_keep
                        pt_s[c, g, p_rows[1], pq] = e1
                    else:
                        pt_s[c, g, p_rows[2], pq] = e0
                        pt_s[c, g, p_rows[1], pq] = e1 * prev_keep
                        pt_s[c, g, p_rows[3], pq] = e1 * cur_keep
            yield

            for g in range(SWA_KV_HEADS):
                vt1 = jnp.concatenate([vt_s[g * SWA_HEAD_DIM:(g + 1) * SWA_HEAD_DIM, band], ones_rows], axis=0)
                ot = _dot(vt1, pt_s[c, g])
                den = ot[SWA_HEAD_DIM:SWA_HEAD_DIM + 1, :] + jnp.concatenate(
                    [sink_terms[half, 2 * g + gp] for half in range(2) for gp in range(2)], axis=1)
                otn = ot[0:SWA_HEAD_DIM, :] * (1.0 / den)
                for hh in range(SWA_GROUP):
                    hd = SWA_GROUP * g + hh
                    hs = slice(hd * SWA_HEAD_DIM, (hd + 1) * SWA_HEAD_DIM)
                    o_h = jnp.concatenate([otn[:, hh * Q_HALF:(hh + 1) * Q_HALF],
                                           otn[:, 2 * LANES + hh * Q_HALF:2 * LANES + (hh + 1) * Q_HALF]], axis=1)
                    mixedt_s[hs, rows] = (o_h * gatet_s[hs, rows]).astype(BF16)
            yield
            if c % 2 == 1:
                out_proj(c)
                yield

        last = slice(SEQ_TILE, SEQ_TILE + CHUNK)
        next_slot[SLOT_KBAND][0:CHUNK, :] = kband_s[last, :]
        next_slot[SLOT_VT][:, 0:CHUNK] = vt_s[:, last]

    def run(parity, do_project=True, do_mix=True, stage_weights=False):
        streams = [project(slots[parity], h_bufs[parity], h_bufs[1 - parity], stage_weights) if do_project
                   else iter(()),
                   mix(slots[1 - parity], slots[parity]) if do_mix else iter(())]
        for which in MIX_ORDER + (0,) * len(MIX_ORDER) + (1,) * len(MIX_ORDER):
            next(streams[which], None)

    is_first, is_last = step == 0, step == n_tiles
    pl.when(is_first)(functools.partial(run, 0, do_mix=False, stage_weights=True))
    pl.when(is_first)(stage_w_out)
    pl.when(is_last)(functools.partial(run, n_tiles % 2, do_project=False))
    for parity in range(2):
        pl.when((step % 2 == parity) & jnp.logical_not(is_first | is_last))(functools.partial(run, parity))


def kernel(x, norm_w, w_in, ret_norm_w, q_norm_w, k_norm_w, sinks, rel_bias, w_out):
    batch, seq, d_model = x.shape
    assert d_model == D_MODEL and seq % SEQ_TILE == 0
    assert w_in.shape == (D_MODEL, D_IN) and w_out.shape == (D_MIX, D_MODEL)
    tb = _constant_tables(seq)
    tiles_per_seq = seq // SEQ_TILE
    n_tiles = batch * tiles_per_seq

    def proj_tile(s):
        return jnp.minimum(s, n_tiles - 1)

    def mix_tile(s):
        return jnp.maximum(s - 1, 0)

    const2 = lambda s: (0, 0)
    const3 = lambda s: (0, 0, 0)
    const4 = lambda s: (0, 0, 0, 0)
    pos_spec = pl.BlockSpec((SEQ_TILE, LANES), lambda s: (proj_tile(s) % tiles_per_seq, 0))
    smem = pl.BlockSpec(memory_space=pltpu.SMEM)
    x_block = (1, SEQ_TILE, D_MODEL)
    mix_map = lambda s: (mix_tile(s) // tiles_per_seq, mix_tile(s) % tiles_per_seq, 0)

    in_specs = [
        pl.BlockSpec(x_block, lambda s: (proj_tile(s + 1) // tiles_per_seq, proj_tile(s + 1) % tiles_per_seq, 0)),
        pl.BlockSpec(x_block, mix_map),
        pl.BlockSpec(memory_space=pl.ANY),
        pl.BlockSpec((1, D_MODEL), const2),
        pl.BlockSpec((1, D_IN), const2),
        pl.BlockSpec(memory_space=pl.ANY),
        pl.BlockSpec(memory_space=pl.ANY),
        pos_spec, pos_spec,
        pl.BlockSpec((2, CHUNK, 2 * CHUNK), const3),
        pl.BlockSpec((2, 2, CHUNK, LANES), const4),
        pl.BlockSpec((2, CHUNK, LANES), const3),
        pl.BlockSpec((2, 1, 2 * RET_DV), const3),
        pl.BlockSpec((1, RET_HEADS * RET_DV), const2),
        pl.BlockSpec((1, SWA_HEAD_DIM), const2),
        pl.BlockSpec((1, SWA_HEAD_DIM), const2),
        pl.BlockSpec((2, CHUNK, LANES), const3),
        smem, smem,
    ]
    slot_bufs = [
        pltpu.VMEM((2, SEQ_TILE, LANES), BF16),
        pltpu.VMEM((2, 2, SEQ_TILE, LANES), BF16),
        pltpu.VMEM((2, N_CHUNKS, 2 * CHUNK, LANES), BF16),
        pltpu.VMEM((2, SEQ_TILE, LANES), BF16),
        pltpu.VMEM((SEQ_TILE, RET_HEADS * RET_DV), BF16),
        pltpu.VMEM((SEQ_TILE, RET_HEADS * RET_DV), F32),
        pltpu.VMEM((N_CHUNKS, 2, SWA_KV_HEADS, SWA_HEAD_DIM, 2 * LANES), BF16),
        pltpu.VMEM((SEQ_TILE + CHUNK, LANES), BF16),
        pltpu.VMEM((SWA_KV_HEADS * SWA_HEAD_DIM, SEQ_TILE + CHUNK), BF16),
        pltpu.VMEM((SWA_HEADS * SWA_HEAD_DIM, SEQ_TILE), F32),
    ]
    assert len(slot_bufs) == N_SLOT_BUFS
    h_buf = pltpu.VMEM((SEQ_TILE, D_MODEL), BF16)
    scratch = [h_buf, h_buf] + slot_bufs + slot_bufs + [
        pltpu.VMEM((N_CHUNKS, SWA_KV_HEADS, 2 * CHUNK, SWA_GROUP * CHUNK), BF16),
        pltpu.VMEM((SEQ_TILE, RET_HEADS * RET_DV), BF16),
        pltpu.VMEM((SWA_HEADS * SWA_HEAD_DIM, SEQ_TILE), BF16),
        pltpu.VMEM((2, 2 * RET_DK, 2 * RET_DV), F32),
        pltpu.VMEM((2 * SWA_HEADS, CHUNK, LANES), F32),
        pltpu.VMEM((D_MODEL, D_TOK), BF16),
        pltpu.VMEM((D_FT, D_MODEL), BF16),
        pltpu.VMEM((D_MIX, D_MODEL), BF16),
        pltpu.VMEM((2, D_MODEL, 4 * LANES), F32),
        pltpu.VMEM((D_MIX // STAGE_ROWS, STAGE_ROWS, D_MODEL), F32),
        pltpu.VMEM((SEQ_TILE // STAGE_ROWS, STAGE_ROWS, D_MODEL), F32),
        pltpu.SemaphoreType.DMA(((SEQ_TILE + D_MIX) // STAGE_ROWS,)),
        pltpu.SemaphoreType.DMA((2,)),
    ]
    return pl.pallas_call(
        functools.partial(_layer_kernel, tiles_per_seq, n_tiles),
        grid=(n_tiles + 1,),
        in_specs=in_specs,
        out_specs=pl.BlockSpec(x_block, mix_map),
        out_shape=jax.ShapeDtypeStruct(x.shape, x.dtype),
        scratch_shapes=scratch,
        compiler_params=pltpu.CompilerParams(
            dimension_semantics=("arbitrary",),
            vmem_limit_bytes=VMEM_LIMIT_BYTES),
    )(x, x, x, norm_w.reshape(1, D_MODEL), tb["col_scale"], w_in, w_out,
      tb["cos"], tb["sin_s"], tb["dec"], tb["xi"], tb["zeta"], tb["cd"],
      ret_norm_w.reshape(1, -1), q_norm_w.reshape(1, -1), k_norm_w.reshape(1, -1), tb["bucket"],
      rel_bias.astype(F32), sinks.astype(F32))
```

```python
import functools
import math

import numpy as np
import jax
import jax.numpy as jnp
from jax import lax
from jax.experimental import pallas as pl
from jax.experimental.pallas import tpu as pltpu

D_MODEL = 1024
RET_HEADS = 4
RET_DK = 64
RET_DV = 128
CHUNK = 128
Q_HALF = CHUNK // 2
RET_ROPE_BASE = 10000.0
SWA_HEADS = 8
SWA_KV_HEADS = 2
SWA_HEAD_DIM = 64
SWA_GROUP = SWA_HEADS // SWA_KV_HEADS
NUM_BUCKETS = 32
MAX_DISTANCE = 128
NORM_EPS = 1e-6
GN_EPS = 1e-5
NEG_INF = -1e30
LOG2_E = math.log2(math.e)

OFF_RQ, OFF_RK, OFF_RV, OFF_RG = 0, 256, 512, 1024
D_TOK = 1536
FT_Q, FT_K, FT_G = 0, 512, 768
D_FT = 1280
D_IN = D_TOK + D_FT
D_MIX = 1024

LANES = 128
SUBLANES = 8
SEQ_TILE = 512
N_CHUNKS = SEQ_TILE // CHUNK
ROW_BLOCK = 32
VMEM_LIMIT_BYTES = 56 * 1024 * 1024
MIX_ORDER = (1, 1, 0, 0, 1, 1, 0, 0, 1, 1, 0, 0, 1, 0, 1, 1, 0, 0, 1, 1, 0, 0, 1, 0, 0)
NORM_AFTER_GROUP = 2
N_SLOT_BUFS = 10
STAGE_ROWS = 128
SLAB_GROUPS = ((OFF_RQ, 4 * LANES), (D_TOK + FT_Q, 4 * LANES), (OFF_RG, 4 * LANES), (D_TOK + FT_G, 4 * LANES),
               (D_TOK + FT_K, 2 * LANES), (OFF_RV, 4 * LANES))
SLOT_KBAND, SLOT_VT = 7, 8

BF16 = jnp.bfloat16
F32 = jnp.float32


def _t5_bucket_np(n):
    max_exact = NUM_BUCKETS // 2
    nf = np.maximum(n, 1).astype(np.float64)
    large = max_exact + (np.log(nf / max_exact) / math.log(MAX_DISTANCE / max_exact)
                         * (NUM_BUCKETS - max_exact)).astype(np.int32)
    large = np.minimum(large, NUM_BUCKETS - 1)
    return np.where(n < max_exact, n, large).astype(np.int32)


def _constant_tables(seq):
    half = RET_DK // 2
    inv_freq = RET_ROPE_BASE ** (-np.arange(half, dtype=np.float64) / half)
    ang = np.arange(seq, dtype=np.float64)[:, None] * inv_freq[None, :]
    cos, sin = np.cos(ang), np.sin(ang)
    cos_t = np.tile(np.concatenate([cos, cos], axis=1), (1, 2))
    sin_s = np.tile(np.concatenate([-sin, sin], axis=1), (1, 2))

    gamma = 1.0 - np.exp2(-5.0 - np.arange(RET_HEADS, dtype=np.float64))
    log_g = np.log(gamma)
    i = np.arange(CHUNK, dtype=np.float64)
    diff = i[:, None] - i[None, :]
    k_scale = RET_DK ** -0.5
    decay = np.where(diff >= 0, np.exp(log_g[:, None, None] * np.maximum(diff, 0.0)), 0.0) * k_scale
    dec = np.stack([np.concatenate([decay[2 * p], decay[2 * p + 1]], axis=1) for p in range(2)])
    xi = np.exp(log_g[:, None] * (i + 1.0))
    zeta = np.exp(log_g[:, None] * (CHUNK - 1.0 - i)) * k_scale

    def per_lane(t):
        return np.stack([np.concatenate([np.repeat(t[2 * p][:, None], RET_DK, 1),
                                         np.repeat(t[2 * p + 1][:, None], RET_DK, 1)], axis=1)
                         for p in range(2)])

    lo = (np.arange(LANES) < RET_DK)[None, None, :]
    xi_l = per_lane(xi)
    xi_m = np.stack([np.where(lo, xi_l, 0.0), np.where(lo, 0.0, xi_l)], axis=1)

    chunk_decay = np.exp(log_g * CHUNK)
    cd = np.stack([np.concatenate([np.full((1, RET_DV), chunk_decay[2 * p]),
                                   np.full((1, RET_DV), chunk_decay[2 * p + 1])], axis=1)
                   for p in range(2)])

    j = np.arange(CHUNK)[None, :, None]
    q = (np.arange(2)[:, None, None] * Q_HALF + np.arange(LANES)[None, None, :] % Q_HALF)
    dist = np.where(j <= q, q - j, q + CHUNK - j)
    bucket = _t5_bucket_np(dist)

    f = lambda a: jnp.asarray(a, dtype=F32)
    col_scale = np.ones((1, D_IN))
    col_scale[:, OFF_RG:OFF_RG + RET_HEADS * RET_DV] = 0.5
    col_scale[:, D_TOK + FT_G:D_TOK + FT_G + SWA_HEADS * SWA_HEAD_DIM] = 0.5

    return dict(cos=f(cos_t), sin_s=f(sin_s), col_scale=f(col_scale), dec=f(dec), xi=f(xi_m),
                zeta=f(per_lane(zeta)), cd=f(cd), bucket=jnp.asarray(bucket, dtype=jnp.int32))


def _dot(a, b):
    return jnp.dot(a, b, preferred_element_type=F32)


def _dot_nt(a, b):
    return lax.dot_general(a, b, (((1,), (1,)), ((), ())), preferred_element_type=F32)


def _dot_tn(a, b):
    return lax.dot_general(a, b, (((0,), (0,)), ((), ())), preferred_element_type=F32)


def _silu_of_half(hg):
    return hg + hg * jnp.tanh(hg)


def _row_blocks(n_rows):
    return [(r0, slice(r0, r0 + ROW_BLOCK)) for r0 in range(0, n_rows, ROW_BLOCK)]


def _layer_kernel(tiles_per_seq, n_tiles,
                  xnext_ref, xres_ref, x_hbm, nwrow_ref, colscale_ref, win_hbm, wout_hbm, cos_ref, sins_ref, dec_ref,
                  xi_ref, zeta_ref, cd_ref, retw_ref, qnw_ref, knw_ref, bkt_ref, relb_ref,
                  sinks_ref, o_ref, h0_s, h1_s, *scratch):
    h_bufs = (h0_s, h1_s)
    slots = (scratch[0:N_SLOT_BUFS], scratch[N_SLOT_BUFS:2 * N_SLOT_BUFS])
    (pt_s, mixed_s, mixedt_s, state_s, bias_s,
     win_ref, wft_ref, wout_ref, slab_s, stage_out, stage_x, stage_sem, slab_sem) = scratch[2 * N_SLOT_BUFS:]
    step = pl.program_id(0)
    mix_tile = jnp.maximum(step - 1, 0)
    first = mix_tile % tiles_per_seq == 0
    first_i = first.astype(jnp.int32)
    keep_state = jnp.where(first, 0.0, 1.0).astype(F32)

    def rms_norm_rows(xb):
        ms = jnp.mean(xb * xb, axis=-1, keepdims=True)
        return (xb * lax.rsqrt(ms + NORM_EPS) * nwrow_ref[...]).astype(BF16)

    def row_copies(hbm, stage, sem0, n_rows):
        return [pltpu.make_async_copy(hbm.at[pl.ds(i * STAGE_ROWS, STAGE_ROWS), :], stage.at[i],
                                      stage_sem.at[sem0 + i]) for i in range(n_rows // STAGE_ROWS)]

    def slab_copy(i):
        col0, width = SLAB_GROUPS[i]
        return pltpu.make_async_copy(win_hbm.at[:, pl.ds(col0, width)], slab_s.at[i % 2, :, pl.ds(0, width)],
                                     slab_sem.at[i % 2])

    def stage_group(i):
        col0, width = SLAB_GROUPS[i]
        slab_copy(i).wait()
        for r0 in range(0, D_MODEL, STAGE_ROWS):
            rows = slice(r0, r0 + STAGE_ROWS)
            for c0 in range(0, width, LANES):
                blk = slab_s[i % 2, rows, c0:c0 + LANES] * colscale_ref[:, col0 + c0:col0 + c0 + LANES]
                if col0 < D_TOK:
                    win_ref[rows, col0 + c0:col0 + c0 + LANES] = blk.astype(BF16)
                else:
                    wft_ref[col0 - D_TOK + c0:col0 - D_TOK + c0 + LANES, rows] = blk.T.astype(BF16)
        if i + 2 < len(SLAB_GROUPS):
            slab_copy(i + 2).start()

    def stage_w_out():
        for i, cp in enumerate(row_copies(wout_hbm, stage_out, SEQ_TILE // STAGE_ROWS, D_MIX)):
            cp.wait()
            wout_ref[i * STAGE_ROWS:(i + 1) * STAGE_ROWS, :] = stage_out[i].astype(BF16)

    @pl.when(step == 0)
    def _init():
        def build_tables():
            key = lax.broadcasted_iota(jnp.int32, (CHUNK, LANES), 0)
            lane_c = lax.broadcasted_iota(jnp.int32, (CHUNK, LANES), 1)
            for half in range(2):
                bk = bkt_ref[half]
                is_cur = key <= half * Q_HALF + lane_c % Q_HALF
                for pair in range(SWA_HEADS // 2):
                    acc = jnp.zeros((CHUNK, LANES), F32)
                    for u in range(NUM_BUCKETS):
                        val = jnp.where(lane_c < Q_HALF, relb_ref[u, 2 * pair], relb_ref[u, 2 * pair + 1]) * LOG2_E
                        acc = jnp.where(bk == u, val, acc)
                    bias_s[half * 4 + pair] = acc
                    bias_s[SWA_HEADS + half * 4 + pair] = jnp.where(is_cur, acc, NEG_INF)
            zeros_blk = jnp.zeros((N_CHUNKS, SWA_KV_HEADS, Q_HALF, 2 * LANES), BF16)
            pt_s[:, :, 3 * Q_HALF:4 * Q_HALF, 0:2 * LANES] = zeros_blk
            pt_s[:, :, 0:Q_HALF, 2 * LANES:4 * LANES] = zeros_blk
            state_s[...] = jnp.zeros_like(state_s)
            slots[0][SLOT_KBAND][0:CHUNK, :] = jnp.zeros((CHUNK, LANES), BF16)
            slots[0][SLOT_VT][:, 0:CHUNK] = jnp.zeros((SWA_KV_HEADS * SWA_HEAD_DIM, CHUNK), BF16)

        x_copies = row_copies(x_hbm.at[0], stage_x, 0, SEQ_TILE)
        for cp in x_copies + row_copies(wout_hbm, stage_out, len(x_copies), D_MIX) + [slab_copy(0), slab_copy(1)]:
            cp.start()
        build_tables()
        for i, cp in enumerate(x_copies):
            cp.wait()
            for r0, rs in _row_blocks(STAGE_ROWS):
                h0_s[i * STAGE_ROWS + r0:i * STAGE_ROWS + r0 + ROW_BLOCK, :] = rms_norm_rows(stage_x[i, rs, :])

    lane = lax.broadcasted_iota(jnp.int32, (ROW_BLOCK, LANES), 1)
    lo = lane < RET_DK
    first_half = lane % RET_DK < RET_DK // 2

    def project(slot, h_s, h_next_s, stage_weights=False):
        rq_s, rqx_s, rk_s, rkz_s, rv_s, gate_s, qt_s, kband_s, vt_s, gatet_s = slot

        def proj(off):
            return _dot(h_s[...], win_ref[:, off:off + 4 * LANES])

        def proj_t(row0, n_rows):
            return _dot_nt(wft_ref[row0:row0 + n_rows, :], h_s[...])

        def rot(v, rs):
            partner = jnp.where(first_half, pltpu.roll(v, LANES - 32, 1), pltpu.roll(v, 32, 1))
            return v * cos_ref[rs, :] + partner * sins_ref[rs, :]

        def head_rms(xt):
            ms = jnp.mean(xt * xt, axis=0, keepdims=True)
            return xt * lax.rsqrt(ms + NORM_EPS)

        tok_blocks = [slice(c0, c0 + LANES) for c0 in range(0, SEQ_TILE, LANES)]

        def post_rqk(r):
            for r0, rs in _row_blocks(SEQ_TILE):
                c, cr = r0 // CHUNK, r0 % CHUNK
                cs = slice(cr, cr + ROW_BLOCK)
                for p in range(2):
                    q = rot(r[rs, p * LANES:(p + 1) * LANES], rs)
                    rq_s[p, rs, :] = q.astype(BF16)
                    rqx_s[p, 0, rs, :] = (q * xi_ref[p, 0, cs, :]).astype(BF16)
                    rqx_s[p, 1, rs, :] = (q * xi_ref[p, 1, cs, :]).astype(BF16)
                    k = rot(r[rs, (2 + p) * LANES:(3 + p) * LANES], rs)
                    rk_s[p, c, cr:cr + ROW_BLOCK, :] = jnp.where(lo, k, 0.0).astype(BF16)
                    rk_s[p, c, CHUNK + cr:CHUNK + cr + ROW_BLOCK, :] = jnp.where(lo, 0.0, k).astype(BF16)
                    rkz_s[p, rs, :] = (k * zeta_ref[p, cs, :]).astype(BF16)

        def post_rv(r):
            for r0, rs in _row_blocks(SEQ_TILE):
                rv_s[rs, :] = r[rs, :].astype(BF16)

        def post_rg(r):
            for r0, rs in _row_blocks(SEQ_TILE):
                gate_s[rs, :] = _silu_of_half(r[rs, :])

        def post_qt(rt):
            for pair in range(SWA_HEADS // 2):
                g, gp = pair // 2, pair % 2
                for c, ts in enumerate(tok_blocks):
                    xe, xo = (head_rms(rt[hd * SWA_HEAD_DIM:(hd + 1) * SWA_HEAD_DIM, ts])
                              for hd in (2 * pair, 2 * pair + 1))
                    for half in range(2):
                        hq = slice(half * Q_HALF, (half + 1) * Q_HALF)
                        qt_s[c, half, g, :, gp * LANES:(gp + 1) * LANES] = jnp.concatenate(
                            [xe[:, hq], xo[:, hq]], axis=1).astype(BF16)

        def post_kvt(rt):
            qk_head = qnw_ref[...] * knw_ref[...] * (SWA_HEAD_DIM ** -0.5 * LOG2_E)
            qk_w = jnp.concatenate([qk_head] * SWA_KV_HEADS, axis=1)
            for ts in tok_blocks:
                knt = jnp.concatenate(
                    [head_rms(rt[g * SWA_HEAD_DIM:(g + 1) * SWA_HEAD_DIM, ts]) for g in range(SWA_KV_HEADS)],
                    axis=0)
                kband_s[CHUNK + ts.start:CHUNK + ts.stop, :] = (knt.T * qk_w).astype(BF16)
                vt_s[:, CHUNK + ts.start:CHUNK + ts.stop] = rt[LANES:2 * LANES, ts].astype(BF16)

        def post_gt(rt):
            for r0 in range(0, SWA_HEADS * SWA_HEAD_DIM, ROW_BLOCK):
                for ts in tok_blocks:
                    gatet_s[r0:r0 + ROW_BLOCK, ts] = _silu_of_half(rt[r0:r0 + ROW_BLOCK, ts])

        groups = [
            (functools.partial(proj, OFF_RQ), post_rqk),
            (functools.partial(proj_t, FT_Q, SWA_HEADS * SWA_HEAD_DIM), post_qt),
            (functools.partial(proj, OFF_RG), post_rg),
            (functools.partial(proj_t, FT_G, SWA_HEADS * SWA_HEAD_DIM), post_gt),
            (functools.partial(proj_t, FT_K, 2 * LANES), post_kvt),
            (functools.partial(proj, OFF_RV), post_rv),
        ]
        pending = None
        for i, (matmul, post) in enumerate(groups):
            if stage_weights:
                stage_group(i)
            result = matmul()
            yield
            if pending is not None:
                pending[0](pending[1])
                yield
            pending = (post, result)
            if i == NORM_AFTER_GROUP:
                for r0, rs in _row_blocks(SEQ_TILE):
                    h_next_s[rs, :] = rms_norm_rows(xnext_ref[0, rs, :])
                yield
        pending[0](pending[1])
        yield

    def mix(slot, next_slot):
        rq_s, rqx_s, rk_s, rkz_s, rv_s, gate_s, qt_s, kband_s, vt_s, gatet_s = slot
        key = lax.broadcasted_iota(jnp.int32, (Q_HALF, LANES), 0)
        lane_h = lax.broadcasted_iota(jnp.int32, (Q_HALF, LANES), 1)
        tri_t = key <= lane_h % Q_HALF
        cur_keep = jnp.where(tri_t, 1.0, 0.0).astype(BF16)
        prev_keep = jnp.where(tri_t, 0.0, 1.0).astype(BF16)
        first_head_lanes = lax.broadcasted_iota(jnp.int32, (1, LANES), 1) < Q_HALF
        zero_q = jnp.zeros((SWA_HEAD_DIM, 2 * LANES), BF16)
        ones_rows = jnp.ones((SUBLANES, 2 * CHUNK), BF16)

        def out_proj(c_hi):
            pair = slice((c_hi - 1) * CHUNK, (c_hi + 1) * CHUNK)
            o_ref[0, pair, :] = (xres_ref[0, pair, :]
                                 + _dot(mixed_s[pair, :], wout_ref[0:RET_HEADS * RET_DV, :])
                                 + _dot_tn(mixedt_s[:, pair], wout_ref[RET_HEADS * RET_DV:D_MIX, :]))

        chunk_rows = [slice(c * CHUNK, (c + 1) * CHUNK) for c in range(N_CHUNKS)]
        chunk_band = [slice(c * CHUNK, (c + 2) * CHUNK) for c in range(N_CHUNKS)]

        def logits_for(c):
            rhs_a, rhs_b = (jnp.concatenate([jnp.concatenate([qt_s[c, half, 0], zero_q], axis=1),
                                             jnp.concatenate([zero_q, qt_s[c, half, 1]], axis=1)], axis=0)
                            for half in range(2))
            k0 = c * CHUNK
            p0_a = _dot(kband_s[k0:k0 + Q_HALF, :], rhs_a)
            mid = _dot(kband_s[k0 + Q_HALF:k0 + 3 * Q_HALF, :], jnp.concatenate([rhs_a, rhs_b], axis=1))
            c1_b = _dot(kband_s[k0 + 3 * Q_HALF:k0 + 4 * Q_HALF, :], rhs_b)
            return p0_a, mid, c1_b

        scores, incs, logits_t = {}, {}, []
        for c, rows in enumerate(chunk_rows):
            for p in range(2):
                s2 = _dot_nt(rq_s[p, rows, :], rk_s[p, c])
                scores[c, p] = jnp.concatenate([(s2[bs, :] * dec_ref[p, bs, :]).astype(BF16)
                                                for _, bs in _row_blocks(CHUNK)], axis=0)
                incs[c, p] = _dot_tn(rkz_s[p, rows, :], rv_s[rows, p * 2 * RET_DV:(p + 1) * 2 * RET_DV])
            logits_t.append(logits_for(c))
        yield

        ret_outs = {}
        for p in range(2):
            full = state_s[p] * keep_state
            for c, rows in enumerate(chunk_rows):
                full_b = full.astype(BF16)
                v2 = rv_s[rows, p * 2 * RET_DV:(p + 1) * 2 * RET_DV]
                s2 = scores[c, p]
                for j in range(2):
                    ret_outs[c, 2 * p + j] = _dot(
                        jnp.concatenate([s2[:, j * CHUNK:(j + 1) * CHUNK], rqx_s[p, j, rows, :]], axis=1),
                        jnp.concatenate([v2[:, j * RET_DV:(j + 1) * RET_DV],
                                         full_b[:, j * RET_DV:(j + 1) * RET_DV]], axis=0))
                full = full * cd_ref[p] + incs[c, p]
            state_s[p] = full
        yield

        for c in range(N_CHUNKS):
            rows, band, lt = chunk_rows[c], chunk_band[c], logits_t[c]

            for hd in range(RET_HEADS):
                o = ret_outs[c, hd]
                cols = slice(hd * RET_DV, (hd + 1) * RET_DV)
                for r0, bs in _row_blocks(CHUNK):
                    ob = o[bs, :]
                    rs = slice(c * CHUNK + r0, c * CHUNK + r0 + ROW_BLOCK)
                    mu = jnp.mean(ob, axis=-1, keepdims=True)
                    d = ob - mu
                    var = jnp.mean(d * d, axis=-1, keepdims=True)
                    y = d * lax.rsqrt(var + GN_EPS) * retw_ref[:, cols]
                    mixed_s[rs, cols] = (y * gate_s[rs, cols]).astype(BF16)
            p0_a, mid, c1_b = lt
            p_rows = [slice(i * Q_HALF, (i + 1) * Q_HALF) for i in range(4)]
            sink_terms = {}
            for half in range(2):
                for pair in range(SWA_HEADS // 2):
                    g, gp = pair // 2, pair % 2
                    cs = slice(pair * LANES, (pair + 1) * LANES)
                    cs_b = slice(4 * LANES + pair * LANES, 4 * LANES + (pair + 1) * LANES)
                    sink = jnp.where(first_head_lanes, sinks_ref[2 * pair], sinks_ref[2 * pair + 1]) * LOG2_E
                    bi = half * 4 + pair
                    bias = bias_s[bi + SWA_HEADS * first_i] if c == 0 else bias_s[bi]
                    if half == 0:
                        lg0 = jnp.where(tri_t, mid[Q_HALF:CHUNK, cs], p0_a[:, cs]) + bias[0:Q_HALF]
                        lg1 = mid[0:Q_HALF, cs] + bias[Q_HALF:CHUNK]
                    else:
                        lg0 = mid[Q_HALF:CHUNK, cs_b] + bias[0:Q_HALF]
                        lg1 = jnp.where(tri_t, c1_b[:, cs], mid[0:Q_HALF, cs_b]) + bias[Q_HALF:CHUNK]
                    m = jnp.maximum(jnp.maximum(jnp.max(lg0, axis=0, keepdims=True),
                                                jnp.max(lg1, axis=0, keepdims=True)), sink)
                    e0 = jnp.exp2(lg0 - m).astype(BF16)
                    e1 = jnp.exp2(lg1 - m).astype(BF16)
                    sink_terms[half, pair] = jnp.exp2(sink - m)
                    pq = slice(half * 2 * LANES + gp * LANES, half * 2 * LANES + (gp + 1) * LANES)
                    if half == 0:
                        pt_s[c, g, p_rows[0], pq] = e0 * prev_keep
                        pt_s[c, g, p_rows[2], pq] = e0 * cur_keep
                        pt_s[c, g, p_rows[1], pq] = e1
                    else:
                        pt_s[c, g, p_rows[2], pq] = e0
                        pt_s[c, g, p_rows[1], pq] = e1 * prev_keep
                        pt_s[c, g, p_rows[3], pq] = e1 * cur_keep
            yield

            for g in range(SWA_KV_HEADS):
                vt1 = jnp.concatenate([vt_s[g * SWA_HEAD_DIM:(g + 1) * SWA_HEAD_DIM, band], ones_rows], axis=0)
                ot = _dot(vt1, pt_s[c, g])
                assert ot.shape == (SWA_HEAD_DIM + SUBLANES, 4 * LANES)
                den = ot[SWA_HEAD_DIM:SWA_HEAD_DIM + 1, :] + jnp.concatenate(
                    [sink_terms[half, 2 * g + gp] for half in range(2) for gp in range(2)], axis=1)
                otn = ot[0:SWA_HEAD_DIM, :] * (1.0 / den)
                for hh in range(SWA_GROUP):
                    hd = SWA_GROUP * g + hh
                    hs = slice(hd * SWA_HEAD_DIM, (hd + 1) * SWA_HEAD_DIM)
                    o_h = jnp.concatenate([otn[:, hh * Q_HALF:(hh + 1) * Q_HALF],
                                           otn[:, 2 * LANES + hh * Q_HALF:2 * LANES + (hh + 1) * Q_HALF]], axis=1)
                    mixedt_s[hs, rows] = (o_h * gatet_s[hs, rows]).astype(BF16)
            yield
            if c % 2 == 1:
                out_proj(c)
                yield

        last = slice(SEQ_TILE, SEQ_TILE + CHUNK)
        next_slot[SLOT_KBAND][0:CHUNK, :] = kband_s[last, :]
        next_slot[SLOT_VT][:, 0:CHUNK] = vt_s[:, last]

    def run(parity, do_project=True, do_mix=True, stage_weights=False):
        streams = [project(slots[parity], h_bufs[parity], h_bufs[1 - parity], stage_weights) if do_project
                   else iter(()),
                   mix(slots[1 - parity], slots[parity]) if do_mix else iter(())]
        for which in MIX_ORDER + (0,) * len(MIX_ORDER) + (1,) * len(MIX_ORDER):
            next(streams[which], None)

    is_first, is_last = step == 0, step == n_tiles
    pl.when(is_first)(functools.partial(run, 0, do_mix=False, stage_weights=True))
    pl.when(is_first)(stage_w_out)
    pl.when(is_last)(functools.partial(run, n_tiles % 2, do_project=False))
    for parity in range(2):
        pl.when((step % 2 == parity) & jnp.logical_not(is_first | is_last))(functools.partial(run, parity))


def kernel(x, norm_w, w_in, ret_norm_w, q_norm_w, k_norm_w, sinks, rel_bias, w_out):
    batch, seq, d_model = x.shape
    assert d_model == D_MODEL and seq % SEQ_TILE == 0
    assert w_in.shape == (D_MODEL, D_IN) and w_out.shape == (D_MIX, D_MODEL)
    tb = _constant_tables(seq)
    tiles_per_seq = seq // SEQ_TILE
    n_tiles = batch * tiles_per_seq

    def proj_tile(s):
        return jnp.minimum(s, n_tiles - 1)

    def mix_tile(s):
        return jnp.maximum(s - 1, 0)

    const2 = lambda s: (0, 0)
    const3 = lambda s: (0, 0, 0)
    const4 = lambda s: (0, 0, 0, 0)
    pos_spec = pl.BlockSpec((SEQ_TILE, LANES), lambda s: (proj_tile(s) % tiles_per_seq, 0))
    smem = pl.BlockSpec(memory_space=pltpu.SMEM)
    x_block = (1, SEQ_TILE, D_MODEL)
    mix_map = lambda s: (mix_tile(s) // tiles_per_seq, mix_tile(s) % tiles_per_seq, 0)

    in_specs = [
        pl.BlockSpec(x_block, lambda s: (proj_tile(s + 1) // tiles_per_seq, proj_tile(s + 1) % tiles_per_seq, 0)),
        pl.BlockSpec(x_block, mix_map),
        pl.BlockSpec(memory_space=pl.ANY),
        pl.BlockSpec((1, D_MODEL), const2),
        pl.BlockSpec((1, D_IN), const2),
        pl.BlockSpec(memory_space=pl.ANY),
        pl.BlockSpec(memory_space=pl.ANY),
        pos_spec, pos_spec,
        pl.BlockSpec((2, CHUNK, 2 * CHUNK), const3),
        pl.BlockSpec((2, 2, CHUNK, LANES), const4),
        pl.BlockSpec((2, CHUNK, LANES), const3),
        pl.BlockSpec((2, 1, 2 * RET_DV), const3),
        pl.BlockSpec((1, RET_HEADS * RET_DV), const2),
        pl.BlockSpec((1, SWA_HEAD_DIM), const2),
        pl.BlockSpec((1, SWA_HEAD_DIM), const2),
        pl.BlockSpec((2, CHUNK, LANES), const3),
        smem, smem,
    ]
    slot_bufs = [
        pltpu.VMEM((2, SEQ_TILE, LANES), BF16),
        pltpu.VMEM((2, 2, SEQ_TILE, LANES), BF16),
        pltpu.VMEM((2, N_CHUNKS, 2 * CHUNK, LANES), BF16),
        pltpu.VMEM((2, SEQ_TILE, LANES), BF16),
        pltpu.VMEM((SEQ_TILE, RET_HEADS * RET_DV), BF16),
        pltpu.VMEM((SEQ_TILE, RET_HEADS * RET_DV), F32),
        pltpu.VMEM((N_CHUNKS, 2, SWA_KV_HEADS, SWA_HEAD_DIM, 2 * LANES), BF16),
        pltpu.VMEM((SEQ_TILE + CHUNK, LANES), BF16),
        pltpu.VMEM((SWA_KV_HEADS * SWA_HEAD_DIM, SEQ_TILE + CHUNK), BF16),
        pltpu.VMEM((SWA_HEADS * SWA_HEAD_DIM, SEQ_TILE), F32),
    ]
    assert len(slot_bufs) == N_SLOT_BUFS
    h_buf = pltpu.VMEM((SEQ_TILE, D_MODEL), BF16)
    scratch = [h_buf, h_buf] + slot_bufs + slot_bufs + [
        pltpu.VMEM((N_CHUNKS, SWA_KV_HEADS, 2 * CHUNK, SWA_GROUP * CHUNK), BF16),
        pltpu.VMEM((SEQ_TILE, RET_HEADS * RET_DV), BF16),
        pltpu.VMEM((SWA_HEADS * SWA_HEAD_DIM, SEQ_TILE), BF16),
        pltpu.VMEM((2, 2 * RET_DK, 2 * RET_DV), F32),
        pltpu.VMEM((2 * SWA_HEADS, CHUNK, LANES), F32),
        pltpu.VMEM((D_MODEL, D_TOK), BF16),
        pltpu.VMEM((D_FT, D_MODEL), BF16),
        pltpu.VMEM((D_MIX, D_MODEL), BF16),
        pltpu.VMEM((2, D_MODEL, 4 * LANES), F32),
        pltpu.VMEM((D_MIX // STAGE_ROWS, STAGE_ROWS, D_MODEL), F32),
        pltpu.VMEM((SEQ_TILE // STAGE_ROWS, STAGE_ROWS, D_MODEL), F32),
        pltpu.SemaphoreType.DMA(((SEQ_TILE + D_MIX) // STAGE_ROWS,)),
        pltpu.SemaphoreType.DMA((2,)),
    ]
    return pl.pallas_call(
        functools.partial(_layer_kernel, tiles_per_seq, n_tiles),
        grid=(n_tiles + 1,),
        in_specs=in_specs,
        out_specs=pl.BlockSpec(x_block, mix_map),
        out_shape=jax.ShapeDtypeStruct(x.shape, x.dtype),
        scratch_shapes=scratch,
        compiler_params=pltpu.CompilerParams(
            dimension_semantics=("arbitrary",),
            vmem_limit_bytes=VMEM_LIMIT_BYTES),
    )(x, x, x, norm_w.reshape(1, D_MODEL), tb["col_scale"], w_in, w_out,
      tb["cos"], tb["sin_s"], tb["dec"], tb["xi"], tb["zeta"], tb["cd"],
      ret_norm_w.reshape(1, -1), q_norm_w.reshape(1, -1), k_norm_w.reshape(1, -1), tb["bucket"],
      rel_bias.astype(F32), sinks.astype(F32))
```

```python
import functools
import math

import numpy as np
import jax
import jax.numpy as jnp
from jax import lax
from jax.experimental import pallas as pl
from jax.experimental.pallas import tpu as pltpu

D_MODEL = 1024
RET_HEADS = 4
RET_DK = 64
RET_DV = 128
CHUNK = 128
Q_HALF = CHUNK // 2
RET_ROPE_BASE = 10000.0
SWA_HEADS = 8
SWA_KV_HEADS = 2
SWA_HEAD_DIM = 64
SWA_GROUP = SWA_HEADS // SWA_KV_HEADS
NUM_BUCKETS = 32
MAX_DISTANCE = 128
NORM_EPS = 1e-6
GN_EPS = 1e-5
NEG_INF = -1e30
LOG2_E = math.log2(math.e)

OFF_RQ, OFF_RK, OFF_RV, OFF_RG = 0, 256, 512, 1024
D_TOK = 1536
FT_Q, FT_K, FT_G = 0, 512, 768
D_FT = 1280
D_IN = D_TOK + D_FT
D_MIX = 1024

LANES = 128
SUBLANES = 8
SEQ_TILE = 512
N_CHUNKS = SEQ_TILE // CHUNK
ROW_BLOCK = 32
VMEM_LIMIT_BYTES = 56 * 1024 * 1024
MIX_ORDER = (1, 1, 0, 0, 1, 1, 0, 0, 1, 1, 0, 0, 1, 0, 1, 1, 0, 0, 1, 1, 0, 0, 1, 0, 0)
NORM_AFTER_GROUP = 2
N_SLOT_BUFS = 10
STAGE_ROWS = 128
SLAB_GROUPS = ((OFF_RQ, 4 * LANES), (D_TOK + FT_Q, 4 * LANES), (OFF_RG, 4 * LANES), (D_TOK + FT_G, 4 * LANES),
               (D_TOK + FT_K, 2 * LANES), (OFF_RV, 4 * LANES))
SLOT_KBAND, SLOT_VT = 7, 8

BF16 = jnp.bfloat16
F32 = jnp.float32


def _t5_bucket_np(n):
    max_exact = NUM_BUCKETS // 2
    nf = np.maximum(n, 1).astype(np.float64)
    large = max_exact + (np.log(nf / max_exact) / math.log(MAX_DISTANCE / max_exact)
                         * (NUM_BUCKETS - max_exact)).astype(np.int32)
    large = np.minimum(large, NUM_BUCKETS - 1)
    return np.where(n < max_exact, n, large).astype(np.int32)


def _constant_tables(seq):
    half = RET_DK // 2
    inv_freq = RET_ROPE_BASE ** (-np.arange(half, dtype=np.float64) / half)
    ang = np.arange(seq, dtype=np.float64)[:, None] * inv_freq[None, :]
    cos, sin = np.cos(ang), np.sin(ang)
    cos_t = np.tile(np.concatenate([cos, cos], axis=1), (1, 2))
    sin_s = np.tile(np.concatenate([-sin, sin], axis=1), (1, 2))

    gamma = 1.0 - np.exp2(-5.0 - np.arange(RET_HEADS, dtype=np.float64))
    log_g = np.log(gamma)
    i = np.arange(CHUNK, dtype=np.float64)
    diff = i[:, None] - i[None, :]
    k_scale = RET_DK ** -0.5
    decay = np.where(diff >= 0, np.exp(log_g[:, None, None] * np.maximum(diff, 0.0)), 0.0) * k_scale
    dec = np.stack([np.concatenate([decay[2 * p], decay[2 * p + 1]], axis=1) for p in range(2)])
    xi = np.exp(log_g[:, None] * (i + 1.0))
    zeta = np.exp(log_g[:, None] * (CHUNK - 1.0 - i)) * k_scale

    def per_lane(t):
        return np.stack([np.concatenate([np.repeat(t[2 * p][:, None], RET_DK, 1),
                                         np.repeat(t[2 * p + 1][:, None], RET_DK, 1)], axis=1)
                         for p in range(2)])

    lo = (np.arange(LANES) < RET_DK)[None, None, :]
    xi_l = per_lane(xi)
    xi_m = np.stack([np.where(lo, xi_l, 0.0), np.where(lo, 0.0, xi_l)], axis=1)

    chunk_decay = np.exp(log_g * CHUNK)
    cd = np.stack([np.concatenate([np.full((1, RET_DV), chunk_decay[2 * p]),
                                   np.full((1, RET_DV), chunk_decay[2 * p + 1])], axis=1)
                   for p in range(2)])

    j = np.arange(CHUNK)[None, :, None]
    q = (np.arange(2)[:, None, None] * Q_HALF + np.arange(LANES)[None, None, :] % Q_HALF)
    dist = np.where(j <= q, q - j, q + CHUNK - j)
    bucket = _t5_bucket_np(dist)

    f = lambda a: jnp.asarray(a, dtype=F32)
    col_scale = np.ones((1, D_IN))
    col_scale[:, OFF_RG:OFF_RG + RET_HEADS * RET_DV] = 0.5
    col_scale[:, D_TOK + FT_G:D_TOK + FT_G + SWA_HEADS * SWA_HEAD_DIM] = 0.5

    return dict(cos=f(cos_t), sin_s=f(sin_s), col_scale=f(col_scale), dec=f(dec), xi=f(xi_m),
                zeta=f(per_lane(zeta)), cd=f(cd), bucket=jnp.asarray(bucket, dtype=jnp.int32))


def _dot(a, b):
    return jnp.dot(a, b, preferred_element_type=F32)


def _dot_nt(a, b):
    return lax.dot_general(a, b, (((1,), (1,)), ((), ())), preferred_element_type=F32)


def _dot_tn(a, b):
    return lax.dot_general(a, b, (((0,), (0,)), ((), ())), preferred_element_type=F32)


def _silu_of_half(hg):
    return hg + hg * jnp.tanh(hg)


def _row_blocks(n_rows):
    return [(r0, slice(r0, r0 + ROW_BLOCK)) for r0 in range(0, n_rows, ROW_BLOCK)]


def _layer_kernel(tiles_per_seq, n_tiles,
                  xnext_ref, xres_ref, x_hbm, nwrow_ref, colscale_ref, win_hbm, wout_hbm, cos_ref, sins_ref, dec_ref,
                  xi_ref, zeta_ref, cd_ref, retw_ref, qnw_ref, knw_ref, bkt_ref, relb_ref,
                  sinks_ref, o_ref, h0_s, h1_s, *scratch):
    h_bufs = (h0_s, h1_s)
    slots = (scratch[0:N_SLOT_BUFS], scratch[N_SLOT_BUFS:2 * N_SLOT_BUFS])
    (pt_s, mixed_s, mixedt_s, state_s, bias_s,
     win_ref, wft_ref, wout_ref, slab_s, stage_out, stage_x, stage_sem, slab_sem) = scratch[2 * N_SLOT_BUFS:]
    step = pl.program_id(0)
    mix_tile = jnp.maximum(step - 1, 0)
    pos0 = pl.multiple_of((jnp.minimum(step, n_tiles - 1) % tiles_per_seq) * SEQ_TILE, SEQ_TILE)
    first = mix_tile % tiles_per_seq == 0
    first_i = first.astype(jnp.int32)
    keep_state = jnp.where(first, 0.0, 1.0).astype(F32)

    def rms_norm_rows(xb):
        ms = jnp.mean(xb * xb, axis=-1, keepdims=True)
        return (xb * lax.rsqrt(ms + NORM_EPS) * nwrow_ref[...]).astype(BF16)

    def row_copies(hbm, stage, sem0, n_rows):
        return [pltpu.make_async_copy(hbm.at[pl.ds(i * STAGE_ROWS, STAGE_ROWS), :], stage.at[i],
                                      stage_sem.at[sem0 + i]) for i in range(n_rows // STAGE_ROWS)]

    def slab_copy(i):
        col0, width = SLAB_GROUPS[i]
        return pltpu.make_async_copy(win_hbm.at[:, pl.ds(col0, width)], slab_s.at[i % 2, :, pl.ds(0, width)],
                                     slab_sem.at[i % 2])

    def stage_group(i):
        col0, width = SLAB_GROUPS[i]
        slab_copy(i).wait()
        for r0 in range(0, D_MODEL, STAGE_ROWS):
            rows = slice(r0, r0 + STAGE_ROWS)
            for c0 in range(0, width, LANES):
                blk = slab_s[i % 2, rows, c0:c0 + LANES] * colscale_ref[:, col0 + c0:col0 + c0 + LANES]
                if col0 < D_TOK:
                    win_ref[rows, col0 + c0:col0 + c0 + LANES] = blk.astype(BF16)
                else:
                    wft_ref[col0 - D_TOK + c0:col0 - D_TOK + c0 + LANES, rows] = blk.T.astype(BF16)
        if i + 2 < len(SLAB_GROUPS):
            slab_copy(i + 2).start()

    def stage_w_out():
        for i, cp in enumerate(row_copies(wout_hbm, stage_out, SEQ_TILE // STAGE_ROWS, D_MIX)):
            cp.wait()
            wout_ref[i * STAGE_ROWS:(i + 1) * STAGE_ROWS, :] = stage_out[i].astype(BF16)

    @pl.when(step == 0)
    def _init():
        def build_tables():
            key = lax.broadcasted_iota(jnp.int32, (CHUNK, LANES), 0)
            lane_c = lax.broadcasted_iota(jnp.int32, (CHUNK, LANES), 1)
            for half in range(2):
                bk = bkt_ref[half]
                is_cur = key <= half * Q_HALF + lane_c % Q_HALF
                for pair in range(SWA_HEADS // 2):
                    acc = jnp.zeros((CHUNK, LANES), F32)
                    for u in range(NUM_BUCKETS):
                        val = jnp.where(lane_c < Q_HALF, relb_ref[u, 2 * pair], relb_ref[u, 2 * pair + 1]) * LOG2_E
                        acc = jnp.where(bk == u, val, acc)
                    bias_s[half * 4 + pair] = acc
                    bias_s[SWA_HEADS + half * 4 + pair] = jnp.where(is_cur, acc, NEG_INF)
            zeros_blk = jnp.zeros((N_CHUNKS, SWA_KV_HEADS, Q_HALF, 2 * LANES), BF16)
            pt_s[:, :, 3 * Q_HALF:4 * Q_HALF, 0:2 * LANES] = zeros_blk
            pt_s[:, :, 0:Q_HALF, 2 * LANES:4 * LANES] = zeros_blk
            state_s[...] = jnp.zeros_like(state_s)
            slots[0][SLOT_KBAND][0:CHUNK, :] = jnp.zeros((CHUNK, LANES), BF16)
            slots[0][SLOT_VT][:, 0:CHUNK] = jnp.zeros((SWA_KV_HEADS * SWA_HEAD_DIM, CHUNK), BF16)

        x_copies = row_copies(x_hbm.at[0], stage_x, 0, SEQ_TILE)
        for cp in x_copies + row_copies(wout_hbm, stage_out, len(x_copies), D_MIX) + [slab_copy(0), slab_copy(1)]:
            cp.start()
        build_tables()
        for i, cp in enumerate(x_copies):
            cp.wait()
            for r0, rs in _row_blocks(STAGE_ROWS):
                h0_s[i * STAGE_ROWS + r0:i * STAGE_ROWS + r0 + ROW_BLOCK, :] = rms_norm_rows(stage_x[i, rs, :])

    lane = lax.broadcasted_iota(jnp.int32, (ROW_BLOCK, LANES), 1)
    lo = lane < RET_DK
    first_half = lane % RET_DK < RET_DK // 2

    def project(slot, h_s, h_next_s, stage_weights=False):
        rq_s, rqx_s, rk_s, rkz_s, rv_s, gate_s, qt_s, kband_s, vt_s, gatet_s = slot

        def proj(off):
            return _dot(h_s[...], win_ref[:, off:off + 4 * LANES])

        def proj_t(row0, n_rows):
            return _dot_nt(wft_ref[row0:row0 + n_rows, :], h_s[...])

        def rot(v, rs):
            pos = pl.ds(pos0 + rs.start, ROW_BLOCK)
            partner = jnp.where(first_half, pltpu.roll(v, LANES - 32, 1), pltpu.roll(v, 32, 1))
            return v * cos_ref[pos, :] + partner * sins_ref[pos, :]

        def head_rms(xt):
            ms = jnp.mean(xt * xt, axis=0, keepdims=True)
            return xt * lax.rsqrt(ms + NORM_EPS)

        tok_blocks = [slice(c0, c0 + LANES) for c0 in range(0, SEQ_TILE, LANES)]

        def post_rqk(r):
            for r0, rs in _row_blocks(SEQ_TILE):
                c, cr = r0 // CHUNK, r0 % CHUNK
                cs = slice(cr, cr + ROW_BLOCK)
                for p in range(2):
                    q = rot(r[rs, p * LANES:(p + 1) * LANES], rs)
                    rq_s[p, rs, :] = q.astype(BF16)
                    rqx_s[p, 0, rs, :] = (q * xi_ref[p, 0, cs, :]).astype(BF16)
                    rqx_s[p, 1, rs, :] = (q * xi_ref[p, 1, cs, :]).astype(BF16)
                    k = rot(r[rs, (2 + p) * LANES:(3 + p) * LANES], rs)
                    rk_s[p, c, cr:cr + ROW_BLOCK, :] = jnp.where(lo, k, 0.0).astype(BF16)
                    rk_s[p, c, CHUNK + cr:CHUNK + cr + ROW_BLOCK, :] = jnp.where(lo, 0.0, k).astype(BF16)
                    rkz_s[p, rs, :] = (k * zeta_ref[p, cs, :]).astype(BF16)

        def post_rv(r):
            for r0, rs in _row_blocks(SEQ_TILE):
                rv_s[rs, :] = r[rs, :].astype(BF16)

        def post_rg(r):
            for r0, rs in _row_blocks(SEQ_TILE):
                gate_s[rs, :] = _silu_of_half(r[rs, :])

        def post_qt(rt):
            for pair in range(SWA_HEADS // 2):
                g, gp = pair // 2, pair % 2
                for c, ts in enumerate(tok_blocks):
                    xe, xo = (head_rms(rt[hd * SWA_HEAD_DIM:(hd + 1) * SWA_HEAD_DIM, ts])
                              for hd in (2 * pair, 2 * pair + 1))
                    for half in range(2):
                        hq = slice(half * Q_HALF, (half + 1) * Q_HALF)
                        qt_s[c, half, g, :, gp * LANES:(gp + 1) * LANES] = jnp.concatenate(
                            [xe[:, hq], xo[:, hq]], axis=1).astype(BF16)

        def post_kvt(rt):
            qk_head = qnw_ref[...] * knw_ref[...] * (SWA_HEAD_DIM ** -0.5 * LOG2_E)
            qk_w = jnp.concatenate([qk_head] * SWA_KV_HEADS, axis=1)
            for ts in tok_blocks:
                knt = jnp.concatenate(
                    [head_rms(rt[g * SWA_HEAD_DIM:(g + 1) * SWA_HEAD_DIM, ts]) for g in range(SWA_KV_HEADS)],
                    axis=0)
                kband_s[CHUNK + ts.start:CHUNK + ts.stop, :] = (knt.T * qk_w).astype(BF16)
                vt_s[:, CHUNK + ts.start:CHUNK + ts.stop] = rt[LANES:2 * LANES, ts].astype(BF16)

        def post_gt(rt):
            for r0 in range(0, SWA_HEADS * SWA_HEAD_DIM, ROW_BLOCK):
                for ts in tok_blocks:
                    gatet_s[r0:r0 + ROW_BLOCK, ts] = _silu_of_half(rt[r0:r0 + ROW_BLOCK, ts])

        groups = [
            (functools.partial(proj, OFF_RQ), post_rqk),
            (functools.partial(proj_t, FT_Q, SWA_HEADS * SWA_HEAD_DIM), post_qt),
            (functools.partial(proj, OFF_RG), post_rg),
            (functools.partial(proj_t, FT_G, SWA_HEADS * SWA_HEAD_DIM), post_gt),
            (functools.partial(proj_t, FT_K, 2 * LANES), post_kvt),
            (functools.partial(proj, OFF_RV), post_rv),
        ]
        pending = None
        for i, (matmul, post) in enumerate(groups):
            if stage_weights:
                stage_group(i)
            result = matmul()
            yield
            if pending is not None:
                pending[0](pending[1])
                yield
            pending = (post, result)
            if i == NORM_AFTER_GROUP:
                for r0, rs in _row_blocks(SEQ_TILE):
                    h_next_s[rs, :] = rms_norm_rows(xnext_ref[0, rs, :])
                yield
        pending[0](pending[1])
        yield

    def mix(slot, next_slot):
        rq_s, rqx_s, rk_s, rkz_s, rv_s, gate_s, qt_s, kband_s, vt_s, gatet_s = slot
        key = lax.broadcasted_iota(jnp.int32, (Q_HALF, LANES), 0)
        lane_h = lax.broadcasted_iota(jnp.int32, (Q_HALF, LANES), 1)
        tri_t = key <= lane_h % Q_HALF
        cur_keep = jnp.where(tri_t, 1.0, 0.0).astype(BF16)
        prev_keep = jnp.where(tri_t, 0.0, 1.0).astype(BF16)
        first_head_lanes = lax.broadcasted_iota(jnp.int32, (1, LANES), 1) < Q_HALF
        zero_q = jnp.zeros((SWA_HEAD_DIM, 2 * LANES), BF16)
        ones_rows = jnp.ones((SUBLANES, 2 * CHUNK), BF16)

        def out_proj(c_hi):
            pair = slice((c_hi - 1) * CHUNK, (c_hi + 1) * CHUNK)
            o_ref[0, pair, :] = (xres_ref[0, pair, :]
                                 + _dot(mixed_s[pair, :], wout_ref[0:RET_HEADS * RET_DV, :])
                                 + _dot_tn(mixedt_s[:, pair], wout_ref[RET_HEADS * RET_DV:D_MIX, :]))

        chunk_rows = [slice(c * CHUNK, (c + 1) * CHUNK) for c in range(N_CHUNKS)]
        chunk_band = [slice(c * CHUNK, (c + 2) * CHUNK) for c in range(N_CHUNKS)]

        def logits_for(c):
            rhs_a, rhs_b = (jnp.concatenate([jnp.concatenate([qt_s[c, half, 0], zero_q], axis=1),
                                             jnp.concatenate([zero_q, qt_s[c, half, 1]], axis=1)], axis=0)
                            for half in range(2))
            k0 = c * CHUNK
            p0_a = _dot(kband_s[k0:k0 + Q_HALF, :], rhs_a)
            mid = _dot(kband_s[k0 + Q_HALF:k0 + 3 * Q_HALF, :], jnp.concatenate([rhs_a, rhs_b], axis=1))
            c1_b = _dot(kband_s[k0 + 3 * Q_HALF:k0 + 4 * Q_HALF, :], rhs_b)
            return p0_a, mid, c1_b

        scores, incs, logits_t = {}, {}, []
        for c, rows in enumerate(chunk_rows):
            for p in range(2):
                s2 = _dot_nt(rq_s[p, rows, :], rk_s[p, c])
                scores[c, p] = jnp.concatenate([(s2[bs, :] * dec_ref[p, bs, :]).astype(BF16)
                                                for _, bs in _row_blocks(CHUNK)], axis=0)
                incs[c, p] = _dot_tn(rkz_s[p, rows, :], rv_s[rows, p * 2 * RET_DV:(p + 1) * 2 * RET_DV])
            logits_t.append(logits_for(c))
        yield

        ret_outs = {}
        for p in range(2):
            full = state_s[p] * keep_state
            for c, rows in enumerate(chunk_rows):
                full_b = full.astype(BF16)
                v2 = rv_s[rows, p * 2 * RET_DV:(p + 1) * 2 * RET_DV]
                s2 = scores[c, p]
                for j in range(2):
                    ret_outs[c, 2 * p + j] = _dot(
                        jnp.concatenate([s2[:, j * CHUNK:(j + 1) * CHUNK], rqx_s[p, j, rows, :]], axis=1),
                        jnp.concatenate([v2[:, j * RET_DV:(j + 1) * RET_DV],
                                         full_b[:, j * RET_DV:(j + 1) * RET_DV]], axis=0))
                full = full * cd_ref[p] + incs[c, p]
            state_s[p] = full
        yield

        for c in range(N_CHUNKS):
            rows, band, lt = chunk_rows[c], chunk_band[c], logits_t[c]

            for hd in range(RET_HEADS):
                o = ret_outs[c, hd]
                cols = slice(hd * RET_DV, (hd + 1) * RET_DV)
                for r0, bs in _row_blocks(CHUNK):
                    ob = o[bs, :]
                    rs = slice(c * CHUNK + r0, c * CHUNK + r0 + ROW_BLOCK)
                    mu = jnp.mean(ob, axis=-1, keepdims=True)
                    d = ob - mu
                    var = jnp.mean(d * d, axis=-1, keepdims=True)
                    y = d * lax.rsqrt(var + GN_EPS) * retw_ref[:, cols]
                    mixed_s[rs, cols] = (y * gate_s[rs, cols]).astype(BF16)
            p0_a, mid, c1_b = lt
            p_rows = [slice(i * Q_HALF, (i + 1) * Q_HALF) for i in range(4)]
            sink_terms = {}
            for half in range(2):
                for pair in range(SWA_HEADS // 2):
                    g, gp = pair // 2, pair % 2
                    cs = slice(pair * LANES, (pair + 1) * LANES)
                    cs_b = slice(4 * LANES + pair * LANES, 4 * LANES + (pair + 1) * LANES)
                    sink = jnp.where(first_head_lanes, sinks_ref[2 * pair], sinks_ref[2 * pair + 1]) * LOG2_E
                    bi = half * 4 + pair
                    bias = bias_s[bi + SWA_HEADS * first_i] if c == 0 else bias_s[bi]
                    if half == 0:
                        lg0 = jnp.where(tri_t, mid[Q_HALF:CHUNK, cs], p0_a[:, cs]) + bias[0:Q_HALF]
                        lg1 = mid[0:Q_HALF, cs] + bias[Q_HALF:CHUNK]
                    else:
                        lg0 = mid[Q_HALF:CHUNK, cs_b] + bias[0:Q_HALF]
                        lg1 = jnp.where(tri_t, c1_b[:, cs], mid[0:Q_HALF, cs_b]) + bias[Q_HALF:CHUNK]
                    m = jnp.maximum(jnp.maximum(jnp.max(lg0, axis=0, keepdims=True),
                                                jnp.max(lg1, axis=0, keepdims=True)), sink)
                    e0 = jnp.exp2(lg0 - m).astype(BF16)
                    e1 = jnp.exp2(lg1 - m).astype(BF16)
                    sink_terms[half, pair] = jnp.exp2(sink - m)
                    pq = slice(half * 2 * LANES + gp * LANES, half * 2 * LANES + (gp + 1) * LANES)
                    if half == 0:
                        pt_s[c, g, p_rows[0], pq] = e0 * prev_keep
                        pt_s[c, g, p_rows[2], pq] = e0 * cur_keep
                        pt_s[c, g, p_rows[1], pq] = e1
                    else:
                        pt_s[c, g, p_rows[2], pq] = e0
                        pt_s[c, g, p_rows[1], pq] = e1 * prev_keep
                        pt_s[c, g, p_rows[3], pq] = e1 * cur_keep
            yield

            for g in range(SWA_KV_HEADS):
                vt1 = jnp.concatenate([vt_s[g * SWA_HEAD_DIM:(g + 1) * SWA_HEAD_DIM, band], ones_rows], axis=0)
                ot = _dot(vt1, pt_s[c, g])
                assert ot.shape == (SWA_HEAD_DIM + SUBLANES, 4 * LANES)
                den = ot[SWA_HEAD_DIM:SWA_HEAD_DIM + 1, :] + jnp.concatenate(
                    [sink_terms[half, 2 * g + gp] for half in range(2) for gp in range(2)], axis=1)
                otn = ot[0:SWA_HEAD_DIM, :] * (1.0 / den)
                for hh in range(SWA_GROUP):
                    hd = SWA_GROUP * g + hh
                    hs = slice(hd * SWA_HEAD_DIM, (hd + 1) * SWA_HEAD_DIM)
                    o_h = jnp.concatenate([otn[:, hh * Q_HALF:(hh + 1) * Q_HALF],
                                           otn[:, 2 * LANES + hh * Q_HALF:2 * LANES + (hh + 1) * Q_HALF]], axis=1)
                    mixedt_s[hs, rows] = (o_h * gatet_s[hs, rows]).astype(BF16)
            yield
            if c % 2 == 1:
                out_proj(c)
                yield

        last = slice(SEQ_TILE, SEQ_TILE + CHUNK)
        next_slot[SLOT_KBAND][0:CHUNK, :] = kband_s[last, :]
        next_slot[SLOT_VT][:, 0:CHUNK] = vt_s[:, last]

    def run(parity, do_project=True, do_mix=True, stage_weights=False):
        streams = [project(slots[parity], h_bufs[parity], h_bufs[1 - parity], stage_weights) if do_project
                   else iter(()),
                   mix(slots[1 - parity], slots[parity]) if do_mix else iter(())]
        for which in MIX_ORDER + (0,) * len(MIX_ORDER) + (1,) * len(MIX_ORDER):
            next(streams[which], None)

    is_first, is_last = step == 0, step == n_tiles
    pl.when(is_first)(functools.partial(run, 0, do_mix=False, stage_weights=True))
    pl.when(is_first)(stage_w_out)
    pl.when(is_last)(functools.partial(run, n_tiles % 2, do_project=False))
    for parity in range(2):
        pl.when((step % 2 == parity) & jnp.logical_not(is_first | is_last))(functools.partial(run, parity))


def kernel(x, norm_w, w_in, ret_norm_w, q_norm_w, k_norm_w, sinks, rel_bias, w_out):
    batch, seq, d_model = x.shape
    assert d_model == D_MODEL and seq % SEQ_TILE == 0
    assert w_in.shape == (D_MODEL, D_IN) and w_out.shape == (D_MIX, D_MODEL)
    tb = _constant_tables(seq)
    tiles_per_seq = seq // SEQ_TILE
    n_tiles = batch * tiles_per_seq

    def proj_tile(s):
        return jnp.minimum(s, n_tiles - 1)

    def mix_tile(s):
        return jnp.maximum(s - 1, 0)

    const2 = lambda s: (0, 0)
    const3 = lambda s: (0, 0, 0)
    const4 = lambda s: (0, 0, 0, 0)
    pos_spec = pl.BlockSpec((seq, LANES), const2, pipeline_mode=pl.Buffered(1))
    smem = pl.BlockSpec(memory_space=pltpu.SMEM)
    x_block = (1, SEQ_TILE, D_MODEL)
    mix_map = lambda s: (mix_tile(s) // tiles_per_seq, mix_tile(s) % tiles_per_seq, 0)

    in_specs = [
        pl.BlockSpec(x_block, lambda s: (proj_tile(s + 1) // tiles_per_seq, proj_tile(s + 1) % tiles_per_seq, 0)),
        pl.BlockSpec(x_block, mix_map),
        pl.BlockSpec(memory_space=pl.ANY),
        pl.BlockSpec((1, D_MODEL), const2),
        pl.BlockSpec((1, D_IN), const2),
        pl.BlockSpec(memory_space=pl.ANY),
        pl.BlockSpec(memory_space=pl.ANY),
        pos_spec, pos_spec,
        pl.BlockSpec((2, CHUNK, 2 * CHUNK), const3),
        pl.BlockSpec((2, 2, CHUNK, LANES), const4),
        pl.BlockSpec((2, CHUNK, LANES), const3),
        pl.BlockSpec((2, 1, 2 * RET_DV), const3),
        pl.BlockSpec((1, RET_HEADS * RET_DV), const2),
        pl.BlockSpec((1, SWA_HEAD_DIM), const2),
        pl.BlockSpec((1, SWA_HEAD_DIM), const2),
        pl.BlockSpec((2, CHUNK, LANES), const3),
        smem, smem,
    ]
    slot_bufs = [
        pltpu.VMEM((2, SEQ_TILE, LANES), BF16),
        pltpu.VMEM((2, 2, SEQ_TILE, LANES), BF16),
        pltpu.VMEM((2, N_CHUNKS, 2 * CHUNK, LANES), BF16),
        pltpu.VMEM((2, SEQ_TILE, LANES), BF16),
        pltpu.VMEM((SEQ_TILE, RET_HEADS * RET_DV), BF16),
        pltpu.VMEM((SEQ_TILE, RET_HEADS * RET_DV), F32),
        pltpu.VMEM((N_CHUNKS, 2, SWA_KV_HEADS, SWA_HEAD_DIM, 2 * LANES), BF16),
        pltpu.VMEM((SEQ_TILE + CHUNK, LANES), BF16),
        pltpu.VMEM((SWA_KV_HEADS * SWA_HEAD_DIM, SEQ_TILE + CHUNK), BF16),
        pltpu.VMEM((SWA_HEADS * SWA_HEAD_DIM, SEQ_TILE), F32),
    ]
    assert len(slot_bufs) == N_SLOT_BUFS
    h_buf = pltpu.VMEM((SEQ_TILE, D_MODEL), BF16)
    scratch = [h_buf, h_buf] + slot_bufs + slot_bufs + [
        pltpu.VMEM((N_CHUNKS, SWA_KV_HEADS, 2 * CHUNK, SWA_GROUP * CHUNK), BF16),
        pltpu.VMEM((SEQ_TILE, RET_HEADS * RET_DV), BF16),
        pltpu.VMEM((SWA_HEADS * SWA_HEAD_DIM, SEQ_TILE), BF16),
        pltpu.VMEM((2, 2 * RET_DK, 2 * RET_DV), F32),
        pltpu.VMEM((2 * SWA_HEADS, CHUNK, LANES), F32),
        pltpu.VMEM((D_MODEL, D_TOK), BF16),
        pltpu.VMEM((D_FT, D_MODEL), BF16),
        pltpu.VMEM((D_MIX, D_MODEL), BF16),
        pltpu.VMEM((2, D_MODEL, 4 * LANES), F32),
        pltpu.VMEM((D_MIX // STAGE_ROWS, STAGE_ROWS, D_MODEL), F32),
        pltpu.VMEM((SEQ_TILE // STAGE_ROWS, STAGE_ROWS, D_MODEL), F32),
        pltpu.SemaphoreType.DMA(((SEQ_TILE + D_MIX) // STAGE_ROWS,)),
        pltpu.SemaphoreType.DMA((2,)),
    ]
    return pl.pallas_call(
        functools.partial(_layer_kernel, tiles_per_seq, n_tiles),
        grid=(n_tiles + 1,),
        in_specs=in_specs,
        out_specs=pl.BlockSpec(x_block, mix_map),
        out_shape=jax.ShapeDtypeStruct(x.shape, x.dtype),
        scratch_shapes=scratch,
        compiler_params=pltpu.CompilerParams(
            dimension_semantics=("arbitrary",),
            vmem_limit_bytes=VMEM_LIMIT_BYTES),
    )(x, x, x, norm_w.reshape(1, D_MODEL), tb["col_scale"], w_in, w_out,
      tb["cos"], tb["sin_s"], tb["dec"], tb["xi"], tb["zeta"], tb["cd"],
      ret_norm_w.reshape(1, -1), q_norm_w.reshape(1, -1), k_norm_w.reshape(1, -1), tb["bucket"],
      rel_bias.astype(F32), sinks.astype(F32))
```

```python
import functools
import math

import numpy as np
import jax
import jax.numpy as jnp
from jax import lax
from jax.experimental import pallas as pl
from jax.experimental.pallas import tpu as pltpu

D_MODEL = 1024
RET_HEADS = 4
RET_DK = 64
RET_DV = 128
ROT_HALF = RET_DK // 2
CHUNK = 128
Q_HALF = CHUNK // 2
RET_ROPE_BASE = 10000.0
SWA_HEADS = 8
SWA_KV_HEADS = 2
SWA_HEAD_DIM = 64
SWA_GROUP = SWA_HEADS // SWA_KV_HEADS
NUM_BUCKETS = 32
MAX_DISTANCE = 128
NORM_EPS = 1e-6
GN_EPS = 1e-5
NEG_INF = -1e30
LOG2_E = math.log2(math.e)

OFF_RQ, OFF_RK, OFF_RV, OFF_RG = 0, 256, 512, 1024
D_TOK = 1536
FT_Q, FT_K, FT_G = 0, 512, 768
D_FT = 1280
D_IN = D_TOK + D_FT
D_MIX = 1024

LANES = 128
SUBLANES = 8
SEQ_TILE = 512
N_CHUNKS = SEQ_TILE // CHUNK
ROW_BLOCK = 32
VMEM_LIMIT_BYTES = 56 * 1024 * 1024
MIX_ORDER = (1, 1, 0, 0, 1, 1, 0, 0, 1, 1, 0, 0, 1, 0, 1, 1, 0, 0, 1, 1, 0, 0, 1, 0, 0)
NORM_AFTER_GROUP = 2
N_SLOT_BUFS = 10
STAGE_ROWS = 128
SLAB_GROUPS = ((OFF_RQ, 4 * LANES), (D_TOK + FT_Q, 4 * LANES), (OFF_RG, 4 * LANES), (D_TOK + FT_G, 4 * LANES),
               (D_TOK + FT_K, 2 * LANES), (OFF_RV, 4 * LANES))
SLOT_KBAND, SLOT_VT = 7, 8

BF16 = jnp.bfloat16
F32 = jnp.float32


def _t5_bucket_np(n):
    max_exact = NUM_BUCKETS // 2
    nf = np.maximum(n, 1).astype(np.float64)
    large = max_exact + (np.log(nf / max_exact) / math.log(MAX_DISTANCE / max_exact)
                         * (NUM_BUCKETS - max_exact)).astype(np.int32)
    large = np.minimum(large, NUM_BUCKETS - 1)
    return np.where(n < max_exact, n, large).astype(np.int32)


def _constant_tables(seq):
    half = RET_DK // 2
    inv_freq = RET_ROPE_BASE ** (-np.arange(half, dtype=np.float64) / half)
    ang = np.arange(seq, dtype=np.float64)[:, None] * inv_freq[None, :]
    cos, sin = np.cos(ang), np.sin(ang)
    cos_t = np.tile(np.concatenate([cos, cos], axis=1), (1, 2))
    sin_s = np.tile(np.concatenate([-sin, sin], axis=1), (1, 2))

    gamma = 1.0 - np.exp2(-5.0 - np.arange(RET_HEADS, dtype=np.float64))
    log_g = np.log(gamma)
    i = np.arange(CHUNK, dtype=np.float64)
    diff = i[:, None] - i[None, :]
    k_scale = RET_DK ** -0.5
    decay = np.where(diff >= 0, np.exp(log_g[:, None, None] * np.maximum(diff, 0.0)), 0.0) * k_scale
    dec = np.stack([np.concatenate([decay[2 * p], decay[2 * p + 1]], axis=1) for p in range(2)])
    xi = np.exp(log_g[:, None] * (i + 1.0))
    zeta = np.exp(log_g[:, None] * (CHUNK - 1.0 - i)) * k_scale

    def per_lane(t):
        return np.stack([np.concatenate([np.repeat(t[2 * p][:, None], RET_DK, 1),
                                         np.repeat(t[2 * p + 1][:, None], RET_DK, 1)], axis=1)
                         for p in range(2)])

    lo = (np.arange(LANES) < RET_DK)[None, None, :]
    xi_l = per_lane(xi)
    xi_m = np.stack([np.where(lo, xi_l, 0.0), np.where(lo, 0.0, xi_l)], axis=1)

    chunk_decay = np.exp(log_g * CHUNK)
    cd = np.stack([np.concatenate([np.full((1, RET_DV), chunk_decay[2 * p]),
                                   np.full((1, RET_DV), chunk_decay[2 * p + 1])], axis=1)
                   for p in range(2)])

    j = np.arange(CHUNK)[None, :, None]
    q = (np.arange(2)[:, None, None] * Q_HALF + np.arange(LANES)[None, None, :] % Q_HALF)
    dist = np.where(j <= q, q - j, q + CHUNK - j)
    bucket = _t5_bucket_np(dist)

    f = lambda a: jnp.asarray(a, dtype=F32)
    col_scale = np.ones((1, D_IN))
    col_scale[:, OFF_RG:OFF_RG + RET_HEADS * RET_DV] = 0.5
    col_scale[:, D_TOK + FT_G:D_TOK + FT_G + SWA_HEADS * SWA_HEAD_DIM] = 0.5

    return dict(cos=f(cos_t), sin_s=f(sin_s), col_scale=f(col_scale), dec=f(dec), xi=f(xi_m),
                zeta=f(per_lane(zeta)), cd=f(cd), bucket=jnp.asarray(bucket, dtype=jnp.int32))


def _dot(a, b):
    return jnp.dot(a, b, preferred_element_type=F32)


def _dot_nt(a, b):
    return lax.dot_general(a, b, (((1,), (1,)), ((), ())), preferred_element_type=F32)


def _dot_tn(a, b):
    return lax.dot_general(a, b, (((0,), (0,)), ((), ())), preferred_element_type=F32)


def _silu_of_half(hg):
    return hg + hg * jnp.tanh(hg)


def _row_blocks(n_rows):
    return [(r0, slice(r0, r0 + ROW_BLOCK)) for r0 in range(0, n_rows, ROW_BLOCK)]


def _layer_kernel(tiles_per_seq, n_tiles,
                  xnext_ref, xres_ref, x_hbm, nwrow_ref, colscale_ref, win_hbm, wout_hbm, cos_ref, sins_ref, dec_ref,
                  xi_ref, zeta_ref, cd_ref, retw_ref, qnw_ref, knw_ref, bkt_ref, relb_ref,
                  sinks_ref, o_ref, h0_s, h1_s, *scratch):
    h_bufs = (h0_s, h1_s)
    slots = (scratch[0:N_SLOT_BUFS], scratch[N_SLOT_BUFS:2 * N_SLOT_BUFS])
    (pt_s, mixed_s, mixedt_s, state_s, bias_s,
     win_ref, wft_ref, wout_ref, slab_s, stage_out, stage_x, stage_sem, slab_sem) = scratch[2 * N_SLOT_BUFS:]
    step = pl.program_id(0)
    mix_tile = jnp.maximum(step - 1, 0)
    pos0 = pl.multiple_of((jnp.minimum(step, n_tiles - 1) % tiles_per_seq) * SEQ_TILE, SEQ_TILE)
    first = mix_tile % tiles_per_seq == 0
    first_i = first.astype(jnp.int32)
    keep_state = jnp.where(first, 0.0, 1.0).astype(F32)

    def rms_norm_rows(xb):
        ms = jnp.mean(xb * xb, axis=-1, keepdims=True)
        return (xb * lax.rsqrt(ms + NORM_EPS) * nwrow_ref[...]).astype(BF16)

    def row_copies(hbm, stage, sem0, n_rows):
        return [pltpu.make_async_copy(hbm.at[pl.ds(i * STAGE_ROWS, STAGE_ROWS), :], stage.at[i],
                                      stage_sem.at[sem0 + i]) for i in range(n_rows // STAGE_ROWS)]

    def slab_copy(i):
        col0, width = SLAB_GROUPS[i]
        return pltpu.make_async_copy(win_hbm.at[:, pl.ds(col0, width)], slab_s.at[i % 2, :, pl.ds(0, width)],
                                     slab_sem.at[i % 2])

    def stage_group(i):
        col0, width = SLAB_GROUPS[i]
        slab_copy(i).wait()
        for r0 in range(0, D_MODEL, STAGE_ROWS):
            rows = slice(r0, r0 + STAGE_ROWS)
            for c0 in range(0, width, LANES):
                blk = slab_s[i % 2, rows, c0:c0 + LANES] * colscale_ref[:, col0 + c0:col0 + c0 + LANES]
                if col0 < D_TOK:
                    win_ref[rows, col0 + c0:col0 + c0 + LANES] = blk.astype(BF16)
                else:
                    wft_ref[col0 - D_TOK + c0:col0 - D_TOK + c0 + LANES, rows] = blk.T.astype(BF16)
        if i + 2 < len(SLAB_GROUPS):
            slab_copy(i + 2).start()

    def stage_w_out():
        for i, cp in enumerate(row_copies(wout_hbm, stage_out, SEQ_TILE // STAGE_ROWS, D_MIX)):
            cp.wait()
            wout_ref[i * STAGE_ROWS:(i + 1) * STAGE_ROWS, :] = stage_out[i].astype(BF16)

    @pl.when(step == 0)
    def _init():
        def build_tables():
            key = lax.broadcasted_iota(jnp.int32, (CHUNK, LANES), 0)
            lane_c = lax.broadcasted_iota(jnp.int32, (CHUNK, LANES), 1)
            for half in range(2):
                bk = bkt_ref[half]
                is_cur = key <= half * Q_HALF + lane_c % Q_HALF
                for pair in range(SWA_HEADS // 2):
                    acc = jnp.zeros((CHUNK, LANES), F32)
                    for u in range(NUM_BUCKETS):
                        val = jnp.where(lane_c < Q_HALF, relb_ref[u, 2 * pair], relb_ref[u, 2 * pair + 1]) * LOG2_E
                        acc = jnp.where(bk == u, val, acc)
                    bias_s[half * 4 + pair] = acc
                    bias_s[SWA_HEADS + half * 4 + pair] = jnp.where(is_cur, acc, NEG_INF)
            zeros_blk = jnp.zeros((N_CHUNKS, SWA_KV_HEADS, Q_HALF, 2 * LANES), BF16)
            pt_s[:, :, 3 * Q_HALF:4 * Q_HALF, 0:2 * LANES] = zeros_blk
            pt_s[:, :, 0:Q_HALF, 2 * LANES:4 * LANES] = zeros_blk
            state_s[...] = jnp.zeros_like(state_s)
            slots[0][SLOT_KBAND][0:CHUNK, :] = jnp.zeros((CHUNK, LANES), BF16)
            slots[0][SLOT_VT][:, 0:CHUNK] = jnp.zeros((SWA_KV_HEADS * SWA_HEAD_DIM, CHUNK), BF16)

        x_copies = row_copies(x_hbm.at[0], stage_x, 0, SEQ_TILE)
        for cp in x_copies + row_copies(wout_hbm, stage_out, len(x_copies), D_MIX) + [slab_copy(0), slab_copy(1)]:
            cp.start()
        build_tables()
        for i, cp in enumerate(x_copies):
            cp.wait()
            for r0, rs in _row_blocks(STAGE_ROWS):
                h0_s[i * STAGE_ROWS + r0:i * STAGE_ROWS + r0 + ROW_BLOCK, :] = rms_norm_rows(stage_x[i, rs, :])

    lane = lax.broadcasted_iota(jnp.int32, (ROW_BLOCK, LANES), 1)
    lo = lane < RET_DK
    first_half = lane % RET_DK < ROT_HALF

    def project(slot, h_s, h_next_s, stage_weights=False):
        rq_s, rqx_s, rk_s, rkz_s, rv_s, gate_s, qt_s, kband_s, vt_s, gatet_s = slot

        def proj(off):
            return _dot(h_s[...], win_ref[:, off:off + 4 * LANES])

        def proj_t(row0, n_rows):
            return _dot_nt(wft_ref[row0:row0 + n_rows, :], h_s[...])

        def rot(v, rs):
            pos = pl.ds(pos0 + rs.start, ROW_BLOCK)
            partner = jnp.where(first_half, pltpu.roll(v, LANES - ROT_HALF, 1), pltpu.roll(v, ROT_HALF, 1))
            return v * cos_ref[pos, :] + partner * sins_ref[pos, :]

        def head_rms(xt):
            ms = jnp.mean(xt * xt, axis=0, keepdims=True)
            return xt * lax.rsqrt(ms + NORM_EPS)

        tok_blocks = [slice(c0, c0 + LANES) for c0 in range(0, SEQ_TILE, LANES)]

        def post_rqk(r):
            for r0, rs in _row_blocks(SEQ_TILE):
                c, cr = r0 // CHUNK, r0 % CHUNK
                cs = slice(cr, cr + ROW_BLOCK)
                for p in range(2):
                    q = rot(r[rs, p * LANES:(p + 1) * LANES], rs)
                    rq_s[p, rs, :] = q.astype(BF16)
                    rqx_s[p, 0, rs, :] = (q * xi_ref[p, 0, cs, :]).astype(BF16)
                    rqx_s[p, 1, rs, :] = (q * xi_ref[p, 1, cs, :]).astype(BF16)
                    k = rot(r[rs, (2 + p) * LANES:(3 + p) * LANES], rs)
                    rk_s[p, c, cr:cr + ROW_BLOCK, :] = jnp.where(lo, k, 0.0).astype(BF16)
                    rk_s[p, c, CHUNK + cr:CHUNK + cr + ROW_BLOCK, :] = jnp.where(lo, 0.0, k).astype(BF16)
                    rkz_s[p, rs, :] = (k * zeta_ref[p, cs, :]).astype(BF16)

        def post_rv(r):
            for r0, rs in _row_blocks(SEQ_TILE):
                rv_s[rs, :] = r[rs, :].astype(BF16)

        def post_rg(r):
            for r0, rs in _row_blocks(SEQ_TILE):
                gate_s[rs, :] = _silu_of_half(r[rs, :])

        def post_qt(rt):
            for pair in range(SWA_HEADS // 2):
                g, gp = pair // 2, pair % 2
                for c, ts in enumerate(tok_blocks):
                    xe, xo = (head_rms(rt[hd * SWA_HEAD_DIM:(hd + 1) * SWA_HEAD_DIM, ts])
                              for hd in (2 * pair, 2 * pair + 1))
                    for half in range(2):
                        hq = slice(half * Q_HALF, (half + 1) * Q_HALF)
                        qt_s[c, half, g, :, gp * LANES:(gp + 1) * LANES] = jnp.concatenate(
                            [xe[:, hq], xo[:, hq]], axis=1).astype(BF16)

        def post_kvt(rt):
            qk_head = qnw_ref[...] * knw_ref[...] * (SWA_HEAD_DIM ** -0.5 * LOG2_E)
            qk_w = jnp.concatenate([qk_head] * SWA_KV_HEADS, axis=1)
            for ts in tok_blocks:
                knt = jnp.concatenate(
                    [head_rms(rt[g * SWA_HEAD_DIM:(g + 1) * SWA_HEAD_DIM, ts]) for g in range(SWA_KV_HEADS)],
                    axis=0)
                kband_s[CHUNK + ts.start:CHUNK + ts.stop, :] = (knt.T * qk_w).astype(BF16)
                vt_s[:, CHUNK + ts.start:CHUNK + ts.stop] = rt[LANES:2 * LANES, ts].astype(BF16)

        def post_gt(rt):
            for r0 in range(0, SWA_HEADS * SWA_HEAD_DIM, ROW_BLOCK):
                for ts in tok_blocks:
                    gatet_s[r0:r0 + ROW_BLOCK, ts] = _silu_of_half(rt[r0:r0 + ROW_BLOCK, ts])

        groups = [
            (functools.partial(proj, OFF_RQ), post_rqk),
            (functools.partial(proj_t, FT_Q, SWA_HEADS * SWA_HEAD_DIM), post_qt),
            (functools.partial(proj, OFF_RG), post_rg),
            (functools.partial(proj_t, FT_G, SWA_HEADS * SWA_HEAD_DIM), post_gt),
            (functools.partial(proj_t, FT_K, 2 * LANES), post_kvt),
            (functools.partial(proj, OFF_RV), post_rv),
        ]
        pending = None
        for i, (matmul, post) in enumerate(groups):
            if stage_weights:
                stage_group(i)
            result = matmul()
            yield
            if pending is not None:
                pending[0](pending[1])
                yield
            pending = (post, result)
            if i == NORM_AFTER_GROUP:
                for r0, rs in _row_blocks(SEQ_TILE):
                    h_next_s[rs, :] = rms_norm_rows(xnext_ref[0, rs, :])
                yield
        pending[0](pending[1])
        yield

    def mix(slot, next_slot):
        rq_s, rqx_s, rk_s, rkz_s, rv_s, gate_s, qt_s, kband_s, vt_s, gatet_s = slot
        key = lax.broadcasted_iota(jnp.int32, (Q_HALF, LANES), 0)
        lane_h = lax.broadcasted_iota(jnp.int32, (Q_HALF, LANES), 1)
        tri_t = key <= lane_h % Q_HALF
        cur_keep = jnp.where(tri_t, 1.0, 0.0).astype(BF16)
        prev_keep = jnp.where(tri_t, 0.0, 1.0).astype(BF16)
        first_head_lanes = lax.broadcasted_iota(jnp.int32, (1, LANES), 1) < Q_HALF
        zero_q = jnp.zeros((SWA_HEAD_DIM, 2 * LANES), BF16)
        ones_rows = jnp.ones((SUBLANES, 2 * CHUNK), BF16)

        def out_proj(c_hi):
            pair = slice((c_hi - 1) * CHUNK, (c_hi + 1) * CHUNK)
            o_ref[0, pair, :] = (xres_ref[0, pair, :]
                                 + _dot(mixed_s[pair, :], wout_ref[0:RET_HEADS * RET_DV, :])
                                 + _dot_tn(mixedt_s[:, pair], wout_ref[RET_HEADS * RET_DV:D_MIX, :]))

        chunk_rows = [slice(c * CHUNK, (c + 1) * CHUNK) for c in range(N_CHUNKS)]
        chunk_band = [slice(c * CHUNK, (c + 2) * CHUNK) for c in range(N_CHUNKS)]

        def logits_for(c):
            rhs_a, rhs_b = (jnp.concatenate([jnp.concatenate([qt_s[c, half, 0], zero_q], axis=1),
                                             jnp.concatenate([zero_q, qt_s[c, half, 1]], axis=1)], axis=0)
                            for half in range(2))
            k0 = c * CHUNK
            p0_a = _dot(kband_s[k0:k0 + Q_HALF, :], rhs_a)
            mid = _dot(kband_s[k0 + Q_HALF:k0 + 3 * Q_HALF, :], jnp.concatenate([rhs_a, rhs_b], axis=1))
            c1_b = _dot(kband_s[k0 + 3 * Q_HALF:k0 + 4 * Q_HALF, :], rhs_b)
            return p0_a, mid, c1_b

        scores, incs, logits_t = {}, {}, []
        for c, rows in enumerate(chunk_rows):
            for p in range(2):
                s2 = _dot_nt(rq_s[p, rows, :], rk_s[p, c])
                scores[c, p] = jnp.concatenate([(s2[bs, :] * dec_ref[p, bs, :]).astype(BF16)
                                                for _, bs in _row_blocks(CHUNK)], axis=0)
                incs[c, p] = _dot_tn(rkz_s[p, rows, :], rv_s[rows, p * 2 * RET_DV:(p + 1) * 2 * RET_DV])
            logits_t.append(logits_for(c))
        yield

        ret_outs = {}
        for p in range(2):
            full = state_s[p] * keep_state
            for c, rows in enumerate(chunk_rows):
                full_b = full.astype(BF16)
                v2 = rv_s[rows, p * 2 * RET_DV:(p + 1) * 2 * RET_DV]
                s2 = scores[c, p]
                for j in range(2):
                    ret_outs[c, 2 * p + j] = _dot(
                        jnp.concatenate([s2[:, j * CHUNK:(j + 1) * CHUNK], rqx_s[p, j, rows, :]], axis=1),
                        jnp.concatenate([v2[:, j * RET_DV:(j + 1) * RET_DV],
                                         full_b[:, j * RET_DV:(j + 1) * RET_DV]], axis=0))
                full = full * cd_ref[p] + incs[c, p]
            state_s[p] = full
        yield

        for c in range(N_CHUNKS):
            rows, band, lt = chunk_rows[c], chunk_band[c], logits_t[c]

            for hd in range(RET_HEADS):
                o = ret_outs[c, hd]
                cols = slice(hd * RET_DV, (hd + 1) * RET_DV)
                for r0, bs in _row_blocks(CHUNK):
                    ob = o[bs, :]
                    rs = slice(c * CHUNK + r0, c * CHUNK + r0 + ROW_BLOCK)
                    mu = jnp.mean(ob, axis=-1, keepdims=True)
                    d = ob - mu
                    var = jnp.mean(d * d, axis=-1, keepdims=True)
                    y = d * lax.rsqrt(var + GN_EPS) * retw_ref[:, cols]
                    mixed_s[rs, cols] = (y * gate_s[rs, cols]).astype(BF16)
            p0_a, mid, c1_b = lt
            p_rows = [slice(i * Q_HALF, (i + 1) * Q_HALF) for i in range(4)]
            sink_terms = {}
            for half in range(2):
                for pair in range(SWA_HEADS // 2):
                    g, gp = pair // 2, pair % 2
                    cs = slice(pair * LANES, (pair + 1) * LANES)
                    cs_b = slice(4 * LANES + pair * LANES, 4 * LANES + (pair + 1) * LANES)
                    sink = jnp.where(first_head_lanes, sinks_ref[2 * pair], sinks_ref[2 * pair + 1]) * LOG2_E
                    bi = half * 4 + pair
                    bias = bias_s[bi + SWA_HEADS * first_i] if c == 0 else bias_s[bi]
                    if half == 0:
                        lg0 = jnp.where(tri_t, mid[Q_HALF:CHUNK, cs], p0_a[:, cs]) + bias[0:Q_HALF]
                        lg1 = mid[0:Q_HALF, cs] + bias[Q_HALF:CHUNK]
                    else:
                        lg0 = mid[Q_HALF:CHUNK, cs_b] + bias[0:Q_HALF]
                        lg1 = jnp.where(tri_t, c1_b[:, cs], mid[0:Q_HALF, cs_b]) + bias[Q_HALF:CHUNK]
                    m = jnp.maximum(jnp.maximum(jnp.max(lg0, axis=0, keepdims=True),
                                                jnp.max(lg1, axis=0, keepdims=True)), sink)
                    e0 = jnp.exp2(lg0 - m).astype(BF16)
                    e1 = jnp.exp2(lg1 - m).astype(BF16)
                    sink_terms[half, pair] = jnp.exp2(sink - m)
                    pq = slice(half * 2 * LANES + gp * LANES, half * 2 * LANES + (gp + 1) * LANES)
                    if half == 0:
                        pt_s[c, g, p_rows[0], pq] = e0 * prev_keep
                        pt_s[c, g, p_rows[2], pq] = e0 * cur_keep
                        pt_s[c, g, p_rows[1], pq] = e1
                    else:
                        pt_s[c, g, p_rows[2], pq] = e0
                        pt_s[c, g, p_rows[1], pq] = e1 * prev_keep
                        pt_s[c, g, p_rows[3], pq] = e1 * cur_keep
            yield

            for g in range(SWA_KV_HEADS):
                vt1 = jnp.concatenate([vt_s[g * SWA_HEAD_DIM:(g + 1) * SWA_HEAD_DIM, band], ones_rows], axis=0)
                ot = _dot(vt1, pt_s[c, g])
                assert ot.shape == (SWA_HEAD_DIM + SUBLANES, 4 * LANES)
                den = ot[SWA_HEAD_DIM:SWA_HEAD_DIM + 1, :] + jnp.concatenate(
                    [sink_terms[half, 2 * g + gp] for half in range(2) for gp in range(2)], axis=1)
                otn = ot[0:SWA_HEAD_DIM, :] * (1.0 / den)
                for hh in range(SWA_GROUP):
                    hd = SWA_GROUP * g + hh
                    hs = slice(hd * SWA_HEAD_DIM, (hd + 1) * SWA_HEAD_DIM)
                    o_h = jnp.concatenate([otn[:, hh * Q_HALF:(hh + 1) * Q_HALF],
                                           otn[:, 2 * LANES + hh * Q_HALF:2 * LANES + (hh + 1) * Q_HALF]], axis=1)
                    mixedt_s[hs, rows] = (o_h * gatet_s[hs, rows]).astype(BF16)
            yield
            if c % 2 == 1:
                out_proj(c)
                yield

        last = slice(SEQ_TILE, SEQ_TILE + CHUNK)
        next_slot[SLOT_KBAND][0:CHUNK, :] = kband_s[last, :]
        next_slot[SLOT_VT][:, 0:CHUNK] = vt_s[:, last]

    def run(parity, do_project=True, do_mix=True, stage_weights=False):
        streams = [project(slots[parity], h_bufs[parity], h_bufs[1 - parity], stage_weights) if do_project
                   else iter(()),
                   mix(slots[1 - parity], slots[parity]) if do_mix else iter(())]
        for which in MIX_ORDER + (0,) * len(MIX_ORDER) + (1,) * len(MIX_ORDER):
            next(streams[which], None)

    is_first, is_last = step == 0, step == n_tiles
    pl.when(is_first)(functools.partial(run, 0, do_mix=False, stage_weights=True))
    pl.when(is_first)(stage_w_out)
    pl.when(is_last)(functools.partial(run, n_tiles % 2, do_project=False))
    for parity in range(2):
        pl.when((step % 2 == parity) & jnp.logical_not(is_first | is_last))(functools.partial(run, parity))


def kernel(x, norm_w, w_in, ret_norm_w, q_norm_w, k_norm_w, sinks, rel_bias, w_out):
    batch, seq, d_model = x.shape
    assert d_model == D_MODEL and seq % SEQ_TILE == 0
    assert w_in.shape == (D_MODEL, D_IN) and w_out.shape == (D_MIX, D_MODEL)
    tb = _constant_tables(seq)
    tiles_per_seq = seq // SEQ_TILE
    n_tiles = batch * tiles_per_seq

    def proj_tile(s):
        return jnp.minimum(s, n_tiles - 1)

    def mix_tile(s):
        return jnp.maximum(s - 1, 0)

    const2 = lambda s: (0, 0)
    const3 = lambda s: (0, 0, 0)
    const4 = lambda s: (0, 0, 0, 0)
    pos_spec = pl.BlockSpec((seq, LANES), const2, pipeline_mode=pl.Buffered(1))
    smem = pl.BlockSpec(memory_space=pltpu.SMEM)
    x_block = (1, SEQ_TILE, D_MODEL)
    mix_map = lambda s: (mix_tile(s) // tiles_per_seq, mix_tile(s) % tiles_per_seq, 0)

    in_specs = [
        pl.BlockSpec(x_block, lambda s: (proj_tile(s + 1) // tiles_per_seq, proj_tile(s + 1) % tiles_per_seq, 0)),
        pl.BlockSpec(x_block, mix_map),
        pl.BlockSpec(memory_space=pl.ANY),
        pl.BlockSpec((1, D_MODEL), const2),
        pl.BlockSpec((1, D_IN), const2),
        pl.BlockSpec(memory_space=pl.ANY),
        pl.BlockSpec(memory_space=pl.ANY),
        pos_spec, pos_spec,
        pl.BlockSpec((2, CHUNK, 2 * CHUNK), const3),
        pl.BlockSpec((2, 2, CHUNK, LANES), const4),
        pl.BlockSpec((2, CHUNK, LANES), const3),
        pl.BlockSpec((2, 1, 2 * RET_DV), const3),
        pl.BlockSpec((1, RET_HEADS * RET_DV), const2),
        pl.BlockSpec((1, SWA_HEAD_DIM), const2),
        pl.BlockSpec((1, SWA_HEAD_DIM), const2),
        pl.BlockSpec((2, CHUNK, LANES), const3),
        smem, smem,
    ]
    slot_bufs = [
        pltpu.VMEM((2, SEQ_TILE, LANES), BF16),
        pltpu.VMEM((2, 2, SEQ_TILE, LANES), BF16),
        pltpu.VMEM((2, N_CHUNKS, 2 * CHUNK, LANES), BF16),
        pltpu.VMEM((2, SEQ_TILE, LANES), BF16),
        pltpu.VMEM((SEQ_TILE, RET_HEADS * RET_DV), BF16),
        pltpu.VMEM((SEQ_TILE, RET_HEADS * RET_DV), F32),
        pltpu.VMEM((N_CHUNKS, 2, SWA_KV_HEADS, SWA_HEAD_DIM, 2 * LANES), BF16),
        pltpu.VMEM((SEQ_TILE + CHUNK, LANES), BF16),
        pltpu.VMEM((SWA_KV_HEADS * SWA_HEAD_DIM, SEQ_TILE + CHUNK), BF16),
        pltpu.VMEM((SWA_HEADS * SWA_HEAD_DIM, SEQ_TILE), F32),
    ]
    assert len(slot_bufs) == N_SLOT_BUFS
    h_buf = pltpu.VMEM((SEQ_TILE, D_MODEL), BF16)
    scratch = [h_buf, h_buf] + slot_bufs + slot_bufs + [
        pltpu.VMEM((N_CHUNKS, SWA_KV_HEADS, 2 * CHUNK, SWA_GROUP * CHUNK), BF16),
        pltpu.VMEM((SEQ_TILE, RET_HEADS * RET_DV), BF16),
        pltpu.VMEM((SWA_HEADS * SWA_HEAD_DIM, SEQ_TILE), BF16),
        pltpu.VMEM((2, 2 * RET_DK, 2 * RET_DV), F32),
        pltpu.VMEM((2 * SWA_HEADS, CHUNK, LANES), F32),
        pltpu.VMEM((D_MODEL, D_TOK), BF16),
        pltpu.VMEM((D_FT, D_MODEL), BF16),
        pltpu.VMEM((D_MIX, D_MODEL), BF16),
        pltpu.VMEM((2, D_MODEL, 4 * LANES), F32),
        pltpu.VMEM((D_MIX // STAGE_ROWS, STAGE_ROWS, D_MODEL), F32),
        pltpu.VMEM((SEQ_TILE // STAGE_ROWS, STAGE_ROWS, D_MODEL), F32),
        pltpu.SemaphoreType.DMA(((SEQ_TILE + D_MIX) // STAGE_ROWS,)),
        pltpu.SemaphoreType.DMA((2,)),
    ]
    return pl.pallas_call(
        functools.partial(_layer_kernel, tiles_per_seq, n_tiles),
        grid=(n_tiles + 1,),
        in_specs=in_specs,
        out_specs=pl.BlockSpec(x_block, mix_map),
        out_shape=jax.ShapeDtypeStruct(x.shape, x.dtype),
        scratch_shapes=scratch,
        compiler_params=pltpu.CompilerParams(
            dimension_semantics=("arbitrary",),
            vmem_limit_bytes=VMEM_LIMIT_BYTES),
    )(x, x, x, norm_w.reshape(1, D_MODEL), tb["col_scale"], w_in, w_out,
      tb["cos"], tb["sin_s"], tb["dec"], tb["xi"], tb["zeta"], tb["cd"],
      ret_norm_w.reshape(1, -1), q_norm_w.reshape(1, -1), k_norm_w.reshape(1, -1), tb["bucket"],
      rel_bias.astype(F32), sinks.astype(F32))
```

```python
import functools
import math

import numpy as np
import jax
import jax.numpy as jnp
from jax import lax
from jax.experimental import pallas as pl
from jax.experimental.pallas import tpu as pltpu

D_MODEL = 1024
RET_HEADS = 4
RET_DK = 64
RET_DV = 128
ROT_HALF = RET_DK // 2
CHUNK = 128
Q_HALF = CHUNK // 2
RET_ROPE_BASE = 10000.0
SWA_HEADS = 8
SWA_KV_HEADS = 2
SWA_HEAD_DIM = 64
SWA_GROUP = SWA_HEADS // SWA_KV_HEADS
NUM_BUCKETS = 32
MAX_DISTANCE = 128
NORM_EPS = 1e-6
GN_EPS = 1e-5
NEG_INF = -1e30
LOG2_E = math.log2(math.e)

OFF_RQ, OFF_RK, OFF_RV, OFF_RG = 0, 256, 512, 1024
D_TOK = 1536
FT_Q, FT_K, FT_G = 0, 512, 768
D_FT = 1280
D_IN = D_TOK + D_FT
D_MIX = 1024

LANES = 128
SUBLANES = 8
SEQ_TILE = 512
N_CHUNKS = SEQ_TILE // CHUNK
ROW_BLOCK = 32
VMEM_LIMIT_BYTES = 56 * 1024 * 1024
MIX_ORDER = (1, 1, 0, 0, 1, 1, 0, 0, 1, 1, 0, 0, 1, 0, 1, 1, 0, 0, 1, 1, 0, 0, 1, 0, 0)
NORM_AFTER_GROUP = 2
N_SLOT_BUFS = 10
STAGE_ROWS = 128
SLAB_GROUPS = ((OFF_RQ, 4 * LANES), (D_TOK + FT_Q, 4 * LANES), (OFF_RG, 4 * LANES), (D_TOK + FT_G, 4 * LANES),
               (D_TOK + FT_K, 2 * LANES), (OFF_RV, 4 * LANES))
SLOT_KBAND, SLOT_VT = 7, 8

BF16 = jnp.bfloat16
F32 = jnp.float32


def _t5_bucket_np(n):
    max_exact = NUM_BUCKETS // 2
    nf = np.maximum(n, 1).astype(np.float64)
    large = max_exact + (np.log(nf / max_exact) / math.log(MAX_DISTANCE / max_exact)
                         * (NUM_BUCKETS - max_exact)).astype(np.int32)
    large = np.minimum(large, NUM_BUCKETS - 1)
    return np.where(n < max_exact, n, large).astype(np.int32)


def _constant_tables(seq):
    half = RET_DK // 2
    inv_freq = RET_ROPE_BASE ** (-np.arange(half, dtype=np.float64) / half)
    ang = np.arange(seq, dtype=np.float64)[:, None] * inv_freq[None, :]
    cos, sin = np.cos(ang), np.sin(ang)
    cos_t = np.tile(np.concatenate([cos, cos], axis=1), (1, 2))
    sin_s = np.tile(np.concatenate([-sin, sin], axis=1), (1, 2))

    gamma = 1.0 - np.exp2(-5.0 - np.arange(RET_HEADS, dtype=np.float64))
    log_g = np.log(gamma)
    i = np.arange(CHUNK, dtype=np.float64)
    diff = i[:, None] - i[None, :]
    k_scale = RET_DK ** -0.5
    decay = np.where(diff >= 0, np.exp(log_g[:, None, None] * np.maximum(diff, 0.0)), 0.0) * k_scale
    dec = np.stack([np.concatenate([decay[2 * p], decay[2 * p + 1]], axis=1) for p in range(2)])
    xi = np.exp(log_g[:, None] * (i + 1.0))
    zeta = np.exp(log_g[:, None] * (CHUNK - 1.0 - i)) * k_scale

    def per_lane(t):
        return np.stack([np.concatenate([np.repeat(t[2 * p][:, None], RET_DK, 1),
                                         np.repeat(t[2 * p + 1][:, None], RET_DK, 1)], axis=1)
                         for p in range(2)])

    lo = (np.arange(LANES) < RET_DK)[None, None, :]
    xi_l = per_lane(xi)
    xi_m = np.stack([np.where(lo, xi_l, 0.0), np.where(lo, 0.0, xi_l)], axis=1)

    chunk_decay = np.exp(log_g * CHUNK)
    cd = np.stack([np.concatenate([np.full((1, RET_DV), chunk_decay[2 * p]),
                                   np.full((1, RET_DV), chunk_decay[2 * p + 1])], axis=1)
                   for p in range(2)])

    j = np.arange(CHUNK)[None, :, None]
    q = (np.arange(2)[:, None, None] * Q_HALF + np.arange(LANES)[None, None, :] % Q_HALF)
    dist = np.where(j <= q, q - j, q + CHUNK - j)
    bucket = _t5_bucket_np(dist)

    f = lambda a: jnp.asarray(a, dtype=F32)
    col_scale = np.ones((1, D_IN))
    col_scale[:, OFF_RG:OFF_RG + RET_HEADS * RET_DV] = 0.5
    col_scale[:, D_TOK + FT_G:D_TOK + FT_G + SWA_HEADS * SWA_HEAD_DIM] = 0.5

    return dict(cos=f(cos_t), sin_s=f(sin_s), col_scale=f(col_scale), dec=f(dec), xi=f(xi_m),
                zeta=f(per_lane(zeta)), cd=f(cd), bucket=jnp.asarray(bucket, dtype=jnp.int32))


def _dot(a, b):
    return jnp.dot(a, b, preferred_element_type=F32)


def _dot_nt(a, b):
    return lax.dot_general(a, b, (((1,), (1,)), ((), ())), preferred_element_type=F32)


def _dot_tn(a, b):
    return lax.dot_general(a, b, (((0,), (0,)), ((), ())), preferred_element_type=F32)


def _silu_of_half(hg):
    return hg + hg * jnp.tanh(hg)


def _row_blocks(n_rows):
    return [(r0, slice(r0, r0 + ROW_BLOCK)) for r0 in range(0, n_rows, ROW_BLOCK)]


def _layer_kernel(tiles_per_seq, n_tiles,
                  xnext_ref, xres_ref, x_hbm, nwrow_ref, colscale_ref, win_hbm, wout_hbm, cos_ref, sins_ref, dec_ref,
                  xi_ref, zeta_ref, cd_ref, retw_ref, qnw_ref, knw_ref, bkt_ref, relb_ref,
                  sinks_ref, o_ref, h0_s, h1_s, *scratch):
    h_bufs = (h0_s, h1_s)
    slots = (scratch[0:N_SLOT_BUFS], scratch[N_SLOT_BUFS:2 * N_SLOT_BUFS])
    (pt_s, mixed_s, mixedt_s, state_s, bias_s,
     win_ref, wft_ref, wout_ref, slab_s, stage_out, stage_x, stage_sem, slab_sem) = scratch[2 * N_SLOT_BUFS:]
    step = pl.program_id(0)
    mix_tile = jnp.maximum(step - 1, 0)
    pos0 = pl.multiple_of((jnp.minimum(step, n_tiles - 1) % tiles_per_seq) * SEQ_TILE, SEQ_TILE)
    first = mix_tile % tiles_per_seq == 0
    first_i = first.astype(jnp.int32)
    keep_state = jnp.where(first, 0.0, 1.0).astype(F32)

    def rms_norm_rows(xb):
        ms = jnp.mean(xb * xb, axis=-1, keepdims=True)
        return (xb * lax.rsqrt(ms + NORM_EPS) * nwrow_ref[...]).astype(BF16)

    def row_copies(hbm, stage, sem0, n_rows):
        return [pltpu.make_async_copy(hbm.at[pl.ds(i * STAGE_ROWS, STAGE_ROWS), :], stage.at[i],
                                      stage_sem.at[sem0 + i]) for i in range(n_rows // STAGE_ROWS)]

    def slab_copy(i):
        col0, width = SLAB_GROUPS[i]
        return pltpu.make_async_copy(win_hbm.at[:, pl.ds(col0, width)], slab_s.at[i % 2, :, pl.ds(0, width)],
                                     slab_sem.at[i % 2])

    def stage_group(i):
        col0, width = SLAB_GROUPS[i]
        slab_copy(i).wait()
        for r0 in range(0, D_MODEL, STAGE_ROWS):
            rows = slice(r0, r0 + STAGE_ROWS)
            for c0 in range(0, width, LANES):
                blk = slab_s[i % 2, rows, c0:c0 + LANES] * colscale_ref[:, col0 + c0:col0 + c0 + LANES]
                if col0 < D_TOK:
                    win_ref[rows, col0 + c0:col0 + c0 + LANES] = blk.astype(BF16)
                else:
                    wft_ref[col0 - D_TOK + c0:col0 - D_TOK + c0 + LANES, rows] = blk.T.astype(BF16)
        if i + 2 < len(SLAB_GROUPS):
            slab_copy(i + 2).start()

    def stage_w_out():
        for i, cp in enumerate(row_copies(wout_hbm, stage_out, SEQ_TILE // STAGE_ROWS, D_MIX)):
            cp.wait()
            wout_ref[i * STAGE_ROWS:(i + 1) * STAGE_ROWS, :] = stage_out[i].astype(BF16)

    @pl.when(step == 0)
    def _init():
        def build_tables():
            key = lax.broadcasted_iota(jnp.int32, (CHUNK, LANES), 0)
            lane_c = lax.broadcasted_iota(jnp.int32, (CHUNK, LANES), 1)
            for half in range(2):
                bk = bkt_ref[half]
                is_cur = key <= half * Q_HALF + lane_c % Q_HALF
                for pair in range(SWA_HEADS // 2):
                    acc = jnp.zeros((CHUNK, LANES), F32)
                    for u in range(NUM_BUCKETS):
                        val = jnp.where(lane_c < Q_HALF, relb_ref[u, 2 * pair], relb_ref[u, 2 * pair + 1]) * LOG2_E
                        acc = jnp.where(bk == u, val, acc)
                    bias_s[half * 4 + pair] = acc
                    bias_s[SWA_HEADS + half * 4 + pair] = jnp.where(is_cur, acc, NEG_INF)
            zeros_blk = jnp.zeros((N_CHUNKS, SWA_KV_HEADS, Q_HALF, 2 * LANES), BF16)
            pt_s[:, :, 3 * Q_HALF:4 * Q_HALF, 0:2 * LANES] = zeros_blk
            pt_s[:, :, 0:Q_HALF, 2 * LANES:4 * LANES] = zeros_blk
            state_s[...] = jnp.zeros_like(state_s)
            slots[0][SLOT_KBAND][0:CHUNK, :] = jnp.zeros((CHUNK, LANES), BF16)
            slots[0][SLOT_VT][:, 0:CHUNK] = jnp.zeros((SWA_KV_HEADS * SWA_HEAD_DIM, CHUNK), BF16)

        x_copies = row_copies(x_hbm.at[0], stage_x, 0, SEQ_TILE)
        for cp in x_copies + row_copies(wout_hbm, stage_out, len(x_copies), D_MIX) + [slab_copy(0), slab_copy(1)]:
            cp.start()
        build_tables()
        for i, cp in enumerate(x_copies):
            cp.wait()
            for r0, rs in _row_blocks(STAGE_ROWS):
                h0_s[i * STAGE_ROWS + r0:i * STAGE_ROWS + r0 + ROW_BLOCK, :] = rms_norm_rows(stage_x[i, rs, :])

    lane = lax.broadcasted_iota(jnp.int32, (ROW_BLOCK, LANES), 1)
    lo = lane < RET_DK
    first_half = lane % RET_DK < ROT_HALF

    def project(slot, h_s, h_next_s, stage_weights=False):
        rq_s, rqx_s, rk_s, rkz_s, rv_s, gate_s, qt_s, kband_s, vt_s, gatet_s = slot

        def proj(off):
            return _dot(h_s[...], win_ref[:, off:off + 4 * LANES])

        def proj_t(row0, n_rows):
            return _dot_nt(wft_ref[row0:row0 + n_rows, :], h_s[...])

        def rot(v, rs):
            pos = pl.ds(pos0 + rs.start, ROW_BLOCK)
            partner = jnp.where(first_half, pltpu.roll(v, LANES - ROT_HALF, 1), pltpu.roll(v, ROT_HALF, 1))
            return v * cos_ref[pos, :] + partner * sins_ref[pos, :]

        def head_rms(xt):
            ms = jnp.mean(xt * xt, axis=0, keepdims=True)
            return xt * lax.rsqrt(ms + NORM_EPS)

        tok_blocks = [slice(c0, c0 + LANES) for c0 in range(0, SEQ_TILE, LANES)]

        def post_rqk(r):
            for r0, rs in _row_blocks(SEQ_TILE):
                c, cr = r0 // CHUNK, r0 % CHUNK
                cs = slice(cr, cr + ROW_BLOCK)
                for p in range(2):
                    q = rot(r[rs, p * LANES:(p + 1) * LANES], rs)
                    rq_s[p, rs, :] = q.astype(BF16)
                    rqx_s[p, 0, rs, :] = (q * xi_ref[p, 0, cs, :]).astype(BF16)
                    rqx_s[p, 1, rs, :] = (q * xi_ref[p, 1, cs, :]).astype(BF16)
                    k = rot(r[rs, (2 + p) * LANES:(3 + p) * LANES], rs)
                    rk_s[p, c, cr:cr + ROW_BLOCK, :] = jnp.where(lo, k, 0.0).astype(BF16)
                    rk_s[p, c, CHUNK + cr:CHUNK + cr + ROW_BLOCK, :] = jnp.where(lo, 0.0, k).astype(BF16)
                    rkz_s[p, rs, :] = (k * zeta_ref[p, cs, :]).astype(BF16)

        def post_rv(r):
            for r0, rs in _row_blocks(SEQ_TILE):
                rv_s[rs, :] = r[rs, :].astype(BF16)

        def post_rg(r):
            for r0, rs in _row_blocks(SEQ_TILE):
                gate_s[rs, :] = _silu_of_half(r[rs, :])

        def post_qt(rt):
            for pair in range(SWA_HEADS // 2):
                g, gp = pair // 2, pair % 2
                for c, ts in enumerate(tok_blocks):
                    xe, xo = (head_rms(rt[hd * SWA_HEAD_DIM:(hd + 1) * SWA_HEAD_DIM, ts])
                              for hd in (2 * pair, 2 * pair + 1))
                    for half in range(2):
                        hq = slice(half * Q_HALF, (half + 1) * Q_HALF)
                        qt_s[c, half, g, :, gp * LANES:(gp + 1) * LANES] = jnp.concatenate(
                            [xe[:, hq], xo[:, hq]], axis=1).astype(BF16)

        def post_kvt(rt):
            qk_head = qnw_ref[...] * knw_ref[...] * (SWA_HEAD_DIM ** -0.5 * LOG2_E)
            qk_w = jnp.concatenate([qk_head] * SWA_KV_HEADS, axis=1)
            for ts in tok_blocks:
                knt = jnp.concatenate(
                    [head_rms(rt[g * SWA_HEAD_DIM:(g + 1) * SWA_HEAD_DIM, ts]) for g in range(SWA_KV_HEADS)],
                    axis=0)
                kband_s[CHUNK + ts.start:CHUNK + ts.stop, :] = (knt.T * qk_w).astype(BF16)
                vt_s[:, CHUNK + ts.start:CHUNK + ts.stop] = rt[LANES:2 * LANES, ts].astype(BF16)

        def post_gt(rt):
            for r0 in range(0, SWA_HEADS * SWA_HEAD_DIM, ROW_BLOCK):
                for ts in tok_blocks:
                    gatet_s[r0:r0 + ROW_BLOCK, ts] = _silu_of_half(rt[r0:r0 + ROW_BLOCK, ts])

        groups = [
            (functools.partial(proj, OFF_RQ), post_rqk),
            (functools.partial(proj_t, FT_Q, SWA_HEADS * SWA_HEAD_DIM), post_qt),
            (functools.partial(proj, OFF_RG), post_rg),
            (functools.partial(proj_t, FT_G, SWA_HEADS * SWA_HEAD_DIM), post_gt),
            (functools.partial(proj_t, FT_K, 2 * LANES), post_kvt),
            (functools.partial(proj, OFF_RV), post_rv),
        ]
        pending = None
        for i, (matmul, post) in enumerate(groups):
            if stage_weights:
                stage_group(i)
            result = matmul()
            yield
            if pending is not None:
                pending[0](pending[1])
                yield
            pending = (post, result)
            if i == NORM_AFTER_GROUP:
                for r0, rs in _row_blocks(SEQ_TILE):
                    h_next_s[rs, :] = rms_norm_rows(xnext_ref[0, rs, :])
                yield
        pending[0](pending[1])
        yield

    def mix(slot, next_slot):
        rq_s, rqx_s, rk_s, rkz_s, rv_s, gate_s, qt_s, kband_s, vt_s, gatet_s = slot
        key = lax.broadcasted_iota(jnp.int32, (Q_HALF, LANES), 0)
        lane_h = lax.broadcasted_iota(jnp.int32, (Q_HALF, LANES), 1)
        tri_t = key <= lane_h % Q_HALF
        cur_keep = jnp.where(tri_t, 1.0, 0.0).astype(BF16)
        prev_keep = jnp.where(tri_t, 0.0, 1.0).astype(BF16)
        first_head_lanes = lax.broadcasted_iota(jnp.int32, (1, LANES), 1) < Q_HALF
        zero_q = jnp.zeros((SWA_HEAD_DIM, 2 * LANES), BF16)
        ones_rows = jnp.ones((SUBLANES, 2 * CHUNK), BF16)

        def out_proj(c_hi):
            pair = slice((c_hi - 1) * CHUNK, (c_hi + 1) * CHUNK)
            o_ref[0, pair, :] = (xres_ref[0, pair, :]
                                 + _dot(mixed_s[pair, :], wout_ref[0:RET_HEADS * RET_DV, :])
                                 + _dot_tn(mixedt_s[:, pair], wout_ref[RET_HEADS * RET_DV:D_MIX, :]))

        chunk_rows = [slice(c * CHUNK, (c + 1) * CHUNK) for c in range(N_CHUNKS)]
        chunk_band = [slice(c * CHUNK, (c + 2) * CHUNK) for c in range(N_CHUNKS)]

        def logits_for(c):
            rhs_a, rhs_b = (jnp.concatenate([jnp.concatenate([qt_s[c, half, 0], zero_q], axis=1),
                                             jnp.concatenate([zero_q, qt_s[c, half, 1]], axis=1)], axis=0)
                            for half in range(2))
            k0 = c * CHUNK
            lt_a = _dot(kband_s[k0:k0 + 3 * Q_HALF, :], rhs_a)
            lt_b = _dot(kband_s[k0 + Q_HALF:k0 + 4 * Q_HALF, :], rhs_b)
            return lt_a, lt_b

        scores, incs, logits_t = {}, {}, []
        for c, rows in enumerate(chunk_rows):
            for p in range(2):
                s2 = _dot_nt(rq_s[p, rows, :], rk_s[p, c])
                scores[c, p] = jnp.concatenate([(s2[bs, :] * dec_ref[p, bs, :]).astype(BF16)
                                                for _, bs in _row_blocks(CHUNK)], axis=0)
                incs[c, p] = _dot_tn(rkz_s[p, rows, :], rv_s[rows, p * 2 * RET_DV:(p + 1) * 2 * RET_DV])
            logits_t.append(logits_for(c))
        yield

        ret_outs = {}
        for p in range(2):
            full = state_s[p] * keep_state
            for c, rows in enumerate(chunk_rows):
                full_b = full.astype(BF16)
                v2 = rv_s[rows, p * 2 * RET_DV:(p + 1) * 2 * RET_DV]
                s2 = scores[c, p]
                for j in range(2):
                    ret_outs[c, 2 * p + j] = _dot(
                        jnp.concatenate([s2[:, j * CHUNK:(j + 1) * CHUNK], rqx_s[p, j, rows, :]], axis=1),
                        jnp.concatenate([v2[:, j * RET_DV:(j + 1) * RET_DV],
                                         full_b[:, j * RET_DV:(j + 1) * RET_DV]], axis=0))
                full = full * cd_ref[p] + incs[c, p]
            state_s[p] = full
        yield

        for c in range(N_CHUNKS):
            rows, band, lt = chunk_rows[c], chunk_band[c], logits_t[c]

            for hd in range(RET_HEADS):
                o = ret_outs[c, hd]
                cols = slice(hd * RET_DV, (hd + 1) * RET_DV)
                for r0, bs in _row_blocks(CHUNK):
                    ob = o[bs, :]
                    rs = slice(c * CHUNK + r0, c * CHUNK + r0 + ROW_BLOCK)
                    mu = jnp.mean(ob, axis=-1, keepdims=True)
                    d = ob - mu
                    var = jnp.mean(d * d, axis=-1, keepdims=True)
                    y = d * lax.rsqrt(var + GN_EPS) * retw_ref[:, cols]
                    mixed_s[rs, cols] = (y * gate_s[rs, cols]).astype(BF16)
            lt_a, lt_b = lt
            blk = [slice(i * Q_HALF, (i + 1) * Q_HALF) for i in range(3)]
            p_rows = [slice(i * Q_HALF, (i + 1) * Q_HALF) for i in range(4)]
            sink_terms = {}
            for half in range(2):
                for pair in range(SWA_HEADS // 2):
                    g, gp = pair // 2, pair % 2
                    cs = slice(pair * LANES, (pair + 1) * LANES)
                    sink = jnp.where(first_head_lanes, sinks_ref[2 * pair], sinks_ref[2 * pair + 1]) * LOG2_E
                    bi = half * 4 + pair
                    bias = bias_s[bi + SWA_HEADS * first_i] if c == 0 else bias_s[bi]
                    if half == 0:
                        lg0 = jnp.where(tri_t, lt_a[blk[2], cs], lt_a[blk[0], cs]) + bias[0:Q_HALF]
                        lg1 = lt_a[blk[1], cs] + bias[Q_HALF:CHUNK]
                    else:
                        lg0 = lt_b[blk[1], cs] + bias[0:Q_HALF]
                        lg1 = jnp.where(tri_t, lt_b[blk[2], cs], lt_b[blk[0], cs]) + bias[Q_HALF:CHUNK]
                    m = jnp.maximum(jnp.maximum(jnp.max(lg0, axis=0, keepdims=True),
                                                jnp.max(lg1, axis=0, keepdims=True)), sink)
                    e0 = jnp.exp2(lg0 - m).astype(BF16)
                    e1 = jnp.exp2(lg1 - m).astype(BF16)
                    sink_terms[half, pair] = jnp.exp2(sink - m)
                    pq = slice(half * 2 * LANES + gp * LANES, half * 2 * LANES + (gp + 1) * LANES)
                    if half == 0:
                        pt_s[c, g, p_rows[0], pq] = e0 * prev_keep
                        pt_s[c, g, p_rows[2], pq] = e0 * cur_keep
                        pt_s[c, g, p_rows[1], pq] = e1
                    else:
                        pt_s[c, g, p_rows[2], pq] = e0
                        pt_s[c, g, p_rows[1], pq] = e1 * prev_keep
                        pt_s[c, g, p_rows[3], pq] = e1 * cur_keep
            yield

            for g in range(SWA_KV_HEADS):
                vt1 = jnp.concatenate([vt_s[g * SWA_HEAD_DIM:(g + 1) * SWA_HEAD_DIM, band], ones_rows], axis=0)
                ot = _dot(vt1, pt_s[c, g])
                assert ot.shape == (SWA_HEAD_DIM + SUBLANES, 4 * LANES)
                den = ot[SWA_HEAD_DIM:SWA_HEAD_DIM + 1, :] + jnp.concatenate(
                    [sink_terms[half, 2 * g + gp] for half in range(2) for gp in range(2)], axis=1)
                otn = ot[0:SWA_HEAD_DIM, :] * (1.0 / den)
                for hh in range(SWA_GROUP):
                    hd = SWA_GROUP * g + hh
                    hs = slice(hd * SWA_HEAD_DIM, (hd + 1) * SWA_HEAD_DIM)
                    o_h = jnp.concatenate([otn[:, hh * Q_HALF:(hh + 1) * Q_HALF],
                                           otn[:, 2 * LANES + hh * Q_HALF:2 * LANES + (hh + 1) * Q_HALF]], axis=1)
                    mixedt_s[hs, rows] = (o_h * gatet_s[hs, rows]).astype(BF16)
            yield
            if c % 2 == 1:
                out_proj(c)
                yield

        last = slice(SEQ_TILE, SEQ_TILE + CHUNK)
        next_slot[SLOT_KBAND][0:CHUNK, :] = kband_s[last, :]
        next_slot[SLOT_VT][:, 0:CHUNK] = vt_s[:, last]

    def run(parity, do_project=True, do_mix=True, stage_weights=False):
        streams = [project(slots[parity], h_bufs[parity], h_bufs[1 - parity], stage_weights) if do_project
                   else iter(()),
                   mix(slots[1 - parity], slots[parity]) if do_mix else iter(())]
        for which in MIX_ORDER + (0,) * len(MIX_ORDER) + (1,) * len(MIX_ORDER):
            next(streams[which], None)

    is_first, is_last = step == 0, step == n_tiles
    pl.when(is_first)(functools.partial(run, 0, do_mix=False, stage_weights=True))
    pl.when(is_first)(stage_w_out)
    pl.when(is_last)(functools.partial(run, n_tiles % 2, do_project=False))
    for parity in range(2):
        pl.when((step % 2 == parity) & jnp.logical_not(is_first | is_last))(functools.partial(run, parity))


def kernel(x, norm_w, w_in, ret_norm_w, q_norm_w, k_norm_w, sinks, rel_bias, w_out):
    batch, seq, d_model = x.shape
    assert d_model == D_MODEL and seq % SEQ_TILE == 0
    assert w_in.shape == (D_MODEL, D_IN) and w_out.shape == (D_MIX, D_MODEL)
    tb = _constant_tables(seq)
    tiles_per_seq = seq // SEQ_TILE
    n_tiles = batch * tiles_per_seq

    def proj_tile(s):
        return jnp.minimum(s, n_tiles - 1)

    def mix_tile(s):
        return jnp.maximum(s - 1, 0)

    const2 = lambda s: (0, 0)
    const3 = lambda s: (0, 0, 0)
    const4 = lambda s: (0, 0, 0, 0)
    pos_spec = pl.BlockSpec((seq, LANES), const2, pipeline_mode=pl.Buffered(1))
    smem = pl.BlockSpec(memory_space=pltpu.SMEM)
    x_block = (1, SEQ_TILE, D_MODEL)
    mix_map = lambda s: (mix_tile(s) // tiles_per_seq, mix_tile(s) % tiles_per_seq, 0)

    in_specs = [
        pl.BlockSpec(x_block, lambda s: (proj_tile(s + 1) // tiles_per_seq, proj_tile(s + 1) % tiles_per_seq, 0)),
        pl.BlockSpec(x_block, mix_map),
        pl.BlockSpec(memory_space=pl.ANY),
        pl.BlockSpec((1, D_MODEL), const2),
        pl.BlockSpec((1, D_IN), const2),
        pl.BlockSpec(memory_space=pl.ANY),
        pl.BlockSpec(memory_space=pl.ANY),
        pos_spec, pos_spec,
        pl.BlockSpec((2, CHUNK, 2 * CHUNK), const3),
        pl.BlockSpec((2, 2, CHUNK, LANES), const4),
        pl.BlockSpec((2, CHUNK, LANES), const3),
        pl.BlockSpec((2, 1, 2 * RET_DV), const3),
        pl.BlockSpec((1, RET_HEADS * RET_DV), const2),
        pl.BlockSpec((1, SWA_HEAD_DIM), const2),
        pl.BlockSpec((1, SWA_HEAD_DIM), const2),
        pl.BlockSpec((2, CHUNK, LANES), const3),
        smem, smem,
    ]
    slot_bufs = [
        pltpu.VMEM((2, SEQ_TILE, LANES), BF16),
        pltpu.VMEM((2, 2, SEQ_TILE, LANES), BF16),
        pltpu.VMEM((2, N_CHUNKS, 2 * CHUNK, LANES), BF16),
        pltpu.VMEM((2, SEQ_TILE, LANES), BF16),
        pltpu.VMEM((SEQ_TILE, RET_HEADS * RET_DV), BF16),
        pltpu.VMEM((SEQ_TILE, RET_HEADS * RET_DV), F32),
        pltpu.VMEM((N_CHUNKS, 2, SWA_KV_HEADS, SWA_HEAD_DIM, 2 * LANES), BF16),
        pltpu.VMEM((SEQ_TILE + CHUNK, LANES), BF16),
        pltpu.VMEM((SWA_KV_HEADS * SWA_HEAD_DIM, SEQ_TILE + CHUNK), BF16),
        pltpu.VMEM((SWA_HEADS * SWA_HEAD_DIM, SEQ_TILE), F32),
    ]
    assert len(slot_bufs) == N_SLOT_BUFS
    h_buf = pltpu.VMEM((SEQ_TILE, D_MODEL), BF16)
    scratch = [h_buf, h_buf] + slot_bufs + slot_bufs + [
        pltpu.VMEM((N_CHUNKS, SWA_KV_HEADS, 2 * CHUNK, SWA_GROUP * CHUNK), BF16),
        pltpu.VMEM((SEQ_TILE, RET_HEADS * RET_DV), BF16),
        pltpu.VMEM((SWA_HEADS * SWA_HEAD_DIM, SEQ_TILE), BF16),
        pltpu.VMEM((2, 2 * RET_DK, 2 * RET_DV), F32),
        pltpu.VMEM((2 * SWA_HEADS, CHUNK, LANES), F32),
        pltpu.VMEM((D_MODEL, D_TOK), BF16),
        pltpu.VMEM((D_FT, D_MODEL), BF16),
        pltpu.VMEM((D_MIX, D_MODEL), BF16),
        pltpu.VMEM((2, D_MODEL, 4 * LANES), F32),
        pltpu.VMEM((D_MIX // STAGE_ROWS, STAGE_ROWS, D_MODEL), F32),
        pltpu.VMEM((SEQ_TILE // STAGE_ROWS, STAGE_ROWS, D_MODEL), F32),
        pltpu.SemaphoreType.DMA(((SEQ_TILE + D_MIX) // STAGE_ROWS,)),
        pltpu.SemaphoreType.DMA((2,)),
    ]
    return pl.pallas_call(
        functools.partial(_layer_kernel, tiles_per_seq, n_tiles),
        grid=(n_tiles + 1,),
        in_specs=in_specs,
        out_specs=pl.BlockSpec(x_block, mix_map),
        out_shape=jax.ShapeDtypeStruct(x.shape, x.dtype),
        scratch_shapes=scratch,
        compiler_params=pltpu.CompilerParams(
            dimension_semantics=("arbitrary",),
            vmem_limit_bytes=VMEM_LIMIT_BYTES),
    )(x, x, x, norm_w.reshape(1, D_MODEL), tb["col_scale"], w_in, w_out,
      tb["cos"], tb["sin_s"], tb["dec"], tb["xi"], tb["zeta"], tb["cd"],
      ret_norm_w.reshape(1, -1), q_norm_w.reshape(1, -1), k_norm_w.reshape(1, -1), tb["bucket"],
      rel_bias.astype(F32), sinks.astype(F32))
```

```python
import functools
import math

import numpy as np
import jax
import jax.numpy as jnp
from jax import lax
from jax.experimental import pallas as pl
from jax.experimental.pallas import tpu as pltpu

D_MODEL = 1024
RET_HEADS = 4
RET_DK = 64
RET_DV = 128
ROT_HALF = RET_DK // 2
CHUNK = 128
Q_HALF = CHUNK // 2
RET_ROPE_BASE = 10000.0
SWA_HEADS = 8
SWA_KV_HEADS = 2
SWA_HEAD_DIM = 64
SWA_GROUP = SWA_HEADS // SWA_KV_HEADS
NUM_BUCKETS = 32
MAX_DISTANCE = 128
NORM_EPS = 1e-6
GN_EPS = 1e-5
NEG_INF = -1e30
LOG2_E = math.log2(math.e)

OFF_RQ, OFF_RK, OFF_RV, OFF_RG = 0, 256, 512, 1024
D_TOK = 1536
FT_Q, FT_K, FT_G = 0, 512, 768
D_FT = 1280
D_IN = D_TOK + D_FT
D_MIX = 1024

LANES = 128
SUBLANES = 8
SEQ_TILE = 512
N_CHUNKS = SEQ_TILE // CHUNK
ROW_BLOCK = 32
VMEM_LIMIT_BYTES = 56 * 1024 * 1024
MIX_ORDER = (1, 1, 0, 0, 1, 1, 0, 0, 1, 1, 0, 0, 1, 0, 1, 1, 0, 0, 1, 1, 0, 1, 0)
NORM_AFTER_GROUP = 2
N_SLOT_BUFS = 10
STAGE_ROWS = 128
SLAB_GROUPS = ((OFF_RQ, 4 * LANES), (D_TOK + FT_Q, 4 * LANES), (D_TOK + FT_K, 2 * LANES), (OFF_RG, 4 * LANES),
               (D_TOK + FT_G, 4 * LANES), (OFF_RV, 4 * LANES))
SLOT_KBAND, SLOT_VT = 7, 8

BF16 = jnp.bfloat16
F32 = jnp.float32


def _t5_bucket_np(n):
    max_exact = NUM_BUCKETS // 2
    nf = np.maximum(n, 1).astype(np.float64)
    large = max_exact + (np.log(nf / max_exact) / math.log(MAX_DISTANCE / max_exact)
                         * (NUM_BUCKETS - max_exact)).astype(np.int32)
    large = np.minimum(large, NUM_BUCKETS - 1)
    return np.where(n < max_exact, n, large).astype(np.int32)


def _constant_tables(seq):
    half = RET_DK // 2
    inv_freq = RET_ROPE_BASE ** (-np.arange(half, dtype=np.float64) / half)
    ang = np.arange(seq, dtype=np.float64)[:, None] * inv_freq[None, :]
    cos, sin = np.cos(ang), np.sin(ang)
    cos_t = np.tile(np.concatenate([cos, cos], axis=1), (1, 2))
    sin_s = np.tile(np.concatenate([-sin, sin], axis=1), (1, 2))

    gamma = 1.0 - np.exp2(-5.0 - np.arange(RET_HEADS, dtype=np.float64))
    log_g = np.log(gamma)
    i = np.arange(CHUNK, dtype=np.float64)
    diff = i[:, None] - i[None, :]
    k_scale = RET_DK ** -0.5
    decay = np.where(diff >= 0, np.exp(log_g[:, None, None] * np.maximum(diff, 0.0)), 0.0) * k_scale
    dec = np.stack([np.concatenate([decay[2 * p], decay[2 * p + 1]], axis=1) for p in range(2)])
    xi = np.exp(log_g[:, None] * (i + 1.0))
    zeta = np.exp(log_g[:, None] * (CHUNK - 1.0 - i)) * k_scale

    def per_lane(t):
        return np.stack([np.concatenate([np.repeat(t[2 * p][:, None], RET_DK, 1),
                                         np.repeat(t[2 * p + 1][:, None], RET_DK, 1)], axis=1)
                         for p in range(2)])

    lo = (np.arange(LANES) < RET_DK)[None, None, :]
    xi_l = per_lane(xi)
    xi_m = np.stack([np.where(lo, xi_l, 0.0), np.where(lo, 0.0, xi_l)], axis=1)

    chunk_decay = np.exp(log_g * CHUNK)
    cd = np.stack([np.concatenate([np.full((1, RET_DV), chunk_decay[2 * p]),
                                   np.full((1, RET_DV), chunk_decay[2 * p + 1])], axis=1)
                   for p in range(2)])

    j = np.arange(CHUNK)[None, :, None]
    q = (np.arange(2)[:, None, None] * Q_HALF + np.arange(LANES)[None, None, :] % Q_HALF)
    dist = np.where(j <= q, q - j, q + CHUNK - j)
    bucket = _t5_bucket_np(dist)

    f = lambda a: jnp.asarray(a, dtype=F32)
    col_scale = np.ones((1, D_IN))
    col_scale[:, OFF_RG:OFF_RG + RET_HEADS * RET_DV] = 0.5
    col_scale[:, D_TOK + FT_G:D_TOK + FT_G + SWA_HEADS * SWA_HEAD_DIM] = 0.5

    return dict(cos=f(cos_t), sin_s=f(sin_s), col_scale=f(col_scale), dec=f(dec), xi=f(xi_m),
                zeta=f(per_lane(zeta)), cd=f(cd), bucket=jnp.asarray(bucket, dtype=jnp.int32))


def _dot(a, b):
    return jnp.dot(a, b, preferred_element_type=F32)


def _dot_nt(a, b):
    return lax.dot_general(a, b, (((1,), (1,)), ((), ())), preferred_element_type=F32)


def _dot_tn(a, b):
    return lax.dot_general(a, b, (((0,), (0,)), ((), ())), preferred_element_type=F32)


def _silu_of_half(hg):
    return hg + hg * jnp.tanh(hg)


def _row_blocks(n_rows):
    return [(r0, slice(r0, r0 + ROW_BLOCK)) for r0 in range(0, n_rows, ROW_BLOCK)]


def _layer_kernel(tiles_per_seq, n_tiles,
                  xnext_ref, xres_ref, x_hbm, nwrow_ref, colscale_ref, win_hbm, wout_hbm, cos_ref, sins_ref, dec_ref,
                  xi_ref, zeta_ref, cd_ref, retw_ref, qnw_ref, knw_ref, bkt_ref, relb_ref,
                  sinks_ref, o_ref, h0_s, h1_s, *scratch):
    h_bufs = (h0_s, h1_s)
    slots = (scratch[0:N_SLOT_BUFS], scratch[N_SLOT_BUFS:2 * N_SLOT_BUFS])
    (pt_s, mixed_s, mixedt_s, state_s, bias_s,
     win_ref, wft_ref, wout_ref, slab_s, stage_out, stage_x, stage_sem, slab_sem) = scratch[2 * N_SLOT_BUFS:]
    step = pl.program_id(0)
    mix_tile = jnp.maximum(step - 1, 0)
    pos0 = pl.multiple_of((jnp.minimum(step, n_tiles - 1) % tiles_per_seq) * SEQ_TILE, SEQ_TILE)
    first = mix_tile % tiles_per_seq == 0
    first_i = first.astype(jnp.int32)
    keep_state = jnp.where(first, 0.0, 1.0).astype(F32)

    def rms_norm_rows(xb):
        ms = jnp.mean(xb * xb, axis=-1, keepdims=True)
        return (xb * lax.rsqrt(ms + NORM_EPS) * nwrow_ref[...]).astype(BF16)

    def row_copies(hbm, stage, sem0, n_rows):
        return [pltpu.make_async_copy(hbm.at[pl.ds(i * STAGE_ROWS, STAGE_ROWS), :], stage.at[i],
                                      stage_sem.at[sem0 + i]) for i in range(n_rows // STAGE_ROWS)]

    def slab_copy(i):
        col0, width = SLAB_GROUPS[i]
        return pltpu.make_async_copy(win_hbm.at[:, pl.ds(col0, width)], slab_s.at[i % 2, :, pl.ds(0, width)],
                                     slab_sem.at[i % 2])

    def stage_group(i):
        col0, width = SLAB_GROUPS[i]
        slab_copy(i).wait()
        for r0 in range(0, D_MODEL, STAGE_ROWS):
            rows = slice(r0, r0 + STAGE_ROWS)
            for c0 in range(0, width, LANES):
                blk = slab_s[i % 2, rows, c0:c0 + LANES] * colscale_ref[:, col0 + c0:col0 + c0 + LANES]
                if col0 < D_TOK:
                    win_ref[rows, col0 + c0:col0 + c0 + LANES] = blk.astype(BF16)
                else:
                    wft_ref[col0 - D_TOK + c0:col0 - D_TOK + c0 + LANES, rows] = blk.T.astype(BF16)
        if i + 2 < len(SLAB_GROUPS):
            slab_copy(i + 2).start()

    def stage_w_out():
        for i, cp in enumerate(row_copies(wout_hbm, stage_out, SEQ_TILE // STAGE_ROWS, D_MIX)):
            cp.wait()
            wout_ref[i * STAGE_ROWS:(i + 1) * STAGE_ROWS, :] = stage_out[i].astype(BF16)

    @pl.when(step == 0)
    def _init():
        def build_tables():
            key = lax.broadcasted_iota(jnp.int32, (CHUNK, LANES), 0)
            lane_c = lax.broadcasted_iota(jnp.int32, (CHUNK, LANES), 1)
            for half in range(2):
                bk = bkt_ref[half]
                is_cur = key <= half * Q_HALF + lane_c % Q_HALF
                for pair in range(SWA_HEADS // 2):
                    acc = jnp.zeros((CHUNK, LANES), F32)
                    for u in range(NUM_BUCKETS):
                        val = jnp.where(lane_c < Q_HALF, relb_ref[u, 2 * pair], relb_ref[u, 2 * pair + 1]) * LOG2_E
                        acc = jnp.where(bk == u, val, acc)
                    bias_s[half * 4 + pair] = acc
                    bias_s[SWA_HEADS + half * 4 + pair] = jnp.where(is_cur, acc, NEG_INF)
            zeros_blk = jnp.zeros((N_CHUNKS, SWA_KV_HEADS, Q_HALF, 2 * LANES), BF16)
            pt_s[:, :, 3 * Q_HALF:4 * Q_HALF, 0:2 * LANES] = zeros_blk
            pt_s[:, :, 0:Q_HALF, 2 * LANES:4 * LANES] = zeros_blk
            state_s[...] = jnp.zeros_like(state_s)
            slots[0][SLOT_KBAND][0:CHUNK, :] = jnp.zeros((CHUNK, LANES), BF16)
            slots[0][SLOT_VT][:, 0:CHUNK] = jnp.zeros((SWA_KV_HEADS * SWA_HEAD_DIM, CHUNK), BF16)

        x_copies = row_copies(x_hbm.at[0], stage_x, 0, SEQ_TILE)
        for cp in x_copies + row_copies(wout_hbm, stage_out, len(x_copies), D_MIX) + [slab_copy(0), slab_copy(1)]:
            cp.start()
        build_tables()
        for i, cp in enumerate(x_copies):
            cp.wait()
            for r0, rs in _row_blocks(STAGE_ROWS):
                h0_s[i * STAGE_ROWS + r0:i * STAGE_ROWS + r0 + ROW_BLOCK, :] = rms_norm_rows(stage_x[i, rs, :])

    lane = lax.broadcasted_iota(jnp.int32, (ROW_BLOCK, LANES), 1)
    lo = lane < RET_DK
    first_half = lane % RET_DK < ROT_HALF

    def project(slot, h_s, h_next_s, stage_weights=False):
        rq_s, rqx_s, rk_s, rkz_s, rv_s, gate_s, qt_s, kband_s, vt_s, gatet_s = slot

        def proj(off):
            return _dot(h_s[...], win_ref[:, off:off + 4 * LANES])

        def proj_t(row0, n_rows):
            return _dot_nt(wft_ref[row0:row0 + n_rows, :], h_s[...])

        def rot(v, rs):
            pos = pl.ds(pos0 + rs.start, ROW_BLOCK)
            partner = jnp.where(first_half, pltpu.roll(v, LANES - ROT_HALF, 1), pltpu.roll(v, ROT_HALF, 1))
            return v * cos_ref[pos, :] + partner * sins_ref[pos, :]

        def head_rms(xt):
            ms = jnp.mean(xt * xt, axis=0, keepdims=True)
            return xt * lax.rsqrt(ms + NORM_EPS)

        tok_blocks = [slice(c0, c0 + LANES) for c0 in range(0, SEQ_TILE, LANES)]

        def post_rqk(r):
            for r0, rs in _row_blocks(SEQ_TILE):
                c, cr = r0 // CHUNK, r0 % CHUNK
                cs = slice(cr, cr + ROW_BLOCK)
                for p in range(2):
                    q = rot(r[rs, p * LANES:(p + 1) * LANES], rs)
                    rq_s[p, rs, :] = q.astype(BF16)
                    rqx_s[p, 0, rs, :] = (q * xi_ref[p, 0, cs, :]).astype(BF16)
                    rqx_s[p, 1, rs, :] = (q * xi_ref[p, 1, cs, :]).astype(BF16)
                    k = rot(r[rs, (2 + p) * LANES:(3 + p) * LANES], rs)
                    rk_s[p, c, cr:cr + ROW_BLOCK, :] = jnp.where(lo, k, 0.0).astype(BF16)
                    rk_s[p, c, CHUNK + cr:CHUNK + cr + ROW_BLOCK, :] = jnp.where(lo, 0.0, k).astype(BF16)
                    rkz_s[p, rs, :] = (k * zeta_ref[p, cs, :]).astype(BF16)

        def post_rv(r):
            for r0, rs in _row_blocks(SEQ_TILE):
                rv_s[rs, :] = r[rs, :].astype(BF16)

        def post_rg(r):
            for r0, rs in _row_blocks(SEQ_TILE):
                gate_s[rs, :] = _silu_of_half(r[rs, :])

        def post_qt(rt):
            for pair in range(SWA_HEADS // 2):
                g, gp = pair // 2, pair % 2
                for c, ts in enumerate(tok_blocks):
                    xe, xo = (head_rms(rt[hd * SWA_HEAD_DIM:(hd + 1) * SWA_HEAD_DIM, ts])
                              for hd in (2 * pair, 2 * pair + 1))
                    for half in range(2):
                        hq = slice(half * Q_HALF, (half + 1) * Q_HALF)
                        qt_s[c, half, g, :, gp * LANES:(gp + 1) * LANES] = jnp.concatenate(
                            [xe[:, hq], xo[:, hq]], axis=1).astype(BF16)

        def post_kvt(rt):
            qk_head = qnw_ref[...] * knw_ref[...] * (SWA_HEAD_DIM ** -0.5 * LOG2_E)
            qk_w = jnp.concatenate([qk_head] * SWA_KV_HEADS, axis=1)
            for ts in tok_blocks:
                knt = jnp.concatenate(
                    [head_rms(rt[g * SWA_HEAD_DIM:(g + 1) * SWA_HEAD_DIM, ts]) for g in range(SWA_KV_HEADS)],
                    axis=0)
                kband_s[CHUNK + ts.start:CHUNK + ts.stop, :] = (knt.T * qk_w).astype(BF16)
                vt_s[:, CHUNK + ts.start:CHUNK + ts.stop] = rt[LANES:2 * LANES, ts].astype(BF16)

        def post_gt(rt):
            for r0 in range(0, SWA_HEADS * SWA_HEAD_DIM, ROW_BLOCK):
                for ts in tok_blocks:
                    gatet_s[r0:r0 + ROW_BLOCK, ts] = _silu_of_half(rt[r0:r0 + ROW_BLOCK, ts])

        def post_qkvt(rt):
            post_qt(rt[0:FT_K])
            post_kvt(rt[FT_K:FT_G])

        groups = [
            (functools.partial(proj, OFF_RQ), post_rqk, (0,)),
            (functools.partial(proj_t, FT_Q, FT_G - FT_Q), post_qkvt, (1, 2)),
            (functools.partial(proj, OFF_RG), post_rg, (3,)),
            (functools.partial(proj_t, FT_G, D_FT - FT_G), post_gt, (4,)),
            (functools.partial(proj, OFF_RV), post_rv, (5,)),
        ]
        pending = None
        for i, (matmul, post, slabs) in enumerate(groups):
            if stage_weights:
                for slab in slabs:
                    stage_group(slab)
            result = matmul()
            yield
            if pending is not None:
                pending[0](pending[1])
                yield
            pending = (post, result)
            if i == NORM_AFTER_GROUP:
                for r0, rs in _row_blocks(SEQ_TILE):
                    h_next_s[rs, :] = rms_norm_rows(xnext_ref[0, rs, :])
                yield
        pending[0](pending[1])
        yield

    def mix(slot, next_slot):
        rq_s, rqx_s, rk_s, rkz_s, rv_s, gate_s, qt_s, kband_s, vt_s, gatet_s = slot
        key = lax.broadcasted_iota(jnp.int32, (Q_HALF, LANES), 0)
        lane_h = lax.broadcasted_iota(jnp.int32, (Q_HALF, LANES), 1)
        tri_t = key <= lane_h % Q_HALF
        cur_keep = jnp.where(tri_t, 1.0, 0.0).astype(BF16)
        prev_keep = jnp.where(tri_t, 0.0, 1.0).astype(BF16)
        first_head_lanes = lax.broadcasted_iota(jnp.int32, (1, LANES), 1) < Q_HALF
        zero_q = jnp.zeros((SWA_HEAD_DIM, 2 * LANES), BF16)
        ones_rows = jnp.ones((SUBLANES, 2 * CHUNK), BF16)

        def out_proj(c_hi):
            pair = slice((c_hi - 1) * CHUNK, (c_hi + 1) * CHUNK)
            o_ref[0, pair, :] = (xres_ref[0, pair, :]
                                 + _dot(mixed_s[pair, :], wout_ref[0:RET_HEADS * RET_DV, :])
                                 + _dot_tn(mixedt_s[:, pair], wout_ref[RET_HEADS * RET_DV:D_MIX, :]))

        chunk_rows = [slice(c * CHUNK, (c + 1) * CHUNK) for c in range(N_CHUNKS)]
        chunk_band = [slice(c * CHUNK, (c + 2) * CHUNK) for c in range(N_CHUNKS)]

        def logits_for(c):
            rhs_a, rhs_b = (jnp.concatenate([jnp.concatenate([qt_s[c, half, 0], zero_q], axis=1),
                                             jnp.concatenate([zero_q, qt_s[c, half, 1]], axis=1)], axis=0)
                            for half in range(2))
            k0 = c * CHUNK
            lt_a = _dot(kband_s[k0:k0 + 3 * Q_HALF, :], rhs_a)
            lt_b = _dot(kband_s[k0 + Q_HALF:k0 + 4 * Q_HALF, :], rhs_b)
            return lt_a, lt_b

        scores, incs, logits_t = {}, {}, []
        for c, rows in enumerate(chunk_rows):
            for p in range(2):
                s2 = _dot_nt(rq_s[p, rows, :], rk_s[p, c])
                scores[c, p] = jnp.concatenate([(s2[bs, :] * dec_ref[p, bs, :]).astype(BF16)
                                                for _, bs in _row_blocks(CHUNK)], axis=0)
                incs[c, p] = _dot_tn(rkz_s[p, rows, :], rv_s[rows, p * 2 * RET_DV:(p + 1) * 2 * RET_DV])
            logits_t.append(logits_for(c))
        yield

        ret_outs = {}
        for p in range(2):
            full = state_s[p] * keep_state
            for c, rows in enumerate(chunk_rows):
                full_b = full.astype(BF16)
                v2 = rv_s[rows, p * 2 * RET_DV:(p + 1) * 2 * RET_DV]
                s2 = scores[c, p]
                for j in range(2):
                    ret_outs[c, 2 * p + j] = _dot(
                        jnp.concatenate([s2[:, j * CHUNK:(j + 1) * CHUNK], rqx_s[p, j, rows, :]], axis=1),
                        jnp.concatenate([v2[:, j * RET_DV:(j + 1) * RET_DV],
                                         full_b[:, j * RET_DV:(j + 1) * RET_DV]], axis=0))
                full = full * cd_ref[p] + incs[c, p]
            state_s[p] = full
        yield

        for c in range(N_CHUNKS):
            rows, band, lt = chunk_rows[c], chunk_band[c], logits_t[c]

            for hd in range(RET_HEADS):
                o = ret_outs[c, hd]
                cols = slice(hd * RET_DV, (hd + 1) * RET_DV)
                for r0, bs in _row_blocks(CHUNK):
                    ob = o[bs, :]
                    rs = slice(c * CHUNK + r0, c * CHUNK + r0 + ROW_BLOCK)
                    mu = jnp.mean(ob, axis=-1, keepdims=True)
                    d = ob - mu
                    var = jnp.mean(d * d, axis=-1, keepdims=True)
                    y = d * lax.rsqrt(var + GN_EPS) * retw_ref[:, cols]
                    mixed_s[rs, cols] = (y * gate_s[rs, cols]).astype(BF16)
            lt_a, lt_b = lt
            blk = [slice(i * Q_HALF, (i + 1) * Q_HALF) for i in range(3)]
            p_rows = [slice(i * Q_HALF, (i + 1) * Q_HALF) for i in range(4)]
            sink_terms = {}
            for half in range(2):
                for pair in range(SWA_HEADS // 2):
                    g, gp = pair // 2, pair % 2
                    cs = slice(pair * LANES, (pair + 1) * LANES)
                    sink = jnp.where(first_head_lanes, sinks_ref[2 * pair], sinks_ref[2 * pair + 1]) * LOG2_E
                    bi = half * 4 + pair
                    bias = bias_s[bi + SWA_HEADS * first_i] if c == 0 else bias_s[bi]
                    if half == 0:
                        lg0 = jnp.where(tri_t, lt_a[blk[2], cs], lt_a[blk[0], cs]) + bias[0:Q_HALF]
                        lg1 = lt_a[blk[1], cs] + bias[Q_HALF:CHUNK]
                    else:
                        lg0 = lt_b[blk[1], cs] + bias[0:Q_HALF]
                        lg1 = jnp.where(tri_t, lt_b[blk[2], cs], lt_b[blk[0], cs]) + bias[Q_HALF:CHUNK]
                    m = jnp.maximum(jnp.maximum(jnp.max(lg0, axis=0, keepdims=True),
                                                jnp.max(lg1, axis=0, keepdims=True)), sink)
                    e0 = jnp.exp2(lg0 - m).astype(BF16)
                    e1 = jnp.exp2(lg1 - m).astype(BF16)
                    sink_terms[half, pair] = jnp.exp2(sink - m)
                    pq = slice(half * 2 * LANES + gp * LANES, half * 2 * LANES + (gp + 1) * LANES)
                    if half == 0:
                        pt_s[c, g, p_rows[0], pq] = e0 * prev_keep
                        pt_s[c, g, p_rows[2], pq] = e0 * cur_keep
                        pt_s[c, g, p_rows[1], pq] = e1
                    else:
                        pt_s[c, g, p_rows[2], pq] = e0
                        pt_s[c, g, p_rows[1], pq] = e1 * prev_keep
                        pt_s[c, g, p_rows[3], pq] = e1 * cur_keep
            yield

            for g in range(SWA_KV_HEADS):
                vt1 = jnp.concatenate([vt_s[g * SWA_HEAD_DIM:(g + 1) * SWA_HEAD_DIM, band], ones_rows], axis=0)
                ot = _dot(vt1, pt_s[c, g])
                assert ot.shape == (SWA_HEAD_DIM + SUBLANES, 4 * LANES)
                den = ot[SWA_HEAD_DIM:SWA_HEAD_DIM + 1, :] + jnp.concatenate(
                    [sink_terms[half, 2 * g + gp] for half in range(2) for gp in range(2)], axis=1)
                otn = ot[0:SWA_HEAD_DIM, :] * (1.0 / den)
                for hh in range(SWA_GROUP):
                    hd = SWA_GROUP * g + hh
                    hs = slice(hd * SWA_HEAD_DIM, (hd + 1) * SWA_HEAD_DIM)
                    o_h = jnp.concatenate([otn[:, hh * Q_HALF:(hh + 1) * Q_HALF],
                                           otn[:, 2 * LANES + hh * Q_HALF:2 * LANES + (hh + 1) * Q_HALF]], axis=1)
                    mixedt_s[hs, rows] = (o_h * gatet_s[hs, rows]).astype(BF16)
            yield
            if c % 2 == 1:
                out_proj(c)
                yield

        last = slice(SEQ_TILE, SEQ_TILE + CHUNK)
        next_slot[SLOT_KBAND][0:CHUNK, :] = kband_s[last, :]
        next_slot[SLOT_VT][:, 0:CHUNK] = vt_s[:, last]

    def run(parity, do_project=True, do_mix=True, stage_weights=False):
        streams = [project(slots[parity], h_bufs[parity], h_bufs[1 - parity], stage_weights) if do_project
                   else iter(()),
                   mix(slots[1 - parity], slots[parity]) if do_mix else iter(())]
        for which in MIX_ORDER + (0,) * len(MIX_ORDER) + (1,) * len(MIX_ORDER):
            next(streams[which], None)

    is_first, is_last = step == 0, step == n_tiles
    pl.when(is_first)(functools.partial(run, 0, do_mix=False, stage_weights=True))
    pl.when(is_first)(stage_w_out)
    pl.when(is_last)(functools.partial(run, n_tiles % 2, do_project=False))
    for parity in range(2):
        pl.when((step % 2 == parity) & jnp.logical_not(is_first | is_last))(functools.partial(run, parity))


def kernel(x, norm_w, w_in, ret_norm_w, q_norm_w, k_norm_w, sinks, rel_bias, w_out):
    batch, seq, d_model = x.shape
    assert d_model == D_MODEL and seq % SEQ_TILE == 0
    assert w_in.shape == (D_MODEL, D_IN) and w_out.shape == (D_MIX, D_MODEL)
    tb = _constant_tables(seq)
    tiles_per_seq = seq // SEQ_TILE
    n_tiles = batch * tiles_per_seq

    def proj_tile(s):
        return jnp.minimum(s, n_tiles - 1)

    def mix_tile(s):
        return jnp.maximum(s - 1, 0)

    const2 = lambda s: (0, 0)
    const3 = lambda s: (0, 0, 0)
    const4 = lambda s: (0, 0, 0, 0)
    pos_spec = pl.BlockSpec((seq, LANES), const2, pipeline_mode=pl.Buffered(1))
    smem = pl.BlockSpec(memory_space=pltpu.SMEM)
    x_block = (1, SEQ_TILE, D_MODEL)
    mix_map = lambda s: (mix_tile(s) // tiles_per_seq, mix_tile(s) % tiles_per_seq, 0)

    in_specs = [
        pl.BlockSpec(x_block, lambda s: (proj_tile(s + 1) // tiles_per_seq, proj_tile(s + 1) % tiles_per_seq, 0)),
        pl.BlockSpec(x_block, mix_map),
        pl.BlockSpec(memory_space=pl.ANY),
        pl.BlockSpec((1, D_MODEL), const2),
        pl.BlockSpec((1, D_IN), const2),
        pl.BlockSpec(memory_space=pl.ANY),
        pl.BlockSpec(memory_space=pl.ANY),
        pos_spec, pos_spec,
        pl.BlockSpec((2, CHUNK, 2 * CHUNK), const3),
        pl.BlockSpec((2, 2, CHUNK, LANES), const4),
        pl.BlockSpec((2, CHUNK, LANES), const3),
        pl.BlockSpec((2, 1, 2 * RET_DV), const3),
        pl.BlockSpec((1, RET_HEADS * RET_DV), const2),
        pl.BlockSpec((1, SWA_HEAD_DIM), const2),
        pl.BlockSpec((1, SWA_HEAD_DIM), const2),
        pl.BlockSpec((2, CHUNK, LANES), const3),
        smem, smem,
    ]
    slot_bufs = [
        pltpu.VMEM((2, SEQ_TILE, LANES), BF16),
        pltpu.VMEM((2, 2, SEQ_TILE, LANES), BF16),
        pltpu.VMEM((2, N_CHUNKS, 2 * CHUNK, LANES), BF16),
        pltpu.VMEM((2, SEQ_TILE, LANES), BF16),
        pltpu.VMEM((SEQ_TILE, RET_HEADS * RET_DV), BF16),
        pltpu.VMEM((SEQ_TILE, RET_HEADS * RET_DV), F32),
        pltpu.VMEM((N_CHUNKS, 2, SWA_KV_HEADS, SWA_HEAD_DIM, 2 * LANES), BF16),
        pltpu.VMEM((SEQ_TILE + CHUNK, LANES), BF16),
        pltpu.VMEM((SWA_KV_HEADS * SWA_HEAD_DIM, SEQ_TILE + CHUNK), BF16),
        pltpu.VMEM((SWA_HEADS * SWA_HEAD_DIM, SEQ_TILE), F32),
    ]
    assert len(slot_bufs) == N_SLOT_BUFS
    h_buf = pltpu.VMEM((SEQ_TILE, D_MODEL), BF16)
    scratch = [h_buf, h_buf] + slot_bufs + slot_bufs + [
        pltpu.VMEM((N_CHUNKS, SWA_KV_HEADS, 2 * CHUNK, SWA_GROUP * CHUNK), BF16),
        pltpu.VMEM((SEQ_TILE, RET_HEADS * RET_DV), BF16),
        pltpu.VMEM((SWA_HEADS * SWA_HEAD_DIM, SEQ_TILE), BF16),
        pltpu.VMEM((2, 2 * RET_DK, 2 * RET_DV), F32),
        pltpu.VMEM((2 * SWA_HEADS, CHUNK, LANES), F32),
        pltpu.VMEM((D_MODEL, D_TOK), BF16),
        pltpu.VMEM((D_FT, D_MODEL), BF16),
        pltpu.VMEM((D_MIX, D_MODEL), BF16),
        pltpu.VMEM((2, D_MODEL, 4 * LANES), F32),
        pltpu.VMEM((D_MIX // STAGE_ROWS, STAGE_ROWS, D_MODEL), F32),
        pltpu.VMEM((SEQ_TILE // STAGE_ROWS, STAGE_ROWS, D_MODEL), F32),
        pltpu.SemaphoreType.DMA(((SEQ_TILE + D_MIX) // STAGE_ROWS,)),
        pltpu.SemaphoreType.DMA((2,)),
    ]
    return pl.pallas_call(
        functools.partial(_layer_kernel, tiles_per_seq, n_tiles),
        grid=(n_tiles + 1,),
        in_specs=in_specs,
        out_specs=pl.BlockSpec(x_block, mix_map),
        out_shape=jax.ShapeDtypeStruct(x.shape, x.dtype),
        scratch_shapes=scratch,
        compiler_params=pltpu.CompilerParams(
            dimension_semantics=("arbitrary",),
            vmem_limit_bytes=VMEM_LIMIT_BYTES),
    )(x, x, x, norm_w.reshape(1, D_MODEL), tb["col_scale"], w_in, w_out,
      tb["cos"], tb["sin_s"], tb["dec"], tb["xi"], tb["zeta"], tb["cd"],
      ret_norm_w.reshape(1, -1), q_norm_w.reshape(1, -1), k_norm_w.reshape(1, -1), tb["bucket"],
      rel_bias.astype(F32), sinks.astype(F32))
```

```python
import functools
import math

import numpy as np
import jax
import jax.numpy as jnp
from jax import lax
from jax.experimental import pallas as pl
from jax.experimental.pallas import tpu as pltpu

D_MODEL = 1024
RET_HEADS = 4
RET_DK = 64
RET_DV = 128
ROT_HALF = RET_DK // 2
CHUNK = 128
Q_HALF = CHUNK // 2
RET_ROPE_BASE = 10000.0
SWA_HEADS = 8
SWA_KV_HEADS = 2
SWA_HEAD_DIM = 64
SWA_GROUP = SWA_HEADS // SWA_KV_HEADS
NUM_BUCKETS = 32
MAX_DISTANCE = 128
NORM_EPS = 1e-6
GN_EPS = 1e-5
NEG_INF = -1e30
LOG2_E = math.log2(math.e)

OFF_RQ, OFF_RK, OFF_RV, OFF_RG = 0, 256, 512, 1024
D_TOK = 1536
FT_Q, FT_K, FT_G = 0, 512, 768
D_FT = 1280
D_IN = D_TOK + D_FT
D_MIX = 1024

LANES = 128
SUBLANES = 8
SEQ_TILE = 512
N_CHUNKS = SEQ_TILE // CHUNK
ROW_BLOCK = 32
VMEM_LIMIT_BYTES = 56 * 1024 * 1024
MIX_ORDER = (1, 1, 0, 0, 1, 1, 0, 0, 1, 1, 0, 0, 1, 0, 1, 1, 0, 0, 1, 1, 0, 0, 1, 0, 0)
NORM_AFTER_GROUP = 2
N_SLOT_BUFS = 10
STAGE_ROWS = 128
SLAB_GROUPS = ((OFF_RQ, 4 * LANES), (D_TOK + FT_Q, 4 * LANES), (OFF_RG, 4 * LANES), (D_TOK + FT_G, 4 * LANES),
               (D_TOK + FT_K, 2 * LANES), (OFF_RV, 4 * LANES))
SLOT_KBAND, SLOT_VT = 7, 8

BF16 = jnp.bfloat16
F32 = jnp.float32


def _t5_bucket_np(n):
    max_exact = NUM_BUCKETS // 2
    nf = np.maximum(n, 1).astype(np.float64)
    large = max_exact + (np.log(nf / max_exact) / math.log(MAX_DISTANCE / max_exact)
                         * (NUM_BUCKETS - max_exact)).astype(np.int32)
    large = np.minimum(large, NUM_BUCKETS - 1)
    return np.where(n < max_exact, n, large).astype(np.int32)


def _constant_tables(seq):
    half = RET_DK // 2
    inv_freq = RET_ROPE_BASE ** (-np.arange(half, dtype=np.float64) / half)
    ang = np.arange(seq, dtype=np.float64)[:, None] * inv_freq[None, :]
    cos, sin = np.cos(ang), np.sin(ang)
    cos_t = np.tile(np.concatenate([cos, cos], axis=1), (1, 2))
    sin_s = np.tile(np.concatenate([-sin, sin], axis=1), (1, 2))

    gamma = 1.0 - np.exp2(-5.0 - np.arange(RET_HEADS, dtype=np.float64))
    log_g = np.log(gamma)
    i = np.arange(CHUNK, dtype=np.float64)
    diff = i[:, None] - i[None, :]
    k_scale = RET_DK ** -0.5
    decay = np.where(diff >= 0, np.exp(log_g[:, None, None] * np.maximum(diff, 0.0)), 0.0) * k_scale
    dec = np.stack([np.concatenate([decay[2 * p], decay[2 * p + 1]], axis=1) for p in range(2)])
    xi = np.exp(log_g[:, None] * (i + 1.0))
    zeta = np.exp(log_g[:, None] * (CHUNK - 1.0 - i)) * k_scale

    def per_lane(t):
        return np.stack([np.concatenate([np.repeat(t[2 * p][:, None], RET_DK, 1),
                                         np.repeat(t[2 * p + 1][:, None], RET_DK, 1)], axis=1)
                         for p in range(2)])

    lo = (np.arange(LANES) < RET_DK)[None, None, :]
    xi_l = per_lane(xi)
    xi_m = np.stack([np.where(lo, xi_l, 0.0), np.where(lo, 0.0, xi_l)], axis=1)

    chunk_decay = np.exp(log_g * CHUNK)
    cd = np.stack([np.concatenate([np.full((1, RET_DV), chunk_decay[2 * p]),
                                   np.full((1, RET_DV), chunk_decay[2 * p + 1])], axis=1)
                   for p in range(2)])

    j = np.arange(CHUNK)[None, :, None]
    q = (np.arange(2)[:, None, None] * Q_HALF + np.arange(LANES)[None, None, :] % Q_HALF)
    dist = np.where(j <= q, q - j, q + CHUNK - j)
    bucket = _t5_bucket_np(dist)

    f = lambda a: jnp.asarray(a, dtype=F32)
    col_scale = np.ones((1, D_IN))
    col_scale[:, OFF_RG:OFF_RG + RET_HEADS * RET_DV] = 0.5
    col_scale[:, D_TOK + FT_G:D_TOK + FT_G + SWA_HEADS * SWA_HEAD_DIM] = 0.5

    return dict(cos=f(cos_t), sin_s=f(sin_s), col_scale=f(col_scale), dec=f(dec), xi=f(xi_m),
                zeta=f(per_lane(zeta)), cd=f(cd), bucket=jnp.asarray(bucket, dtype=jnp.int32))


def _dot(a, b):
    return jnp.dot(a, b, preferred_element_type=F32)


def _dot_nt(a, b):
    return lax.dot_general(a, b, (((1,), (1,)), ((), ())), preferred_element_type=F32)


def _dot_tn(a, b):
    return lax.dot_general(a, b, (((0,), (0,)), ((), ())), preferred_element_type=F32)


def _silu_of_half(hg):
    return hg + hg * jnp.tanh(hg)


def _row_blocks(n_rows):
    return [(r0, slice(r0, r0 + ROW_BLOCK)) for r0 in range(0, n_rows, ROW_BLOCK)]


def _layer_kernel(tiles_per_seq, n_tiles,
                  xnext_ref, xres_ref, x_hbm, nwrow_ref, colscale_ref, win_hbm, wout_hbm, cos_ref, sins_ref, dec_ref,
                  xi_ref, zeta_ref, cd_ref, retw_ref, qnw_ref, knw_ref, bkt_ref, relb_ref,
                  sinks_ref, o_ref, h0_s, h1_s, *scratch):
    h_bufs = (h0_s, h1_s)
    slots = (scratch[0:N_SLOT_BUFS], scratch[N_SLOT_BUFS:2 * N_SLOT_BUFS])
    (pt_s, mixed_s, mixedt_s, state_s, bias_s,
     win_ref, wft_ref, wout_ref, slab_s, stage_out, stage_x, stage_sem, slab_sem) = scratch[2 * N_SLOT_BUFS:]
    step = pl.program_id(0)
    mix_tile = jnp.maximum(step - 1, 0)
    pos0 = pl.multiple_of((jnp.minimum(step, n_tiles - 1) % tiles_per_seq) * SEQ_TILE, SEQ_TILE)
    first = mix_tile % tiles_per_seq == 0
    first_i = first.astype(jnp.int32)
    keep_state = jnp.where(first, 0.0, 1.0).astype(F32)

    def rms_norm_rows(xb):
        ms = jnp.mean(xb * xb, axis=-1, keepdims=True)
        return (xb * lax.rsqrt(ms + NORM_EPS) * nwrow_ref[...]).astype(BF16)

    def row_copies(hbm, stage, sem0, n_rows):
        return [pltpu.make_async_copy(hbm.at[pl.ds(i * STAGE_ROWS, STAGE_ROWS), :], stage.at[i],
                                      stage_sem.at[sem0 + i]) for i in range(n_rows // STAGE_ROWS)]

    def slab_copy(i):
        col0, width = SLAB_GROUPS[i]
        return pltpu.make_async_copy(win_hbm.at[:, pl.ds(col0, width)], slab_s.at[i % 2, :, pl.ds(0, width)],
                                     slab_sem.at[i % 2])

    def stage_group(i):
        col0, width = SLAB_GROUPS[i]
        slab_copy(i).wait()
        for r0 in range(0, D_MODEL, STAGE_ROWS):
            rows = slice(r0, r0 + STAGE_ROWS)
            for c0 in range(0, width, LANES):
                blk = slab_s[i % 2, rows, c0:c0 + LANES] * colscale_ref[:, col0 + c0:col0 + c0 + LANES]
                if col0 < D_TOK:
                    win_ref[rows, col0 + c0:col0 + c0 + LANES] = blk.astype(BF16)
                else:
                    wft_ref[col0 - D_TOK + c0:col0 - D_TOK + c0 + LANES, rows] = blk.T.astype(BF16)
        if i + 2 < len(SLAB_GROUPS):
            slab_copy(i + 2).start()

    def stage_w_out():
        for i, cp in enumerate(row_copies(wout_hbm, stage_out, SEQ_TILE // STAGE_ROWS, D_MIX)):
            cp.wait()
            wout_ref[i * STAGE_ROWS:(i + 1) * STAGE_ROWS, :] = stage_out[i].astype(BF16)

    @pl.when(step == 0)
    def _init():
        def build_tables():
            key = lax.broadcasted_iota(jnp.int32, (CHUNK, LANES), 0)
            lane_c = lax.broadcasted_iota(jnp.int32, (CHUNK, LANES), 1)
            for half in range(2):
                bk = bkt_ref[half]
                is_cur = key <= half * Q_HALF + lane_c % Q_HALF
                for pair in range(SWA_HEADS // 2):
                    acc = jnp.zeros((CHUNK, LANES), F32)
                    for u in range(NUM_BUCKETS):
                        val = jnp.where(lane_c < Q_HALF, relb_ref[u, 2 * pair], relb_ref[u, 2 * pair + 1]) * LOG2_E
                        acc = jnp.where(bk == u, val, acc)
                    bias_s[half * 4 + pair] = acc
                    bias_s[SWA_HEADS + half * 4 + pair] = jnp.where(is_cur, acc, NEG_INF)
            zeros_blk = jnp.zeros((N_CHUNKS, SWA_KV_HEADS, Q_HALF, 2 * LANES), BF16)
            pt_s[:, :, 3 * Q_HALF:4 * Q_HALF, 0:2 * LANES] = zeros_blk
            pt_s[:, :, 0:Q_HALF, 2 * LANES:4 * LANES] = zeros_blk
            state_s[...] = jnp.zeros_like(state_s)
            slots[0][SLOT_KBAND][0:CHUNK, :] = jnp.zeros((CHUNK, LANES), BF16)
            slots[0][SLOT_VT][:, 0:CHUNK] = jnp.zeros((SWA_KV_HEADS * SWA_HEAD_DIM, CHUNK), BF16)

        x_copies = row_copies(x_hbm.at[0], stage_x, 0, SEQ_TILE)
        for cp in x_copies + row_copies(wout_hbm, stage_out, len(x_copies), D_MIX) + [slab_copy(0), slab_copy(1)]:
            cp.start()
        build_tables()
        for i, cp in enumerate(x_copies):
            cp.wait()
            for r0, rs in _row_blocks(STAGE_ROWS):
                h0_s[i * STAGE_ROWS + r0:i * STAGE_ROWS + r0 + ROW_BLOCK, :] = rms_norm_rows(stage_x[i, rs, :])

    lane = lax.broadcasted_iota(jnp.int32, (ROW_BLOCK, LANES), 1)
    lo = lane < RET_DK
    first_half = lane % RET_DK < ROT_HALF

    def project(slot, h_s, h_next_s, stage_weights=False):
        rq_s, rqx_s, rk_s, rkz_s, rv_s, gate_s, qt_s, kband_s, vt_s, gatet_s = slot

        def proj(off):
            return _dot(h_s[...], win_ref[:, off:off + 4 * LANES])

        def proj_t(row0, n_rows):
            return _dot_nt(wft_ref[row0:row0 + n_rows, :], h_s[...])

        def rot(v, rs):
            pos = pl.ds(pos0 + rs.start, ROW_BLOCK)
            partner = jnp.where(first_half, pltpu.roll(v, LANES - ROT_HALF, 1), pltpu.roll(v, ROT_HALF, 1))
            return v * cos_ref[pos, :] + partner * sins_ref[pos, :]

        def head_rms(xt):
            ms = jnp.mean(xt * xt, axis=0, keepdims=True)
            return xt * lax.rsqrt(ms + NORM_EPS)

        tok_blocks = [slice(c0, c0 + LANES) for c0 in range(0, SEQ_TILE, LANES)]

        def post_rqk(r):
            for r0, rs in _row_blocks(SEQ_TILE):
                c, cr = r0 // CHUNK, r0 % CHUNK
                cs = slice(cr, cr + ROW_BLOCK)
                for p in range(2):
                    q = rot(r[rs, p * LANES:(p + 1) * LANES], rs)
                    rq_s[p, rs, :] = q.astype(BF16)
                    rqx_s[p, 0, rs, :] = (q * xi_ref[p, 0, cs, :]).astype(BF16)
                    rqx_s[p, 1, rs, :] = (q * xi_ref[p, 1, cs, :]).astype(BF16)
                    k = rot(r[rs, (2 + p) * LANES:(3 + p) * LANES], rs)
                    rk_s[p, c, cr:cr + ROW_BLOCK, :] = jnp.where(lo, k, 0.0).astype(BF16)
                    rk_s[p, c, CHUNK + cr:CHUNK + cr + ROW_BLOCK, :] = jnp.where(lo, 0.0, k).astype(BF16)
                    rkz_s[p, rs, :] = (k * zeta_ref[p, cs, :]).astype(BF16)

        def post_rv(r):
            for r0, rs in _row_blocks(SEQ_TILE):
                rv_s[rs, :] = r[rs, :].astype(BF16)

        def post_rg(r):
            for r0, rs in _row_blocks(SEQ_TILE):
                gate_s[rs, :] = _silu_of_half(r[rs, :])

        def post_qt(rt):
            for pair in range(SWA_HEADS // 2):
                g, gp = pair // 2, pair % 2
                for c, ts in enumerate(tok_blocks):
                    xe, xo = (head_rms(rt[hd * SWA_HEAD_DIM:(hd + 1) * SWA_HEAD_DIM, ts])
                              for hd in (2 * pair, 2 * pair + 1))
                    for half in range(2):
                        hq = slice(half * Q_HALF, (half + 1) * Q_HALF)
                        qt_s[c, half, g, :, gp * LANES:(gp + 1) * LANES] = jnp.concatenate(
                            [xe[:, hq], xo[:, hq]], axis=1).astype(BF16)

        def post_kvt(rt):
            qk_head = qnw_ref[...] * knw_ref[...] * (SWA_HEAD_DIM ** -0.5 * LOG2_E)
            qk_w = jnp.concatenate([qk_head] * SWA_KV_HEADS, axis=1)
            for ts in tok_blocks:
                knt = jnp.concatenate(
                    [head_rms(rt[g * SWA_HEAD_DIM:(g + 1) * SWA_HEAD_DIM, ts]) for g in range(SWA_KV_HEADS)],
                    axis=0)
                kband_s[CHUNK + ts.start:CHUNK + ts.stop, :] = (knt.T * qk_w).astype(BF16)
                vt_s[:, CHUNK + ts.start:CHUNK + ts.stop] = rt[LANES:2 * LANES, ts].astype(BF16)

        def post_gt(rt):
            for r0 in range(0, SWA_HEADS * SWA_HEAD_DIM, ROW_BLOCK):
                for ts in tok_blocks:
                    gatet_s[r0:r0 + ROW_BLOCK, ts] = _silu_of_half(rt[r0:r0 + ROW_BLOCK, ts])

        groups = [
            (functools.partial(proj, OFF_RQ), post_rqk),
            (functools.partial(proj_t, FT_Q, SWA_HEADS * SWA_HEAD_DIM), post_qt),
            (functools.partial(proj, OFF_RG), post_rg),
            (functools.partial(proj_t, FT_G, SWA_HEADS * SWA_HEAD_DIM), post_gt),
            (functools.partial(proj_t, FT_K, 2 * LANES), post_kvt),
            (functools.partial(proj, OFF_RV), post_rv),
        ]
        pending = None
        if stage_weights:
            stage_group(0)
        for i, (matmul, post) in enumerate(groups):
            result = matmul()
            if stage_weights and i + 1 < len(groups):
                stage_group(i + 1)
            yield
            if pending is not None:
                pending[0](pending[1])
                yield
            pending = (post, result)
            if i == NORM_AFTER_GROUP:
                for r0, rs in _row_blocks(SEQ_TILE):
                    h_next_s[rs, :] = rms_norm_rows(xnext_ref[0, rs, :])
                yield
        pending[0](pending[1])
        yield

    def mix(slot, next_slot):
        rq_s, rqx_s, rk_s, rkz_s, rv_s, gate_s, qt_s, kband_s, vt_s, gatet_s = slot
        key = lax.broadcasted_iota(jnp.int32, (Q_HALF, LANES), 0)
        lane_h = lax.broadcasted_iota(jnp.int32, (Q_HALF, LANES), 1)
        tri_t = key <= lane_h % Q_HALF
        cur_keep = jnp.where(tri_t, 1.0, 0.0).astype(BF16)
        prev_keep = jnp.where(tri_t, 0.0, 1.0).astype(BF16)
        first_head_lanes = lax.broadcasted_iota(jnp.int32, (1, LANES), 1) < Q_HALF
        zero_q = jnp.zeros((SWA_HEAD_DIM, 2 * LANES), BF16)
        ones_rows = jnp.ones((SUBLANES, 2 * CHUNK), BF16)

        def out_proj(c_hi):
            pair = slice((c_hi - 1) * CHUNK, (c_hi + 1) * CHUNK)
            o_ref[0, pair, :] = (xres_ref[0, pair, :]
                                 + _dot(mixed_s[pair, :], wout_ref[0:RET_HEADS * RET_DV, :])
                                 + _dot_tn(mixedt_s[:, pair], wout_ref[RET_HEADS * RET_DV:D_MIX, :]))

        chunk_rows = [slice(c * CHUNK, (c + 1) * CHUNK) for c in range(N_CHUNKS)]
        chunk_band = [slice(c * CHUNK, (c + 2) * CHUNK) for c in range(N_CHUNKS)]

        def logits_for(c):
            rhs_a, rhs_b = (jnp.concatenate([jnp.concatenate([qt_s[c, half, 0], zero_q], axis=1),
                                             jnp.concatenate([zero_q, qt_s[c, half, 1]], axis=1)], axis=0)
                            for half in range(2))
            k0 = c * CHUNK
            lt_a = _dot(kband_s[k0:k0 + 3 * Q_HALF, :], rhs_a)
            lt_b = _dot(kband_s[k0 + Q_HALF:k0 + 4 * Q_HALF, :], rhs_b)
            return lt_a, lt_b

        scores, incs, logits_t = {}, {}, []
        for c, rows in enumerate(chunk_rows):
            for p in range(2):
                s2 = _dot_nt(rq_s[p, rows, :], rk_s[p, c])
                scores[c, p] = jnp.concatenate([(s2[bs, :] * dec_ref[p, bs, :]).astype(BF16)
                                                for _, bs in _row_blocks(CHUNK)], axis=0)
                incs[c, p] = _dot_tn(rkz_s[p, rows, :], rv_s[rows, p * 2 * RET_DV:(p + 1) * 2 * RET_DV])
            logits_t.append(logits_for(c))
        yield

        ret_outs = {}
        for p in range(2):
            full = state_s[p] * keep_state
            for c, rows in enumerate(chunk_rows):
                full_b = full.astype(BF16)
                v2 = rv_s[rows, p * 2 * RET_DV:(p + 1) * 2 * RET_DV]
                s2 = scores[c, p]
                for j in range(2):
                    ret_outs[c, 2 * p + j] = _dot(
                        jnp.concatenate([s2[:, j * CHUNK:(j + 1) * CHUNK], rqx_s[p, j, rows, :]], axis=1),
                        jnp.concatenate([v2[:, j * RET_DV:(j + 1) * RET_DV],
                                         full_b[:, j * RET_DV:(j + 1) * RET_DV]], axis=0))
                full = full * cd_ref[p] + incs[c, p]
            state_s[p] = full
        yield

        for c in range(N_CHUNKS):
            rows, band, lt = chunk_rows[c], chunk_band[c], logits_t[c]

            for hd in range(RET_HEADS):
                o = ret_outs[c, hd]
                cols = slice(hd * RET_DV, (hd + 1) * RET_DV)
                for r0, bs in _row_blocks(CHUNK):
                    ob = o[bs, :]
                    rs = slice(c * CHUNK + r0, c * CHUNK + r0 + ROW_BLOCK)
                    mu = jnp.mean(ob, axis=-1, keepdims=True)
                    d = ob - mu
                    var = jnp.mean(d * d, axis=-1, keepdims=True)
                    y = d * lax.rsqrt(var + GN_EPS) * retw_ref[:, cols]
                    mixed_s[rs, cols] = (y * gate_s[rs, cols]).astype(BF16)
            lt_a, lt_b = lt
            blk = [slice(i * Q_HALF, (i + 1) * Q_HALF) for i in range(3)]
            p_rows = [slice(i * Q_HALF, (i + 1) * Q_HALF) for i in range(4)]
            sink_terms = {}
            for half in range(2):
                for pair in range(SWA_HEADS // 2):
                    g, gp = pair // 2, pair % 2
                    cs = slice(pair * LANES, (pair + 1) * LANES)
                    sink = jnp.where(first_head_lanes, sinks_ref[2 * pair], sinks_ref[2 * pair + 1]) * LOG2_E
                    bi = half * 4 + pair
                    bias = bias_s[bi + SWA_HEADS * first_i] if c == 0 else bias_s[bi]
                    if half == 0:
                        lg0 = jnp.where(tri_t, lt_a[blk[2], cs], lt_a[blk[0], cs]) + bias[0:Q_HALF]
                        lg1 = lt_a[blk[1], cs] + bias[Q_HALF:CHUNK]
                    else:
                        lg0 = lt_b[blk[1], cs] + bias[0:Q_HALF]
                        lg1 = jnp.where(tri_t, lt_b[blk[2], cs], lt_b[blk[0], cs]) + bias[Q_HALF:CHUNK]
                    m = jnp.maximum(jnp.maximum(jnp.max(lg0, axis=0, keepdims=True),
                                                jnp.max(lg1, axis=0, keepdims=True)), sink)
                    e0 = jnp.exp2(lg0 - m).astype(BF16)
                    e1 = jnp.exp2(lg1 - m).astype(BF16)
                    sink_terms[half, pair] = jnp.exp2(sink - m)
                    pq = slice(half * 2 * LANES + gp * LANES, half * 2 * LANES + (gp + 1) * LANES)
                    if half == 0:
                        pt_s[c, g, p_rows[0], pq] = e0 * prev_keep
                        pt_s[c, g, p_rows[2], pq] = e0 * cur_keep
                        pt_s[c, g, p_rows[1], pq] = e1
                    else:
                        pt_s[c, g, p_rows[2], pq] = e0
                        pt_s[c, g, p_rows[1], pq] = e1 * prev_keep
                        pt_s[c, g, p_rows[3], pq] = e1 * cur_keep
            yield

            for g in range(SWA_KV_HEADS):
                vt1 = jnp.concatenate([vt_s[g * SWA_HEAD_DIM:(g + 1) * SWA_HEAD_DIM, band], ones_rows], axis=0)
                ot = _dot(vt1, pt_s[c, g])
                assert ot.shape == (SWA_HEAD_DIM + SUBLANES, 4 * LANES)
                den = ot[SWA_HEAD_DIM:SWA_HEAD_DIM + 1, :] + jnp.concatenate(
                    [sink_terms[half, 2 * g + gp] for half in range(2) for gp in range(2)], axis=1)
                otn = ot[0:SWA_HEAD_DIM, :] * (1.0 / den)
                for hh in range(SWA_GROUP):
                    hd = SWA_GROUP * g + hh
                    hs = slice(hd * SWA_HEAD_DIM, (hd + 1) * SWA_HEAD_DIM)
                    o_h = jnp.concatenate([otn[:, hh * Q_HALF:(hh + 1) * Q_HALF],
                                           otn[:, 2 * LANES + hh * Q_HALF:2 * LANES + (hh + 1) * Q_HALF]], axis=1)
                    mixedt_s[hs, rows] = (o_h * gatet_s[hs, rows]).astype(BF16)
            yield
            if c % 2 == 1:
                out_proj(c)
                yield

        last = slice(SEQ_TILE, SEQ_TILE + CHUNK)
        next_slot[SLOT_KBAND][0:CHUNK, :] = kband_s[last, :]
        next_slot[SLOT_VT][:, 0:CHUNK] = vt_s[:, last]

    def run(parity, do_project=True, do_mix=True, stage_weights=False):
        streams = [project(slots[parity], h_bufs[parity], h_bufs[1 - parity], stage_weights) if do_project
                   else iter(()),
                   mix(slots[1 - parity], slots[parity]) if do_mix else iter(())]
        for which in MIX_ORDER + (0,) * len(MIX_ORDER) + (1,) * len(MIX_ORDER):
            next(streams[which], None)

    is_first, is_last = step == 0, step == n_tiles
    pl.when(is_first)(functools.partial(run, 0, do_mix=False, stage_weights=True))
    pl.when(is_first)(stage_w_out)
    pl.when(is_last)(functools.partial(run, n_tiles % 2, do_project=False))
    for parity in range(2):
        pl.when((step % 2 == parity) & jnp.logical_not(is_first | is_last))(functools.partial(run, parity))


def kernel(x, norm_w, w_in, ret_norm_w, q_norm_w, k_norm_w, sinks, rel_bias, w_out):
    batch, seq, d_model = x.shape
    assert d_model == D_MODEL and seq % SEQ_TILE == 0
    assert w_in.shape == (D_MODEL, D_IN) and w_out.shape == (D_MIX, D_MODEL)
    tb = _constant_tables(seq)
    tiles_per_seq = seq // SEQ_TILE
    n_tiles = batch * tiles_per_seq

    def proj_tile(s):
        return jnp.minimum(s, n_tiles - 1)

    def mix_tile(s):
        return jnp.maximum(s - 1, 0)

    const2 = lambda s: (0, 0)
    const3 = lambda s: (0, 0, 0)
    const4 = lambda s: (0, 0, 0, 0)
    pos_spec = pl.BlockSpec((seq, LANES), const2, pipeline_mode=pl.Buffered(1))
    smem = pl.BlockSpec(memory_space=pltpu.SMEM)
    x_block = (1, SEQ_TILE, D_MODEL)
    mix_map = lambda s: (mix_tile(s) // tiles_per_seq, mix_tile(s) % tiles_per_seq, 0)

    in_specs = [
        pl.BlockSpec(x_block, lambda s: (proj_tile(s + 1) // tiles_per_seq, proj_tile(s + 1) % tiles_per_seq, 0)),
        pl.BlockSpec(x_block, mix_map),
        pl.BlockSpec(memory_space=pl.ANY),
        pl.BlockSpec((1, D_MODEL), const2),
        pl.BlockSpec((1, D_IN), const2),
        pl.BlockSpec(memory_space=pl.ANY),
        pl.BlockSpec(memory_space=pl.ANY),
        pos_spec, pos_spec,
        pl.BlockSpec((2, CHUNK, 2 * CHUNK), const3),
        pl.BlockSpec((2, 2, CHUNK, LANES), const4),
        pl.BlockSpec((2, CHUNK, LANES), const3),
        pl.BlockSpec((2, 1, 2 * RET_DV), const3),
        pl.BlockSpec((1, RET_HEADS * RET_DV), const2),
        pl.BlockSpec((1, SWA_HEAD_DIM), const2),
        pl.BlockSpec((1, SWA_HEAD_DIM), const2),
        pl.BlockSpec((2, CHUNK, LANES), const3),
        smem, smem,
    ]
    slot_bufs = [
        pltpu.VMEM((2, SEQ_TILE, LANES), BF16),
        pltpu.VMEM((2, 2, SEQ_TILE, LANES), BF16),
        pltpu.VMEM((2, N_CHUNKS, 2 * CHUNK, LANES), BF16),
        pltpu.VMEM((2, SEQ_TILE, LANES), BF16),
        pltpu.VMEM((SEQ_TILE, RET_HEADS * RET_DV), BF16),
        pltpu.VMEM((SEQ_TILE, RET_HEADS * RET_DV), F32),
        pltpu.VMEM((N_CHUNKS, 2, SWA_KV_HEADS, SWA_HEAD_DIM, 2 * LANES), BF16),
        pltpu.VMEM((SEQ_TILE + CHUNK, LANES), BF16),
        pltpu.VMEM((SWA_KV_HEADS * SWA_HEAD_DIM, SEQ_TILE + CHUNK), BF16),
        pltpu.VMEM((SWA_HEADS * SWA_HEAD_DIM, SEQ_TILE), F32),
    ]
    assert len(slot_bufs) == N_SLOT_BUFS
    h_buf = pltpu.VMEM((SEQ_TILE, D_MODEL), BF16)
    scratch = [h_buf, h_buf] + slot_bufs + slot_bufs + [
        pltpu.VMEM((N_CHUNKS, SWA_KV_HEADS, 2 * CHUNK, SWA_GROUP * CHUNK), BF16),
        pltpu.VMEM((SEQ_TILE, RET_HEADS * RET_DV), BF16),
        pltpu.VMEM((SWA_HEADS * SWA_HEAD_DIM, SEQ_TILE), BF16),
        pltpu.VMEM((2, 2 * RET_DK, 2 * RET_DV), F32),
        pltpu.VMEM((2 * SWA_HEADS, CHUNK, LANES), F32),
        pltpu.VMEM((D_MODEL, D_TOK), BF16),
        pltpu.VMEM((D_FT, D_MODEL), BF16),
        pltpu.VMEM((D_MIX, D_MODEL), BF16),
        pltpu.VMEM((2, D_MODEL, 4 * LANES), F32),
        pltpu.VMEM((D_MIX // STAGE_ROWS, STAGE_ROWS, D_MODEL), F32),
        pltpu.VMEM((SEQ_TILE // STAGE_ROWS, STAGE_ROWS, D_MODEL), F32),
        pltpu.SemaphoreType.DMA(((SEQ_TILE + D_MIX) // STAGE_ROWS,)),
        pltpu.SemaphoreType.DMA((2,)),
    ]
    return pl.pallas_call(
        functools.partial(_layer_kernel, tiles_per_seq, n_tiles),
        grid=(n_tiles + 1,),
        in_specs=in_specs,
        out_specs=pl.BlockSpec(x_block, mix_map),
        out_shape=jax.ShapeDtypeStruct(x.shape, x.dtype),
        scratch_shapes=scratch,
        compiler_params=pltpu.CompilerParams(
            dimension_semantics=("arbitrary",),
            vmem_limit_bytes=VMEM_LIMIT_BYTES),
    )(x, x, x, norm_w.reshape(1, D_MODEL), tb["col_scale"], w_in, w_out,
      tb["cos"], tb["sin_s"], tb["dec"], tb["xi"], tb["zeta"], tb["cd"],
      ret_norm_w.reshape(1, -1), q_norm_w.reshape(1, -1), k_norm_w.reshape(1, -1), tb["bucket"],
      rel_bias.astype(F32), sinks.astype(F32))
```
